```python
import math
import jax
import jax.numpy as jnp
from jax import lax
import numpy as np


D_MODEL = 1024
BATCH = 16
SEQ = 2048
DEPTH = 1
DEC_BATCH = 4
DEC_SEQ = 8192
PAST_LEN = 128

D_SSM = 512
SSM_GROUP = 16
N_SSM_GROUPS = D_SSM // SSM_GROUP
SSM_STATE = 64
D_SGU = D_MODEL - D_SSM
SGU_HEADS = 8
SGU_HEAD_DIM = D_SGU // SGU_HEADS
CHUNK = 128
D_IN_PROJ = D_SSM + 2 * D_SGU
N_EXPERTS = 32
TOP_K = 4
D_EXPERT = D_MODEL
SWIGLU_LIMIT = 7.0
SWIGLU_ALPHA = 1.702
MOE_BLOCK = 128
DT_MIN = 0.001
DT_MAX = 0.1
RMS_EPS = 1e-6
LN_EPS = 1e-5

kernel_name = 'hybrid_s5_sgu_moe_encoder'


def _rmsnorm(x, g):
    xf = x.astype(jnp.float32)
    y = xf * lax.rsqrt(jnp.mean(xf * xf, axis=-1, keepdims=True) + RMS_EPS) * g.astype(jnp.float32)
    return y.astype(x.dtype)


def _layernorm(x, g, b):
    xf = x.astype(jnp.float32)
    mu = jnp.mean(xf, axis=-1, keepdims=True)
    xc = xf - mu
    y = xc * lax.rsqrt(jnp.mean(xc * xc, axis=-1, keepdims=True) + LN_EPS)
    return (y * g.astype(jnp.float32) + b.astype(jnp.float32)).astype(x.dtype)


def _ssm_combine(e1, e2):
    a1r, a1i, b1r, b1i = e1
    a2r, a2i, b2r, b2i = e2
    return (a2r * a1r - a2i * a1i,
            a2r * a1i + a2i * a1r,
            a2r * b1r - a2i * b1i + b2r,
            a2r * b1i + a2i * b1r + b2i)


def _s5_direction(u, lam_re, lam_im, log_dt, b_re, b_im, c_re, c_im, reverse):
    f32 = jnp.float32
    lam_re = lam_re.astype(f32)
    lam_im = lam_im.astype(f32)
    dt = jnp.exp(log_dt.astype(f32))[:, None]
    mag = jnp.exp(lam_re * dt)
    ar = mag * jnp.cos(lam_im * dt)
    ai = mag * jnp.sin(lam_im * dt)
    den = lam_re * lam_re + lam_im * lam_im
    nr = ar - 1.0
    fr = (nr * lam_re + ai * lam_im) / den
    fi = (ai * lam_re - nr * lam_im) / den
    b_re = b_re.astype(f32)
    b_im = b_im.astype(f32)
    bbr = fr[..., None] * b_re - fi[..., None] * b_im
    bbi = fr[..., None] * b_im + fi[..., None] * b_re
    bu_r = jnp.einsum('blgp,gnp->blgn', u, bbr)
    bu_i = jnp.einsum('blgp,gnp->blgn', u, bbi)
    a_r = jnp.broadcast_to(ar, bu_r.shape)
    a_i = jnp.broadcast_to(ai, bu_i.shape)
    _, _, h_r, h_i = lax.associative_scan(_ssm_combine, (a_r, a_i, bu_r, bu_i), reverse=reverse, axis=1)
    return (jnp.einsum('blgn,gpn->blgp', h_r, c_re.astype(f32))
            - jnp.einsum('blgn,gpn->blgp', h_i, c_im.astype(f32)))


def _s5_mixer(z, lam_re, lam_im, log_dt, b_re, b_im, c_re, c_im, d_skip, w_glu, b_glu):
    bsz, seqlen, _ = z.shape
    zf = z.astype(jnp.float32)
    u = zf.reshape(bsz, seqlen, N_SSM_GROUPS, SSM_GROUP)
    y = _s5_direction(u, lam_re[0], lam_im[0], log_dt[0], b_re[0], b_im[0], c_re[0], c_im[0], False)
    y = y + _s5_direction(u, lam_re[1], lam_im[1], log_dt[1], b_re[1], b_im[1], c_re[1], c_im[1], True)
    y = y.reshape(bsz, seqlen, D_SSM) + d_skip.astype(jnp.float32) * zf
    g = jax.nn.gelu(y, approximate=False)
    out = g * jax.nn.sigmoid(g @ w_glu.astype(jnp.float32) + b_glu.astype(jnp.float32))
    return out.astype(z.dtype)


def _sgu_mixer(zu, zv, ln_g, ln_b, w_s, b_s):
    bsz, seqlen, _ = zu.shape
    u = jax.nn.gelu(zu, approximate=False)
    v = _layernorm(jax.nn.gelu(zv, approximate=False), ln_g, ln_b)
    vc = v.reshape(bsz, seqlen // CHUNK, CHUNK, SGU_HEADS, SGU_HEAD_DIM)
    s = jnp.einsum('hpq,bcqhd->bcphd', w_s, vc) + b_s.T[None, None, :, :, None]
    return u * s.reshape(bsz, seqlen, D_SGU)


def _moe(h, w_router, b_router, w_gate_up, b_gate_up, w_down, b_down):
    bsz, seqlen, d = h.shape
    n_tok = bsz * seqlen
    xt = h.reshape(n_tok, d)
    logits = xt.astype(jnp.float32) @ w_router.astype(jnp.float32) + b_router.astype(jnp.float32)
    top_v, top_i = lax.top_k(logits, TOP_K)
    top_w = jax.nn.softmax(top_v, axis=-1)
    n_assign = n_tok * TOP_K
    flat_e = top_i.reshape(n_assign).astype(jnp.int32)
    flat_t = jnp.repeat(jnp.arange(n_tok, dtype=jnp.int32), TOP_K)
    flat_w = top_w.reshape(n_assign)
    order = jnp.argsort(flat_e)
    sorted_e = flat_e[order]
    counts = jnp.bincount(flat_e, length=N_EXPERTS).astype(jnp.int32)
    padded = (counts + MOE_BLOCK - 1) // MOE_BLOCK * MOE_BLOCK
    ends = jnp.cumsum(padded)
    starts = ends - padded
    offs = jnp.cumsum(counts) - counts
    dest = starts[sorted_e] + jnp.arange(n_assign, dtype=jnp.int32) - offs[sorted_e]
    n_slots = n_assign + N_EXPERTS * MOE_BLOCK
    n_blocks = n_slots // MOE_BLOCK
    slot_tok = jnp.zeros((n_slots,), jnp.int32).at[dest].set(flat_t[order])
    slot_w = jnp.zeros((n_slots,), jnp.float32).at[dest].set(flat_w[order])
    block_start = jnp.arange(n_blocks, dtype=jnp.int32) * MOE_BLOCK
    block_e = jnp.minimum(jnp.searchsorted(ends, block_start, side='right'), N_EXPERTS - 1).astype(jnp.int32)

    def expert_block(args):
        tok, e = args
        xb = xt[tok]
        gu = xb @ w_gate_up[e] + b_gate_up[e]
        gate = jnp.minimum(gu[:, :D_EXPERT], SWIGLU_LIMIT)
        up = jnp.clip(gu[:, D_EXPERT:], -SWIGLU_LIMIT, SWIGLU_LIMIT)
        act = gate * jax.nn.sigmoid(SWIGLU_ALPHA * gate) * (up + 1.0)
        return act @ w_down[e] + b_down[e]

    out = lax.map(expert_block, (slot_tok.reshape(n_blocks, MOE_BLOCK), block_e))
    out = out.reshape(n_slots, d) * slot_w[:, None].astype(out.dtype)
    y = jnp.zeros((n_tok, d), out.dtype).at[slot_tok].add(out)
    return y.reshape(bsz, seqlen, d).astype(h.dtype)


def _trunk(x, g_mix, w_in, ssm_lam_re, ssm_lam_im, ssm_log_dt, ssm_b_re, ssm_b_im, ssm_c_re, ssm_c_im,
           ssm_d, w_glu, b_glu, sgu_ln_g, sgu_ln_b, sgu_w_s, sgu_b_s, g_out_ssm, g_out_sgu, w_out,
           g_ffn, w_router, b_router, w_gate_up, b_gate_up, w_down, b_down, g_final):
    for l in range(DEPTH):
        h = _rmsnorm(x, g_mix[l])
        z = h @ w_in[l]
        za = z[..., :D_SSM]
        zu = z[..., D_SSM:D_SSM + D_SGU]
        zv = z[..., D_SSM + D_SGU:]
        ya = _s5_mixer(za, ssm_lam_re[l], ssm_lam_im[l], ssm_log_dt[l], ssm_b_re[l], ssm_b_im[l],
                       ssm_c_re[l], ssm_c_im[l], ssm_d[l], w_glu[l], b_glu[l])
        yb = _sgu_mixer(zu, zv, sgu_ln_g[l], sgu_ln_b[l], sgu_w_s[l], sgu_b_s[l])
        mixed = jnp.concatenate([_rmsnorm(ya, g_out_ssm[l]), _rmsnorm(yb, g_out_sgu[l])], axis=-1)
        x = x + mixed @ w_out[l]
        x = x + _moe(_rmsnorm(x, g_ffn[l]), w_router[l], b_router[l], w_gate_up[l], b_gate_up[l],
                     w_down[l], b_down[l])
    return _rmsnorm(x, g_final)


def setup_inputs(seed: int = 0) -> dict:
    key = jax.random.key(seed)
    ks = jax.random.split(key, 32)
    f32 = jnp.float32
    nrm = lambda k, shape, scale: jax.random.normal(k, shape, f32) * scale
    G, N, P = N_SSM_GROUPS, SSM_STATE, SSM_GROUP
    n_idx = jnp.arange(N, dtype=f32)
    lam_re = -0.5 + nrm(ks[2], (DEPTH, 2, G, N), 0.01)
    lam_im = math.pi * n_idx + nrm(ks[3], (DEPTH, 2, G, N), 0.01)
    log_dt = jax.random.uniform(ks[4], (DEPTH, 2, G), f32, math.log(DT_MIN), math.log(DT_MAX))
    return {
        'x_prompt': nrm(ks[0], (BATCH, SEQ, D_MODEL), 1.0),
        'x_sample': nrm(ks[1], (DEC_BATCH, DEC_SEQ, D_MODEL), 1.0),
        'g_mix': 1.0 + nrm(ks[5], (DEPTH, D_MODEL), 0.02),
        'w_in': nrm(ks[6], (DEPTH, D_MODEL, D_IN_PROJ), D_MODEL ** -0.5),
        'ssm_lam_re': lam_re,
        'ssm_lam_im': lam_im,
        'ssm_log_dt': log_dt,
        'ssm_b_re': nrm(ks[7], (DEPTH, 2, G, N, P), (2.0 * P) ** -0.5),
        'ssm_b_im': nrm(ks[8], (DEPTH, 2, G, N, P), (2.0 * P) ** -0.5),
        'ssm_c_re': nrm(ks[9], (DEPTH, 2, G, P, N), (2.0 * N) ** -0.5),
        'ssm_c_im': nrm(ks[10], (DEPTH, 2, G, P, N), (2.0 * N) ** -0.5),
        'ssm_d': nrm(ks[11], (DEPTH, D_SSM), 1.0),
        'w_glu': nrm(ks[12], (DEPTH, D_SSM, D_SSM), D_SSM ** -0.5),
        'b_glu': nrm(ks[13], (DEPTH, D_SSM), 0.01),
        'sgu_ln_g': 1.0 + nrm(ks[14], (DEPTH, D_SGU), 0.02),
        'sgu_ln_b': nrm(ks[15], (DEPTH, D_SGU), 0.01),
        'sgu_w_s': nrm(ks[16], (DEPTH, SGU_HEADS, CHUNK, CHUNK), CHUNK ** -0.5),
        'sgu_b_s': 1.0 + nrm(ks[17], (DEPTH, SGU_HEADS, CHUNK), 0.01),
        'g_out_ssm': 1.0 + nrm(ks[18], (DEPTH, D_SSM), 0.02),
        'g_out_sgu': 1.0 + nrm(ks[19], (DEPTH, D_SGU), 0.02),
        'w_out': nrm(ks[20], (DEPTH, D_MODEL, D_MODEL), D_MODEL ** -0.5),
        'g_ffn': 1.0 + nrm(ks[21], (DEPTH, D_MODEL), 0.02),
        'w_router': nrm(ks[22], (DEPTH, D_MODEL, N_EXPERTS), D_MODEL ** -0.5),
        'b_router': nrm(ks[23], (DEPTH, N_EXPERTS), 0.01),
        'w_gate_up': nrm(ks[24], (DEPTH, N_EXPERTS, D_MODEL, 2 * D_EXPERT), D_MODEL ** -0.5),
        'b_gate_up': nrm(ks[25], (DEPTH, N_EXPERTS, 2 * D_EXPERT), 0.01),
        'w_down': nrm(ks[26], (DEPTH, N_EXPERTS, D_EXPERT, D_MODEL), D_EXPERT ** -0.5),
        'b_down': nrm(ks[27], (DEPTH, N_EXPERTS, D_MODEL), 0.01),
        'g_final': 1.0 + nrm(ks[28], (D_MODEL,), 0.02),
    }


def reference(x_prompt, x_sample, g_mix, w_in, ssm_lam_re, ssm_lam_im, ssm_log_dt, ssm_b_re, ssm_b_im,
              ssm_c_re, ssm_c_im, ssm_d, w_glu, b_glu, sgu_ln_g, sgu_ln_b, sgu_w_s, sgu_b_s,
              g_out_ssm, g_out_sgu, w_out, g_ffn, w_router, b_router, w_gate_up, b_gate_up,
              w_down, b_down, g_final):
    weights = (g_mix, w_in, ssm_lam_re, ssm_lam_im, ssm_log_dt, ssm_b_re, ssm_b_im, ssm_c_re, ssm_c_im,
               ssm_d, w_glu, b_glu, sgu_ln_g, sgu_ln_b, sgu_w_s, sgu_b_s, g_out_ssm, g_out_sgu, w_out,
               g_ffn, w_router, b_router, w_gate_up, b_gate_up, w_down, b_down, g_final)
    y_prompt = _trunk(x_prompt, *weights)
    y_sample = _trunk(x_sample, *weights)
    return (y_prompt, y_sample)
```

```python
import functools
import math

import jax
import jax.numpy as jnp
from jax import lax
from jax.experimental import pallas as pl
from jax.experimental.pallas import tpu as pltpu

F32 = jnp.float32
BF16 = jnp.bfloat16
I32 = jnp.int32

SSM_GROUP = 16
SGU_HEADS = 8
CHUNK = 128
TOP_K = 4
SWIGLU_LIMIT = 7.0
SWIGLU_ALPHA = 1.702
RMS_EPS = 1e-6
LN_EPS = 1e-5

LANES = 128
SUBLANES = 8
MXU_DIM = 256
SSM_CHUNK = MXU_DIM // SSM_GROUP

TOKEN_TILE = 512
ROUTE_TILE = 256
MOE_TILE = 256
SSM_ROW_TILE = 1024
VMEM_LIMIT = 56 * 1024 * 1024


def _cparams(n_axes=1, vmem=None):
    return pltpu.CompilerParams(
        dimension_semantics=("arbitrary",) * n_axes,
        vmem_limit_bytes=vmem if vmem is not None else VMEM_LIMIT,
    )


def _rms(x, g):
    return x * lax.rsqrt(jnp.mean(x * x, axis=-1, keepdims=True) + RMS_EPS) * g


def _gelu(x):
    return x * (lax.erf(x * (1.0 / math.sqrt(2.0))) + 1.0) * 0.5


def _inproj_kernel(xp_ref, xs_ref, g_ref, w_ref, za_ref, zuv_ref, *, n_prompt_tiles, d_ssm):
    i = pl.program_id(0)

    def body(x_ref):
        h = _rms(x_ref[...], g_ref[...])
        z = jnp.dot(h.astype(BF16), w_ref[...], preferred_element_type=F32)
        za_ref[...] = z[:, :d_ssm].astype(BF16)
        zuv_ref[...] = z[:, d_ssm:].astype(BF16)

    @pl.when(i < n_prompt_tiles)
    def _():
        body(xp_ref)

    @pl.when(i >= n_prompt_tiles)
    def _():
        body(xs_ref)


def _dual_specs(tile, width, n_prompt_tiles):
    last = n_prompt_tiles - 1
    sp = pl.BlockSpec((tile, width), lambda i: (jnp.minimum(i, last), 0))
    ss = pl.BlockSpec((tile, width), lambda i: (jnp.maximum(i - n_prompt_tiles, 0), 0))
    return sp, ss


def _inproj(xp, xs, g_mix, w_in, d_ssm):
    tp, d = xp.shape
    ts = xs.shape[0]
    tm = TOKEN_TILE
    npt = tp // tm
    nt = (tp + ts) // tm
    d_in = w_in.shape[1]
    sp, ss = _dual_specs(tm, d, npt)
    return pl.pallas_call(
        functools.partial(_inproj_kernel, n_prompt_tiles=npt, d_ssm=d_ssm),
        grid=(nt,),
        in_specs=[sp, ss,
                  pl.BlockSpec((1, d), lambda i: (0, 0)),
                  pl.BlockSpec((d, d_in), lambda i: (0, 0))],
        out_specs=[pl.BlockSpec((tm, d_ssm), lambda i: (i, 0)),
                   pl.BlockSpec((tm, d_in - d_ssm), lambda i: (i, 0))],
        out_shape=[jax.ShapeDtypeStruct((tp + ts, d_ssm), BF16),
                   jax.ShapeDtypeStruct((tp + ts, d_in - d_ssm), BF16)],
        compiler_params=_cparams(),
        name="inproj",
    )(xp, xs, g_mix, w_in)


def _ssm_matrices(lam_re, lam_im, log_dt, b_re, b_im, c_re, c_im):
    hp = lax.Precision.HIGHEST
    _, g, n = lam_re.shape
    p = b_re.shape[-1]
    lc = SSM_CHUNK
    dt = jnp.exp(log_dt)[..., None]
    mag = jnp.exp(lam_re * dt)
    ar = mag * jnp.cos(lam_im * dt)
    ai = mag * jnp.sin(lam_im * dt)
    den = lam_re * lam_re + lam_im * lam_im
    nr = ar - 1.0
    fr = (nr * lam_re + ai * lam_im) / den
    fi = (ai * lam_re - nr * lam_im) / den
    bbr = fr[..., None] * b_re - fi[..., None] * b_im
    bbi = fr[..., None] * b_im + fi[..., None] * b_re

    def mul(carry, _):
        pr, pi = carry
        return (pr * ar - pi * ai, pr * ai + pi * ar), (pr, pi)
    _, (pw_r, pw_i) = lax.scan(mul, (jnp.ones_like(ar), jnp.zeros_like(ai)), None, length=lc + 1)

    car = c_re[None] * pw_r[:lc, :, :, None, :] - c_im[None] * pw_i[:lc, :, :, None, :]
    cai = c_re[None] * pw_i[:lc, :, :, None, :] + c_im[None] * pw_r[:lc, :, :, None, :]
    kk = (jnp.einsum('tdgpn,dgnq->tdgpq', car, bbr, precision=hp)
          - jnp.einsum('tdgpn,dgnq->tdgpq', cai, bbi, precision=hp))
    kf, kb = kk[:, 0], kk[:, 1]
    jj = jnp.arange(lc)[:, None]
    ii = jnp.arange(lc)[None, :]
    tau = ii - jj
    mf = jnp.where((tau >= 0)[:, :, None, None, None], kf[jnp.clip(tau, 0, lc - 1)], 0.0)
    mb = jnp.where((tau <= 0)[:, :, None, None, None], kb[jnp.clip(-tau, 0, lc - 1)], 0.0)
    m = jnp.transpose(mf + mb, (2, 0, 4, 1, 3))
    m = m.reshape(g, lc * p, lc * p)

    ef = lc - 1 - jnp.arange(lc)
    eb = jnp.arange(lc)
    def qmat(d, e):
        wr, wi = pw_r[e, d], pw_i[e, d]
        qr = wr[..., None] * bbr[d][None] - wi[..., None] * bbi[d][None]
        qi = wr[..., None] * bbi[d][None] + wi[..., None] * bbr[d][None]
        to = lambda a: jnp.transpose(a, (1, 0, 3, 2)).reshape(g, lc * p, n)
        return to(qr), to(qi)
    qfr, qfi = qmat(0, ef)
    qbr, qbi = qmat(1, eb)

    pf_e = jnp.arange(lc) + 1
    pb_e = lc - jnp.arange(lc)
    def pmat(d, e):
        wr, wi = pw_r[e, d], pw_i[e, d]
        cr, ci = c_re[d], c_im[d]
        pr = cr[None] * wr[:, :, None, :] - ci[None] * wi[:, :, None, :]
        pi = cr[None] * wi[:, :, None, :] + ci[None] * wr[:, :, None, :]
        to = lambda a: jnp.transpose(a, (1, 3, 0, 2)).reshape(g, n, lc * p)
        return to(pr), to(-pi)
    pfr, pfi = pmat(0, pf_e)
    pbr, pbi = pmat(1, pb_e)

    np_ = g // 2
    w = lc * p
    qpair = jnp.zeros((np_, 2 * w, 4 * 2 * n), F32)
    ppair = jnp.zeros((np_, 4 * 2 * n, 2 * w), F32)
    for s in range(2):
        for blk, (qm, pm) in enumerate(((qfr, pfr), (qfi, pfi), (qbr, pbr), (qbi, pbi))):
            c0 = blk * 2 * n + s * n
            qpair = qpair.at[:, s * w:(s + 1) * w, c0:c0 + n].set(qm[s::2])
            ppair = ppair.at[:, c0:c0 + n, s * w:(s + 1) * w].set(pm[s::2])
    mpair = m.reshape(np_, 2, w, w)
    al = jnp.stack([pw_r[lc, 0], pw_i[lc, 0], pw_r[lc, 1], pw_i[lc, 1]], axis=1)
    alpha = jnp.transpose(al.reshape(np_, 2, 4, n), (0, 2, 1, 3)).reshape(np_, 4, 2 * n)
    return mpair.astype(BF16), qpair.astype(BF16), ppair.astype(BF16), alpha


def _ssm_v_kernel(u_ref, q_ref, v_ref):
    v_ref[...] = jnp.dot(u_ref[...], q_ref[...], preferred_element_type=F32)


def _ssm_scan_kernel(v_ref, a_ref, s_ref, *, n_chunks):
    afr, afi = a_ref[0:1, :], a_ref[1:2, :]
    abr, abi = a_ref[2:3, :], a_ref[3:4, :]
    b = v_ref.shape[1]
    w = LANES
    zero = jnp.zeros((b, w), F32)

    def step(c, carry):
        fr, fi, rr, ri = carry
        cb = n_chunks - 1 - c
        s_ref[c, :, 0:w] = fr
        s_ref[c, :, w:2 * w] = fi
        s_ref[cb, :, 2 * w:3 * w] = rr
        s_ref[cb, :, 3 * w:4 * w] = ri
        vfr = v_ref[c, :, 0:w]
        vfi = v_ref[c, :, w:2 * w]
        vbr = v_ref[cb, :, 2 * w:3 * w]
        vbi = v_ref[cb, :, 3 * w:4 * w]
        return (afr * fr - afi * fi + vfr, afr * fi + afi * fr + vfi,
                abr * rr - abi * ri + vbr, abr * ri + abi * rr + vbi)

    lax.fori_loop(0, n_chunks, step, (zero, zero, zero, zero))


def _ssm_y_kernel(u_ref, s_ref, m_ref, p_ref, y_ref):
    w = MXU_DIM
    y0 = jnp.dot(u_ref[:, :w], m_ref[0], preferred_element_type=F32)
    y1 = jnp.dot(u_ref[:, w:], m_ref[1], preferred_element_type=F32)
    ys = jnp.dot(s_ref[...].astype(BF16), p_ref[...], preferred_element_type=F32)
    y_ref[...] = (jnp.concatenate([y0, y1], axis=1) + ys).astype(BF16)


def _ssm_trunk(za, bsz, seqlen, mats):
    mpair, qpair, ppair, alpha = mats
    d_ssm = za.shape[1]
    g = d_ssm // SSM_GROUP
    lc = SSM_CHUNK
    nc = seqlen // lc
    rows = nc * bsz
    np_ = g // 2
    pw = 2 * lc * SSM_GROUP
    sw = qpair.shape[2]
    u = za.reshape(bsz, nc, lc, g, SSM_GROUP)
    u = jnp.transpose(u, (1, 0, 3, 2, 4)).reshape(rows, g * lc * SSM_GROUP)
    tr = min(SSM_ROW_TILE, rows)
    v = pl.pallas_call(
        _ssm_v_kernel,
        grid=(np_, rows // tr),
        in_specs=[pl.BlockSpec((tr, pw), lambda p, r: (r, p)),
                  pl.BlockSpec((None, pw, sw), lambda p, r: (p, 0, 0))],
        out_specs=pl.BlockSpec((tr, sw), lambda p, r: (r, p)),
        out_shape=jax.ShapeDtypeStruct((rows, np_ * sw), F32),
        compiler_params=_cparams(2),
        name="ssm_v",
    )(u, qpair)
    s = pl.pallas_call(
        functools.partial(_ssm_scan_kernel, n_chunks=nc),
        grid=(np_,),
        in_specs=[pl.BlockSpec((nc, bsz, sw), lambda p: (0, 0, p)),
                  pl.BlockSpec((None, 4, LANES), lambda p: (p, 0, 0))],
        out_specs=pl.BlockSpec((nc, bsz, sw), lambda p: (0, 0, p)),
        out_shape=jax.ShapeDtypeStruct((nc, bsz, np_ * sw), F32),
        compiler_params=_cparams(1),
        name="ssm_scan",
    )(v.reshape(nc, bsz, np_ * sw), alpha)
    y = pl.pallas_call(
        _ssm_y_kernel,
        grid=(np_, rows // tr),
        in_specs=[pl.BlockSpec((tr, pw), lambda p, r: (r, p)),
                  pl.BlockSpec((tr, sw), lambda p, r: (r, p)),
                  pl.BlockSpec((None, 2, MXU_DIM, MXU_DIM), lambda p, r: (p, 0, 0, 0)),
                  pl.BlockSpec((None, sw, pw), lambda p, r: (p, 0, 0))],
        out_specs=pl.BlockSpec((tr, pw), lambda p, r: (r, p)),
        out_shape=jax.ShapeDtypeStruct((rows, np_ * pw), BF16),
        compiler_params=_cparams(2),
        name="ssm_y",
    )(u, s.reshape(rows, np_ * sw), mpair, ppair)
    y = y.reshape(nc, bsz, g, lc, SSM_GROUP)
    return jnp.transpose(y, (1, 0, 3, 2, 4)).reshape(bsz * seqlen, d_ssm)


def _mix_kernel(za_ref, ya_ref, zuv_ref, dsk_ref, wglu_ref, bglu_ref, gssm_ref,
                lng_ref, lnb_ref, ws_ref, bs_ref, gsgu_ref, o_ref):
    d_ssm = za_ref.shape[1]
    tm = za_ref.shape[0]
    y = ya_ref[...].astype(F32) + dsk_ref[...] * za_ref[...].astype(F32)
    gl = _gelu(y)
    gate = jnp.dot(gl.astype(BF16), wglu_ref[...], preferred_element_type=F32) + bglu_ref[...]
    o_ref[:, :d_ssm] = _rms(gl * jax.nn.sigmoid(gate), gssm_ref[...]).astype(BF16)
    d_sgu = zuv_ref.shape[1] // 2
    u = _gelu(zuv_ref[:, :d_sgu].astype(F32))
    gv = _gelu(zuv_ref[:, d_sgu:].astype(F32))
    xc = gv - jnp.mean(gv, axis=-1, keepdims=True)
    v = xc * lax.rsqrt(jnp.mean(xc * xc, axis=-1, keepdims=True) + LN_EPS)
    v = (v * lng_ref[...] + lnb_ref[...]).astype(BF16)
    lo = lax.broadcasted_iota(I32, (CHUNK, LANES), 1) < (LANES // 2)
    zero = jnp.zeros((CHUNK, LANES), BF16)
    n_hp = d_sgu // LANES
    rows = []
    for c in range(tm // CHUNK):
        cols = []
        for j in range(n_hp):
            vp = v[c * CHUNK:(c + 1) * CHUNK, j * LANES:(j + 1) * LANES]
            rhs = jnp.concatenate([jnp.where(lo, vp, zero), jnp.where(lo, zero, vp)], axis=0)
            cols.append(jnp.dot(ws_ref[j], rhs, preferred_element_type=F32))
        rows.append(jnp.concatenate(cols, axis=1) + bs_ref[...])
    s = jnp.concatenate(rows, axis=0)
    o_ref[:, d_ssm:] = _rms(u * s, gsgu_ref[...]).astype(BF16)


def _mix(za, ya, zuv, dsk, wglu, bglu, gssm, lng, lnb, ws_pairs, bias_s, gsgu):
    t, d_ssm = za.shape
    d_sgu = zuv.shape[1] // 2
    tm = TOKEN_TILE
    row = lambda w: pl.BlockSpec((1, w), lambda i: (0, 0))
    return pl.pallas_call(
        _mix_kernel,
        grid=(t // tm,),
        in_specs=[pl.BlockSpec((tm, d_ssm), lambda i: (i, 0)),
                  pl.BlockSpec((tm, d_ssm), lambda i: (i, 0)),
                  pl.BlockSpec((tm, 2 * d_sgu), lambda i: (i, 0)),
                  row(d_ssm),
                  pl.BlockSpec((d_ssm, d_ssm), lambda i: (0, 0)),
                  row(d_ssm), row(d_ssm), row(d_sgu), row(d_sgu),
                  pl.BlockSpec(ws_pairs.shape, lambda i: (0, 0, 0)),
                  pl.BlockSpec(bias_s.shape, lambda i: (0, 0)),
                  row(d_sgu)],
        out_specs=pl.BlockSpec((tm, d_ssm + d_sgu), lambda i: (i, 0)),
        out_shape=jax.ShapeDtypeStruct((t, d_ssm + d_sgu), BF16),
        compiler_params=_cparams(),
        name="mix",
    )(za, ya, zuv, dsk, wglu, bglu, gssm, lng, lnb, ws_pairs, bias_s, gsgu)


def _route_kernel(xp_ref, xs_ref, mix_ref, wout_ref, gffn_ref, wr_ref, br_ref, tri_ref,
                  x1_ref, h2_ref, e_ref, rank_ref, wtok_ref, cnt_ref, *, n_prompt_tiles):
    i = pl.program_id(0)
    tm = mix_ref.shape[0]
    n_exp = wr_ref.shape[0]

    @pl.when(i == 0)
    def _():
        cnt_ref[...] = jnp.zeros_like(cnt_ref)

    def body(x_ref):
        x1 = x_ref[...] + jnp.dot(mix_ref[...], wout_ref[...], preferred_element_type=F32)
        x1_ref[...] = x1
        h2 = _rms(x1, gffn_ref[...])
        h2_ref[...] = h2
        logits = lax.dot_general(wr_ref[...], h2, (((1,), (1,)), ((), ())),
                                 precision=lax.Precision.HIGHEST,
                                 preferred_element_type=F32) + br_ref[...]
        eio = lax.broadcasted_iota(I32, (n_exp, tm), 0)
        vals, idxs = [], []
        l = logits
        for _ in range(TOP_K):
            m = jnp.max(l, axis=0, keepdims=True)
            idx = jnp.min(jnp.where(l == m, eio, n_exp), axis=0, keepdims=True)
            vals.append(m)
            idxs.append(idx)
            l = jnp.where(eio == idx, -jnp.inf, l)
        ex = [jnp.exp(v - vals[0]) for v in vals]
        den = ex[0] + ex[1] + ex[2] + ex[3]
        ws = [e / den for e in ex]
        hot = [eio == idx for idx in idxs]
        cnt = sum(h.astype(F32) for h in hot)
        prefix = jnp.dot(cnt.astype(BF16), tri_ref[...], preferred_element_type=F32)
        tot = prefix + cnt_ref[:, 0:1]
        ranks = [jnp.sum(jnp.where(h, tot, 0.0), axis=0, keepdims=True) for h in hot]
        e_ref[...] = jnp.concatenate(idxs, axis=0)
        rank_ref[...] = jnp.concatenate(ranks, axis=0).astype(I32)
        wpad = jnp.concatenate(ws + [jnp.zeros((LANES - TOP_K, tm), F32)], axis=0)
        wtok_ref[...] = wpad.T
        cnt_ref[...] = cnt_ref[...] + jnp.sum(cnt, axis=1, keepdims=True)

    @pl.when(i < n_prompt_tiles)
    def _():
        body(xp_ref)

    @pl.when(i >= n_prompt_tiles)
    def _():
        body(xs_ref)


def _route(xp, xs, mixed, w_out, g_ffn, wr_t, b_r):
    tp, d = xp.shape
    t = mixed.shape[0]
    tm = ROUTE_TILE
    npt = tp // tm
    nt = t // tm
    n_exp = wr_t.shape[0]
    tri = (lax.broadcasted_iota(I32, (tm, tm), 0) < lax.broadcasted_iota(I32, (tm, tm), 1)).astype(BF16)
    sp, ss = _dual_specs(tm, d, npt)
    const = lambda shape: pl.BlockSpec(shape, lambda i: (0,) * len(shape))
    return pl.pallas_call(
        functools.partial(_route_kernel, n_prompt_tiles=npt),
        grid=(nt,),
        in_specs=[sp, ss,
                  pl.BlockSpec((tm, mixed.shape[1]), lambda i: (i, 0)),
                  const(w_out.shape), const((1, d)), const(wr_t.shape), const((n_exp, 1)),
                  const((tm, tm))],
        out_specs=[pl.BlockSpec((tm, d), lambda i: (i, 0)),
                   pl.BlockSpec((tm, d), lambda i: (i, 0)),
                   pl.BlockSpec((None, TOP_K, tm), lambda i: (i, 0, 0)),
                   pl.BlockSpec((None, TOP_K, tm), lambda i: (i, 0, 0)),
                   pl.BlockSpec((tm, LANES), lambda i: (i, 0)),
                   const((n_exp, LANES))],
        out_shape=[jax.ShapeDtypeStruct((t, d), F32),
                   jax.ShapeDtypeStruct((t, d), F32),
                   jax.ShapeDtypeStruct((nt, TOP_K, tm), I32),
                   jax.ShapeDtypeStruct((nt, TOP_K, tm), I32),
                   jax.ShapeDtypeStruct((t, LANES), F32),
                   jax.ShapeDtypeStruct((n_exp, LANES), F32)],
        compiler_params=_cparams(),
        name="route",
    )(xp, xs, mixed, w_out, g_ffn, wr_t, b_r, tri)


def _dest_kernel(starts_ref, e_ref, rank_ref, d_ref):
    e = e_ref[...]
    d = rank_ref[...]
    for ex in range(starts_ref.shape[0]):
        d = d + jnp.where(e == ex, starts_ref[ex], 0)
    d_ref[...] = d


def _dest(starts, e_t, rank_t):
    nt, k, tm = e_t.shape
    blk = pl.BlockSpec((None, k, tm), lambda i, s: (i, 0, 0))
    return pl.pallas_call(
        _dest_kernel,
        grid_spec=pltpu.PrefetchScalarGridSpec(
            num_scalar_prefetch=1, grid=(nt,), in_specs=[blk, blk], out_specs=blk),
        out_shape=jax.ShapeDtypeStruct((nt, k, tm), I32),
        compiler_params=_cparams(),
        name="dest",
    )(starts, e_t, rank_t)


def _dispatch_kernel(zstart_ref, dest_ref, h_ref, xs_ref, zero_ref, sem, zsem):
    i = pl.program_id(0)
    tm = h_ref.shape[0]
    n_exp = zstart_ref.shape[0]
    pad_rows = zero_ref.shape[0]

    def zero_copy(ex):
        z0 = pl.multiple_of(zstart_ref[ex], SUBLANES)
        return pltpu.make_async_copy(zero_ref, xs_ref.at[pl.ds(z0, pad_rows), :], zsem)

    @pl.when(i == 0)
    def _():
        zero_ref[...] = jnp.zeros_like(zero_ref)
        for ex in range(n_exp):
            zero_copy(ex).start()
        for ex in range(n_exp):
            zero_copy(ex).wait()

    def row_copy(t, k):
        return pltpu.make_async_copy(h_ref.at[pl.ds(t, 1), :],
                                     xs_ref.at[pl.ds(dest_ref[k, t], 1), :], sem)

    def issue(t, c):
        for k in range(TOP_K):
            row_copy(t, k).start()
        return c

    def drain(t, c):
        for k in range(TOP_K):
            row_copy(t, k).wait()
        return c

    lax.fori_loop(0, tm, issue, 0)
    lax.fori_loop(0, tm, drain, 0)


def _dispatch(zstart, dest, h2, n_rows):
    t, d = h2.shape
    nt, k, tm = dest.shape
    return pl.pallas_call(
        _dispatch_kernel,
        grid_spec=pltpu.PrefetchScalarGridSpec(
            num_scalar_prefetch=1, grid=(nt,),
            in_specs=[pl.BlockSpec((None, k, tm), lambda i, s: (i, 0, 0), memory_space=pltpu.SMEM),
                      pl.BlockSpec((tm, d), lambda i, s: (i, 0))],
            out_specs=pl.BlockSpec(memory_space=pl.ANY),
            scratch_shapes=[pltpu.VMEM((MOE_TILE + SUBLANES, d), h2.dtype),
                            pltpu.SemaphoreType.DMA, pltpu.SemaphoreType.DMA]),
        out_shape=jax.ShapeDtypeStruct((n_rows, d), h2.dtype),
        compiler_params=_cparams(),
        name="dispatch",
    )(zstart, dest, h2)


def _moe_kernel(be_ref, nu_ref, x_ref, wgu_ref, bgu_ref, wd_ref, bd_ref, o_ref):
    j = pl.program_id(0)
    f = wd_ref.shape[0]

    @pl.when(j < nu_ref[0])
    def _():
        gu = jnp.dot(x_ref[...].astype(BF16), wgu_ref[...], preferred_element_type=F32) + bgu_ref[...]
        gate = jnp.minimum(gu[:, :f], SWIGLU_LIMIT)
        up = jnp.clip(gu[:, f:], -SWIGLU_LIMIT, SWIGLU_LIMIT)
        act = gate * jax.nn.sigmoid(SWIGLU_ALPHA * gate) * (up + 1.0)
        o_ref[...] = jnp.dot(act.astype(BF16), wd_ref[...], preferred_element_type=F32) + bd_ref[...]


def _moe(block_e, n_used, xs, wgu, bgu, wd, bd, n_blocks):
    d = xs.shape[1]
    f2 = wgu.shape[2]
    f = wd.shape[1]
    tm = MOE_TILE
    slot = lambda j, be, nu: (jnp.minimum(j, nu[0] - 1), 0)
    return pl.pallas_call(
        _moe_kernel,
        grid_spec=pltpu.PrefetchScalarGridSpec(
            num_scalar_prefetch=2, grid=(n_blocks,),
            in_specs=[pl.BlockSpec((tm, d), slot),
                      pl.BlockSpec((None, d, f2), lambda j, be, nu: (be[j], 0, 0)),
                      pl.BlockSpec((None, 1, f2), lambda j, be, nu: (be[j], 0, 0)),
                      pl.BlockSpec((None, f, d), lambda j, be, nu: (be[j], 0, 0)),
                      pl.BlockSpec((None, 1, d), lambda j, be, nu: (be[j], 0, 0))],
            out_specs=pl.BlockSpec((tm, d), slot)),
        out_shape=jax.ShapeDtypeStruct((n_blocks * tm, d), F32),
        compiler_params=_cparams(),
        name="moe",
    )(block_e, n_used, xs, wgu, bgu, wd, bd)


def _combine_kernel(dest_ref, wtok_ref, x1_ref, g_ref, eo_ref, y_ref, buf_ref, sem):
    tm = x1_ref.shape[0]

    def row_copy(t, k):
        return pltpu.make_async_copy(eo_ref.at[pl.ds(dest_ref[k, t], 1), :],
                                     buf_ref.at[k, pl.ds(t, 1), :], sem)

    def issue(t, c):
        for k in range(TOP_K):
            row_copy(t, k).start()
        return c

    def drain(t, c):
        for k in range(TOP_K):
            row_copy(t, k).wait()
        return c

    lax.fori_loop(0, tm, issue, 0)
    lax.fori_loop(0, tm, drain, 0)
    acc = x1_ref[...]
    for k in range(TOP_K):
        acc = acc + wtok_ref[:, k:k + 1] * buf_ref[k]
    y_ref[...] = _rms(acc, g_ref[...])


def _combine(dest, wtok, x1, g_final, eo, tile_off, n_tokens):
    d = x1.shape[1]
    _, k, tm = dest.shape
    return pl.pallas_call(
        _combine_kernel,
        grid=(n_tokens // tm,),
        in_specs=[pl.BlockSpec((None, k, tm), lambda i: (i + tile_off, 0, 0), memory_space=pltpu.SMEM),
                  pl.BlockSpec((tm, LANES), lambda i: (i + tile_off, 0)),
                  pl.BlockSpec((tm, d), lambda i: (i + tile_off, 0)),
                  pl.BlockSpec((1, d), lambda i: (0, 0)),
                  pl.BlockSpec(memory_space=pl.ANY)],
        out_specs=pl.BlockSpec((tm, d), lambda i: (i, 0)),
        out_shape=jax.ShapeDtypeStruct((n_tokens, d), F32),
        scratch_shapes=[pltpu.VMEM((k, tm, d), F32), pltpu.SemaphoreType.DMA],
        compiler_params=_cparams(),
        name="combine",
    )(dest, wtok, x1, g_final, eo)


def kernel(x_prompt, x_sample, g_mix, w_in, ssm_lam_re, ssm_lam_im, ssm_log_dt, ssm_b_re, ssm_b_im,
           ssm_c_re, ssm_c_im, ssm_d, w_glu, b_glu, sgu_ln_g, sgu_ln_b, sgu_w_s, sgu_b_s,
           g_out_ssm, g_out_sgu, w_out, g_ffn, w_router, b_router, w_gate_up, b_gate_up,
           w_down, b_down, g_final):
    assert g_mix.shape[0] == 1, "single-layer trunk"
    bp, lp, d = x_prompt.shape
    bs, ls, _ = x_sample.shape
    tp, ts = bp * lp, bs * ls
    t = tp + ts
    d_ssm = ssm_d.shape[1]
    d_sgu = sgu_ln_g.shape[1]
    n_exp = w_router.shape[2]
    assert tp % TOKEN_TILE == 0 and ts % TOKEN_TILE == 0
    assert lp % CHUNK == 0 and ls % CHUNK == 0 and TOKEN_TILE % CHUNK == 0
    assert SSM_CHUNK * SSM_GROUP == MXU_DIM and 2 * ssm_lam_re.shape[-1] == LANES
    assert d_sgu // SGU_HEADS == LANES // 2

    xp = x_prompt.reshape(tp, d)
    xs = x_sample.reshape(ts, d)
    row = lambda a: a.reshape(1, -1).astype(F32)

    za, zuv = _inproj(xp, xs, row(g_mix[0]), w_in[0].astype(BF16), d_ssm)

    mats = _ssm_matrices(ssm_lam_re[0], ssm_lam_im[0], ssm_log_dt[0], ssm_b_re[0], ssm_b_im[0],
                         ssm_c_re[0], ssm_c_im[0])
    ya = jnp.concatenate([_ssm_trunk(za[:tp], bp, lp, mats), _ssm_trunk(za[tp:], bs, ls, mats)], axis=0)

    ws = sgu_w_s[0]
    ws_pairs = jnp.concatenate([ws[0::2], ws[1::2]], axis=2).astype(BF16)
    bias_s = jnp.repeat(sgu_b_s[0].T, d_sgu // SGU_HEADS, axis=1).astype(F32)
    mixed = _mix(za, ya, zuv, row(ssm_d[0]), w_glu[0].astype(BF16), row(b_glu[0]), row(g_out_ssm[0]),
                 row(sgu_ln_g[0]), row(sgu_ln_b[0]), ws_pairs, bias_s, row(g_out_sgu[0]))

    x1, h2, e_t, rank_t, wtok, cnt = _route(xp, xs, mixed, w_out[0].astype(BF16), row(g_ffn[0]),
                                            w_router[0].T.astype(F32),
                                            b_router[0].reshape(n_exp, 1).astype(F32))

    counts = cnt[:, 0].astype(I32)
    padded = (counts + MOE_TILE - 1) // MOE_TILE * MOE_TILE
    ends = jnp.cumsum(padded)
    starts = ends - padded
    n_blocks = t * TOP_K // MOE_TILE + n_exp
    n_used = (ends[-1] // MOE_TILE).astype(I32)
    blk = jnp.arange(n_blocks, dtype=I32)
    be = jnp.minimum(jnp.searchsorted(ends, blk * MOE_TILE, side='right'), n_exp - 1).astype(I32)
    block_e = jnp.where(blk < n_used, be, be[jnp.maximum(n_used - 1, 0)])

    dest = _dest(starts.astype(I32), e_t, rank_t)
    zstart = starts + counts // SUBLANES * SUBLANES
    xslots = _dispatch(zstart.astype(I32), dest, h2, (n_blocks + 1) * MOE_TILE + SUBLANES)
    eo = _moe(block_e, n_used.reshape(1), xslots, w_gate_up[0].astype(BF16),
              b_gate_up[0][:, None, :].astype(F32), w_down[0].astype(BF16),
              b_down[0][:, None, :].astype(F32), n_blocks)

    gf = row(g_final)
    y_p = _combine(dest, wtok, x1, gf, eo, 0, tp)
    y_s = _combine(dest, wtok, x1, gf, eo, tp // ROUTE_TILE, ts)
    return y_p.reshape(bp, lp, d), y_s.reshape(bs, ls, d)
```

```python
import functools
import math

import jax
import jax.numpy as jnp
from jax import lax
from jax.experimental import pallas as pl
from jax.experimental.pallas import tpu as pltpu

F32 = jnp.float32
BF16 = jnp.bfloat16
I32 = jnp.int32

SSM_GROUP = 16
SGU_HEADS = 8
CHUNK = 128
TOP_K = 4
SWIGLU_LIMIT = 7.0
SWIGLU_ALPHA = 1.702
RMS_EPS = 1e-6
LN_EPS = 1e-5

LANES = 128
SUBLANES = 8
MXU_DIM = 256
SSM_CHUNK = MXU_DIM // SSM_GROUP
PAIR_W = 2 * SSM_CHUNK * SSM_GROUP
ATOM = 2 * SSM_GROUP
ATOMS = LANES // ATOM

TOKEN_TILE = 512
ROUTE_TILE = 256
MOE_TILE = 256
SSM_ROW_TILE = 1024
VMEM_LIMIT = 56 * 1024 * 1024


def _cparams(n_axes=1, vmem=None):
    return pltpu.CompilerParams(
        dimension_semantics=("arbitrary",) * n_axes,
        vmem_limit_bytes=vmem if vmem is not None else VMEM_LIMIT,
    )


def _rms(x, g):
    return x * lax.rsqrt(jnp.mean(x * x, axis=-1, keepdims=True) + RMS_EPS) * g


def _gelu(x):
    return x * (lax.erf(x * (1.0 / math.sqrt(2.0))) + 1.0) * 0.5


def _atom_masks(rows):
    lane = lax.broadcasted_iota(I32, (rows, LANES), 1)
    return [(lane >= a * ATOM) & (lane < (a + 1) * ATOM) for a in range(ATOMS)]


def _atom_transpose(src, masks):
    dst = []
    for b in range(ATOMS):
        acc = None
        for a in range(ATOMS):
            r = (a - b) % ATOMS
            piece = src[a] if r == 0 else pltpu.roll(src[a], ATOM * r, axis=1)
            acc = piece if acc is None else jnp.where(masks[a], piece, acc)
        dst.append(acc)
    return dst


def _inproj_kernel(x_ref, g_ref, w_ref, u_ref, zuv_ref, za_scr, *, d_ssm):
    tm = x_ref.shape[0]
    nc = tm // SSM_CHUNK
    h = _rms(x_ref[...], g_ref[...])
    z = jnp.dot(h.astype(BF16), w_ref[...], preferred_element_type=F32)
    zuv_ref[...] = z[:, d_ssm:].astype(BF16)
    n_blk = d_ssm // LANES
    for b in range(n_blk):
        za_scr[b] = z[:, b * LANES:(b + 1) * LANES]
    masks = _atom_masks(nc)
    n_quads = SSM_CHUNK // ATOMS
    for b in range(n_blk):
        for v in range(n_quads):
            src = [za_scr[b, pl.ds(ATOMS * v + jj, nc, stride=SSM_CHUNK), :] for jj in range(ATOMS)]
            dst = _atom_transpose(src, masks)
            for pi in range(ATOMS):
                c0 = (ATOMS * b + pi) * PAIR_W + v * LANES
                u_ref[:, c0:c0 + LANES] = dst[pi].astype(BF16)


def _inproj(x, g_mix, w_in, d_ssm, bsz, seqlen):
    t, d = x.shape
    tm = TOKEN_TILE
    d_in = w_in.shape[1]
    per_seq = seqlen // tm
    nc = tm // SSM_CHUNK
    uw = d_ssm * SSM_CHUNK
    return pl.pallas_call(
        functools.partial(_inproj_kernel, d_ssm=d_ssm),
        grid=(t // tm,),
        in_specs=[pl.BlockSpec((tm, d), lambda i: (i, 0)),
                  pl.BlockSpec((1, d), lambda i: (0, 0)),
                  pl.BlockSpec((d, d_in), lambda i: (0, 0))],
        out_specs=[pl.BlockSpec((nc, uw), lambda i: (i % per_seq, i // per_seq)),
                   pl.BlockSpec((tm, d_in - d_ssm), lambda i: (i, 0))],
        out_shape=[jax.ShapeDtypeStruct((seqlen // SSM_CHUNK, bsz * uw), BF16),
                   jax.ShapeDtypeStruct((t, d_in - d_ssm), BF16)],
        scratch_shapes=[pltpu.VMEM((d_ssm // LANES, tm, LANES), F32)],
        compiler_params=_cparams(),
        name="inproj",
    )(x, g_mix, w_in)


def _ssm_matrices(lam_re, lam_im, log_dt, b_re, b_im, c_re, c_im, d_skip):
    hp = lax.Precision.HIGHEST
    _, g, n = lam_re.shape
    p = b_re.shape[-1]
    lc = SSM_CHUNK
    np_ = g // 2
    dt = jnp.exp(log_dt)[..., None]
    mag = jnp.exp(lam_re * dt)
    ar = mag * jnp.cos(lam_im * dt)
    ai = mag * jnp.sin(lam_im * dt)
    den = lam_re * lam_re + lam_im * lam_im
    nr = ar - 1.0
    fr = (nr * lam_re + ai * lam_im) / den
    fi = (ai * lam_re - nr * lam_im) / den
    bbr = fr[..., None] * b_re - fi[..., None] * b_im
    bbi = fr[..., None] * b_im + fi[..., None] * b_re

    prs, pis = [jnp.ones_like(ar)], [jnp.zeros_like(ai)]
    for _ in range(lc):
        pr, pi = prs[-1], pis[-1]
        prs.append(pr * ar - pi * ai)
        pis.append(pr * ai + pi * ar)
    pw_r, pw_i = jnp.stack(prs), jnp.stack(pis)

    car = c_re[None] * pw_r[:lc, :, :, None, :] - c_im[None] * pw_i[:lc, :, :, None, :]
    cai = c_re[None] * pw_i[:lc, :, :, None, :] + c_im[None] * pw_r[:lc, :, :, None, :]
    kk = (jnp.einsum('tdgpn,dgnq->tdgpq', car, bbr, precision=hp)
          - jnp.einsum('tdgpn,dgnq->tdgpq', cai, bbi, precision=hp))
    kf, kb = kk[:, 0], kk[:, 1]
    jj = jnp.arange(lc)[:, None]
    ii = jnp.arange(lc)[None, :]
    tau = ii - jj
    mf = jnp.where((tau >= 0)[:, :, None, None, None], kf[jnp.clip(tau, 0, lc - 1)], 0.0)
    mb = jnp.where((tau <= 0)[:, :, None, None, None], kb[jnp.clip(-tau, 0, lc - 1)], 0.0)
    m = (mf + mb).reshape(lc, lc, np_, 2, p, p)
    eye = jnp.eye(2, dtype=F32)
    mpair = jnp.einsum('jinspq,st->njsqitp', m, eye).reshape(np_, lc * 2 * p, lc * 2 * p)

    ef = lc - 1 - jnp.arange(lc)
    eb = jnp.arange(lc)
    def qmat(d, e):
        wr, wi = pw_r[e, d], pw_i[e, d]
        qr = wr[..., None] * bbr[d][None] - wi[..., None] * bbi[d][None]
        qi = wr[..., None] * bbi[d][None] + wi[..., None] * bbr[d][None]
        to = lambda a: jnp.transpose(a, (0, 1, 3, 2)).reshape(lc, np_, 2, p, n)
        return to(qr), to(qi)
    q4 = jnp.stack(qmat(0, ef) + qmat(1, eb))
    qpair = jnp.einsum('bjnsqm,st->njsqbtm', q4, eye).reshape(np_, lc * 2 * p, 4 * 2 * n)

    pf_e = jnp.arange(lc) + 1
    pb_e = lc - jnp.arange(lc)
    def pmat(d, e):
        wr, wi = pw_r[e, d], pw_i[e, d]
        cr, ci = c_re[d], c_im[d]
        pr = cr[None] * wr[:, :, None, :] - ci[None] * wi[:, :, None, :]
        pi = cr[None] * wi[:, :, None, :] + ci[None] * wr[:, :, None, :]
        to = lambda a: jnp.transpose(a, (1, 3, 0, 2)).reshape(np_, 2, n, lc, p)
        return to(pr), to(-pi)
    p4 = jnp.stack(pmat(0, pf_e) + pmat(1, pb_e))
    ppair = jnp.einsum('bnsmip,st->nbsmitp', p4, eye).reshape(np_, 4 * 2 * n, lc * 2 * p)

    al = jnp.stack([pw_r[lc, 0], pw_i[lc, 0], pw_r[lc, 1], pw_i[lc, 1]])
    alpha = jnp.transpose(al.reshape(4, np_, 2 * n), (1, 0, 2))
    skip = jnp.broadcast_to(d_skip.reshape(np_, 1, 2 * p), (np_, lc, 2 * p)).reshape(np_, 1, lc * 2 * p)
    return mpair.astype(BF16), qpair.astype(BF16), ppair.astype(BF16), alpha, skip.astype(F32)


def _ssm_v_kernel(u_ref, q_ref, v_ref):
    v_ref[...] = jnp.dot(u_ref[...], q_ref[...], preferred_element_type=F32)


def _ssm_scan_kernel(v_ref, a_ref, s_ref, *, n_chunks):
    afr, afi = a_ref[0:1, :], a_ref[1:2, :]
    abr, abi = a_ref[2:3, :], a_ref[3:4, :]
    b = v_ref.shape[1]
    w = LANES
    zero = jnp.zeros((b, w), F32)

    def step(c, carry):
        fr, fi, rr, ri = carry
        cb = n_chunks - 1 - c
        s_ref[c, :, 0:w] = fr
        s_ref[c, :, w:2 * w] = fi
        s_ref[cb, :, 2 * w:3 * w] = rr
        s_ref[cb, :, 3 * w:4 * w] = ri
        vfr = v_ref[c, :, 0:w]
        vfi = v_ref[c, :, w:2 * w]
        vbr = v_ref[cb, :, 2 * w:3 * w]
        vbi = v_ref[cb, :, 3 * w:4 * w]
        return (afr * fr - afi * fi + vfr, afr * fi + afi * fr + vfi,
                abr * rr - abi * ri + vbr, abr * ri + abi * rr + vbi)

    lax.fori_loop(0, n_chunks, step, (zero, zero, zero, zero))


def _ssm_y_kernel(u_ref, s_ref, m_ref, p_ref, d_ref, y_ref):
    u = u_ref[...]
    y = jnp.dot(u, m_ref[...], preferred_element_type=F32)
    y = y + jnp.dot(s_ref[...].astype(BF16), p_ref[...], preferred_element_type=F32)
    y_ref[...] = (y + d_ref[...] * u.astype(F32)).astype(BF16)


def _ssm_trunk(u2, bsz, mats):
    mpair, qpair, ppair, alpha, skip = mats
    nc = u2.shape[0]
    np_ = mpair.shape[0]
    rows = nc * bsz
    pw = PAIR_W
    sw = qpair.shape[2]
    u = u2.reshape(rows, np_ * pw)
    tr = min(SSM_ROW_TILE, rows)
    v = pl.pallas_call(
        _ssm_v_kernel,
        grid=(np_, rows // tr),
        in_specs=[pl.BlockSpec((tr, pw), lambda p, r: (r, p)),
                  pl.BlockSpec((None, pw, sw), lambda p, r: (p, 0, 0))],
        out_specs=pl.BlockSpec((tr, sw), lambda p, r: (r, p)),
        out_shape=jax.ShapeDtypeStruct((rows, np_ * sw), F32),
        compiler_params=_cparams(2),
        name="ssm_v",
    )(u, qpair)
    s = pl.pallas_call(
        functools.partial(_ssm_scan_kernel, n_chunks=nc),
        grid=(np_,),
        in_specs=[pl.BlockSpec((nc, bsz, sw), lambda p: (0, 0, p)),
                  pl.BlockSpec((None, 4, LANES), lambda p: (p, 0, 0))],
        out_specs=pl.BlockSpec((nc, bsz, sw), lambda p: (0, 0, p)),
        out_shape=jax.ShapeDtypeStruct((nc, bsz, np_ * sw), F32),
        compiler_params=_cparams(1),
        name="ssm_scan",
    )(v.reshape(nc, bsz, np_ * sw), alpha)
    y = pl.pallas_call(
        _ssm_y_kernel,
        grid=(np_, rows // tr),
        in_specs=[pl.BlockSpec((tr, pw), lambda p, r: (r, p)),
                  pl.BlockSpec((tr, sw), lambda p, r: (r, p)),
                  pl.BlockSpec((None, pw, pw), lambda p, r: (p, 0, 0)),
                  pl.BlockSpec((None, sw, pw), lambda p, r: (p, 0, 0)),
                  pl.BlockSpec((None, 1, pw), lambda p, r: (p, 0, 0))],
        out_specs=pl.BlockSpec((tr, pw), lambda p, r: (r, p)),
        out_shape=jax.ShapeDtypeStruct((rows, np_ * pw), BF16),
        compiler_params=_cparams(2),
        name="ssm_y",
    )(u, s.reshape(rows, np_ * sw), mpair, ppair, skip)
    return y.reshape(nc, bsz * np_ * pw)


def _mix_kernel(yp_ref, ys_ref, zp_ref, zs_ref, wglu_ref, bglu_ref, gssm_ref,
                lng_ref, lnb_ref, ws_ref, bs_ref, gsgu_ref, o_ref, ya_scr, *, n_prompt_tiles):
    i = pl.program_id(0)
    tm = o_ref.shape[0]
    n_blk = ya_scr.shape[0]
    d_ssm = n_blk * LANES
    nc = tm // SSM_CHUNK

    def body(y_ref, zuv_ref):
        masks = _atom_masks(nc)
        for b in range(n_blk):
            for v in range(SSM_CHUNK // ATOMS):
                src = [y_ref[:, (ATOMS * b + pi) * PAIR_W + v * LANES:
                             (ATOMS * b + pi) * PAIR_W + (v + 1) * LANES].astype(F32)
                       for pi in range(ATOMS)]
                dst = _atom_transpose(src, masks)
                for jj in range(ATOMS):
                    ya_scr[b, pl.ds(ATOMS * v + jj, nc, stride=SSM_CHUNK), :] = dst[jj]
        y = jnp.concatenate([ya_scr[b] for b in range(n_blk)], axis=1)
        gl = _gelu(y)
        gate = jnp.dot(gl.astype(BF16), wglu_ref[...], preferred_element_type=F32) + bglu_ref[...]
        o_ref[:, :d_ssm] = _rms(gl * jax.nn.sigmoid(gate), gssm_ref[...]).astype(BF16)
        d_sgu = zuv_ref.shape[1] // 2
        u = _gelu(zuv_ref[:, :d_sgu].astype(F32))
        gv = _gelu(zuv_ref[:, d_sgu:].astype(F32))
        xc = gv - jnp.mean(gv, axis=-1, keepdims=True)
        v = xc * lax.rsqrt(jnp.mean(xc * xc, axis=-1, keepdims=True) + LN_EPS)
        v = (v * lng_ref[...] + lnb_ref[...]).astype(BF16)
        lo = lax.broadcasted_iota(I32, (CHUNK, LANES), 1) < (LANES // 2)
        zero = jnp.zeros((CHUNK, LANES), BF16)
        rows = []
        for c in range(tm // CHUNK):
            cols = []
            for j in range(d_sgu // LANES):
                vp = v[c * CHUNK:(c + 1) * CHUNK, j * LANES:(j + 1) * LANES]
                rhs = jnp.concatenate([jnp.where(lo, vp, zero), jnp.where(lo, zero, vp)], axis=0)
                cols.append(jnp.dot(ws_ref[j], rhs, preferred_element_type=F32))
            rows.append(jnp.concatenate(cols, axis=1) + bs_ref[...])
        s = jnp.concatenate(rows, axis=0)
        o_ref[:, d_ssm:] = _rms(u * s, gsgu_ref[...]).astype(BF16)

    @pl.when(i < n_prompt_tiles)
    def _():
        body(yp_ref, zp_ref)

    @pl.when(i >= n_prompt_tiles)
    def _():
        body(ys_ref, zs_ref)


def _mix(y_p, y_s, zuv_p, zuv_s, lp, ls, wglu, bglu, gssm, lng, lnb, ws_pairs, bias_s, gsgu):
    tp, ts = zuv_p.shape[0], zuv_s.shape[0]
    d_sgu = zuv_p.shape[1] // 2
    d_ssm = wglu.shape[0]
    tm = TOKEN_TILE
    npt = tp // tm
    nc = tm // SSM_CHUNK
    uw = d_ssm * SSM_CHUNK
    pp, ps = lp // tm, ls // tm
    ip = lambda i: jnp.minimum(i, npt - 1)
    isamp = lambda i: jnp.maximum(i - npt, 0)
    row = lambda w: pl.BlockSpec((1, w), lambda i: (0, 0))
    return pl.pallas_call(
        functools.partial(_mix_kernel, n_prompt_tiles=npt),
        grid=((tp + ts) // tm,),
        in_specs=[pl.BlockSpec((nc, uw), lambda i: (ip(i) % pp, ip(i) // pp)),
                  pl.BlockSpec((nc, uw), lambda i: (isamp(i) % ps, isamp(i) // ps)),
                  pl.BlockSpec((tm, 2 * d_sgu), lambda i: (ip(i), 0)),
                  pl.BlockSpec((tm, 2 * d_sgu), lambda i: (isamp(i), 0)),
                  pl.BlockSpec((d_ssm, d_ssm), lambda i: (0, 0)),
                  row(d_ssm), row(d_ssm), row(d_sgu), row(d_sgu),
                  pl.BlockSpec(ws_pairs.shape, lambda i: (0, 0, 0)),
                  pl.BlockSpec(bias_s.shape, lambda i: (0, 0)),
                  row(d_sgu)],
        out_specs=pl.BlockSpec((tm, d_ssm + d_sgu), lambda i: (i, 0)),
        out_shape=jax.ShapeDtypeStruct((tp + ts, d_ssm + d_sgu), BF16),
        scratch_shapes=[pltpu.VMEM((d_ssm // LANES, tm, LANES), F32)],
        compiler_params=_cparams(),
        name="mix",
    )(y_p, y_s, zuv_p, zuv_s, wglu, bglu, gssm, lng, lnb, ws_pairs, bias_s, gsgu)


def _route_kernel(xp_ref, xs_ref, mix_ref, wout_ref, gffn_ref, wr_ref, br_ref, tri_ref,
                  x1_ref, h2_ref, e_ref, rank_ref, wtok_ref, cnt_ref, *, n_prompt_tiles, n_exp):
    i = pl.program_id(0)
    tm = mix_ref.shape[0]

    @pl.when(i == 0)
    def _():
        cnt_ref[...] = jnp.zeros_like(cnt_ref)

    def body(x_ref):
        x1 = x_ref[...] + jnp.dot(mix_ref[...], wout_ref[...], preferred_element_type=F32)
        x1_ref[...] = x1
        h2 = _rms(x1, gffn_ref[...])
        h2_ref[...] = h2
        lt = jnp.dot(h2, wr_ref[...], precision=lax.Precision.HIGHEST,
                     preferred_element_type=F32) + br_ref[...]
        logits = lt.T[:n_exp]
        eio = lax.broadcasted_iota(I32, (n_exp, tm), 0)
        vals, idxs = [], []
        l = logits
        for _ in range(TOP_K):
            m = jnp.max(l, axis=0, keepdims=True)
            idx = jnp.min(jnp.where(l == m, eio, n_exp), axis=0, keepdims=True)
            vals.append(m)
            idxs.append(idx)
            l = jnp.where(eio == idx, -jnp.inf, l)
        ex = [jnp.exp(v - vals[0]) for v in vals]
        den = ex[0] + ex[1] + ex[2] + ex[3]
        ws = [e / den for e in ex]
        hot = [eio == idx for idx in idxs]
        cnt = sum(h.astype(F32) for h in hot)
        prefix = jnp.dot(cnt.astype(BF16), tri_ref[...], preferred_element_type=F32)
        tot = prefix + cnt_ref[:, 0:1]
        ranks = [jnp.sum(jnp.where(h, tot, 0.0), axis=0, keepdims=True) for h in hot]
        e_ref[...] = jnp.concatenate(idxs, axis=0)
        rank_ref[...] = jnp.concatenate(ranks, axis=0).astype(I32)
        wpad = jnp.concatenate(ws + [jnp.zeros((LANES - TOP_K, tm), F32)], axis=0)
        wtok_ref[...] = wpad.T
        cnt_ref[...] = cnt_ref[...] + jnp.sum(cnt, axis=1, keepdims=True)

    @pl.when(i < n_prompt_tiles)
    def _():
        body(xp_ref)

    @pl.when(i >= n_prompt_tiles)
    def _():
        body(xs_ref)


def _dual_specs(tile, width, n_prompt_tiles):
    last = n_prompt_tiles - 1
    sp = pl.BlockSpec((tile, width), lambda i: (jnp.minimum(i, last), 0))
    ss = pl.BlockSpec((tile, width), lambda i: (jnp.maximum(i - n_prompt_tiles, 0), 0))
    return sp, ss


def _route(xp, xs, mixed, w_out, g_ffn, wr_pad, br_pad, n_exp):
    tp, d = xp.shape
    t = mixed.shape[0]
    tm = ROUTE_TILE
    npt = tp // tm
    nt = t // tm
    tri = (lax.broadcasted_iota(I32, (tm, tm), 0) < lax.broadcasted_iota(I32, (tm, tm), 1)).astype(BF16)
    sp, ss = _dual_specs(tm, d, npt)
    const = lambda shape: pl.BlockSpec(shape, lambda i: (0,) * len(shape))
    return pl.pallas_call(
        functools.partial(_route_kernel, n_prompt_tiles=npt, n_exp=n_exp),
        grid=(nt,),
        in_specs=[sp, ss,
                  pl.BlockSpec((tm, mixed.shape[1]), lambda i: (i, 0)),
                  const(w_out.shape), const((1, d)), const(wr_pad.shape), const((1, LANES)),
                  const((tm, tm))],
        out_specs=[pl.BlockSpec((tm, d), lambda i: (i, 0)),
                   pl.BlockSpec((tm, d), lambda i: (i, 0)),
                   pl.BlockSpec((None, TOP_K, tm), lambda i: (i, 0, 0)),
                   pl.BlockSpec((None, TOP_K, tm), lambda i: (i, 0, 0)),
                   pl.BlockSpec((tm, LANES), lambda i: (i, 0)),
                   const((n_exp, LANES))],
        out_shape=[jax.ShapeDtypeStruct((t, d), F32),
                   jax.ShapeDtypeStruct((t, d), F32),
                   jax.ShapeDtypeStruct((nt, TOP_K, tm), I32),
                   jax.ShapeDtypeStruct((nt, TOP_K, tm), I32),
                   jax.ShapeDtypeStruct((t, LANES), F32),
                   jax.ShapeDtypeStruct((n_exp, LANES), F32)],
        compiler_params=_cparams(),
        name="route",
    )(xp, xs, mixed, w_out, g_ffn, wr_pad, br_pad, tri)


def _dest_kernel(starts_ref, e_ref, rank_ref, d_ref):
    e = e_ref[...]
    d = rank_ref[...]
    for ex in range(starts_ref.shape[0]):
        d = d + jnp.where(e == ex, starts_ref[ex], 0)
    d_ref[...] = d


def _dest(starts, e_t, rank_t):
    nt, k, tm = e_t.shape
    blk = pl.BlockSpec((None, k, tm), lambda i, s: (i, 0, 0))
    return pl.pallas_call(
        _dest_kernel,
        grid_spec=pltpu.PrefetchScalarGridSpec(
            num_scalar_prefetch=1, grid=(nt,), in_specs=[blk, blk], out_specs=blk),
        out_shape=jax.ShapeDtypeStruct((nt, k, tm), I32),
        compiler_params=_cparams(),
        name="dest",
    )(starts, e_t, rank_t)


def _dispatch_kernel(zstart_ref, dest_ref, h_ref, xs_ref, zero_ref, sem, zsem):
    i = pl.program_id(0)
    tm = h_ref.shape[0]
    n_exp = zstart_ref.shape[0]
    pad_rows = zero_ref.shape[0]

    def zero_copy(ex):
        z0 = pl.multiple_of(zstart_ref[ex], SUBLANES)
        return pltpu.make_async_copy(zero_ref, xs_ref.at[pl.ds(z0, pad_rows), :], zsem)

    @pl.when(i == 0)
    def _():
        zero_ref[...] = jnp.zeros_like(zero_ref)
        for ex in range(n_exp):
            zero_copy(ex).start()
        for ex in range(n_exp):
            zero_copy(ex).wait()

    def row_copy(t, k):
        return pltpu.make_async_copy(h_ref.at[pl.ds(t, 1), :],
                                     xs_ref.at[pl.ds(dest_ref[k, t], 1), :], sem)

    def issue(t, c):
        for k in range(TOP_K):
            row_copy(t, k).start()
        return c

    def drain(t, c):
        for k in range(TOP_K):
            row_copy(t, k).wait()
        return c

    lax.fori_loop(0, tm, issue, 0)
    lax.fori_loop(0, tm, drain, 0)


def _dispatch(zstart, dest, h2, n_rows):
    t, d = h2.shape
    nt, k, tm = dest.shape
    return pl.pallas_call(
        _dispatch_kernel,
        grid_spec=pltpu.PrefetchScalarGridSpec(
            num_scalar_prefetch=1, grid=(nt,),
            in_specs=[pl.BlockSpec((None, k, tm), lambda i, s: (i, 0, 0), memory_space=pltpu.SMEM),
                      pl.BlockSpec((tm, d), lambda i, s: (i, 0))],
            out_specs=pl.BlockSpec(memory_space=pl.ANY),
            scratch_shapes=[pltpu.VMEM((MOE_TILE + SUBLANES, d), h2.dtype),
                            pltpu.SemaphoreType.DMA, pltpu.SemaphoreType.DMA]),
        out_shape=jax.ShapeDtypeStruct((n_rows, d), h2.dtype),
        compiler_params=_cparams(),
        name="dispatch",
    )(zstart, dest, h2)


def _moe_kernel(be_ref, nu_ref, x_ref, wgu_ref, bgu_ref, wd_ref, bd_ref, o_ref):
    j = pl.program_id(0)
    f = wd_ref.shape[0]

    @pl.when(j < nu_ref[0])
    def _():
        gu = jnp.dot(x_ref[...].astype(BF16), wgu_ref[...], preferred_element_type=F32) + bgu_ref[...]
        gate = jnp.minimum(gu[:, :f], SWIGLU_LIMIT)
        up = jnp.clip(gu[:, f:], -SWIGLU_LIMIT, SWIGLU_LIMIT)
        act = gate * jax.nn.sigmoid(SWIGLU_ALPHA * gate) * (up + 1.0)
        o_ref[...] = jnp.dot(act.astype(BF16), wd_ref[...], preferred_element_type=F32) + bd_ref[...]


def _moe(block_e, n_used, xs, wgu, bgu, wd, bd, n_blocks):
    d = xs.shape[1]
    f2 = wgu.shape[2]
    f = wd.shape[1]
    tm = MOE_TILE
    slot = lambda j, be, nu: (jnp.minimum(j, nu[0] - 1), 0)
    return pl.pallas_call(
        _moe_kernel,
        grid_spec=pltpu.PrefetchScalarGridSpec(
            num_scalar_prefetch=2, grid=(n_blocks,),
            in_specs=[pl.BlockSpec((tm, d), slot),
                      pl.BlockSpec((None, d, f2), lambda j, be, nu: (be[j], 0, 0)),
                      pl.BlockSpec((None, 1, f2), lambda j, be, nu: (be[j], 0, 0)),
                      pl.BlockSpec((None, f, d), lambda j, be, nu: (be[j], 0, 0)),
                      pl.BlockSpec((None, 1, d), lambda j, be, nu: (be[j], 0, 0))],
            out_specs=pl.BlockSpec((tm, d), slot)),
        out_shape=jax.ShapeDtypeStruct((n_blocks * tm, d), F32),
        compiler_params=_cparams(),
        name="moe",
    )(block_e, n_used, xs, wgu, bgu, wd, bd)


def _combine_kernel(dest_ref, wtok_ref, x1_ref, g_ref, eo_ref, y_ref, buf_ref, sem):
    tm = x1_ref.shape[0]

    def row_copy(t, k):
        return pltpu.make_async_copy(eo_ref.at[pl.ds(dest_ref[k, t], 1), :],
                                     buf_ref.at[k, pl.ds(t, 1), :], sem)

    def issue(t, c):
        for k in range(TOP_K):
            row_copy(t, k).start()
        return c

    def drain(t, c):
        for k in range(TOP_K):
            row_copy(t, k).wait()
        return c

    lax.fori_loop(0, tm, issue, 0)
    lax.fori_loop(0, tm, drain, 0)
    acc = x1_ref[...]
    for k in range(TOP_K):
        acc = acc + wtok_ref[:, k:k + 1] * buf_ref[k]
    y_ref[...] = _rms(acc, g_ref[...])


def _combine(dest, wtok, x1, g_final, eo, tile_off, n_tokens):
    d = x1.shape[1]
    _, k, tm = dest.shape
    return pl.pallas_call(
        _combine_kernel,
        grid=(n_tokens // tm,),
        in_specs=[pl.BlockSpec((None, k, tm), lambda i: (i + tile_off, 0, 0), memory_space=pltpu.SMEM),
                  pl.BlockSpec((tm, LANES), lambda i: (i + tile_off, 0)),
                  pl.BlockSpec((tm, d), lambda i: (i + tile_off, 0)),
                  pl.BlockSpec((1, d), lambda i: (0, 0)),
                  pl.BlockSpec(memory_space=pl.ANY)],
        out_specs=pl.BlockSpec((tm, d), lambda i: (i, 0)),
        out_shape=jax.ShapeDtypeStruct((n_tokens, d), F32),
        scratch_shapes=[pltpu.VMEM((k, tm, d), F32), pltpu.SemaphoreType.DMA],
        compiler_params=_cparams(),
        name="combine",
    )(dest, wtok, x1, g_final, eo)


def kernel(x_prompt, x_sample, g_mix, w_in, ssm_lam_re, ssm_lam_im, ssm_log_dt, ssm_b_re, ssm_b_im,
           ssm_c_re, ssm_c_im, ssm_d, w_glu, b_glu, sgu_ln_g, sgu_ln_b, sgu_w_s, sgu_b_s,
           g_out_ssm, g_out_sgu, w_out, g_ffn, w_router, b_router, w_gate_up, b_gate_up,
           w_down, b_down, g_final):
    assert g_mix.shape[0] == 1, "single-layer trunk"
    bp, lp, d = x_prompt.shape
    bs, ls, _ = x_sample.shape
    tp, ts = bp * lp, bs * ls
    t = tp + ts
    d_ssm = ssm_d.shape[1]
    d_sgu = sgu_ln_g.shape[1]
    n_exp = w_router.shape[2]
    assert lp % TOKEN_TILE == 0 and ls % TOKEN_TILE == 0 and TOKEN_TILE % CHUNK == 0
    assert SSM_CHUNK * SSM_GROUP == MXU_DIM and 2 * ssm_lam_re.shape[-1] == LANES
    assert d_sgu // SGU_HEADS == LANES // 2 and n_exp <= LANES

    xp = x_prompt.reshape(tp, d)
    xs = x_sample.reshape(ts, d)
    row = lambda a: a.reshape(1, -1).astype(F32)

    w_in_b = w_in[0].astype(BF16)
    u_p, zuv_p = _inproj(xp, row(g_mix[0]), w_in_b, d_ssm, bp, lp)
    u_s, zuv_s = _inproj(xs, row(g_mix[0]), w_in_b, d_ssm, bs, ls)

    mats = _ssm_matrices(ssm_lam_re[0], ssm_lam_im[0], ssm_log_dt[0], ssm_b_re[0], ssm_b_im[0],
                         ssm_c_re[0], ssm_c_im[0], ssm_d[0])
    y_p = _ssm_trunk(u_p, bp, mats)
    y_s = _ssm_trunk(u_s, bs, mats)

    ws = sgu_w_s[0]
    ws_pairs = jnp.concatenate([ws[0::2], ws[1::2]], axis=2).astype(BF16)
    bias_s = jnp.repeat(sgu_b_s[0].T, d_sgu // SGU_HEADS, axis=1).astype(F32)
    mixed = _mix(y_p, y_s, zuv_p, zuv_s, lp, ls, w_glu[0].astype(BF16), row(b_glu[0]), row(g_out_ssm[0]),
                 row(sgu_ln_g[0]), row(sgu_ln_b[0]), ws_pairs, bias_s, row(g_out_sgu[0]))

    wr_pad = jnp.pad(w_router[0].astype(F32), ((0, 0), (0, LANES - n_exp)))
    br_pad = jnp.pad(b_router[0].astype(F32), (0, LANES - n_exp)).reshape(1, LANES)
    x1, h2, e_t, rank_t, wtok, cnt = _route(xp, xs, mixed, w_out[0].astype(BF16), row(g_ffn[0]),
                                            wr_pad, br_pad, n_exp)

    counts = cnt[:, 0].astype(I32)
    padded = (counts + MOE_TILE - 1) // MOE_TILE * MOE_TILE
    ends = jnp.cumsum(padded)
    starts = ends - padded
    n_blocks = t * TOP_K // MOE_TILE + n_exp
    n_used = (ends[-1] // MOE_TILE).astype(I32)
    blk = jnp.arange(n_blocks, dtype=I32)
    be = jnp.minimum(jnp.searchsorted(ends, blk * MOE_TILE, side='right'), n_exp - 1).astype(I32)
    block_e = jnp.where(blk < n_used, be, be[jnp.maximum(n_used - 1, 0)])

    dest = _dest(starts.astype(I32), e_t, rank_t)
    zstart = starts + counts // SUBLANES * SUBLANES
    xslots = _dispatch(zstart.astype(I32), dest, h2, (n_blocks + 1) * MOE_TILE + SUBLANES)
    eo = _moe(block_e, n_used.reshape(1), xslots, w_gate_up[0].astype(BF16),
              b_gate_up[0][:, None, :].astype(F32), w_down[0].astype(BF16),
              b_down[0][:, None, :].astype(F32), n_blocks)

    gf = row(g_final)
    y_prompt = _combine(dest, wtok, x1, gf, eo, 0, tp)
    y_sample = _combine(dest, wtok, x1, gf, eo, tp // ROUTE_TILE, ts)
    return y_prompt.reshape(bp, lp, d), y_sample.reshape(bs, ls, d)
```

```python
import functools
import math

import jax
import jax.numpy as jnp
from jax import lax
from jax.experimental import pallas as pl
from jax.experimental.pallas import tpu as pltpu

F32 = jnp.float32
BF16 = jnp.bfloat16
I32 = jnp.int32
U32 = jnp.uint32

SSM_GROUP = 16
SGU_HEADS = 8
CHUNK = 128
TOP_K = 4
SWIGLU_LIMIT = 7.0
SWIGLU_ALPHA = 1.702
RMS_EPS = 1e-6
LN_EPS = 1e-5

LANES = 128
SUBLANES = 8
MXU_DIM = 256
SSM_CHUNK = MXU_DIM // SSM_GROUP
PAIR_W = 2 * SSM_CHUNK * SSM_GROUP
ATOM = 2 * SSM_GROUP
ATOMS = LANES // ATOM

TOKEN_TILE = 512
MOE_TILE = 256
MOE_GROUPS = MOE_TILE // SUBLANES
SORT_CHUNK = 256
SSM_ROW_TILE = 1024
VMEM_LIMIT = 56 * 1024 * 1024


def _cparams(n_axes=1, vmem=None):
    return pltpu.CompilerParams(
        dimension_semantics=("arbitrary",) * n_axes,
        vmem_limit_bytes=vmem if vmem is not None else VMEM_LIMIT,
    )


def _rms(x, g):
    return x * lax.rsqrt(jnp.mean(x * x, axis=-1, keepdims=True) + RMS_EPS) * g


def _gelu(x):
    return x * (lax.erf(x * (1.0 / math.sqrt(2.0))) + 1.0) * 0.5


def _atom_masks(rows):
    lane = lax.broadcasted_iota(I32, (rows, LANES), 1)
    return [(lane >= a * ATOM) & (lane < (a + 1) * ATOM) for a in range(ATOMS)]


def _atom_transpose(src, masks):
    dst = []
    for b in range(ATOMS):
        acc = None
        for a in range(ATOMS):
            r = (a - b) % ATOMS
            piece = src[a] if r == 0 else pltpu.roll(src[a], ATOM * r, axis=1)
            acc = piece if acc is None else jnp.where(masks[a], piece, acc)
        dst.append(acc)
    return dst


def _pack_bf16_pairs(x):
    w = x.shape[1] // 2
    lo = lax.bitcast_convert_type(x[:, :w], U32) >> 16
    hi = lax.bitcast_convert_type(x[:, w:], U32) & jnp.uint32(0xFFFF0000)
    return hi | lo


def _unpack_bf16_pairs(u):
    lo = lax.bitcast_convert_type(u << 16, F32)
    hi = lax.bitcast_convert_type(u & jnp.uint32(0xFFFF0000), F32)
    return jnp.concatenate([lo.astype(BF16), hi.astype(BF16)], axis=1)


def _inproj_kernel(x_ref, g_ref, w_ref, u_ref, zuv_ref, za_scr, *, d_ssm):
    tm = x_ref.shape[0]
    nc = tm // SSM_CHUNK
    h = _rms(x_ref[...], g_ref[...])
    z = jnp.dot(h.astype(BF16), w_ref[...], preferred_element_type=F32)
    zuv_ref[...] = z[:, d_ssm:].astype(BF16)
    n_blk = d_ssm // LANES
    for b in range(n_blk):
        za_scr[b] = z[:, b * LANES:(b + 1) * LANES]
    masks = _atom_masks(nc)
    n_quads = SSM_CHUNK // ATOMS
    for b in range(n_blk):
        for v in range(n_quads):
            src = [za_scr[b, pl.ds(ATOMS * v + jj, nc, stride=SSM_CHUNK), :] for jj in range(ATOMS)]
            dst = _atom_transpose(src, masks)
            for pi in range(ATOMS):
                c0 = (ATOMS * b + pi) * PAIR_W + v * LANES
                u_ref[:, c0:c0 + LANES] = dst[pi].astype(BF16)


def _inproj(x, g_mix, w_in, d_ssm, bsz, seqlen):
    t, d = x.shape
    tm = TOKEN_TILE
    d_in = w_in.shape[1]
    per_seq = seqlen // tm
    nc = tm // SSM_CHUNK
    uw = d_ssm * SSM_CHUNK
    return pl.pallas_call(
        functools.partial(_inproj_kernel, d_ssm=d_ssm),
        grid=(t // tm,),
        in_specs=[pl.BlockSpec((tm, d), lambda i: (i, 0)),
                  pl.BlockSpec((1, d), lambda i: (0, 0)),
                  pl.BlockSpec((d, d_in), lambda i: (0, 0))],
        out_specs=[pl.BlockSpec((nc, uw), lambda i: (i % per_seq, i // per_seq)),
                   pl.BlockSpec((tm, d_in - d_ssm), lambda i: (i, 0))],
        out_shape=[jax.ShapeDtypeStruct((seqlen // SSM_CHUNK, bsz * uw), BF16),
                   jax.ShapeDtypeStruct((t, d_in - d_ssm), BF16)],
        scratch_shapes=[pltpu.VMEM((d_ssm // LANES, tm, LANES), F32)],
        compiler_params=_cparams(),
        name="inproj",
    )(x, g_mix, w_in)


def _ssm_matrices(lam_re, lam_im, log_dt, b_re, b_im, c_re, c_im, d_skip):
    hp = lax.Precision.HIGHEST
    _, g, n = lam_re.shape
    p = b_re.shape[-1]
    lc = SSM_CHUNK
    np_ = g // 2
    dt = jnp.exp(log_dt)[..., None]
    mag = jnp.exp(lam_re * dt)
    ar = mag * jnp.cos(lam_im * dt)
    ai = mag * jnp.sin(lam_im * dt)
    den = lam_re * lam_re + lam_im * lam_im
    nr = ar - 1.0
    fr = (nr * lam_re + ai * lam_im) / den
    fi = (ai * lam_re - nr * lam_im) / den
    bbr = fr[..., None] * b_re - fi[..., None] * b_im
    bbi = fr[..., None] * b_im + fi[..., None] * b_re

    prs, pis = [jnp.ones_like(ar)], [jnp.zeros_like(ai)]
    for _ in range(lc):
        pr, pi = prs[-1], pis[-1]
        prs.append(pr * ar - pi * ai)
        pis.append(pr * ai + pi * ar)
    pw_r, pw_i = jnp.stack(prs), jnp.stack(pis)

    car = c_re[None] * pw_r[:lc, :, :, None, :] - c_im[None] * pw_i[:lc, :, :, None, :]
    cai = c_re[None] * pw_i[:lc, :, :, None, :] + c_im[None] * pw_r[:lc, :, :, None, :]
    kk = (jnp.einsum('tdgpn,dgnq->tdgpq', car, bbr, precision=hp)
          - jnp.einsum('tdgpn,dgnq->tdgpq', cai, bbi, precision=hp))
    kf, kb = kk[:, 0], kk[:, 1]
    jj = jnp.arange(lc)[:, None]
    ii = jnp.arange(lc)[None, :]
    tau = ii - jj
    mf = jnp.where((tau >= 0)[:, :, None, None, None], kf[jnp.clip(tau, 0, lc - 1)], 0.0)
    mb = jnp.where((tau <= 0)[:, :, None, None, None], kb[jnp.clip(-tau, 0, lc - 1)], 0.0)
    m = (mf + mb).reshape(lc, lc, np_, 2, p, p)
    eye = jnp.eye(2, dtype=F32)
    mpair = jnp.einsum('jinspq,st->njsqitp', m, eye).reshape(np_, lc * 2 * p, lc * 2 * p)

    ef = lc - 1 - jnp.arange(lc)
    eb = jnp.arange(lc)
    def qmat(d, e):
        wr, wi = pw_r[e, d], pw_i[e, d]
        qr = wr[..., None] * bbr[d][None] - wi[..., None] * bbi[d][None]
        qi = wr[..., None] * bbi[d][None] + wi[..., None] * bbr[d][None]
        to = lambda a: jnp.transpose(a, (0, 1, 3, 2)).reshape(lc, np_, 2, p, n)
        return to(qr), to(qi)
    q4 = jnp.stack(qmat(0, ef) + qmat(1, eb))
    qpair = jnp.einsum('bjnsqm,st->njsqbtm', q4, eye).reshape(np_, lc * 2 * p, 4 * 2 * n)

    pf_e = jnp.arange(lc) + 1
    pb_e = lc - jnp.arange(lc)
    def pmat(d, e):
        wr, wi = pw_r[e, d], pw_i[e, d]
        cr, ci = c_re[d], c_im[d]
        pr = cr[None] * wr[:, :, None, :] - ci[None] * wi[:, :, None, :]
        pi = cr[None] * wi[:, :, None, :] + ci[None] * wr[:, :, None, :]
        to = lambda a: jnp.transpose(a, (1, 3, 0, 2)).reshape(np_, 2, n, lc, p)
        return to(pr), to(-pi)
    p4 = jnp.stack(pmat(0, pf_e) + pmat(1, pb_e))
    ppair = jnp.einsum('bnsmip,st->nbsmitp', p4, eye).reshape(np_, 4 * 2 * n, lc * 2 * p)

    al = jnp.stack([pw_r[lc, 0], pw_i[lc, 0], pw_r[lc, 1], pw_i[lc, 1]])
    alpha = jnp.transpose(al.reshape(4, np_, 2 * n), (1, 0, 2))
    skip = jnp.broadcast_to(d_skip.reshape(np_, 1, 2 * p), (np_, lc, 2 * p)).reshape(np_, 1, lc * 2 * p)
    return mpair.astype(BF16), qpair.astype(BF16), ppair.astype(BF16), alpha, skip.astype(F32)


def _ssm_v_kernel(u_ref, q_ref, v_ref):
    v_ref[...] = jnp.dot(u_ref[...], q_ref[...], preferred_element_type=F32)


def _ssm_scan_kernel(v_ref, a_ref, s_ref, *, n_chunks):
    afr, afi = a_ref[0:1, :], a_ref[1:2, :]
    abr, abi = a_ref[2:3, :], a_ref[3:4, :]
    b = v_ref.shape[1]
    w = LANES
    zero = jnp.zeros((b, w), F32)

    def step(c, carry):
        fr, fi, rr, ri = carry
        cb = n_chunks - 1 - c
        s_ref[c, :, 0:w] = fr
        s_ref[c, :, w:2 * w] = fi
        s_ref[cb, :, 2 * w:3 * w] = rr
        s_ref[cb, :, 3 * w:4 * w] = ri
        vfr = v_ref[c, :, 0:w]
        vfi = v_ref[c, :, w:2 * w]
        vbr = v_ref[cb, :, 2 * w:3 * w]
        vbi = v_ref[cb, :, 3 * w:4 * w]
        return (afr * fr - afi * fi + vfr, afr * fi + afi * fr + vfi,
                abr * rr - abi * ri + vbr, abr * ri + abi * rr + vbi)

    lax.fori_loop(0, n_chunks, step, (zero, zero, zero, zero))


def _ssm_y_kernel(u_ref, s_ref, m_ref, p_ref, d_ref, y_ref):
    u = u_ref[...]
    y = jnp.dot(u, m_ref[...], preferred_element_type=F32)
    y = y + jnp.dot(s_ref[...].astype(BF16), p_ref[...], preferred_element_type=F32)
    y_ref[...] = (y + d_ref[...] * u.astype(F32)).astype(BF16)


def _ssm_trunk(u2, bsz, mats):
    mpair, qpair, ppair, alpha, skip = mats
    nc = u2.shape[0]
    np_ = mpair.shape[0]
    rows = nc * bsz
    pw = PAIR_W
    sw = qpair.shape[2]
    u = u2.reshape(rows, np_ * pw)
    tr = min(SSM_ROW_TILE, rows)
    v = pl.pallas_call(
        _ssm_v_kernel,
        grid=(np_, rows // tr),
        in_specs=[pl.BlockSpec((tr, pw), lambda p, r: (r, p)),
                  pl.BlockSpec((None, pw, sw), lambda p, r: (p, 0, 0))],
        out_specs=pl.BlockSpec((tr, sw), lambda p, r: (r, p)),
        out_shape=jax.ShapeDtypeStruct((rows, np_ * sw), F32),
        compiler_params=_cparams(2),
        name="ssm_v",
    )(u, qpair)
    s = pl.pallas_call(
        functools.partial(_ssm_scan_kernel, n_chunks=nc),
        grid=(np_,),
        in_specs=[pl.BlockSpec((nc, bsz, sw), lambda p: (0, 0, p)),
                  pl.BlockSpec((None, 4, LANES), lambda p: (p, 0, 0))],
        out_specs=pl.BlockSpec((nc, bsz, sw), lambda p: (0, 0, p)),
        out_shape=jax.ShapeDtypeStruct((nc, bsz, np_ * sw), F32),
        compiler_params=_cparams(1),
        name="ssm_scan",
    )(v.reshape(nc, bsz, np_ * sw), alpha)
    y = pl.pallas_call(
        _ssm_y_kernel,
        grid=(np_, rows // tr),
        in_specs=[pl.BlockSpec((tr, pw), lambda p, r: (r, p)),
                  pl.BlockSpec((tr, sw), lambda p, r: (r, p)),
                  pl.BlockSpec((None, pw, pw), lambda p, r: (p, 0, 0)),
                  pl.BlockSpec((None, sw, pw), lambda p, r: (p, 0, 0)),
                  pl.BlockSpec((None, 1, pw), lambda p, r: (p, 0, 0))],
        out_specs=pl.BlockSpec((tr, pw), lambda p, r: (r, p)),
        out_shape=jax.ShapeDtypeStruct((rows, np_ * pw), BF16),
        compiler_params=_cparams(2),
        name="ssm_y",
    )(u, s.reshape(rows, np_ * sw), mpair, ppair, skip)
    return y.reshape(nc, bsz * np_ * pw)


def _mix_kernel(yp_ref, ys_ref, zp_ref, zs_ref, wglu_ref, bglu_ref, gssm_ref,
                lng_ref, lnb_ref, ws_ref, bs_ref, gsgu_ref, o_ref, ya_scr, *, n_prompt_tiles):
    i = pl.program_id(0)
    tm = o_ref.shape[0]
    n_blk = ya_scr.shape[0]
    d_ssm = n_blk * LANES
    nc = tm // SSM_CHUNK

    def body(y_ref, zuv_ref):
        masks = _atom_masks(nc)
        for b in range(n_blk):
            for v in range(SSM_CHUNK // ATOMS):
                src = [y_ref[:, (ATOMS * b + pi) * PAIR_W + v * LANES:
                             (ATOMS * b + pi) * PAIR_W + (v + 1) * LANES].astype(F32)
                       for pi in range(ATOMS)]
                dst = _atom_transpose(src, masks)
                for jj in range(ATOMS):
                    ya_scr[b, pl.ds(ATOMS * v + jj, nc, stride=SSM_CHUNK), :] = dst[jj]
        y = jnp.concatenate([ya_scr[b] for b in range(n_blk)], axis=1)
        gl = _gelu(y)
        gate = jnp.dot(gl.astype(BF16), wglu_ref[...], preferred_element_type=F32) + bglu_ref[...]
        o_ref[:, :d_ssm] = _rms(gl * jax.nn.sigmoid(gate), gssm_ref[...]).astype(BF16)
        d_sgu = zuv_ref.shape[1] // 2
        u = _gelu(zuv_ref[:, :d_sgu].astype(F32))
        gv = _gelu(zuv_ref[:, d_sgu:].astype(F32))
        xc = gv - jnp.mean(gv, axis=-1, keepdims=True)
        v = xc * lax.rsqrt(jnp.mean(xc * xc, axis=-1, keepdims=True) + LN_EPS)
        v = (v * lng_ref[...] + lnb_ref[...]).astype(BF16)
        lo = lax.broadcasted_iota(I32, (CHUNK, LANES), 1) < (LANES // 2)
        zero = jnp.zeros((CHUNK, LANES), BF16)
        rows = []
        for c in range(tm // CHUNK):
            cols = []
            for j in range(d_sgu // LANES):
                vp = v[c * CHUNK:(c + 1) * CHUNK, j * LANES:(j + 1) * LANES]
                rhs = jnp.concatenate([jnp.where(lo, vp, zero), jnp.where(lo, zero, vp)], axis=0)
                cols.append(jnp.dot(ws_ref[j], rhs, preferred_element_type=F32))
            rows.append(jnp.concatenate(cols, axis=1) + bs_ref[...])
        s = jnp.concatenate(rows, axis=0)
        o_ref[:, d_ssm:] = _rms(u * s, gsgu_ref[...]).astype(BF16)

    @pl.when(i < n_prompt_tiles)
    def _():
        body(yp_ref, zp_ref)

    @pl.when(i >= n_prompt_tiles)
    def _():
        body(ys_ref, zs_ref)


def _mix(y_p, y_s, zuv_p, zuv_s, lp, ls, wglu, bglu, gssm, lng, lnb, ws_pairs, bias_s, gsgu):
    tp, ts = zuv_p.shape[0], zuv_s.shape[0]
    d_sgu = zuv_p.shape[1] // 2
    d_ssm = wglu.shape[0]
    tm = TOKEN_TILE
    npt = tp // tm
    nc = tm // SSM_CHUNK
    uw = d_ssm * SSM_CHUNK
    pp, ps = lp // tm, ls // tm
    ip = lambda i: jnp.minimum(i, npt - 1)
    isamp = lambda i: jnp.maximum(i - npt, 0)
    row = lambda w: pl.BlockSpec((1, w), lambda i: (0, 0))
    return pl.pallas_call(
        functools.partial(_mix_kernel, n_prompt_tiles=npt),
        grid=((tp + ts) // tm,),
        in_specs=[pl.BlockSpec((nc, uw), lambda i: (ip(i) % pp, ip(i) // pp)),
                  pl.BlockSpec((nc, uw), lambda i: (isamp(i) % ps, isamp(i) // ps)),
                  pl.BlockSpec((tm, 2 * d_sgu), lambda i: (ip(i), 0)),
                  pl.BlockSpec((tm, 2 * d_sgu), lambda i: (isamp(i), 0)),
                  pl.BlockSpec((d_ssm, d_ssm), lambda i: (0, 0)),
                  row(d_ssm), row(d_ssm), row(d_sgu), row(d_sgu),
                  pl.BlockSpec(ws_pairs.shape, lambda i: (0, 0, 0)),
                  pl.BlockSpec(bias_s.shape, lambda i: (0, 0)),
                  row(d_sgu)],
        out_specs=pl.BlockSpec((tm, d_ssm + d_sgu), lambda i: (i, 0)),
        out_shape=jax.ShapeDtypeStruct((tp + ts, d_ssm + d_sgu), BF16),
        scratch_shapes=[pltpu.VMEM((d_ssm // LANES, tm, LANES), F32)],
        compiler_params=_cparams(),
        name="mix",
    )(y_p, y_s, zuv_p, zuv_s, wglu, bglu, gssm, lng, lnb, ws_pairs, bias_s, gsgu)


def _sorted_rows(tile):
    return TOP_K * tile + MXU_DIM


def _route_kernel(xp_ref, xs_ref, mix_ref, wout_ref, gffn_ref, whi_ref, wlo_ref, br_ref, tri_ref,
                  ltri_ref, x1_ref, xsort_ref, meta_ref, c8_ref, *, n_prompt_tiles, n_exp):
    i = pl.program_id(0)
    tm = mix_ref.shape[0]
    p_rows = xsort_ref.shape[0]

    def body(x_ref):
        x1 = x_ref[...] + jnp.dot(mix_ref[...], wout_ref[...], preferred_element_type=F32)
        x1_ref[...] = x1
        h2 = _rms(x1, gffn_ref[...])
        hi = h2.astype(BF16)
        lo = (h2 - hi.astype(F32)).astype(BF16)
        lt = (jnp.dot(hi, whi_ref[...], preferred_element_type=F32)
              + (jnp.dot(lo, whi_ref[...], preferred_element_type=F32)
                 + jnp.dot(hi, wlo_ref[...], preferred_element_type=F32))) + br_ref[...]
        logits = lt.T[:n_exp]
        eio = lax.broadcasted_iota(I32, (n_exp, tm), 0)
        vals, idxs = [], []
        l = logits
        for _ in range(TOP_K):
            m = jnp.max(l, axis=0, keepdims=True)
            idx = jnp.min(jnp.where(l == m, eio, n_exp), axis=0, keepdims=True)
            vals.append(m)
            idxs.append(idx)
            l = jnp.where(eio == idx, -jnp.inf, l)
        ex = [jnp.exp(v - vals[0]) for v in vals]
        den = ex[0] + ex[1] + ex[2] + ex[3]
        ws = [e / den for e in ex]
        hot = [eio == idx for idx in idxs]
        cnt = sum(h.astype(F32) for h in hot)
        prefix = jnp.dot(cnt.astype(BF16), tri_ref[...], preferred_element_type=F32)
        c = jnp.sum(cnt, axis=1, keepdims=True)
        c8 = jnp.floor((c + (SUBLANES - 1)) * (1.0 / SUBLANES)) * SUBLANES
        c8b = jnp.broadcast_to(c8, (n_exp, LANES))
        c8_ref[...] = c8b
        run0 = jnp.dot(ltri_ref[...], c8b.astype(BF16), preferred_element_type=F32)[:, 0:1]
        base = run0 + prefix
        pos = [jnp.sum(jnp.where(h, base, 0.0), axis=0, keepdims=True) for h in hot]
        meta = jnp.concatenate(ws + pos + [jnp.zeros((LANES - 2 * TOP_K, tm), F32)], axis=0)
        meta_ref[...] = meta.T
        posi = [p.astype(I32) for p in pos]
        for r in range(p_rows // SORT_CHUNK):
            rio = lax.broadcasted_iota(I32, (SORT_CHUNK, tm), 0) + r * SORT_CHUNK
            sel = (rio == posi[0]) | (rio == posi[1]) | (rio == posi[2]) | (rio == posi[3])
            srt = jnp.dot(jnp.where(sel, 1.0, 0.0).astype(BF16), hi, preferred_element_type=F32)
            xsort_ref[r * SORT_CHUNK:(r + 1) * SORT_CHUNK, :] = _pack_bf16_pairs(srt)

    @pl.when(i < n_prompt_tiles)
    def _():
        body(xp_ref)

    @pl.when(i >= n_prompt_tiles)
    def _():
        body(xs_ref)


def _dual_specs(tile, width, n_prompt_tiles):
    last = n_prompt_tiles - 1
    sp = pl.BlockSpec((tile, width), lambda i: (jnp.minimum(i, last), 0))
    ss = pl.BlockSpec((tile, width), lambda i: (jnp.maximum(i - n_prompt_tiles, 0), 0))
    return sp, ss


def _route(xp, xs, mixed, w_out, g_ffn, wr_hi, wr_lo, br_pad, n_exp):
    tp, d = xp.shape
    t = mixed.shape[0]
    tm = TOKEN_TILE
    npt = tp // tm
    nt = t // tm
    p_rows = _sorted_rows(tm)
    iota = lambda n, ax: lax.broadcasted_iota(I32, (n, n), ax)
    tri = (iota(tm, 0) < iota(tm, 1)).astype(BF16)
    ltri = (iota(n_exp, 1) < iota(n_exp, 0)).astype(BF16)
    sp, ss = _dual_specs(tm, d, npt)
    const = lambda shape: pl.BlockSpec(shape, lambda i: (0,) * len(shape))
    return pl.pallas_call(
        functools.partial(_route_kernel, n_prompt_tiles=npt, n_exp=n_exp),
        grid=(nt,),
        in_specs=[sp, ss,
                  pl.BlockSpec((tm, mixed.shape[1]), lambda i: (i, 0)),
                  const(w_out.shape), const((1, d)), const(wr_hi.shape), const(wr_lo.shape),
                  const((1, LANES)), const((tm, tm)), const((n_exp, n_exp))],
        out_specs=[pl.BlockSpec((tm, d), lambda i: (i, 0)),
                   pl.BlockSpec((p_rows, d // 2), lambda i: (i, 0)),
                   pl.BlockSpec((tm, LANES), lambda i: (i, 0)),
                   pl.BlockSpec((None, n_exp, LANES), lambda i: (i, 0, 0))],
        out_shape=[jax.ShapeDtypeStruct((t, d), F32),
                   jax.ShapeDtypeStruct((nt * p_rows, d // 2), U32),
                   jax.ShapeDtypeStruct((t, LANES), F32),
                   jax.ShapeDtypeStruct((nt, n_exp, LANES), F32)],
        compiler_params=_cparams(),
        name="route",
    )(xp, xs, mixed, w_out, g_ffn, wr_hi, wr_lo, br_pad, tri, ltri)


def _slot_tables(c8, p_rows, n_blocks):
    nt, n_exp = c8.shape
    run0 = jnp.cumsum(c8, axis=1) - c8
    p_used = jnp.sum(c8, axis=1)
    seg_len = c8.T
    cum = jnp.cumsum(seg_len, axis=1)
    tot = cum[:, -1]
    padded = (tot + MOE_TILE - 1) // MOE_TILE * MOE_TILE
    ends = jnp.cumsum(padded)
    starts = ends - padded
    n_used = (ends[-1] // MOE_TILE).astype(I32)
    blk = jnp.arange(n_blocks, dtype=I32)
    be = jnp.minimum(jnp.sum(ends[None, :] <= (blk * MOE_TILE)[:, None], axis=1), n_exp - 1).astype(I32)
    block_e = jnp.where(blk < n_used, be, be[jnp.maximum(n_used - 1, 0)])
    grow = (blk[:, None] * MOE_TILE + jnp.arange(MOE_GROUPS, dtype=I32)[None, :] * SUBLANES)
    rel = grow - starts[be][:, None]
    cum_e = cum[be]
    tile = jnp.sum(cum_e[:, None, :] <= rel[:, :, None], axis=2)
    valid = (rel < tot[be][:, None]) & (blk < n_used)[:, None]
    tile_c = jnp.minimum(tile, nt - 1)
    seg0 = jnp.take_along_axis(cum_e - seg_len[be], tile_c, axis=1)
    src_row = tile_c * p_rows + jnp.take_along_axis(run0.T[be], tile_c, axis=1) + (rel - seg0)
    zero_group = (p_rows - SUBLANES) // SUBLANES
    g_in = jnp.where(valid, src_row // SUBLANES, zero_group).astype(I32)
    scratch0 = nt * p_rows // SUBLANES
    g_scr = scratch0 + (blk[:, None] % 2) * MOE_GROUPS + jnp.arange(MOE_GROUPS, dtype=I32)[None, :]
    g_out = jnp.where(valid, src_row // SUBLANES, g_scr).astype(I32)
    return block_e, n_used, g_in, g_out, p_used.astype(I32)


def _moe_kernel(be_ref, nu_ref, gin_ref, gnext_ref, gout_ref, xs_hbm, wgu_ref, bgu_ref, wd_ref, bd_ref,
                eo_hbm, xbuf, obuf, isem, osem):
    j = pl.program_id(0)
    nu = nu_ref[0]
    f = wd_ref.shape[0]
    slot = lax.rem(j, 2)
    other = 1 - slot

    def in_copy(tbl_ref, r, sl):
        row = pl.multiple_of(tbl_ref[0, r] * SUBLANES, SUBLANES)
        return pltpu.make_async_copy(xs_hbm.at[pl.ds(row, SUBLANES), :],
                                     xbuf.at[sl, pl.ds(r * SUBLANES, SUBLANES), :], isem.at[sl])

    def out_copy(r, sl):
        row = pl.multiple_of(gout_ref[0, r] * SUBLANES, SUBLANES)
        return pltpu.make_async_copy(obuf.at[sl, pl.ds(r * SUBLANES, SUBLANES), :],
                                     eo_hbm.at[pl.ds(row, SUBLANES), :], osem.at[sl])

    def wait_in(sl):
        pltpu.make_async_copy(xs_hbm.at[pl.ds(0, MOE_TILE), :], xbuf.at[sl], isem.at[sl]).wait()

    def wait_out(sl):
        pltpu.make_async_copy(obuf.at[sl], eo_hbm.at[pl.ds(0, MOE_TILE), :], osem.at[sl]).wait()

    @pl.when(j == 0)
    def _():
        for r in range(MOE_GROUPS):
            in_copy(gin_ref, r, 0).start()

    @pl.when(j + 1 < nu)
    def _():
        for r in range(MOE_GROUPS):
            in_copy(gnext_ref, r, other).start()

    @pl.when(j < nu)
    def _():
        wait_in(slot)

        @pl.when(j >= 2)
        def _():
            wait_out(slot)

        x = _unpack_bf16_pairs(xbuf[slot])
        gu = jnp.dot(x, wgu_ref[...], preferred_element_type=F32) + bgu_ref[...]
        gate = jnp.minimum(gu[:, :f], SWIGLU_LIMIT)
        up = jnp.clip(gu[:, f:], -SWIGLU_LIMIT, SWIGLU_LIMIT)
        act = gate * jax.nn.sigmoid(SWIGLU_ALPHA * gate) * (up + 1.0)
        out = jnp.dot(act.astype(BF16), wd_ref[...], preferred_element_type=F32) + bd_ref[...]
        obuf[slot] = _pack_bf16_pairs(out.astype(BF16).astype(F32))
        for r in range(MOE_GROUPS):
            out_copy(r, slot).start()

        @pl.when(j == nu - 1)
        def _():
            wait_out(slot)

            @pl.when(j >= 1)
            def _():
                wait_out(other)


def _moe(block_e, n_used, g_in, g_out, xsorted, wgu, bgu, wd, bd, n_out_rows):
    n_blocks = block_e.shape[0]
    dh = xsorted.shape[1]
    d = 2 * dh
    f2 = wgu.shape[2]
    f = wd.shape[1]
    g_in3 = g_in.reshape(n_blocks, 1, MOE_GROUPS)
    g_next3 = jnp.concatenate([g_in3[1:], g_in3[:1]], axis=0)
    g_out3 = g_out.reshape(n_blocks, 1, MOE_GROUPS)
    tbl = pl.BlockSpec((None, 1, MOE_GROUPS), lambda j, be, nu: (j, 0, 0), memory_space=pltpu.SMEM)
    return pl.pallas_call(
        _moe_kernel,
        grid_spec=pltpu.PrefetchScalarGridSpec(
            num_scalar_prefetch=2, grid=(n_blocks,),
            in_specs=[tbl, tbl, tbl,
                      pl.BlockSpec(memory_space=pl.ANY),
                      pl.BlockSpec((None, d, f2), lambda j, be, nu: (be[j], 0, 0)),
                      pl.BlockSpec((None, 1, f2), lambda j, be, nu: (be[j], 0, 0)),
                      pl.BlockSpec((None, f, d), lambda j, be, nu: (be[j], 0, 0)),
                      pl.BlockSpec((None, 1, d), lambda j, be, nu: (be[j], 0, 0))],
            out_specs=pl.BlockSpec(memory_space=pl.ANY),
            scratch_shapes=[pltpu.VMEM((2, MOE_TILE, dh), U32), pltpu.VMEM((2, MOE_TILE, dh), U32),
                            pltpu.SemaphoreType.DMA((2,)), pltpu.SemaphoreType.DMA((2,))]),
        out_shape=jax.ShapeDtypeStruct((n_out_rows, dh), U32),
        compiler_params=_cparams(),
        name="moe",
    )(block_e, n_used, g_in3, g_next3, g_out3, xsorted, wgu, bgu, wd, bd)


def _combine_kernel(pu_ref, meta_ref, x1_ref, g_ref, eo_ref, y_ref, *, tile_off):
    i = pl.program_id(0) + tile_off
    tm = x1_ref.shape[0]
    p_rows = eo_ref.shape[0]
    used = pu_ref[i]
    lane = lax.broadcasted_iota(I32, (tm, SORT_CHUNK), 1)
    ws = [meta_ref[:, k:k + 1] for k in range(TOP_K)]
    pos = [meta_ref[:, TOP_K + k:TOP_K + k + 1].astype(I32) for k in range(TOP_K)]
    acc = x1_ref[...]
    for r in range(p_rows // SORT_CHUNK):
        rows = lax.broadcasted_iota(I32, (SORT_CHUNK, eo_ref.shape[1]), 0) + r * SORT_CHUNK
        eu = eo_ref[r * SORT_CHUNK:(r + 1) * SORT_CHUNK, :]
        buf = _unpack_bf16_pairs(jnp.where(rows < used, eu, jnp.uint32(0)))
        wm = jnp.zeros((tm, SORT_CHUNK), F32)
        for k in range(TOP_K):
            wm = jnp.where(lane + r * SORT_CHUNK == pos[k], ws[k], wm)
        acc = acc + jnp.dot(wm.astype(BF16), buf, preferred_element_type=F32)
    y_ref[...] = _rms(acc, g_ref[...])


def _combine(p_used, meta, x1, g_final, eo, tile_off, n_tokens):
    d = x1.shape[1]
    tm = TOKEN_TILE
    p_rows = _sorted_rows(tm)
    return pl.pallas_call(
        functools.partial(_combine_kernel, tile_off=tile_off),
        grid_spec=pltpu.PrefetchScalarGridSpec(
            num_scalar_prefetch=1, grid=(n_tokens // tm,),
            in_specs=[pl.BlockSpec((tm, LANES), lambda i, pu: (i + tile_off, 0)),
                      pl.BlockSpec((tm, d), lambda i, pu: (i + tile_off, 0)),
                      pl.BlockSpec((1, d), lambda i, pu: (0, 0)),
                      pl.BlockSpec((p_rows, d // 2), lambda i, pu: (i + tile_off, 0))],
            out_specs=pl.BlockSpec((tm, d), lambda i, pu: (i, 0))),
        out_shape=jax.ShapeDtypeStruct((n_tokens, d), F32),
        compiler_params=_cparams(),
        name="combine",
    )(p_used, meta, x1, g_final, eo)


def kernel(x_prompt, x_sample, g_mix, w_in, ssm_lam_re, ssm_lam_im, ssm_log_dt, ssm_b_re, ssm_b_im,
           ssm_c_re, ssm_c_im, ssm_d, w_glu, b_glu, sgu_ln_g, sgu_ln_b, sgu_w_s, sgu_b_s,
           g_out_ssm, g_out_sgu, w_out, g_ffn, w_router, b_router, w_gate_up, b_gate_up,
           w_down, b_down, g_final):
    assert g_mix.shape[0] == 1, "single-layer trunk"
    bp, lp, d = x_prompt.shape
    bs, ls, _ = x_sample.shape
    tp, ts = bp * lp, bs * ls
    t = tp + ts
    d_ssm = ssm_d.shape[1]
    d_sgu = sgu_ln_g.shape[1]
    n_exp = w_router.shape[2]
    assert lp % TOKEN_TILE == 0 and ls % TOKEN_TILE == 0 and TOKEN_TILE % CHUNK == 0
    assert SSM_CHUNK * SSM_GROUP == MXU_DIM and 2 * ssm_lam_re.shape[-1] == LANES
    assert d_sgu // SGU_HEADS == LANES // 2 and n_exp <= LANES
    assert n_exp * (SUBLANES - 1) + SUBLANES <= MXU_DIM and _sorted_rows(TOKEN_TILE) % SORT_CHUNK == 0

    xp = x_prompt.reshape(tp, d)
    xs = x_sample.reshape(ts, d)
    row = lambda a: a.reshape(1, -1).astype(F32)

    w_in_b = w_in[0].astype(BF16)
    u_p, zuv_p = _inproj(xp, row(g_mix[0]), w_in_b, d_ssm, bp, lp)
    u_s, zuv_s = _inproj(xs, row(g_mix[0]), w_in_b, d_ssm, bs, ls)

    mats = _ssm_matrices(ssm_lam_re[0], ssm_lam_im[0], ssm_log_dt[0], ssm_b_re[0], ssm_b_im[0],
                         ssm_c_re[0], ssm_c_im[0], ssm_d[0])
    y_p = _ssm_trunk(u_p, bp, mats)
    y_s = _ssm_trunk(u_s, bs, mats)

    ws = sgu_w_s[0]
    ws_pairs = jnp.concatenate([ws[0::2], ws[1::2]], axis=2).astype(BF16)
    bias_s = jnp.repeat(sgu_b_s[0].T, d_sgu // SGU_HEADS, axis=1).astype(F32)
    mixed = _mix(y_p, y_s, zuv_p, zuv_s, lp, ls, w_glu[0].astype(BF16), row(b_glu[0]), row(g_out_ssm[0]),
                 row(sgu_ln_g[0]), row(sgu_ln_b[0]), ws_pairs, bias_s, row(g_out_sgu[0]))

    wr_pad = jnp.pad(w_router[0].astype(F32), ((0, 0), (0, LANES - n_exp)))
    wr_hi = wr_pad.astype(BF16)
    wr_lo = (wr_pad - wr_hi.astype(F32)).astype(BF16)
    br_pad = jnp.pad(b_router[0].astype(F32), (0, LANES - n_exp)).reshape(1, LANES)
    x1, xsorted, meta, c8 = _route(xp, xs, mixed, w_out[0].astype(BF16), row(g_ffn[0]),
                                   wr_hi, wr_lo, br_pad, n_exp)

    nt = t // TOKEN_TILE
    p_rows = _sorted_rows(TOKEN_TILE)
    max_rows = t * TOP_K + nt * n_exp * (SUBLANES - 1) + n_exp * (MOE_TILE - 1)
    n_blocks = -(-max_rows // MOE_TILE)
    block_e, n_used, g_in, g_out, p_used = _slot_tables(c8[:, :, 0].astype(I32), p_rows, n_blocks)
    eo = _moe(block_e, n_used.reshape(1), g_in, g_out, xsorted, w_gate_up[0].astype(BF16),
              b_gate_up[0][:, None, :].astype(F32), w_down[0].astype(BF16),
              b_down[0][:, None, :].astype(F32), nt * p_rows + 2 * MOE_TILE)

    gf = row(g_final)
    y_prompt = _combine(p_used, meta, x1, gf, eo, 0, tp)
    y_sample = _combine(p_used, meta, x1, gf, eo, tp // TOKEN_TILE, ts)
    return y_prompt.reshape(bp, lp, d), y_sample.reshape(bs, ls, d)
```

```python
import functools
import math

import jax
import jax.numpy as jnp
from jax import lax
from jax.experimental import pallas as pl
from jax.experimental.pallas import tpu as pltpu

F32 = jnp.float32
BF16 = jnp.bfloat16
I32 = jnp.int32
U32 = jnp.uint32

SSM_GROUP = 16
SGU_HEADS = 8
CHUNK = 128
TOP_K = 4
SWIGLU_LIMIT = 7.0
SWIGLU_ALPHA = 1.702
RMS_EPS = 1e-6
LN_EPS = 1e-5

LANES = 128
SUBLANES = 8
MXU_DIM = 256
SSM_CHUNK = MXU_DIM // SSM_GROUP
PAIR_W = 2 * SSM_CHUNK * SSM_GROUP
ATOM = 2 * SSM_GROUP
ATOMS = LANES // ATOM

TOKEN_TILE = 512
MOE_TILE = 512
MOE_GROUPS = MOE_TILE // SUBLANES
SORT_CHUNK = 256
SSM_ROW_TILE = 1024
VMEM_LIMIT = 56 * 1024 * 1024


def _cparams(n_axes=1, vmem=None):
    return pltpu.CompilerParams(
        dimension_semantics=("arbitrary",) * n_axes,
        vmem_limit_bytes=vmem if vmem is not None else VMEM_LIMIT,
    )


def _rms(x, g):
    return x * lax.rsqrt(jnp.mean(x * x, axis=-1, keepdims=True) + RMS_EPS) * g


def _gelu(x):
    return x * (lax.erf(x * (1.0 / math.sqrt(2.0))) + 1.0) * 0.5


def _atom_masks(rows):
    lane = lax.broadcasted_iota(I32, (rows, LANES), 1)
    return [(lane >= a * ATOM) & (lane < (a + 1) * ATOM) for a in range(ATOMS)]


def _atom_transpose(src, masks):
    dst = []
    for b in range(ATOMS):
        acc = None
        for a in range(ATOMS):
            r = (a - b) % ATOMS
            piece = src[a] if r == 0 else pltpu.roll(src[a], ATOM * r, axis=1)
            acc = piece if acc is None else jnp.where(masks[a], piece, acc)
        dst.append(acc)
    return dst


def _pack_bf16_pairs(x):
    w = x.shape[1] // 2
    lo = lax.bitcast_convert_type(x[:, :w], U32) >> 16
    hi = lax.bitcast_convert_type(x[:, w:], U32) & jnp.uint32(0xFFFF0000)
    return hi | lo


def _unpack_bf16_pairs(u):
    lo = lax.bitcast_convert_type(u << 16, F32)
    hi = lax.bitcast_convert_type(u & jnp.uint32(0xFFFF0000), F32)
    return jnp.concatenate([lo.astype(BF16), hi.astype(BF16)], axis=1)


def _inproj_kernel(x_ref, g_ref, w_ref, u_ref, zuv_ref, za_scr, *, d_ssm):
    tm = x_ref.shape[0]
    nc = tm // SSM_CHUNK
    h = _rms(x_ref[...], g_ref[...])
    z = jnp.dot(h.astype(BF16), w_ref[...], preferred_element_type=F32)
    zuv_ref[...] = z[:, d_ssm:].astype(BF16)
    n_blk = d_ssm // LANES
    for b in range(n_blk):
        za_scr[b] = z[:, b * LANES:(b + 1) * LANES]
    masks = _atom_masks(nc)
    n_quads = SSM_CHUNK // ATOMS
    for b in range(n_blk):
        for v in range(n_quads):
            src = [za_scr[b, pl.ds(ATOMS * v + jj, nc, stride=SSM_CHUNK), :] for jj in range(ATOMS)]
            dst = _atom_transpose(src, masks)
            for pi in range(ATOMS):
                c0 = (ATOMS * b + pi) * PAIR_W + v * LANES
                u_ref[:, c0:c0 + LANES] = dst[pi].astype(BF16)


def _inproj(x, g_mix, w_in, d_ssm, bsz, seqlen):
    t, d = x.shape
    tm = TOKEN_TILE
    d_in = w_in.shape[1]
    per_seq = seqlen // tm
    nc = tm // SSM_CHUNK
    uw = d_ssm * SSM_CHUNK
    return pl.pallas_call(
        functools.partial(_inproj_kernel, d_ssm=d_ssm),
        grid=(t // tm,),
        in_specs=[pl.BlockSpec((tm, d), lambda i: (i, 0)),
                  pl.BlockSpec((1, d), lambda i: (0, 0)),
                  pl.BlockSpec((d, d_in), lambda i: (0, 0))],
        out_specs=[pl.BlockSpec((nc, uw), lambda i: (i % per_seq, i // per_seq)),
                   pl.BlockSpec((tm, d_in - d_ssm), lambda i: (i, 0))],
        out_shape=[jax.ShapeDtypeStruct((seqlen // SSM_CHUNK, bsz * uw), BF16),
                   jax.ShapeDtypeStruct((t, d_in - d_ssm), BF16)],
        scratch_shapes=[pltpu.VMEM((d_ssm // LANES, tm, LANES), F32)],
        compiler_params=_cparams(),
        name="inproj",
    )(x, g_mix, w_in)


def _ssm_matrices(lam_re, lam_im, log_dt, b_re, b_im, c_re, c_im, d_skip):
    hp = lax.Precision.HIGHEST
    _, g, n = lam_re.shape
    p = b_re.shape[-1]
    lc = SSM_CHUNK
    np_ = g // 2
    dt = jnp.exp(log_dt)[..., None]
    mag = jnp.exp(lam_re * dt)
    ar = mag * jnp.cos(lam_im * dt)
    ai = mag * jnp.sin(lam_im * dt)
    den = lam_re * lam_re + lam_im * lam_im
    nr = ar - 1.0
    fr = (nr * lam_re + ai * lam_im) / den
    fi = (ai * lam_re - nr * lam_im) / den
    bbr = fr[..., None] * b_re - fi[..., None] * b_im
    bbi = fr[..., None] * b_im + fi[..., None] * b_re

    prs, pis = [jnp.ones_like(ar)], [jnp.zeros_like(ai)]
    for _ in range(lc):
        pr, pi = prs[-1], pis[-1]
        prs.append(pr * ar - pi * ai)
        pis.append(pr * ai + pi * ar)
    pw_r, pw_i = jnp.stack(prs), jnp.stack(pis)

    car = c_re[None] * pw_r[:lc, :, :, None, :] - c_im[None] * pw_i[:lc, :, :, None, :]
    cai = c_re[None] * pw_i[:lc, :, :, None, :] + c_im[None] * pw_r[:lc, :, :, None, :]
    kk = (jnp.einsum('tdgpn,dgnq->tdgpq', car, bbr, precision=hp)
          - jnp.einsum('tdgpn,dgnq->tdgpq', cai, bbi, precision=hp))
    kf, kb = kk[:, 0], kk[:, 1]
    jj = jnp.arange(lc)[:, None]
    ii = jnp.arange(lc)[None, :]
    tau = ii - jj
    mf = jnp.where((tau >= 0)[:, :, None, None, None], kf[jnp.clip(tau, 0, lc - 1)], 0.0)
    mb = jnp.where((tau <= 0)[:, :, None, None, None], kb[jnp.clip(-tau, 0, lc - 1)], 0.0)
    m = (mf + mb).reshape(lc, lc, np_, 2, p, p)
    eye = jnp.eye(2, dtype=F32)
    mpair = jnp.einsum('jinspq,st->njsqitp', m, eye).reshape(np_, lc * 2 * p, lc * 2 * p)

    ef = lc - 1 - jnp.arange(lc)
    eb = jnp.arange(lc)
    def qmat(d, e):
        wr, wi = pw_r[e, d], pw_i[e, d]
        qr = wr[..., None] * bbr[d][None] - wi[..., None] * bbi[d][None]
        qi = wr[..., None] * bbi[d][None] + wi[..., None] * bbr[d][None]
        to = lambda a: jnp.transpose(a, (0, 1, 3, 2)).reshape(lc, np_, 2, p, n)
        return to(qr), to(qi)
    q4 = jnp.stack(qmat(0, ef) + qmat(1, eb))
    qpair = jnp.einsum('bjnsqm,st->njsqbtm', q4, eye).reshape(np_, lc * 2 * p, 4 * 2 * n)

    pf_e = jnp.arange(lc) + 1
    pb_e = lc - jnp.arange(lc)
    def pmat(d, e):
        wr, wi = pw_r[e, d], pw_i[e, d]
        cr, ci = c_re[d], c_im[d]
        pr = cr[None] * wr[:, :, None, :] - ci[None] * wi[:, :, None, :]
        pi = cr[None] * wi[:, :, None, :] + ci[None] * wr[:, :, None, :]
        to = lambda a: jnp.transpose(a, (1, 3, 0, 2)).reshape(np_, 2, n, lc, p)
        return to(pr), to(-pi)
    p4 = jnp.stack(pmat(0, pf_e) + pmat(1, pb_e))
    ppair = jnp.einsum('bnsmip,st->nbsmitp', p4, eye).reshape(np_, 4 * 2 * n, lc * 2 * p)

    al = jnp.stack([pw_r[lc, 0], pw_i[lc, 0], pw_r[lc, 1], pw_i[lc, 1]])
    alpha = jnp.transpose(al.reshape(4, np_, 2 * n), (1, 0, 2))
    skip = jnp.broadcast_to(d_skip.reshape(np_, 1, 2 * p), (np_, lc, 2 * p)).reshape(np_, 1, lc * 2 * p)
    return mpair.astype(BF16), qpair.astype(BF16), ppair.astype(BF16), alpha, skip.astype(F32)


def _ssm_v_kernel(u_ref, q_ref, v_ref):
    v_ref[...] = jnp.dot(u_ref[...], q_ref[...], preferred_element_type=F32)


def _ssm_scan_kernel(v_ref, a_ref, s_ref, *, n_chunks):
    afr, afi = a_ref[0:1, :], a_ref[1:2, :]
    abr, abi = a_ref[2:3, :], a_ref[3:4, :]
    b = v_ref.shape[1]
    w = LANES
    zero = jnp.zeros((b, w), F32)

    def step(c, carry):
        fr, fi, rr, ri = carry
        cb = n_chunks - 1 - c
        s_ref[c, :, 0:w] = fr
        s_ref[c, :, w:2 * w] = fi
        s_ref[cb, :, 2 * w:3 * w] = rr
        s_ref[cb, :, 3 * w:4 * w] = ri
        vfr = v_ref[c, :, 0:w]
        vfi = v_ref[c, :, w:2 * w]
        vbr = v_ref[cb, :, 2 * w:3 * w]
        vbi = v_ref[cb, :, 3 * w:4 * w]
        return (afr * fr - afi * fi + vfr, afr * fi + afi * fr + vfi,
                abr * rr - abi * ri + vbr, abr * ri + abi * rr + vbi)

    lax.fori_loop(0, n_chunks, step, (zero, zero, zero, zero))


def _ssm_y_kernel(u_ref, s_ref, m_ref, p_ref, d_ref, y_ref):
    u = u_ref[...]
    y = jnp.dot(u, m_ref[...], preferred_element_type=F32)
    y = y + jnp.dot(s_ref[...].astype(BF16), p_ref[...], preferred_element_type=F32)
    y_ref[...] = (y + d_ref[...] * u.astype(F32)).astype(BF16)


def _ssm_trunk(u2, bsz, mats):
    mpair, qpair, ppair, alpha, skip = mats
    nc = u2.shape[0]
    np_ = mpair.shape[0]
    rows = nc * bsz
    pw = PAIR_W
    sw = qpair.shape[2]
    u = u2.reshape(rows, np_ * pw)
    tr = min(SSM_ROW_TILE, rows)
    v = pl.pallas_call(
        _ssm_v_kernel,
        grid=(np_, rows // tr),
        in_specs=[pl.BlockSpec((tr, pw), lambda p, r: (r, p)),
                  pl.BlockSpec((None, pw, sw), lambda p, r: (p, 0, 0))],
        out_specs=pl.BlockSpec((tr, sw), lambda p, r: (r, p)),
        out_shape=jax.ShapeDtypeStruct((rows, np_ * sw), F32),
        compiler_params=_cparams(2),
        name="ssm_v",
    )(u, qpair)
    s = pl.pallas_call(
        functools.partial(_ssm_scan_kernel, n_chunks=nc),
        grid=(np_,),
        in_specs=[pl.BlockSpec((nc, bsz, sw), lambda p: (0, 0, p)),
                  pl.BlockSpec((None, 4, LANES), lambda p: (p, 0, 0))],
        out_specs=pl.BlockSpec((nc, bsz, sw), lambda p: (0, 0, p)),
        out_shape=jax.ShapeDtypeStruct((nc, bsz, np_ * sw), F32),
        compiler_params=_cparams(1),
        name="ssm_scan",
    )(v.reshape(nc, bsz, np_ * sw), alpha)
    y = pl.pallas_call(
        _ssm_y_kernel,
        grid=(np_, rows // tr),
        in_specs=[pl.BlockSpec((tr, pw), lambda p, r: (r, p)),
                  pl.BlockSpec((tr, sw), lambda p, r: (r, p)),
                  pl.BlockSpec((None, pw, pw), lambda p, r: (p, 0, 0)),
                  pl.BlockSpec((None, sw, pw), lambda p, r: (p, 0, 0)),
                  pl.BlockSpec((None, 1, pw), lambda p, r: (p, 0, 0))],
        out_specs=pl.BlockSpec((tr, pw), lambda p, r: (r, p)),
        out_shape=jax.ShapeDtypeStruct((rows, np_ * pw), BF16),
        compiler_params=_cparams(2),
        name="ssm_y",
    )(u, s.reshape(rows, np_ * sw), mpair, ppair, skip)
    return y.reshape(nc, bsz * np_ * pw)


def _mix_kernel(yp_ref, ys_ref, zp_ref, zs_ref, wglu_ref, bglu_ref, gssm_ref,
                lng_ref, lnb_ref, ws_ref, bs_ref, gsgu_ref, o_ref, ya_scr, *, n_prompt_tiles):
    i = pl.program_id(0)
    tm = o_ref.shape[0]
    n_blk = ya_scr.shape[0]
    d_ssm = n_blk * LANES
    nc = tm // SSM_CHUNK

    def body(y_ref, zuv_ref):
        masks = _atom_masks(nc)
        for b in range(n_blk):
            for v in range(SSM_CHUNK // ATOMS):
                src = [y_ref[:, (ATOMS * b + pi) * PAIR_W + v * LANES:
                             (ATOMS * b + pi) * PAIR_W + (v + 1) * LANES].astype(F32)
                       for pi in range(ATOMS)]
                dst = _atom_transpose(src, masks)
                for jj in range(ATOMS):
                    ya_scr[b, pl.ds(ATOMS * v + jj, nc, stride=SSM_CHUNK), :] = dst[jj]
        y = jnp.concatenate([ya_scr[b] for b in range(n_blk)], axis=1)
        gl = _gelu(y)
        gate = jnp.dot(gl.astype(BF16), wglu_ref[...], preferred_element_type=F32) + bglu_ref[...]
        o_ref[:, :d_ssm] = _rms(gl * jax.nn.sigmoid(gate), gssm_ref[...]).astype(BF16)
        d_sgu = zuv_ref.shape[1] // 2
        u = _gelu(zuv_ref[:, :d_sgu].astype(F32))
        gv = _gelu(zuv_ref[:, d_sgu:].astype(F32))
        xc = gv - jnp.mean(gv, axis=-1, keepdims=True)
        v = xc * lax.rsqrt(jnp.mean(xc * xc, axis=-1, keepdims=True) + LN_EPS)
        v = (v * lng_ref[...] + lnb_ref[...]).astype(BF16)
        lo = lax.broadcasted_iota(I32, (CHUNK, LANES), 1) < (LANES // 2)
        zero = jnp.zeros((CHUNK, LANES), BF16)
        rows = []
        for c in range(tm // CHUNK):
            cols = []
            for j in range(d_sgu // LANES):
                vp = v[c * CHUNK:(c + 1) * CHUNK, j * LANES:(j + 1) * LANES]
                rhs = jnp.concatenate([jnp.where(lo, vp, zero), jnp.where(lo, zero, vp)], axis=0)
                cols.append(jnp.dot(ws_ref[j], rhs, preferred_element_type=F32))
            rows.append(jnp.concatenate(cols, axis=1) + bs_ref[...])
        s = jnp.concatenate(rows, axis=0)
        o_ref[:, d_ssm:] = _rms(u * s, gsgu_ref[...]).astype(BF16)

    @pl.when(i < n_prompt_tiles)
    def _():
        body(yp_ref, zp_ref)

    @pl.when(i >= n_prompt_tiles)
    def _():
        body(ys_ref, zs_ref)


def _mix(y_p, y_s, zuv_p, zuv_s, lp, ls, wglu, bglu, gssm, lng, lnb, ws_pairs, bias_s, gsgu):
    tp, ts = zuv_p.shape[0], zuv_s.shape[0]
    d_sgu = zuv_p.shape[1] // 2
    d_ssm = wglu.shape[0]
    tm = TOKEN_TILE
    npt = tp // tm
    nc = tm // SSM_CHUNK
    uw = d_ssm * SSM_CHUNK
    pp, ps = lp // tm, ls // tm
    ip = lambda i: jnp.minimum(i, npt - 1)
    isamp = lambda i: jnp.maximum(i - npt, 0)
    row = lambda w: pl.BlockSpec((1, w), lambda i: (0, 0))
    return pl.pallas_call(
        functools.partial(_mix_kernel, n_prompt_tiles=npt),
        grid=((tp + ts) // tm,),
        in_specs=[pl.BlockSpec((nc, uw), lambda i: (ip(i) % pp, ip(i) // pp)),
                  pl.BlockSpec((nc, uw), lambda i: (isamp(i) % ps, isamp(i) // ps)),
                  pl.BlockSpec((tm, 2 * d_sgu), lambda i: (ip(i), 0)),
                  pl.BlockSpec((tm, 2 * d_sgu), lambda i: (isamp(i), 0)),
                  pl.BlockSpec((d_ssm, d_ssm), lambda i: (0, 0)),
                  row(d_ssm), row(d_ssm), row(d_sgu), row(d_sgu),
                  pl.BlockSpec(ws_pairs.shape, lambda i: (0, 0, 0)),
                  pl.BlockSpec(bias_s.shape, lambda i: (0, 0)),
                  row(d_sgu)],
        out_specs=pl.BlockSpec((tm, d_ssm + d_sgu), lambda i: (i, 0)),
        out_shape=jax.ShapeDtypeStruct((tp + ts, d_ssm + d_sgu), BF16),
        scratch_shapes=[pltpu.VMEM((d_ssm // LANES, tm, LANES), F32)],
        compiler_params=_cparams(),
        name="mix",
    )(y_p, y_s, zuv_p, zuv_s, wglu, bglu, gssm, lng, lnb, ws_pairs, bias_s, gsgu)


def _sorted_rows(tile):
    return TOP_K * tile + MXU_DIM


def _route_kernel(xp_ref, xs_ref, mix_ref, wout_ref, gffn_ref, whi_ref, wlo_ref, br_ref, tri_ref,
                  ltri_ref, x1_ref, xsort_ref, meta_ref, c8_ref, *, n_prompt_tiles, n_tiles, n_exp):
    i = pl.program_id(0)
    tm = mix_ref.shape[0]
    p_rows = xsort_ref.shape[0]

    def body(x_ref):
        x1 = x_ref[...] + jnp.dot(mix_ref[...], wout_ref[...], preferred_element_type=F32)
        x1_ref[...] = x1
        h2 = _rms(x1, gffn_ref[...])
        hi = h2.astype(BF16)
        lo = (h2 - hi.astype(F32)).astype(BF16)
        lt = (jnp.dot(hi, whi_ref[...], preferred_element_type=F32)
              + (jnp.dot(lo, whi_ref[...], preferred_element_type=F32)
                 + jnp.dot(hi, wlo_ref[...], preferred_element_type=F32))) + br_ref[...]
        logits = lt.T[:n_exp]
        eio = lax.broadcasted_iota(I32, (n_exp, tm), 0)
        vals, idxs = [], []
        l = logits
        for _ in range(TOP_K):
            m = jnp.max(l, axis=0, keepdims=True)
            idx = jnp.min(jnp.where(l == m, eio, n_exp), axis=0, keepdims=True)
            vals.append(m)
            idxs.append(idx)
            l = jnp.where(eio == idx, -jnp.inf, l)
        ex = [jnp.exp(v - vals[0]) for v in vals]
        den = ex[0] + ex[1] + ex[2] + ex[3]
        ws = [e / den for e in ex]
        hot = [eio == idx for idx in idxs]
        cnt = sum(h.astype(F32) for h in hot)
        prefix = jnp.dot(cnt.astype(BF16), tri_ref[...], preferred_element_type=F32)
        c = jnp.sum(cnt, axis=1, keepdims=True)
        c8 = jnp.floor((c + (SUBLANES - 1)) * (1.0 / SUBLANES)) * SUBLANES
        c8b = jnp.broadcast_to(c8, (n_exp, LANES))
        c8_ref[...] = c8b
        run0 = jnp.dot(ltri_ref[...], c8b.astype(BF16), preferred_element_type=F32)[:, 0:1]
        base = run0 + prefix
        pos = [jnp.sum(jnp.where(h, base, 0.0), axis=0, keepdims=True) for h in hot]
        meta = jnp.concatenate(ws + pos + [jnp.zeros((LANES - 2 * TOP_K, tm), F32)], axis=0)
        meta_ref[...] = meta.T
        posi = [p.astype(I32) for p in pos]
        for r in range(p_rows // SORT_CHUNK):
            rio = lax.broadcasted_iota(I32, (SORT_CHUNK, tm), 0) + r * SORT_CHUNK
            sel = (rio == posi[0]) | (rio == posi[1]) | (rio == posi[2]) | (rio == posi[3])
            srt = jnp.dot(jnp.where(sel, 1.0, 0.0).astype(BF16), hi, preferred_element_type=F32)
            xsort_ref[r * SORT_CHUNK:(r + 1) * SORT_CHUNK, :] = _pack_bf16_pairs(srt)

    @pl.when(i < n_prompt_tiles)
    def _():
        body(xp_ref)

    @pl.when((i >= n_prompt_tiles) & (i < n_tiles))
    def _():
        body(xs_ref)

    @pl.when(i == n_tiles)
    def _():
        xsort_ref[...] = jnp.zeros_like(xsort_ref)


def _dual_specs(tile, width, n_prompt_tiles, n_tiles):
    last_p = n_prompt_tiles - 1
    last_s = n_tiles - n_prompt_tiles - 1
    sp = pl.BlockSpec((tile, width), lambda i: (jnp.minimum(i, last_p), 0))
    ss = pl.BlockSpec((tile, width), lambda i: (jnp.clip(i - n_prompt_tiles, 0, last_s), 0))
    return sp, ss


def _route(xp, xs, mixed, w_out, g_ffn, wr_hi, wr_lo, br_pad, n_exp):
    tp, d = xp.shape
    t = mixed.shape[0]
    tm = TOKEN_TILE
    npt = tp // tm
    nt = t // tm
    p_rows = _sorted_rows(tm)
    iota = lambda n, ax: lax.broadcasted_iota(I32, (n, n), ax)
    tri = (iota(tm, 0) < iota(tm, 1)).astype(BF16)
    ltri = (iota(n_exp, 1) < iota(n_exp, 0)).astype(BF16)
    sp, ss = _dual_specs(tm, d, npt, nt)
    const = lambda shape: pl.BlockSpec(shape, lambda i: (0,) * len(shape))
    tile = lambda i: jnp.minimum(i, nt - 1)
    return pl.pallas_call(
        functools.partial(_route_kernel, n_prompt_tiles=npt, n_tiles=nt, n_exp=n_exp),
        grid=(nt + 1,),
        in_specs=[sp, ss,
                  pl.BlockSpec((tm, mixed.shape[1]), lambda i: (tile(i), 0)),
                  const(w_out.shape), const((1, d)), const(wr_hi.shape), const(wr_lo.shape),
                  const((1, LANES)), const((tm, tm)), const((n_exp, n_exp))],
        out_specs=[pl.BlockSpec((tm, d), lambda i: (tile(i), 0)),
                   pl.BlockSpec((p_rows, d // 2), lambda i: (i, 0)),
                   pl.BlockSpec((tm, LANES), lambda i: (tile(i), 0)),
                   pl.BlockSpec((None, n_exp, LANES), lambda i: (tile(i), 0, 0))],
        out_shape=[jax.ShapeDtypeStruct((t, d), F32),
                   jax.ShapeDtypeStruct(((nt + 1) * p_rows, d // 2), U32),
                   jax.ShapeDtypeStruct((t, LANES), F32),
                   jax.ShapeDtypeStruct((nt, n_exp, LANES), F32)],
        compiler_params=_cparams(),
        name="route",
    )(xp, xs, mixed, w_out, g_ffn, wr_hi, wr_lo, br_pad, tri, ltri)


def _slot_tables(c8, p_rows, n_blocks):
    nt, n_exp = c8.shape
    run0 = jnp.cumsum(c8, axis=1) - c8
    seg_len = c8.T
    cum = jnp.cumsum(seg_len, axis=1)
    tot = cum[:, -1]
    padded = (tot + MOE_TILE - 1) // MOE_TILE * MOE_TILE
    ends = jnp.cumsum(padded)
    starts = ends - padded
    n_used = (ends[-1] // MOE_TILE).astype(I32)
    blk = jnp.arange(n_blocks, dtype=I32)
    be = jnp.minimum(jnp.sum(ends[None, :] <= (blk * MOE_TILE)[:, None], axis=1), n_exp - 1).astype(I32)
    block_e = jnp.where(blk < n_used, be, be[jnp.maximum(n_used - 1, 0)])
    grow = (blk[:, None] * MOE_TILE + jnp.arange(MOE_GROUPS, dtype=I32)[None, :] * SUBLANES)
    rel = grow - starts[be][:, None]
    cum_e = cum[be]
    tile = jnp.sum(cum_e[:, None, :] <= rel[:, :, None], axis=2)
    valid = (rel < tot[be][:, None]) & (blk < n_used)[:, None]
    tile_c = jnp.minimum(tile, nt - 1)
    seg0 = jnp.take_along_axis(cum_e - seg_len[be], tile_c, axis=1)
    src_row = tile_c * p_rows + jnp.take_along_axis(run0.T[be], tile_c, axis=1) + (rel - seg0)
    scratch0 = nt * p_rows // SUBLANES
    zero_group = scratch0 + 2 * MOE_GROUPS
    g_in = jnp.where(valid, src_row // SUBLANES, zero_group).astype(I32)
    g_scr = scratch0 + (blk[:, None] % 2) * MOE_GROUPS + jnp.arange(MOE_GROUPS, dtype=I32)[None, :]
    g_out = jnp.where(valid, src_row // SUBLANES, g_scr).astype(I32)
    return block_e, n_used, g_in, g_out


def _moe_kernel(be_ref, nu_ref, gin_ref, gnext_ref, gout_ref, xs_hbm, wgu_ref, bgu_ref, wd_ref, bd_ref,
                eo_hbm, xbuf, obuf, isem, osem):
    j = pl.program_id(0)
    nu = nu_ref[0]
    f = wd_ref.shape[0]
    slot = lax.rem(j, 2)
    other = 1 - slot

    def in_copy(tbl_ref, r, sl):
        row = pl.multiple_of(tbl_ref[0, r] * SUBLANES, SUBLANES)
        return pltpu.make_async_copy(xs_hbm.at[pl.ds(row, SUBLANES), :],
                                     xbuf.at[sl, pl.ds(r * SUBLANES, SUBLANES), :], isem.at[sl])

    def out_copy(r, sl):
        row = pl.multiple_of(gout_ref[0, r] * SUBLANES, SUBLANES)
        return pltpu.make_async_copy(obuf.at[sl, pl.ds(r * SUBLANES, SUBLANES), :],
                                     eo_hbm.at[pl.ds(row, SUBLANES), :], osem.at[sl])

    def wait_in(sl):
        pltpu.make_async_copy(xs_hbm.at[pl.ds(0, MOE_TILE), :], xbuf.at[sl], isem.at[sl]).wait()

    def wait_out(sl):
        pltpu.make_async_copy(obuf.at[sl], eo_hbm.at[pl.ds(0, MOE_TILE), :], osem.at[sl]).wait()

    @pl.when(j == 0)
    def _():
        for r in range(MOE_GROUPS):
            in_copy(gin_ref, r, 0).start()

    @pl.when(j + 1 < nu)
    def _():
        for r in range(MOE_GROUPS):
            in_copy(gnext_ref, r, other).start()

    @pl.when(j < nu)
    def _():
        wait_in(slot)

        @pl.when(j >= 2)
        def _():
            wait_out(slot)

        x = _unpack_bf16_pairs(xbuf[slot])
        gu = jnp.dot(x, wgu_ref[...], preferred_element_type=F32) + bgu_ref[...]
        gate = jnp.minimum(gu[:, :f], SWIGLU_LIMIT)
        up = jnp.clip(gu[:, f:], -SWIGLU_LIMIT, SWIGLU_LIMIT)
        act = gate * jax.nn.sigmoid(SWIGLU_ALPHA * gate) * (up + 1.0)
        out = jnp.dot(act.astype(BF16), wd_ref[...], preferred_element_type=F32) + bd_ref[...]
        obuf[slot] = _pack_bf16_pairs(out.astype(BF16).astype(F32))
        for r in range(MOE_GROUPS):
            out_copy(r, slot).start()

        @pl.when(j == nu - 1)
        def _():
            wait_out(slot)

            @pl.when(j >= 1)
            def _():
                wait_out(other)


def _moe(block_e, n_used, g_in, g_out, xsorted, wgu, bgu, wd, bd):
    n_blocks = block_e.shape[0]
    dh = xsorted.shape[1]
    d = 2 * dh
    f2 = wgu.shape[2]
    f = wd.shape[1]
    g_in3 = g_in.reshape(n_blocks, 1, MOE_GROUPS)
    g_next3 = jnp.concatenate([g_in3[1:], g_in3[:1]], axis=0)
    g_out3 = g_out.reshape(n_blocks, 1, MOE_GROUPS)
    tbl = pl.BlockSpec((None, 1, MOE_GROUPS), lambda j, be, nu: (j, 0, 0), memory_space=pltpu.SMEM)
    return pl.pallas_call(
        _moe_kernel,
        grid_spec=pltpu.PrefetchScalarGridSpec(
            num_scalar_prefetch=2, grid=(n_blocks,),
            in_specs=[tbl, tbl, tbl,
                      pl.BlockSpec(memory_space=pl.ANY),
                      pl.BlockSpec((None, d, f2), lambda j, be, nu: (be[j], 0, 0)),
                      pl.BlockSpec((None, 1, f2), lambda j, be, nu: (be[j], 0, 0)),
                      pl.BlockSpec((None, f, d), lambda j, be, nu: (be[j], 0, 0)),
                      pl.BlockSpec((None, 1, d), lambda j, be, nu: (be[j], 0, 0))],
            out_specs=pl.BlockSpec(memory_space=pl.ANY),
            scratch_shapes=[pltpu.VMEM((2, MOE_TILE, dh), U32), pltpu.VMEM((2, MOE_TILE, dh), U32),
                            pltpu.SemaphoreType.DMA((2,)), pltpu.SemaphoreType.DMA((2,))]),
        out_shape=jax.ShapeDtypeStruct(xsorted.shape, U32),
        input_output_aliases={5: 0},
        compiler_params=_cparams(),
        name="moe",
    )(block_e, n_used, g_in3, g_next3, g_out3, xsorted, wgu, bgu, wd, bd)


def _combine_kernel(meta_ref, x1_ref, g_ref, eo_ref, y_ref):
    tm = x1_ref.shape[0]
    p_rows = eo_ref.shape[0]
    lane = lax.broadcasted_iota(I32, (tm, SORT_CHUNK), 1)
    ws = [meta_ref[:, k:k + 1] for k in range(TOP_K)]
    pos = [meta_ref[:, TOP_K + k:TOP_K + k + 1].astype(I32) for k in range(TOP_K)]
    acc = x1_ref[...]
    for r in range(p_rows // SORT_CHUNK):
        buf = _unpack_bf16_pairs(eo_ref[r * SORT_CHUNK:(r + 1) * SORT_CHUNK, :])
        wm = jnp.zeros((tm, SORT_CHUNK), F32)
        for k in range(TOP_K):
            wm = jnp.where(lane + r * SORT_CHUNK == pos[k], ws[k], wm)
        acc = acc + jnp.dot(wm.astype(BF16), buf, preferred_element_type=F32)
    y_ref[...] = _rms(acc, g_ref[...])


def _combine(meta, x1, g_final, eo, tile_off, n_tokens):
    d = x1.shape[1]
    tm = TOKEN_TILE
    p_rows = _sorted_rows(tm)
    return pl.pallas_call(
        _combine_kernel,
        grid=(n_tokens // tm,),
        in_specs=[pl.BlockSpec((tm, LANES), lambda i: (i + tile_off, 0)),
                  pl.BlockSpec((tm, d), lambda i: (i + tile_off, 0)),
                  pl.BlockSpec((1, d), lambda i: (0, 0)),
                  pl.BlockSpec((p_rows, d // 2), lambda i: (i + tile_off, 0))],
        out_specs=pl.BlockSpec((tm, d), lambda i: (i, 0)),
        out_shape=jax.ShapeDtypeStruct((n_tokens, d), F32),
        compiler_params=_cparams(),
        name="combine",
    )(meta, x1, g_final, eo)


def kernel(x_prompt, x_sample, g_mix, w_in, ssm_lam_re, ssm_lam_im, ssm_log_dt, ssm_b_re, ssm_b_im,
           ssm_c_re, ssm_c_im, ssm_d, w_glu, b_glu, sgu_ln_g, sgu_ln_b, sgu_w_s, sgu_b_s,
           g_out_ssm, g_out_sgu, w_out, g_ffn, w_router, b_router, w_gate_up, b_gate_up,
           w_down, b_down, g_final):
    assert g_mix.shape[0] == 1, "single-layer trunk"
    bp, lp, d = x_prompt.shape
    bs, ls, _ = x_sample.shape
    tp, ts = bp * lp, bs * ls
    t = tp + ts
    d_ssm = ssm_d.shape[1]
    d_sgu = sgu_ln_g.shape[1]
    n_exp = w_router.shape[2]
    assert lp % TOKEN_TILE == 0 and ls % TOKEN_TILE == 0 and TOKEN_TILE % CHUNK == 0
    assert SSM_CHUNK * SSM_GROUP == MXU_DIM and 2 * ssm_lam_re.shape[-1] == LANES
    assert d_sgu // SGU_HEADS == LANES // 2 and n_exp <= LANES
    assert n_exp * (SUBLANES - 1) <= MXU_DIM and _sorted_rows(TOKEN_TILE) % SORT_CHUNK == 0
    assert (2 * MOE_GROUPS + 1) * SUBLANES <= _sorted_rows(TOKEN_TILE)

    xp = x_prompt.reshape(tp, d)
    xs = x_sample.reshape(ts, d)
    row = lambda a: a.reshape(1, -1).astype(F32)

    w_in_b = w_in[0].astype(BF16)
    u_p, zuv_p = _inproj(xp, row(g_mix[0]), w_in_b, d_ssm, bp, lp)
    u_s, zuv_s = _inproj(xs, row(g_mix[0]), w_in_b, d_ssm, bs, ls)

    mats = _ssm_matrices(ssm_lam_re[0], ssm_lam_im[0], ssm_log_dt[0], ssm_b_re[0], ssm_b_im[0],
                         ssm_c_re[0], ssm_c_im[0], ssm_d[0])
    y_p = _ssm_trunk(u_p, bp, mats)
    y_s = _ssm_trunk(u_s, bs, mats)

    ws = sgu_w_s[0]
    ws_pairs = jnp.concatenate([ws[0::2], ws[1::2]], axis=2).astype(BF16)
    bias_s = jnp.repeat(sgu_b_s[0].T, d_sgu // SGU_HEADS, axis=1).astype(F32)
    mixed = _mix(y_p, y_s, zuv_p, zuv_s, lp, ls, w_glu[0].astype(BF16), row(b_glu[0]), row(g_out_ssm[0]),
                 row(sgu_ln_g[0]), row(sgu_ln_b[0]), ws_pairs, bias_s, row(g_out_sgu[0]))

    wr_pad = jnp.pad(w_router[0].astype(F32), ((0, 0), (0, LANES - n_exp)))
    wr_hi = wr_pad.astype(BF16)
    wr_lo = (wr_pad - wr_hi.astype(F32)).astype(BF16)
    br_pad = jnp.pad(b_router[0].astype(F32), (0, LANES - n_exp)).reshape(1, LANES)
    x1, xsorted, meta, c8 = _route(xp, xs, mixed, w_out[0].astype(BF16), row(g_ffn[0]),
                                   wr_hi, wr_lo, br_pad, n_exp)

    nt = t // TOKEN_TILE
    p_rows = _sorted_rows(TOKEN_TILE)
    max_rows = t * TOP_K + nt * n_exp * (SUBLANES - 1) + n_exp * (MOE_TILE - 1)
    n_blocks = -(-max_rows // MOE_TILE)
    block_e, n_used, g_in, g_out = _slot_tables(c8[:, :, 0].astype(I32), p_rows, n_blocks)
    eo = _moe(block_e, n_used.reshape(1), g_in, g_out, xsorted, w_gate_up[0].astype(BF16),
              b_gate_up[0][:, None, :].astype(F32), w_down[0].astype(BF16),
              b_down[0][:, None, :].astype(F32))

    gf = row(g_final)
    y_prompt = _combine(meta, x1, gf, eo, 0, tp)
    y_sample = _combine(meta, x1, gf, eo, tp // TOKEN_TILE, ts)
    return y_prompt.reshape(bp, lp, d), y_sample.reshape(bs, ls, d)
```

```python
import functools
import math

import jax
import jax.numpy as jnp
from jax import lax
from jax.experimental import pallas as pl
from jax.experimental.pallas import tpu as pltpu

F32 = jnp.float32
BF16 = jnp.bfloat16
I32 = jnp.int32
U32 = jnp.uint32

SSM_GROUP = 16
SGU_HEADS = 8
CHUNK = 128
TOP_K = 4
SWIGLU_LIMIT = 7.0
SWIGLU_ALPHA = 1.702
RMS_EPS = 1e-6
LN_EPS = 1e-5

LANES = 128
SUBLANES = 8
MXU_DIM = 256
SSM_CHUNK = MXU_DIM // SSM_GROUP
PAIR_W = 2 * SSM_CHUNK * SSM_GROUP
ATOM = 2 * SSM_GROUP
ATOMS = LANES // ATOM

TOKEN_TILE = 512
MOE_TILE = 512
MOE_GROUPS = MOE_TILE // SUBLANES
SORT_CHUNK = 256
SSM_ROW_TILE = 1024
VMEM_LIMIT = 56 * 1024 * 1024


def _cparams(n_axes=1, vmem=None):
    return pltpu.CompilerParams(
        dimension_semantics=("arbitrary",) * n_axes,
        vmem_limit_bytes=vmem if vmem is not None else VMEM_LIMIT,
    )


def _rms(x, g):
    return x * lax.rsqrt(jnp.mean(x * x, axis=-1, keepdims=True) + RMS_EPS) * g


def _gelu(x):
    return x * (lax.erf(x * (1.0 / math.sqrt(2.0))) + 1.0) * 0.5


def _atom_masks(rows):
    lane = lax.broadcasted_iota(I32, (rows, LANES), 1)
    return [(lane >= a * ATOM) & (lane < (a + 1) * ATOM) for a in range(ATOMS)]


def _atom_transpose(src, masks):
    dst = []
    for b in range(ATOMS):
        acc = None
        for a in range(ATOMS):
            r = (a - b) % ATOMS
            piece = src[a] if r == 0 else pltpu.roll(src[a], ATOM * r, axis=1)
            acc = piece if acc is None else jnp.where(masks[a], piece, acc)
        dst.append(acc)
    return dst


def _pack_bf16_pairs(x):
    w = x.shape[1] // 2
    lo = lax.bitcast_convert_type(x[:, :w], U32) >> 16
    hi = lax.bitcast_convert_type(x[:, w:], U32) & jnp.uint32(0xFFFF0000)
    return hi | lo


def _unpack_bf16_pairs(u):
    lo = lax.bitcast_convert_type(u << 16, F32)
    hi = lax.bitcast_convert_type(u & jnp.uint32(0xFFFF0000), F32)
    return jnp.concatenate([lo.astype(BF16), hi.astype(BF16)], axis=1)


def _inproj_kernel(x_ref, g_ref, w_ref, u_ref, zuv_ref, za_scr, *, d_ssm):
    tm = x_ref.shape[0]
    nc = tm // SSM_CHUNK
    h = _rms(x_ref[...], g_ref[...])
    z = jnp.dot(h.astype(BF16), w_ref[...], preferred_element_type=F32)
    zuv_ref[...] = z[:, d_ssm:].astype(BF16)
    n_blk = d_ssm // LANES
    for b in range(n_blk):
        za_scr[b] = z[:, b * LANES:(b + 1) * LANES]
    masks = _atom_masks(nc)
    n_quads = SSM_CHUNK // ATOMS
    for b in range(n_blk):
        for v in range(n_quads):
            src = [za_scr[b, pl.ds(ATOMS * v + jj, nc, stride=SSM_CHUNK), :] for jj in range(ATOMS)]
            dst = _atom_transpose(src, masks)
            for pi in range(ATOMS):
                c0 = (ATOMS * b + pi) * PAIR_W + v * LANES
                u_ref[:, c0:c0 + LANES] = dst[pi].astype(BF16)


def _inproj(x, g_mix, w_in, d_ssm, bsz, seqlen):
    t, d = x.shape
    tm = TOKEN_TILE
    d_in = w_in.shape[1]
    per_seq = seqlen // tm
    nc = tm // SSM_CHUNK
    uw = d_ssm * SSM_CHUNK
    return pl.pallas_call(
        functools.partial(_inproj_kernel, d_ssm=d_ssm),
        grid=(t // tm,),
        in_specs=[pl.BlockSpec((tm, d), lambda i: (i, 0)),
                  pl.BlockSpec((1, d), lambda i: (0, 0)),
                  pl.BlockSpec((d, d_in), lambda i: (0, 0))],
        out_specs=[pl.BlockSpec((nc, uw), lambda i: (i % per_seq, i // per_seq)),
                   pl.BlockSpec((tm, d_in - d_ssm), lambda i: (i, 0))],
        out_shape=[jax.ShapeDtypeStruct((seqlen // SSM_CHUNK, bsz * uw), BF16),
                   jax.ShapeDtypeStruct((t, d_in - d_ssm), BF16)],
        scratch_shapes=[pltpu.VMEM((d_ssm // LANES, tm, LANES), F32)],
        compiler_params=_cparams(),
        name="inproj",
    )(x, g_mix, w_in)


def _ssm_matrices(lam_re, lam_im, log_dt, b_re, b_im, c_re, c_im, d_skip):
    hp = lax.Precision.HIGHEST
    _, g, n = lam_re.shape
    p = b_re.shape[-1]
    lc = SSM_CHUNK
    np_ = g // 2
    dt = jnp.exp(log_dt)[..., None]
    mag = jnp.exp(lam_re * dt)
    ar = mag * jnp.cos(lam_im * dt)
    ai = mag * jnp.sin(lam_im * dt)
    den = lam_re * lam_re + lam_im * lam_im
    nr = ar - 1.0
    fr = (nr * lam_re + ai * lam_im) / den
    fi = (ai * lam_re - nr * lam_im) / den
    bbr = fr[..., None] * b_re - fi[..., None] * b_im
    bbi = fr[..., None] * b_im + fi[..., None] * b_re

    prs, pis = [jnp.ones_like(ar)], [jnp.zeros_like(ai)]
    for _ in range(lc):
        pr, pi = prs[-1], pis[-1]
        prs.append(pr * ar - pi * ai)
        pis.append(pr * ai + pi * ar)
    pw_r, pw_i = jnp.stack(prs), jnp.stack(pis)

    car = c_re[None] * pw_r[:lc, :, :, None, :] - c_im[None] * pw_i[:lc, :, :, None, :]
    cai = c_re[None] * pw_i[:lc, :, :, None, :] + c_im[None] * pw_r[:lc, :, :, None, :]
    kk = (jnp.einsum('tdgpn,dgnq->tdgpq', car, bbr, precision=hp)
          - jnp.einsum('tdgpn,dgnq->tdgpq', cai, bbi, precision=hp))
    kf, kb = kk[:, 0], kk[:, 1]
    eye = jnp.eye(2, dtype=F32)
    a2 = p * 2

    k_all = jnp.concatenate([kb[:0:-1], (kf[0] + kb[0])[None], kf[1:]], axis=0)
    k_all = jnp.transpose(k_all, (1, 3, 0, 2)).reshape(np_, 2, p, 2 * lc - 1, p)
    kflat = jnp.einsum('nsqup,st->nsqutp', k_all, eye).reshape(np_, a2, (2 * lc - 1) * a2)
    kflat = jnp.pad(kflat, ((0, 0), (0, 0), (0, a2)))

    def atoms_q(x):
        x = jnp.transpose(x, (0, 1, 3, 2)).reshape(2, np_, 2, p, n)
        return jnp.einsum('dnsqm,st->dnsqtm', x, eye).reshape(2, np_, a2, 2 * n)
    bq_r, bq_i = atoms_q(bbr), atoms_q(bbi)
    bq = jnp.stack([bq_r[0], bq_i[0], bq_r[1], bq_i[1]], axis=1)
    ef = lc - 1 - jnp.arange(lc)
    eb = jnp.arange(lc)
    rows_q = lambda w, e, d: jnp.transpose(w[e, d].reshape(lc, np_, 2 * n), (1, 0, 2))
    wq = jnp.stack([rows_q(pw_r, ef, 0), rows_q(pw_i, ef, 0),
                    rows_q(pw_r, eb, 1), rows_q(pw_i, eb, 1)], axis=1)

    def atoms_p(x):
        x = jnp.transpose(x, (0, 1, 3, 2)).reshape(2, np_, 2, n, p)
        return jnp.einsum('dnsmp,st->dnsmtp', x, eye).reshape(2, np_, 2 * n, a2)
    cp_r, cp_i = atoms_p(c_re), atoms_p(c_im)
    cp = jnp.stack([cp_r[0], cp_i[0], cp_r[1], cp_i[1]], axis=1)
    pf_e = jnp.arange(lc) + 1
    pb_e = lc - jnp.arange(lc)
    cols_p = lambda w, e, d: jnp.transpose(w[e, d].reshape(lc, np_, 2 * n), (1, 2, 0))
    wp = jnp.stack([cols_p(pw_r, pf_e, 0), cols_p(pw_i, pf_e, 0),
                    cols_p(pw_r, pb_e, 1), cols_p(pw_i, pb_e, 1)], axis=1)
    lane = jnp.arange(lc * a2)
    til = (lane[None, :] % a2 == jnp.arange(a2)[:, None]).astype(BF16)
    rep = (lane[None, :] // a2 == jnp.arange(lc)[:, None]).astype(BF16)

    al = jnp.stack([pw_r[lc, 0], pw_i[lc, 0], pw_r[lc, 1], pw_i[lc, 1]])
    alpha = jnp.transpose(al.reshape(4, np_, 2 * n), (1, 0, 2))
    skip = jnp.broadcast_to(d_skip.reshape(np_, 1, a2), (np_, lc, a2)).reshape(np_, 1, lc * a2)
    return kflat, bq, wq, cp, wp, til, rep, alpha, skip.astype(F32)


def _ssm_v_kernel(u_ref, bq_ref, wq_ref, v_ref, q_scr):
    a2 = bq_ref.shape[1]
    w = bq_ref.shape[2]

    @pl.when(pl.program_id(1) == 0)
    def _():
        for d in range(2):
            br, bi = bq_ref[2 * d], bq_ref[2 * d + 1]
            for j in range(SSM_CHUNK):
                wr = wq_ref[2 * d, j:j + 1, :]
                wi = wq_ref[2 * d + 1, j:j + 1, :]
                q_scr[j * a2:(j + 1) * a2, (2 * d) * w:(2 * d + 1) * w] = (wr * br - wi * bi).astype(BF16)
                q_scr[j * a2:(j + 1) * a2, (2 * d + 1) * w:(2 * d + 2) * w] = (wr * bi + wi * br).astype(BF16)

    v_ref[...] = jnp.dot(u_ref[...], q_scr[...], preferred_element_type=F32)


def _ssm_scan_kernel(v_ref, a_ref, s_ref, *, n_chunks):
    afr, afi = a_ref[0:1, :], a_ref[1:2, :]
    abr, abi = a_ref[2:3, :], a_ref[3:4, :]
    b = v_ref.shape[1]
    w = LANES
    zero = jnp.zeros((b, w), F32)

    def step(c, carry):
        fr, fi, rr, ri = carry
        cb = n_chunks - 1 - c
        s_ref[c, :, 0:w] = fr
        s_ref[c, :, w:2 * w] = fi
        s_ref[cb, :, 2 * w:3 * w] = rr
        s_ref[cb, :, 3 * w:4 * w] = ri
        vfr = v_ref[c, :, 0:w]
        vfi = v_ref[c, :, w:2 * w]
        vbr = v_ref[cb, :, 2 * w:3 * w]
        vbi = v_ref[cb, :, 3 * w:4 * w]
        return (afr * fr - afi * fi + vfr, afr * fi + afi * fr + vfi,
                abr * rr - abi * ri + vbr, abr * ri + abi * rr + vbi)

    lax.fori_loop(0, n_chunks, step, (zero, zero, zero, zero))


def _ssm_y_kernel(u_ref, s_ref, kflat_ref, cp_ref, wp_ref, til_ref, rep_ref, d_ref, y_ref, m_scr, p_scr):
    a2 = kflat_ref.shape[0]
    w = cp_ref.shape[1]

    def expand(x, e_ref):
        hi = x.astype(BF16)
        lo = (x - hi.astype(F32)).astype(BF16)
        return (jnp.dot(hi, e_ref[...], preferred_element_type=F32)
                + jnp.dot(lo, e_ref[...], preferred_element_type=F32))

    @pl.when(pl.program_id(1) == 0)
    def _():
        kflat = kflat_ref[...]
        for j in range(SSM_CHUNK):
            off = (SSM_CHUNK - 1 - j) * a2
            m_scr[j * a2:(j + 1) * a2, :] = kflat[:, off:off + PAIR_W].astype(BF16)
        for d in range(2):
            cr, ci = expand(cp_ref[2 * d], til_ref), expand(cp_ref[2 * d + 1], til_ref)
            wr, wi = expand(wp_ref[2 * d], rep_ref), expand(wp_ref[2 * d + 1], rep_ref)
            p_scr[(2 * d) * w:(2 * d + 1) * w, :] = (cr * wr - ci * wi).astype(BF16)
            p_scr[(2 * d + 1) * w:(2 * d + 2) * w, :] = (-(cr * wi + ci * wr)).astype(BF16)

    u = u_ref[...]
    y = jnp.dot(u, m_scr[...], preferred_element_type=F32)
    y = y + jnp.dot(s_ref[...].astype(BF16), p_scr[...], preferred_element_type=F32)
    y_ref[...] = (y + d_ref[...] * u.astype(F32)).astype(BF16)


def _ssm_trunk(u2, bsz, mats):
    kflat, bq, wq, cp, wp, til, rep, alpha, skip = mats
    nc = u2.shape[0]
    np_ = kflat.shape[0]
    rows = nc * bsz
    pw = PAIR_W
    sw = 4 * bq.shape[3]
    u = u2.reshape(rows, np_ * pw)
    tr = min(SSM_ROW_TILE, rows)
    per_pair = lambda a: pl.BlockSpec((None,) + a.shape[1:], lambda p, r: (p,) + (0,) * (a.ndim - 1))
    whole = lambda a: pl.BlockSpec(a.shape, lambda p, r: (0,) * a.ndim)
    v = pl.pallas_call(
        _ssm_v_kernel,
        grid=(np_, rows // tr),
        in_specs=[pl.BlockSpec((tr, pw), lambda p, r: (r, p)), per_pair(bq), per_pair(wq)],
        out_specs=pl.BlockSpec((tr, sw), lambda p, r: (r, p)),
        out_shape=jax.ShapeDtypeStruct((rows, np_ * sw), F32),
        scratch_shapes=[pltpu.VMEM((pw, sw), BF16)],
        compiler_params=_cparams(2),
        name="ssm_v",
    )(u, bq, wq)
    s = pl.pallas_call(
        functools.partial(_ssm_scan_kernel, n_chunks=nc),
        grid=(np_,),
        in_specs=[pl.BlockSpec((nc, bsz, sw), lambda p: (0, 0, p)),
                  pl.BlockSpec((None, 4, LANES), lambda p: (p, 0, 0))],
        out_specs=pl.BlockSpec((nc, bsz, sw), lambda p: (0, 0, p)),
        out_shape=jax.ShapeDtypeStruct((nc, bsz, np_ * sw), F32),
        compiler_params=_cparams(1),
        name="ssm_scan",
    )(v.reshape(nc, bsz, np_ * sw), alpha)
    y = pl.pallas_call(
        _ssm_y_kernel,
        grid=(np_, rows // tr),
        in_specs=[pl.BlockSpec((tr, pw), lambda p, r: (r, p)),
                  pl.BlockSpec((tr, sw), lambda p, r: (r, p)),
                  per_pair(kflat), per_pair(cp), per_pair(wp), whole(til), whole(rep), per_pair(skip)],
        out_specs=pl.BlockSpec((tr, pw), lambda p, r: (r, p)),
        out_shape=jax.ShapeDtypeStruct((rows, np_ * pw), BF16),
        scratch_shapes=[pltpu.VMEM((pw, pw), BF16), pltpu.VMEM((sw, pw), BF16)],
        compiler_params=_cparams(2),
        name="ssm_y",
    )(u, s.reshape(rows, np_ * sw), kflat, cp, wp, til, rep, skip)
    return y.reshape(nc, bsz * np_ * pw)


def _mix_kernel(yp_ref, ys_ref, zp_ref, zs_ref, wglu_ref, bglu_ref, gssm_ref,
                lng_ref, lnb_ref, ws_ref, bs_ref, gsgu_ref, o_ref, ya_scr, *, n_prompt_tiles):
    i = pl.program_id(0)
    tm = o_ref.shape[0]
    n_blk = ya_scr.shape[0]
    d_ssm = n_blk * LANES
    nc = tm // SSM_CHUNK

    def body(y_ref, zuv_ref):
        masks = _atom_masks(nc)
        for b in range(n_blk):
            for v in range(SSM_CHUNK // ATOMS):
                src = [y_ref[:, (ATOMS * b + pi) * PAIR_W + v * LANES:
                             (ATOMS * b + pi) * PAIR_W + (v + 1) * LANES].astype(F32)
                       for pi in range(ATOMS)]
                dst = _atom_transpose(src, masks)
                for jj in range(ATOMS):
                    ya_scr[b, pl.ds(ATOMS * v + jj, nc, stride=SSM_CHUNK), :] = dst[jj]
        y = jnp.concatenate([ya_scr[b] for b in range(n_blk)], axis=1)
        gl = _gelu(y)
        gate = jnp.dot(gl.astype(BF16), wglu_ref[...], preferred_element_type=F32) + bglu_ref[...]
        o_ref[:, :d_ssm] = _rms(gl * jax.nn.sigmoid(gate), gssm_ref[...]).astype(BF16)
        d_sgu = zuv_ref.shape[1] // 2
        u = _gelu(zuv_ref[:, :d_sgu].astype(F32))
        gv = _gelu(zuv_ref[:, d_sgu:].astype(F32))
        xc = gv - jnp.mean(gv, axis=-1, keepdims=True)
        v = xc * lax.rsqrt(jnp.mean(xc * xc, axis=-1, keepdims=True) + LN_EPS)
        v = (v * lng_ref[...] + lnb_ref[...]).astype(BF16)
        lo = lax.broadcasted_iota(I32, (CHUNK, LANES), 1) < (LANES // 2)
        zero = jnp.zeros((CHUNK, LANES), BF16)
        rows = []
        for c in range(tm // CHUNK):
            cols = []
            for j in range(d_sgu // LANES):
                vp = v[c * CHUNK:(c + 1) * CHUNK, j * LANES:(j + 1) * LANES]
                rhs = jnp.concatenate([jnp.where(lo, vp, zero), jnp.where(lo, zero, vp)], axis=0)
                cols.append(jnp.dot(ws_ref[j], rhs, preferred_element_type=F32))
            rows.append(jnp.concatenate(cols, axis=1) + bs_ref[...])
        s = jnp.concatenate(rows, axis=0)
        o_ref[:, d_ssm:] = _rms(u * s, gsgu_ref[...]).astype(BF16)

    @pl.when(i < n_prompt_tiles)
    def _():
        body(yp_ref, zp_ref)

    @pl.when(i >= n_prompt_tiles)
    def _():
        body(ys_ref, zs_ref)


def _mix(y_p, y_s, zuv_p, zuv_s, lp, ls, wglu, bglu, gssm, lng, lnb, ws_pairs, bias_s, gsgu):
    tp, ts = zuv_p.shape[0], zuv_s.shape[0]
    d_sgu = zuv_p.shape[1] // 2
    d_ssm = wglu.shape[0]
    tm = TOKEN_TILE
    npt = tp // tm
    nc = tm // SSM_CHUNK
    uw = d_ssm * SSM_CHUNK
    pp, ps = lp // tm, ls // tm
    ip = lambda i: jnp.minimum(i, npt - 1)
    isamp = lambda i: jnp.maximum(i - npt, 0)
    row = lambda w: pl.BlockSpec((1, w), lambda i: (0, 0))
    return pl.pallas_call(
        functools.partial(_mix_kernel, n_prompt_tiles=npt),
        grid=((tp + ts) // tm,),
        in_specs=[pl.BlockSpec((nc, uw), lambda i: (ip(i) % pp, ip(i) // pp)),
                  pl.BlockSpec((nc, uw), lambda i: (isamp(i) % ps, isamp(i) // ps)),
                  pl.BlockSpec((tm, 2 * d_sgu), lambda i: (ip(i), 0)),
                  pl.BlockSpec((tm, 2 * d_sgu), lambda i: (isamp(i), 0)),
                  pl.BlockSpec((d_ssm, d_ssm), lambda i: (0, 0)),
                  row(d_ssm), row(d_ssm), row(d_sgu), row(d_sgu),
                  pl.BlockSpec(ws_pairs.shape, lambda i: (0, 0, 0)),
                  pl.BlockSpec(bias_s.shape, lambda i: (0, 0)),
                  row(d_sgu)],
        out_specs=pl.BlockSpec((tm, d_ssm + d_sgu), lambda i: (i, 0)),
        out_shape=jax.ShapeDtypeStruct((tp + ts, d_ssm + d_sgu), BF16),
        scratch_shapes=[pltpu.VMEM((d_ssm // LANES, tm, LANES), F32)],
        compiler_params=_cparams(),
        name="mix",
    )(y_p, y_s, zuv_p, zuv_s, wglu, bglu, gssm, lng, lnb, ws_pairs, bias_s, gsgu)


def _sorted_rows(tile):
    return TOP_K * tile + MXU_DIM


def _route_kernel(xp_ref, xs_ref, mix_ref, wout_ref, gffn_ref, whi_ref, wlo_ref, br_ref, tri_ref,
                  ltri_ref, x1_ref, xsort_ref, meta_ref, c8_ref, *, n_prompt_tiles, n_tiles, n_exp):
    i = pl.program_id(0)
    tm = mix_ref.shape[0]
    p_rows = xsort_ref.shape[0]

    def body(x_ref):
        x1 = x_ref[...] + jnp.dot(mix_ref[...], wout_ref[...], preferred_element_type=F32)
        x1_ref[...] = x1
        h2 = _rms(x1, gffn_ref[...])
        hi = h2.astype(BF16)
        lo = (h2 - hi.astype(F32)).astype(BF16)
        lt = (jnp.dot(hi, whi_ref[...], preferred_element_type=F32)
              + (jnp.dot(lo, whi_ref[...], preferred_element_type=F32)
                 + jnp.dot(hi, wlo_ref[...], preferred_element_type=F32))) + br_ref[...]
        logits = lt.T[:n_exp]
        eio = lax.broadcasted_iota(I32, (n_exp, tm), 0)
        vals, idxs = [], []
        l = logits
        for _ in range(TOP_K):
            m = jnp.max(l, axis=0, keepdims=True)
            idx = jnp.min(jnp.where(l == m, eio, n_exp), axis=0, keepdims=True)
            vals.append(m)
            idxs.append(idx)
            l = jnp.where(eio == idx, -jnp.inf, l)
        ex = [jnp.exp(v - vals[0]) for v in vals]
        den = ex[0] + ex[1] + ex[2] + ex[3]
        ws = [e / den for e in ex]
        hot = [eio == idx for idx in idxs]
        cnt = sum(h.astype(F32) for h in hot)
        prefix = jnp.dot(cnt.astype(BF16), tri_ref[...], preferred_element_type=F32)
        c = jnp.sum(cnt, axis=1, keepdims=True)
        c8 = jnp.floor((c + (SUBLANES - 1)) * (1.0 / SUBLANES)) * SUBLANES
        c8b = jnp.broadcast_to(c8, (n_exp, LANES))
        c8_ref[...] = c8b
        run0 = jnp.dot(ltri_ref[...], c8b.astype(BF16), preferred_element_type=F32)[:, 0:1]
        base = run0 + prefix
        pos = [jnp.sum(jnp.where(h, base, 0.0), axis=0, keepdims=True) for h in hot]
        meta = jnp.concatenate(ws + pos + [jnp.zeros((LANES - 2 * TOP_K, tm), F32)], axis=0)
        meta_ref[...] = meta.T
        rio = lax.broadcasted_iota(I32, (SORT_CHUNK, tm), 0).astype(F32).astype(BF16)
        one = jnp.ones((SORT_CHUNK, tm), BF16)
        zero = jnp.zeros((SORT_CHUNK, tm), BF16)
        for r in range(p_rows // SORT_CHUNK):
            lo_r = float(r * SORT_CHUNK)
            rel = [jnp.where((p >= lo_r) & (p < lo_r + SORT_CHUNK), p - lo_r, -1.0).astype(BF16) for p in pos]
            sel = (rio == rel[0]) | (rio == rel[1]) | (rio == rel[2]) | (rio == rel[3])
            srt = jnp.dot(jnp.where(sel, one, zero), hi, preferred_element_type=F32)
            xsort_ref[r * SORT_CHUNK:(r + 1) * SORT_CHUNK, :] = _pack_bf16_pairs(srt)

    @pl.when(i < n_prompt_tiles)
    def _():
        body(xp_ref)

    @pl.when((i >= n_prompt_tiles) & (i < n_tiles))
    def _():
        body(xs_ref)

    @pl.when(i == n_tiles)
    def _():
        xsort_ref[...] = jnp.zeros_like(xsort_ref)


def _dual_specs(tile, width, n_prompt_tiles, n_tiles):
    last_p = n_prompt_tiles - 1
    last_s = n_tiles - n_prompt_tiles - 1
    sp = pl.BlockSpec((tile, width), lambda i: (jnp.minimum(i, last_p), 0))
    ss = pl.BlockSpec((tile, width), lambda i: (jnp.clip(i - n_prompt_tiles, 0, last_s), 0))
    return sp, ss


def _route(xp, xs, mixed, w_out, g_ffn, wr_hi, wr_lo, br_pad, n_exp):
    tp, d = xp.shape
    t = mixed.shape[0]
    tm = TOKEN_TILE
    npt = tp // tm
    nt = t // tm
    p_rows = _sorted_rows(tm)
    iota = lambda n, ax: lax.broadcasted_iota(I32, (n, n), ax)
    tri = (iota(tm, 0) < iota(tm, 1)).astype(BF16)
    ltri = (iota(n_exp, 1) < iota(n_exp, 0)).astype(BF16)
    sp, ss = _dual_specs(tm, d, npt, nt)
    const = lambda shape: pl.BlockSpec(shape, lambda i: (0,) * len(shape))
    tile = lambda i: jnp.minimum(i, nt - 1)
    return pl.pallas_call(
        functools.partial(_route_kernel, n_prompt_tiles=npt, n_tiles=nt, n_exp=n_exp),
        grid=(nt + 1,),
        in_specs=[sp, ss,
                  pl.BlockSpec((tm, mixed.shape[1]), lambda i: (tile(i), 0)),
                  const(w_out.shape), const((1, d)), const(wr_hi.shape), const(wr_lo.shape),
                  const((1, LANES)), const((tm, tm)), const((n_exp, n_exp))],
        out_specs=[pl.BlockSpec((tm, d), lambda i: (tile(i), 0)),
                   pl.BlockSpec((p_rows, d // 2), lambda i: (i, 0)),
                   pl.BlockSpec((tm, LANES), lambda i: (tile(i), 0)),
                   pl.BlockSpec((None, n_exp, LANES), lambda i: (tile(i), 0, 0))],
        out_shape=[jax.ShapeDtypeStruct((t, d), F32),
                   jax.ShapeDtypeStruct(((nt + 1) * p_rows, d // 2), U32),
                   jax.ShapeDtypeStruct((t, LANES), F32),
                   jax.ShapeDtypeStruct((nt, n_exp, LANES), F32)],
        compiler_params=_cparams(),
        name="route",
    )(xp, xs, mixed, w_out, g_ffn, wr_hi, wr_lo, br_pad, tri, ltri)


def _slot_tables(c8, p_rows, n_blocks):
    nt, n_exp = c8.shape
    run0 = jnp.cumsum(c8, axis=1) - c8
    seg_len = c8.T
    cum = jnp.cumsum(seg_len, axis=1)
    tot = cum[:, -1]
    padded = (tot + MOE_TILE - 1) // MOE_TILE * MOE_TILE
    ends = jnp.cumsum(padded)
    starts = ends - padded
    n_used = (ends[-1] // MOE_TILE).astype(I32)
    blk = jnp.arange(n_blocks, dtype=I32)
    be = jnp.minimum(jnp.sum(ends[None, :] <= (blk * MOE_TILE)[:, None], axis=1), n_exp - 1).astype(I32)
    block_e = jnp.where(blk < n_used, be, be[jnp.maximum(n_used - 1, 0)])
    grow = (blk[:, None] * MOE_TILE + jnp.arange(MOE_GROUPS, dtype=I32)[None, :] * SUBLANES)
    rel = grow - starts[be][:, None]
    cum_e = cum[be]
    tile = jnp.sum(cum_e[:, None, :] <= rel[:, :, None], axis=2)
    valid = (rel < tot[be][:, None]) & (blk < n_used)[:, None]
    tile_c = jnp.minimum(tile, nt - 1)
    seg0 = jnp.take_along_axis(cum_e - seg_len[be], tile_c, axis=1)
    src_row = tile_c * p_rows + jnp.take_along_axis(run0.T[be], tile_c, axis=1) + (rel - seg0)
    scratch0 = nt * p_rows // SUBLANES
    zero_group = scratch0 + 2 * MOE_GROUPS
    g_in = jnp.where(valid, src_row // SUBLANES, zero_group).astype(I32)
    g_scr = scratch0 + (blk[:, None] % 2) * MOE_GROUPS + jnp.arange(MOE_GROUPS, dtype=I32)[None, :]
    g_out = jnp.where(valid, src_row // SUBLANES, g_scr).astype(I32)
    return block_e, n_used, g_in, g_out


def _moe_kernel(be_ref, nu_ref, gin_ref, gnext_ref, gout_ref, xs_hbm, wgu_ref, bgu_ref, wd_ref, bd_ref,
                eo_hbm, xbuf, obuf, isem, osem):
    j = pl.program_id(0)
    nu = nu_ref[0]
    f = wd_ref.shape[0]
    slot = lax.rem(j, 2)
    other = 1 - slot

    def in_copy(tbl_ref, r, sl):
        row = pl.multiple_of(tbl_ref[0, r] * SUBLANES, SUBLANES)
        return pltpu.make_async_copy(xs_hbm.at[pl.ds(row, SUBLANES), :],
                                     xbuf.at[sl, pl.ds(r * SUBLANES, SUBLANES), :], isem.at[sl])

    def out_copy(r, sl):
        row = pl.multiple_of(gout_ref[0, r] * SUBLANES, SUBLANES)
        return pltpu.make_async_copy(obuf.at[sl, pl.ds(r * SUBLANES, SUBLANES), :],
                                     eo_hbm.at[pl.ds(row, SUBLANES), :], osem.at[sl])

    def wait_in(sl):
        pltpu.make_async_copy(xs_hbm.at[pl.ds(0, MOE_TILE), :], xbuf.at[sl], isem.at[sl]).wait()

    def wait_out(sl):
        pltpu.make_async_copy(obuf.at[sl], eo_hbm.at[pl.ds(0, MOE_TILE), :], osem.at[sl]).wait()

    @pl.when(j == 0)
    def _():
        for r in range(MOE_GROUPS):
            in_copy(gin_ref, r, 0).start()

    @pl.when(j + 1 < nu)
    def _():
        for r in range(MOE_GROUPS):
            in_copy(gnext_ref, r, other).start()

    @pl.when(j < nu)
    def _():
        wait_in(slot)

        @pl.when(j >= 2)
        def _():
            wait_out(slot)

        x = _unpack_bf16_pairs(xbuf[slot])
        gu = jnp.dot(x, wgu_ref[...], preferred_element_type=F32) + bgu_ref[...]
        gate = jnp.minimum(gu[:, :f], SWIGLU_LIMIT)
        up = jnp.clip(gu[:, f:], -SWIGLU_LIMIT, SWIGLU_LIMIT)
        act = gate * jax.nn.sigmoid(SWIGLU_ALPHA * gate) * (up + 1.0)
        out = jnp.dot(act.astype(BF16), wd_ref[...], preferred_element_type=F32) + bd_ref[...]
        obuf[slot] = _pack_bf16_pairs(out.astype(BF16).astype(F32))
        for r in range(MOE_GROUPS):
            out_copy(r, slot).start()

        @pl.when(j == nu - 1)
        def _():
            wait_out(slot)

            @pl.when(j >= 1)
            def _():
                wait_out(other)


def _moe(block_e, n_used, g_in, g_out, xsorted, wgu, bgu, wd, bd):
    n_blocks = block_e.shape[0]
    dh = xsorted.shape[1]
    d = 2 * dh
    f2 = wgu.shape[2]
    f = wd.shape[1]
    g_in3 = g_in.reshape(n_blocks, 1, MOE_GROUPS)
    g_next3 = jnp.concatenate([g_in3[1:], g_in3[:1]], axis=0)
    g_out3 = g_out.reshape(n_blocks, 1, MOE_GROUPS)
    tbl = pl.BlockSpec((None, 1, MOE_GROUPS), lambda j, be, nu: (j, 0, 0), memory_space=pltpu.SMEM)
    return pl.pallas_call(
        _moe_kernel,
        grid_spec=pltpu.PrefetchScalarGridSpec(
            num_scalar_prefetch=2, grid=(n_blocks,),
            in_specs=[tbl, tbl, tbl,
                      pl.BlockSpec(memory_space=pl.ANY),
                      pl.BlockSpec((None, d, f2), lambda j, be, nu: (be[j], 0, 0)),
                      pl.BlockSpec((None, 1, f2), lambda j, be, nu: (be[j], 0, 0)),
                      pl.BlockSpec((None, f, d), lambda j, be, nu: (be[j], 0, 0)),
                      pl.BlockSpec((None, 1, d), lambda j, be, nu: (be[j], 0, 0))],
            out_specs=pl.BlockSpec(memory_space=pl.ANY),
            scratch_shapes=[pltpu.VMEM((2, MOE_TILE, dh), U32), pltpu.VMEM((2, MOE_TILE, dh), U32),
                            pltpu.SemaphoreType.DMA((2,)), pltpu.SemaphoreType.DMA((2,))]),
        out_shape=jax.ShapeDtypeStruct(xsorted.shape, U32),
        input_output_aliases={5: 0},
        compiler_params=_cparams(),
        name="moe",
    )(block_e, n_used, g_in3, g_next3, g_out3, xsorted, wgu, bgu, wd, bd)


def _combine_kernel(meta_ref, x1_ref, g_ref, eo_ref, y_ref):
    tm = x1_ref.shape[0]
    p_rows = eo_ref.shape[0]
    lane = lax.broadcasted_iota(I32, (tm, SORT_CHUNK), 1).astype(F32).astype(BF16)
    ws = [meta_ref[:, k:k + 1].astype(BF16) for k in range(TOP_K)]
    pos = [meta_ref[:, TOP_K + k:TOP_K + k + 1] for k in range(TOP_K)]
    acc = x1_ref[...]
    for r in range(p_rows // SORT_CHUNK):
        buf = _unpack_bf16_pairs(eo_ref[r * SORT_CHUNK:(r + 1) * SORT_CHUNK, :])
        lo_r = float(r * SORT_CHUNK)
        wm = jnp.zeros((tm, SORT_CHUNK), BF16)
        for k in range(TOP_K):
            rel = jnp.where((pos[k] >= lo_r) & (pos[k] < lo_r + SORT_CHUNK), pos[k] - lo_r, -1.0).astype(BF16)
            wm = jnp.where(lane == rel, ws[k], wm)
        acc = acc + jnp.dot(wm, buf, preferred_element_type=F32)
    y_ref[...] = _rms(acc, g_ref[...])


def _combine(meta, x1, g_final, eo, tile_off, n_tokens):
    d = x1.shape[1]
    tm = TOKEN_TILE
    p_rows = _sorted_rows(tm)
    return pl.pallas_call(
        _combine_kernel,
        grid=(n_tokens // tm,),
        in_specs=[pl.BlockSpec((tm, LANES), lambda i: (i + tile_off, 0)),
                  pl.BlockSpec((tm, d), lambda i: (i + tile_off, 0)),
                  pl.BlockSpec((1, d), lambda i: (0, 0)),
                  pl.BlockSpec((p_rows, d // 2), lambda i: (i + tile_off, 0))],
        out_specs=pl.BlockSpec((tm, d), lambda i: (i, 0)),
        out_shape=jax.ShapeDtypeStruct((n_tokens, d), F32),
        compiler_params=_cparams(),
        name="combine",
    )(meta, x1, g_final, eo)


def kernel(x_prompt, x_sample, g_mix, w_in, ssm_lam_re, ssm_lam_im, ssm_log_dt, ssm_b_re, ssm_b_im,
           ssm_c_re, ssm_c_im, ssm_d, w_glu, b_glu, sgu_ln_g, sgu_ln_b, sgu_w_s, sgu_b_s,
           g_out_ssm, g_out_sgu, w_out, g_ffn, w_router, b_router, w_gate_up, b_gate_up,
           w_down, b_down, g_final):
    assert g_mix.shape[0] == 1, "single-layer trunk"
    bp, lp, d = x_prompt.shape
    bs, ls, _ = x_sample.shape
    tp, ts = bp * lp, bs * ls
    t = tp + ts
    d_ssm = ssm_d.shape[1]
    d_sgu = sgu_ln_g.shape[1]
    n_exp = w_router.shape[2]
    assert lp % TOKEN_TILE == 0 and ls % TOKEN_TILE == 0 and TOKEN_TILE % CHUNK == 0
    assert SSM_CHUNK * SSM_GROUP == MXU_DIM and 2 * ssm_lam_re.shape[-1] == LANES
    assert d_sgu // SGU_HEADS == LANES // 2 and n_exp <= LANES
    assert n_exp * (SUBLANES - 1) <= MXU_DIM and _sorted_rows(TOKEN_TILE) % SORT_CHUNK == 0
    assert (2 * MOE_GROUPS + 1) * SUBLANES <= _sorted_rows(TOKEN_TILE)

    xp = x_prompt.reshape(tp, d)
    xs = x_sample.reshape(ts, d)
    row = lambda a: a.reshape(1, -1).astype(F32)

    w_in_b = w_in[0].astype(BF16)
    u_p, zuv_p = _inproj(xp, row(g_mix[0]), w_in_b, d_ssm, bp, lp)
    u_s, zuv_s = _inproj(xs, row(g_mix[0]), w_in_b, d_ssm, bs, ls)

    mats = _ssm_matrices(ssm_lam_re[0], ssm_lam_im[0], ssm_log_dt[0], ssm_b_re[0], ssm_b_im[0],
                         ssm_c_re[0], ssm_c_im[0], ssm_d[0])
    y_p = _ssm_trunk(u_p, bp, mats)
    y_s = _ssm_trunk(u_s, bs, mats)

    ws = sgu_w_s[0]
    ws_pairs = jnp.concatenate([ws[0::2], ws[1::2]], axis=2).astype(BF16)
    bias_s = jnp.repeat(sgu_b_s[0].T, d_sgu // SGU_HEADS, axis=1).astype(F32)
    mixed = _mix(y_p, y_s, zuv_p, zuv_s, lp, ls, w_glu[0].astype(BF16), row(b_glu[0]), row(g_out_ssm[0]),
                 row(sgu_ln_g[0]), row(sgu_ln_b[0]), ws_pairs, bias_s, row(g_out_sgu[0]))

    wr_pad = jnp.pad(w_router[0].astype(F32), ((0, 0), (0, LANES - n_exp)))
    wr_hi = wr_pad.astype(BF16)
    wr_lo = (wr_pad - wr_hi.astype(F32)).astype(BF16)
    br_pad = jnp.pad(b_router[0].astype(F32), (0, LANES - n_exp)).reshape(1, LANES)
    x1, xsorted, meta, c8 = _route(xp, xs, mixed, w_out[0].astype(BF16), row(g_ffn[0]),
                                   wr_hi, wr_lo, br_pad, n_exp)

    nt = t // TOKEN_TILE
    p_rows = _sorted_rows(TOKEN_TILE)
    max_rows = t * TOP_K + nt * n_exp * (SUBLANES - 1) + n_exp * (MOE_TILE - 1)
    n_blocks = -(-max_rows // MOE_TILE)
    block_e, n_used, g_in, g_out = _slot_tables(c8[:, :, 0].astype(I32), p_rows, n_blocks)
    eo = _moe(block_e, n_used.reshape(1), g_in, g_out, xsorted, w_gate_up[0].astype(BF16),
              b_gate_up[0][:, None, :].astype(F32), w_down[0].astype(BF16),
              b_down[0][:, None, :].astype(F32))

    gf = row(g_final)
    y_prompt = _combine(meta, x1, gf, eo, 0, tp)
    y_sample = _combine(meta, x1, gf, eo, tp // TOKEN_TILE, ts)
    return y_prompt.reshape(bp, lp, d), y_sample.reshape(bs, ls, d)
```

```python
import functools
import math

import jax
import jax.numpy as jnp
from jax import lax
from jax.experimental import pallas as pl
from jax.experimental.pallas import tpu as pltpu

F32 = jnp.float32
BF16 = jnp.bfloat16
I32 = jnp.int32
U32 = jnp.uint32

SSM_GROUP = 16
SGU_HEADS = 8
CHUNK = 128
TOP_K = 4
SWIGLU_LIMIT = 7.0
SWIGLU_ALPHA = 1.702
RMS_EPS = 1e-6
LN_EPS = 1e-5

LANES = 128
SUBLANES = 8
MXU_DIM = 256
SSM_CHUNK = MXU_DIM // SSM_GROUP
PAIR_W = 2 * SSM_CHUNK * SSM_GROUP
ATOM = 2 * SSM_GROUP
ATOMS = LANES // ATOM

TOKEN_TILE = 512
MOE_TILE = 512
MOE_GROUPS = MOE_TILE // SUBLANES
SORT_CHUNK = 256
CAST_ROWS = 128
SSM_ROW_TILE = 1024
SCAN_SEQS = 4
VMEM_LIMIT = 56 * 1024 * 1024


def _cparams(n_axes=1, vmem=None):
    return pltpu.CompilerParams(
        dimension_semantics=("arbitrary",) * n_axes,
        vmem_limit_bytes=vmem if vmem is not None else VMEM_LIMIT,
    )


def _rms(x, g):
    return x * lax.rsqrt(jnp.mean(x * x, axis=-1, keepdims=True) + RMS_EPS) * g


def _gelu(x):
    return x * (lax.erf(x * (1.0 / math.sqrt(2.0))) + 1.0) * 0.5


def _atom_masks(rows):
    lane = lax.broadcasted_iota(I32, (rows, LANES), 1)
    return [(lane >= a * ATOM) & (lane < (a + 1) * ATOM) for a in range(ATOMS)]


def _atom_transpose(src, masks):
    dst = []
    for b in range(ATOMS):
        acc = None
        for a in range(ATOMS):
            r = (a - b) % ATOMS
            piece = src[a] if r == 0 else pltpu.roll(src[a], ATOM * r, axis=1)
            acc = piece if acc is None else jnp.where(masks[a], piece, acc)
        dst.append(acc)
    return dst


def _pack_bf16_pairs(x):
    w = x.shape[1] // 2
    lo = lax.bitcast_convert_type(x[:, :w], U32) >> 16
    hi = lax.bitcast_convert_type(x[:, w:], U32) & jnp.uint32(0xFFFF0000)
    return hi | lo


def _unpack_bf16_pairs(u):
    lo = lax.bitcast_convert_type(u << 16, F32)
    hi = lax.bitcast_convert_type(u & jnp.uint32(0xFFFF0000), F32)
    return jnp.concatenate([lo.astype(BF16), hi.astype(BF16)], axis=1)


def _inproj_kernel(x_ref, g_ref, w_ref, u_ref, zuv_ref, za_scr, *, d_ssm):
    tm = x_ref.shape[0]
    nc = tm // SSM_CHUNK
    h = _rms(x_ref[...], g_ref[...])
    z = jnp.dot(h.astype(BF16), w_ref[...], preferred_element_type=F32)
    zuv_ref[...] = z[:, d_ssm:].astype(BF16)
    n_blk = d_ssm // LANES
    for b in range(n_blk):
        za_scr[b] = z[:, b * LANES:(b + 1) * LANES]
    masks = _atom_masks(nc)
    n_quads = SSM_CHUNK // ATOMS
    for b in range(n_blk):
        for v in range(n_quads):
            src = [za_scr[b, pl.ds(ATOMS * v + jj, nc, stride=SSM_CHUNK), :] for jj in range(ATOMS)]
            dst = _atom_transpose(src, masks)
            for pi in range(ATOMS):
                c0 = (ATOMS * b + pi) * PAIR_W + v * LANES
                u_ref[:, c0:c0 + LANES] = dst[pi].astype(BF16)


def _inproj(x, g_mix, w_in, d_ssm):
    t, d = x.shape
    tm = TOKEN_TILE
    d_in = w_in.shape[1]
    nc = tm // SSM_CHUNK
    uw = d_ssm * SSM_CHUNK
    return pl.pallas_call(
        functools.partial(_inproj_kernel, d_ssm=d_ssm),
        grid=(t // tm,),
        in_specs=[pl.BlockSpec((tm, d), lambda i: (i, 0)),
                  pl.BlockSpec((1, d), lambda i: (0, 0)),
                  pl.BlockSpec((d, d_in), lambda i: (0, 0))],
        out_specs=[pl.BlockSpec((nc, uw), lambda i: (i, 0)),
                   pl.BlockSpec((tm, d_in - d_ssm), lambda i: (i, 0))],
        out_shape=[jax.ShapeDtypeStruct((t // SSM_CHUNK, uw), BF16),
                   jax.ShapeDtypeStruct((t, d_in - d_ssm), BF16)],
        scratch_shapes=[pltpu.VMEM((d_ssm // LANES, tm, LANES), F32)],
        compiler_params=_cparams(),
        name="inproj",
    )(x, g_mix, w_in)


def _ssm_matrices(lam_re, lam_im, log_dt, b_re, b_im, c_re, c_im, d_skip):
    hp = lax.Precision.HIGHEST
    _, g, n = lam_re.shape
    p = b_re.shape[-1]
    lc = SSM_CHUNK
    np_ = g // 2
    dt = jnp.exp(log_dt)[..., None]
    mag = jnp.exp(lam_re * dt)
    ar = mag * jnp.cos(lam_im * dt)
    ai = mag * jnp.sin(lam_im * dt)
    den = lam_re * lam_re + lam_im * lam_im
    nr = ar - 1.0
    fr = (nr * lam_re + ai * lam_im) / den
    fi = (ai * lam_re - nr * lam_im) / den
    bbr = fr[..., None] * b_re - fi[..., None] * b_im
    bbi = fr[..., None] * b_im + fi[..., None] * b_re

    prs, pis = [jnp.ones_like(ar)], [jnp.zeros_like(ai)]
    for _ in range(lc):
        pr, pi = prs[-1], pis[-1]
        prs.append(pr * ar - pi * ai)
        pis.append(pr * ai + pi * ar)
    pw_r, pw_i = jnp.stack(prs), jnp.stack(pis)

    car = c_re[None] * pw_r[:lc, :, :, None, :] - c_im[None] * pw_i[:lc, :, :, None, :]
    cai = c_re[None] * pw_i[:lc, :, :, None, :] + c_im[None] * pw_r[:lc, :, :, None, :]
    kk = (jnp.einsum('tdgpn,dgnq->tdgpq', car, bbr, precision=hp)
          - jnp.einsum('tdgpn,dgnq->tdgpq', cai, bbi, precision=hp))
    kf, kb = kk[:, 0], kk[:, 1]
    eye = jnp.eye(2, dtype=F32)
    a2 = p * 2

    k_all = jnp.concatenate([kb[:0:-1], (kf[0] + kb[0])[None], kf[1:]], axis=0)
    k_all = jnp.transpose(k_all, (1, 3, 0, 2)).reshape(np_, 2, p, 2 * lc - 1, p)
    kflat = jnp.einsum('nsqup,st->nsqutp', k_all, eye).reshape(np_, a2, (2 * lc - 1) * a2)
    kflat = jnp.pad(kflat, ((0, 0), (0, 0), (0, a2)))

    def atoms_q(x):
        x = jnp.transpose(x, (0, 1, 3, 2)).reshape(2, np_, 2, p, n)
        return jnp.einsum('dnsqm,st->dnsqtm', x, eye).reshape(2, np_, a2, 2 * n)
    bq_r, bq_i = atoms_q(bbr), atoms_q(bbi)
    bq = jnp.stack([bq_r[0], bq_i[0], bq_r[1], bq_i[1]], axis=1)
    ef = lc - 1 - jnp.arange(lc)
    eb = jnp.arange(lc)
    rows_q = lambda w, e, d: jnp.transpose(w[e, d].reshape(lc, np_, 2 * n), (1, 0, 2))
    wq = jnp.stack([rows_q(pw_r, ef, 0), rows_q(pw_i, ef, 0),
                    rows_q(pw_r, eb, 1), rows_q(pw_i, eb, 1)], axis=1)

    def atoms_p(x):
        x = jnp.transpose(x, (0, 1, 3, 2)).reshape(2, np_, 2, n, p)
        return jnp.einsum('dnsmp,st->dnsmtp', x, eye).reshape(2, np_, 2 * n, a2)
    cp_r, cp_i = atoms_p(c_re), atoms_p(c_im)
    cp = jnp.stack([cp_r[0], cp_i[0], cp_r[1], cp_i[1]], axis=1)
    pf_e = jnp.arange(lc) + 1
    pb_e = lc - jnp.arange(lc)
    cols_p = lambda w, e, d: jnp.transpose(w[e, d].reshape(lc, np_, 2 * n), (1, 2, 0))
    wp = jnp.stack([cols_p(pw_r, pf_e, 0), cols_p(pw_i, pf_e, 0),
                    cols_p(pw_r, pb_e, 1), cols_p(pw_i, pb_e, 1)], axis=1)
    lane = jnp.arange(lc * a2)
    til = (lane[None, :] % a2 == jnp.arange(a2)[:, None]).astype(BF16)
    rep = (lane[None, :] // a2 == jnp.arange(lc)[:, None]).astype(BF16)

    al = jnp.stack([pw_r[lc, 0], pw_i[lc, 0], pw_r[lc, 1], pw_i[lc, 1]])
    alpha = jnp.transpose(al.reshape(4, np_, 2 * n), (1, 0, 2))
    skip = jnp.broadcast_to(d_skip.reshape(np_, 1, a2), (np_, lc, a2)).reshape(np_, 1, lc * a2)
    return kflat, bq, wq, cp, wp, til, rep, alpha, skip.astype(F32)


def _ssm_v_kernel(u_ref, bq_ref, wq_ref, v_ref, q_scr):
    a2 = bq_ref.shape[1]
    w = bq_ref.shape[2]

    @pl.when(pl.program_id(1) == 0)
    def _():
        for d in range(2):
            br, bi = bq_ref[2 * d], bq_ref[2 * d + 1]
            for j in range(SSM_CHUNK):
                wr = wq_ref[2 * d, j:j + 1, :]
                wi = wq_ref[2 * d + 1, j:j + 1, :]
                q_scr[j * a2:(j + 1) * a2, (2 * d) * w:(2 * d + 1) * w] = (wr * br - wi * bi).astype(BF16)
                q_scr[j * a2:(j + 1) * a2, (2 * d + 1) * w:(2 * d + 2) * w] = (wr * bi + wi * br).astype(BF16)

    v_ref[...] = jnp.dot(u_ref[...], q_scr[...], preferred_element_type=F32)


def _ssm_scan_kernel(v_ref, a_ref, s_ref, *, n_chunks, bsz):
    w = LANES
    sub = SUBLANES
    n_groups = n_chunks // sub
    row = lax.broadcasted_iota(I32, (sub, w), 0)

    def cmul(ar, ai, xr, xi):
        return ar * xr - ai * xi, ar * xi + ai * xr

    def powers(ar, ai):
        p2 = cmul(ar, ai, ar, ai)
        p4 = cmul(*p2, *p2)
        p8 = cmul(*p4, *p4)
        p3 = cmul(*p2, ar, ai)
        p5 = cmul(*p4, ar, ai)
        p6 = cmul(*p4, *p2)
        p7 = cmul(*p4, *p3)
        seq = [(jnp.ones_like(ar), jnp.zeros_like(ai)), (ar, ai), p2, p3, p4, p5, p6, p7]
        tr = jnp.concatenate([s[0] for s in seq], axis=0)
        ti = jnp.concatenate([s[1] for s in seq], axis=0)
        return (ar, ai), p2, p4, p8, (tr, ti)

    def tile_scan(vr, vi, pw, reverse):
        a1, a2, a4, _, _ = pw

        def shift(x, s):
            if reverse:
                return jnp.where(row < sub - s, pltpu.roll(x, sub - s, axis=0), 0.0)
            return jnp.where(row >= s, pltpu.roll(x, s, axis=0), 0.0)

        xr, xi = shift(vr, 1), shift(vi, 1)
        for s, (ar, ai) in ((1, a1), (2, a2), (4, a4)):
            mr, mi = cmul(ar, ai, shift(xr, s), shift(xi, s))
            xr, xi = xr + mr, xi + mi
        e = 0 if reverse else sub - 1
        lr, li = cmul(a1[0], a1[1], xr[e:e + 1], xi[e:e + 1])
        return xr, xi, lr + vr[e:e + 1], li + vi[e:e + 1]

    pw_f = powers(a_ref[0:1, :], a_ref[1:2, :])
    pw_b = powers(a_ref[2:3, :], a_ref[3:4, :])
    tbr = jnp.concatenate([pw_b[4][0][sub - 1 - k:sub - k] for k in range(sub)], axis=0)
    tbi = jnp.concatenate([pw_b[4][1][sub - 1 - k:sub - k] for k in range(sub)], axis=0)

    def step(g, carry, b0):
        out = []
        for b in range(b0, b0 + len(carry) // 4):
            cfr, cfi, cbr, cbi = carry[4 * (b - b0):4 * (b - b0) + 4]
            rf = pl.multiple_of(b * n_chunks + g * sub, sub)
            rb = pl.multiple_of(b * n_chunks + (n_groups - 1 - g) * sub, sub)
            xr, xi, lr, li = tile_scan(v_ref[pl.ds(rf, sub), 0:w], v_ref[pl.ds(rf, sub), w:2 * w], pw_f, False)
            mr, mi = cmul(pw_f[4][0], pw_f[4][1], cfr, cfi)
            s_ref[pl.ds(rf, sub), 0:w] = xr + mr
            s_ref[pl.ds(rf, sub), w:2 * w] = xi + mi
            nr, ni = cmul(pw_f[3][0], pw_f[3][1], cfr, cfi)
            out += [nr + lr, ni + li]
            xr, xi, lr, li = tile_scan(v_ref[pl.ds(rb, sub), 2 * w:3 * w], v_ref[pl.ds(rb, sub), 3 * w:4 * w],
                                       pw_b, True)
            mr, mi = cmul(tbr, tbi, cbr, cbi)
            s_ref[pl.ds(rb, sub), 2 * w:3 * w] = xr + mr
            s_ref[pl.ds(rb, sub), 3 * w:4 * w] = xi + mi
            nr, ni = cmul(pw_b[3][0], pw_b[3][1], cbr, cbi)
            out += [nr + lr, ni + li]
        return tuple(out)

    zero = jnp.zeros((1, w), F32)
    for b0 in range(0, bsz, SCAN_SEQS):
        nb = min(SCAN_SEQS, bsz - b0)
        lax.fori_loop(0, n_groups, functools.partial(step, b0=b0), (zero,) * (4 * nb))


def _ssm_y_kernel(u_ref, s_ref, kflat_ref, cp_ref, wp_ref, til_ref, rep_ref, d_ref, y_ref, m_scr, p_scr):
    a2 = kflat_ref.shape[0]
    w = cp_ref.shape[1]

    def expand(x, e_ref):
        hi = x.astype(BF16)
        lo = (x - hi.astype(F32)).astype(BF16)
        return (jnp.dot(hi, e_ref[...], preferred_element_type=F32)
                + jnp.dot(lo, e_ref[...], preferred_element_type=F32))

    @pl.when(pl.program_id(1) == 0)
    def _():
        kflat = kflat_ref[...]
        for j in range(SSM_CHUNK):
            off = (SSM_CHUNK - 1 - j) * a2
            m_scr[j * a2:(j + 1) * a2, :] = kflat[:, off:off + PAIR_W].astype(BF16)
        for d in range(2):
            cr, ci = expand(cp_ref[2 * d], til_ref), expand(cp_ref[2 * d + 1], til_ref)
            wr, wi = expand(wp_ref[2 * d], rep_ref), expand(wp_ref[2 * d + 1], rep_ref)
            p_scr[(2 * d) * w:(2 * d + 1) * w, :] = (cr * wr - ci * wi).astype(BF16)
            p_scr[(2 * d + 1) * w:(2 * d + 2) * w, :] = (-(cr * wi + ci * wr)).astype(BF16)

    u = u_ref[...]
    y = jnp.dot(u, m_scr[...], preferred_element_type=F32)
    y = y + jnp.dot(s_ref[...].astype(BF16), p_scr[...], preferred_element_type=F32)
    y_ref[...] = (y + d_ref[...] * u.astype(F32)).astype(BF16)


def _ssm_trunk(u, bsz, mats):
    kflat, bq, wq, cp, wp, til, rep, alpha, skip = mats
    rows = u.shape[0]
    nc = rows // bsz
    np_ = kflat.shape[0]
    pw = PAIR_W
    sw = 4 * bq.shape[3]
    tr = min(SSM_ROW_TILE, rows)
    per_pair = lambda a: pl.BlockSpec((None,) + a.shape[1:], lambda p, r: (p,) + (0,) * (a.ndim - 1))
    whole = lambda a: pl.BlockSpec(a.shape, lambda p, r: (0,) * a.ndim)
    v = pl.pallas_call(
        _ssm_v_kernel,
        grid=(np_, rows // tr),
        in_specs=[pl.BlockSpec((tr, pw), lambda p, r: (r, p)), per_pair(bq), per_pair(wq)],
        out_specs=pl.BlockSpec((tr, sw), lambda p, r: (r, p)),
        out_shape=jax.ShapeDtypeStruct((rows, np_ * sw), F32),
        scratch_shapes=[pltpu.VMEM((pw, sw), BF16)],
        compiler_params=_cparams(2),
        name="ssm_v",
    )(u, bq, wq)
    s = pl.pallas_call(
        functools.partial(_ssm_scan_kernel, n_chunks=nc, bsz=bsz),
        grid=(np_,),
        in_specs=[pl.BlockSpec((rows, sw), lambda p: (0, p)),
                  pl.BlockSpec((None, 4, LANES), lambda p: (p, 0, 0))],
        out_specs=pl.BlockSpec((rows, sw), lambda p: (0, p)),
        out_shape=jax.ShapeDtypeStruct((rows, np_ * sw), F32),
        compiler_params=_cparams(1),
        name="ssm_scan",
    )(v, alpha)
    y = pl.pallas_call(
        _ssm_y_kernel,
        grid=(np_, rows // tr),
        in_specs=[pl.BlockSpec((tr, pw), lambda p, r: (r, p)),
                  pl.BlockSpec((tr, sw), lambda p, r: (r, p)),
                  per_pair(kflat), per_pair(cp), per_pair(wp), whole(til), whole(rep), per_pair(skip)],
        out_specs=pl.BlockSpec((tr, pw), lambda p, r: (r, p)),
        out_shape=jax.ShapeDtypeStruct((rows, np_ * pw), BF16),
        scratch_shapes=[pltpu.VMEM((pw, pw), BF16), pltpu.VMEM((sw, pw), BF16)],
        compiler_params=_cparams(2),
        name="ssm_y",
    )(u, s, kflat, cp, wp, til, rep, skip)
    return y


def _mix_kernel(yp_ref, ys_ref, zp_ref, zs_ref, wglu_ref, bglu_ref, gssm_ref,
                lng_ref, lnb_ref, ws_ref, bs_ref, gsgu_ref, o_ref, ya_scr, *, n_prompt_tiles):
    i = pl.program_id(0)
    tm = o_ref.shape[0]
    n_blk = ya_scr.shape[0]
    d_ssm = n_blk * LANES
    nc = tm // SSM_CHUNK

    def body(y_ref, zuv_ref):
        masks = _atom_masks(nc)
        for b in range(n_blk):
            for v in range(SSM_CHUNK // ATOMS):
                src = [y_ref[:, (ATOMS * b + pi) * PAIR_W + v * LANES:
                             (ATOMS * b + pi) * PAIR_W + (v + 1) * LANES].astype(F32)
                       for pi in range(ATOMS)]
                dst = _atom_transpose(src, masks)
                for jj in range(ATOMS):
                    ya_scr[b, pl.ds(ATOMS * v + jj, nc, stride=SSM_CHUNK), :] = dst[jj]
        y = jnp.concatenate([ya_scr[b] for b in range(n_blk)], axis=1)
        gl = _gelu(y)
        gate = jnp.dot(gl.astype(BF16), wglu_ref[...], preferred_element_type=F32) + bglu_ref[...]
        o_ref[:, :d_ssm] = _rms(gl * jax.nn.sigmoid(gate), gssm_ref[...]).astype(BF16)
        d_sgu = zuv_ref.shape[1] // 2
        u = _gelu(zuv_ref[:, :d_sgu].astype(F32))
        gv = _gelu(zuv_ref[:, d_sgu:].astype(F32))
        xc = gv - jnp.mean(gv, axis=-1, keepdims=True)
        v = xc * lax.rsqrt(jnp.mean(xc * xc, axis=-1, keepdims=True) + LN_EPS)
        v = (v * lng_ref[...] + lnb_ref[...]).astype(BF16)
        lo = lax.broadcasted_iota(I32, (CHUNK, LANES), 1) < (LANES // 2)
        zero = jnp.zeros((CHUNK, LANES), BF16)
        rows = []
        for c in range(tm // CHUNK):
            cols = []
            for j in range(d_sgu // LANES):
                vp = v[c * CHUNK:(c + 1) * CHUNK, j * LANES:(j + 1) * LANES]
                rhs = jnp.concatenate([jnp.where(lo, vp, zero), jnp.where(lo, zero, vp)], axis=0)
                cols.append(jnp.dot(ws_ref[j], rhs, preferred_element_type=F32))
            rows.append(jnp.concatenate(cols, axis=1) + bs_ref[...])
        s = jnp.concatenate(rows, axis=0)
        o_ref[:, d_ssm:] = _rms(u * s, gsgu_ref[...]).astype(BF16)

    @pl.when(i < n_prompt_tiles)
    def _():
        body(yp_ref, zp_ref)

    @pl.when(i >= n_prompt_tiles)
    def _():
        body(ys_ref, zs_ref)


def _mix(y_p, y_s, zuv_p, zuv_s, wglu, bglu, gssm, lng, lnb, ws_pairs, bias_s, gsgu):
    tp, ts = zuv_p.shape[0], zuv_s.shape[0]
    d_sgu = zuv_p.shape[1] // 2
    d_ssm = wglu.shape[0]
    tm = TOKEN_TILE
    npt = tp // tm
    nc = tm // SSM_CHUNK
    uw = d_ssm * SSM_CHUNK
    ip = lambda i: jnp.minimum(i, npt - 1)
    isamp = lambda i: jnp.maximum(i - npt, 0)
    row = lambda w: pl.BlockSpec((1, w), lambda i: (0, 0))
    return pl.pallas_call(
        functools.partial(_mix_kernel, n_prompt_tiles=npt),
        grid=((tp + ts) // tm,),
        in_specs=[pl.BlockSpec((nc, uw), lambda i: (ip(i), 0)),
                  pl.BlockSpec((nc, uw), lambda i: (isamp(i), 0)),
                  pl.BlockSpec((tm, 2 * d_sgu), lambda i: (ip(i), 0)),
                  pl.BlockSpec((tm, 2 * d_sgu), lambda i: (isamp(i), 0)),
                  pl.BlockSpec((d_ssm, d_ssm), lambda i: (0, 0)),
                  row(d_ssm), row(d_ssm), row(d_sgu), row(d_sgu),
                  pl.BlockSpec(ws_pairs.shape, lambda i: (0, 0, 0)),
                  pl.BlockSpec(bias_s.shape, lambda i: (0, 0)),
                  row(d_sgu)],
        out_specs=pl.BlockSpec((tm, d_ssm + d_sgu), lambda i: (i, 0)),
        out_shape=jax.ShapeDtypeStruct((tp + ts, d_ssm + d_sgu), BF16),
        scratch_shapes=[pltpu.VMEM((d_ssm // LANES, tm, LANES), F32)],
        compiler_params=_cparams(),
        name="mix",
    )(y_p, y_s, zuv_p, zuv_s, wglu, bglu, gssm, lng, lnb, ws_pairs, bias_s, gsgu)


def _sorted_rows(tile):
    return TOP_K * tile + MXU_DIM


def _route_kernel(xp_ref, xs_ref, mix_ref, wout_ref, gffn_ref, whi_ref, wlo_ref, br_ref, tri_ref,
                  ltri_ref, x1_ref, xsort_ref, meta_ref, c8_ref, *, n_prompt_tiles, n_tiles, n_exp):
    i = pl.program_id(0)
    tm = mix_ref.shape[0]
    p_rows = xsort_ref.shape[0]

    def body(x_ref):
        x1 = x_ref[...] + jnp.dot(mix_ref[...], wout_ref[...], preferred_element_type=F32)
        x1_ref[...] = x1
        h2 = _rms(x1, gffn_ref[...])
        hi = h2.astype(BF16)
        lo = (h2 - hi.astype(F32)).astype(BF16)
        lt = (jnp.dot(hi, whi_ref[...], preferred_element_type=F32)
              + (jnp.dot(lo, whi_ref[...], preferred_element_type=F32)
                 + jnp.dot(hi, wlo_ref[...], preferred_element_type=F32))) + br_ref[...]
        logits = lt.T[:n_exp]
        eio = lax.broadcasted_iota(I32, (n_exp, tm), 0)
        vals, idxs = [], []
        l = logits
        for _ in range(TOP_K):
            m = jnp.max(l, axis=0, keepdims=True)
            idx = jnp.min(jnp.where(l == m, eio, n_exp), axis=0, keepdims=True)
            vals.append(m)
            idxs.append(idx)
            l = jnp.where(eio == idx, -jnp.inf, l)
        ex = [jnp.exp(v - vals[0]) for v in vals]
        den = ex[0] + ex[1] + ex[2] + ex[3]
        ws = [e / den for e in ex]
        hot = [eio == idx for idx in idxs]
        cnt = sum(h.astype(F32) for h in hot)
        prefix = jnp.dot(cnt.astype(BF16), tri_ref[...], preferred_element_type=F32)
        c = jnp.sum(cnt, axis=1, keepdims=True)
        c8 = jnp.floor((c + (SUBLANES - 1)) * (1.0 / SUBLANES)) * SUBLANES
        c8b = jnp.broadcast_to(c8, (n_exp, LANES))
        c8_ref[...] = c8b
        run0 = jnp.dot(ltri_ref[...], c8b.astype(BF16), preferred_element_type=F32)[:, 0:1]
        base = run0 + prefix
        pos = [jnp.sum(jnp.where(h, base, 0.0), axis=0, keepdims=True) for h in hot]
        meta = jnp.concatenate(ws + pos + [jnp.zeros((LANES - 2 * TOP_K, tm), F32)], axis=0)
        meta_ref[...] = meta.T
        rio = lax.broadcasted_iota(I32, (SORT_CHUNK, tm), 0).astype(F32).astype(BF16)
        one = jnp.ones((SORT_CHUNK, tm), BF16)
        zero = jnp.zeros((SORT_CHUNK, tm), BF16)
        for r in range(p_rows // SORT_CHUNK):
            lo_r = float(r * SORT_CHUNK)
            rel = [jnp.where((p >= lo_r) & (p < lo_r + SORT_CHUNK), p - lo_r, -1.0).astype(BF16) for p in pos]
            sel = (rio == rel[0]) | (rio == rel[1]) | (rio == rel[2]) | (rio == rel[3])
            srt = jnp.dot(jnp.where(sel, one, zero), hi, preferred_element_type=F32)
            xsort_ref[r * SORT_CHUNK:(r + 1) * SORT_CHUNK, :] = _pack_bf16_pairs(srt)

    @pl.when(i < n_prompt_tiles)
    def _():
        body(xp_ref)

    @pl.when((i >= n_prompt_tiles) & (i < n_tiles))
    def _():
        body(xs_ref)

    @pl.when(i == n_tiles)
    def _():
        xsort_ref[...] = jnp.zeros_like(xsort_ref)


def _dual_specs(tile, width, n_prompt_tiles, n_tiles):
    last_p = n_prompt_tiles - 1
    last_s = n_tiles - n_prompt_tiles - 1
    sp = pl.BlockSpec((tile, width), lambda i: (jnp.minimum(i, last_p), 0))
    ss = pl.BlockSpec((tile, width), lambda i: (jnp.clip(i - n_prompt_tiles, 0, last_s), 0))
    return sp, ss


def _route(xp, xs, mixed, w_out, g_ffn, wr_hi, wr_lo, br_pad, n_exp):
    tp, d = xp.shape
    t = mixed.shape[0]
    tm = TOKEN_TILE
    npt = tp // tm
    nt = t // tm
    p_rows = _sorted_rows(tm)
    iota = lambda n, ax: lax.broadcasted_iota(I32, (n, n), ax)
    tri = (iota(tm, 0) < iota(tm, 1)).astype(BF16)
    ltri = (iota(n_exp, 1) < iota(n_exp, 0)).astype(BF16)
    sp, ss = _dual_specs(tm, d, npt, nt)
    const = lambda shape: pl.BlockSpec(shape, lambda i: (0,) * len(shape))
    tile = lambda i: jnp.minimum(i, nt - 1)
    return pl.pallas_call(
        functools.partial(_route_kernel, n_prompt_tiles=npt, n_tiles=nt, n_exp=n_exp),
        grid=(nt + 1,),
        in_specs=[sp, ss,
                  pl.BlockSpec((tm, mixed.shape[1]), lambda i: (tile(i), 0)),
                  const(w_out.shape), const((1, d)), const(wr_hi.shape), const(wr_lo.shape),
                  const((1, LANES)), const((tm, tm)), const((n_exp, n_exp))],
        out_specs=[pl.BlockSpec((tm, d), lambda i: (tile(i), 0)),
                   pl.BlockSpec((p_rows, d // 2), lambda i: (i, 0)),
                   pl.BlockSpec((tm, LANES), lambda i: (tile(i), 0)),
                   pl.BlockSpec((None, n_exp, LANES), lambda i: (tile(i), 0, 0))],
        out_shape=[jax.ShapeDtypeStruct((t, d), F32),
                   jax.ShapeDtypeStruct(((nt + 1) * p_rows, d // 2), U32),
                   jax.ShapeDtypeStruct((t, LANES), F32),
                   jax.ShapeDtypeStruct((nt, n_exp, LANES), F32)],
        compiler_params=_cparams(),
        name="route",
    )(xp, xs, mixed, w_out, g_ffn, wr_hi, wr_lo, br_pad, tri, ltri)


def _slot_tables(c8, p_rows, n_blocks):
    nt, n_exp = c8.shape
    run0 = jnp.cumsum(c8, axis=1) - c8
    seg_len = c8.T
    cum = jnp.cumsum(seg_len, axis=1)
    tot = cum[:, -1]
    padded = (tot + MOE_TILE - 1) // MOE_TILE * MOE_TILE
    ends = jnp.cumsum(padded)
    starts = ends - padded
    n_used = (ends[-1] // MOE_TILE).astype(I32)
    blk = jnp.arange(n_blocks, dtype=I32)
    be = jnp.minimum(jnp.sum(ends[None, :] <= (blk * MOE_TILE)[:, None], axis=1), n_exp - 1).astype(I32)
    block_e = jnp.where(blk < n_used, be, be[jnp.maximum(n_used - 1, 0)])
    grow = (blk[:, None] * MOE_TILE + jnp.arange(MOE_GROUPS, dtype=I32)[None, :] * SUBLANES)
    rel = grow - starts[be][:, None]
    cum_e = cum[be]
    tile = jnp.sum(cum_e[:, None, :] <= rel[:, :, None], axis=2)
    valid = (rel < tot[be][:, None]) & (blk < n_used)[:, None]
    tile_c = jnp.minimum(tile, nt - 1)
    seg0 = jnp.take_along_axis(cum_e - seg_len[be], tile_c, axis=1)
    src_row = tile_c * p_rows + jnp.take_along_axis(run0.T[be], tile_c, axis=1) + (rel - seg0)
    scratch0 = nt * p_rows // SUBLANES
    zero_group = scratch0 + 2 * MOE_GROUPS
    g_in = jnp.where(valid, src_row // SUBLANES, zero_group).astype(I32)
    g_scr = scratch0 + (blk[:, None] % 2) * MOE_GROUPS + jnp.arange(MOE_GROUPS, dtype=I32)[None, :]
    g_out = jnp.where(valid, src_row // SUBLANES, g_scr).astype(I32)
    return block_e, n_used, g_in, g_out


def _moe_kernel(be_ref, nu_ref, gin_ref, gnext_ref, gout_ref, xs_hbm, wgu_ref, bgu_ref, wd_ref, bd_ref,
                eo_hbm, xbuf, obuf, wgu_s, wd_s, isem, osem):
    j = pl.program_id(0)
    nu = nu_ref[0]
    f = wd_ref.shape[0]
    slot = lax.rem(j, 2)
    other = 1 - slot

    @pl.when((j < nu) & ((j == 0) | (be_ref[j] != be_ref[jnp.maximum(j - 1, 0)])))
    def _():
        for c in range(0, wgu_ref.shape[0], CAST_ROWS):
            wgu_s[c:c + CAST_ROWS, :] = wgu_ref[c:c + CAST_ROWS, :].astype(BF16)
        for c in range(0, wd_ref.shape[0], CAST_ROWS):
            wd_s[c:c + CAST_ROWS, :] = wd_ref[c:c + CAST_ROWS, :].astype(BF16)

    def in_copy(tbl_ref, r, sl):
        row = pl.multiple_of(tbl_ref[0, r] * SUBLANES, SUBLANES)
        return pltpu.make_async_copy(xs_hbm.at[pl.ds(row, SUBLANES), :],
                                     xbuf.at[sl, pl.ds(r * SUBLANES, SUBLANES), :], isem.at[sl])

    def out_copy(r, sl):
        row = pl.multiple_of(gout_ref[0, r] * SUBLANES, SUBLANES)
        return pltpu.make_async_copy(obuf.at[sl, pl.ds(r * SUBLANES, SUBLANES), :],
                                     eo_hbm.at[pl.ds(row, SUBLANES), :], osem.at[sl])

    def wait_in(sl):
        pltpu.make_async_copy(xs_hbm.at[pl.ds(0, MOE_TILE), :], xbuf.at[sl], isem.at[sl]).wait()

    def wait_out(sl):
        pltpu.make_async_copy(obuf.at[sl], eo_hbm.at[pl.ds(0, MOE_TILE), :], osem.at[sl]).wait()

    @pl.when(j == 0)
    def _():
        for r in range(MOE_GROUPS):
            in_copy(gin_ref, r, 0).start()

    @pl.when(j + 1 < nu)
    def _():
        for r in range(MOE_GROUPS):
            in_copy(gnext_ref, r, other).start()

    @pl.when(j < nu)
    def _():
        wait_in(slot)

        @pl.when(j >= 2)
        def _():
            wait_out(slot)

        x = _unpack_bf16_pairs(xbuf[slot])
        gu = jnp.dot(x, wgu_s[...], preferred_element_type=F32) + bgu_ref[...]
        gate = jnp.minimum(gu[:, :f], SWIGLU_LIMIT)
        up = jnp.clip(gu[:, f:], -SWIGLU_LIMIT, SWIGLU_LIMIT)
        act = gate * jax.nn.sigmoid(SWIGLU_ALPHA * gate) * (up + 1.0)
        out = jnp.dot(act.astype(BF16), wd_s[...], preferred_element_type=F32) + bd_ref[...]
        obuf[slot] = _pack_bf16_pairs(out.astype(BF16).astype(F32))
        for r in range(MOE_GROUPS):
            out_copy(r, slot).start()

        @pl.when(j == nu - 1)
        def _():
            wait_out(slot)

            @pl.when(j >= 1)
            def _():
                wait_out(other)


def _moe(block_e, n_used, g_in, g_out, xsorted, wgu, bgu, wd, bd):
    n_blocks = block_e.shape[0]
    dh = xsorted.shape[1]
    d = 2 * dh
    f2 = wgu.shape[2]
    f = wd.shape[1]
    g_in3 = g_in.reshape(n_blocks, 1, MOE_GROUPS)
    g_next3 = jnp.concatenate([g_in3[1:], g_in3[:1]], axis=0)
    g_out3 = g_out.reshape(n_blocks, 1, MOE_GROUPS)
    tbl = pl.BlockSpec((None, 1, MOE_GROUPS), lambda j, be, nu: (j, 0, 0), memory_space=pltpu.SMEM)
    return pl.pallas_call(
        _moe_kernel,
        grid_spec=pltpu.PrefetchScalarGridSpec(
            num_scalar_prefetch=2, grid=(n_blocks,),
            in_specs=[tbl, tbl, tbl,
                      pl.BlockSpec(memory_space=pl.ANY),
                      pl.BlockSpec((None, d, f2), lambda j, be, nu: (be[j], 0, 0)),
                      pl.BlockSpec((None, 1, f2), lambda j, be, nu: (be[j], 0, 0)),
                      pl.BlockSpec((None, f, d), lambda j, be, nu: (be[j], 0, 0)),
                      pl.BlockSpec((None, 1, d), lambda j, be, nu: (be[j], 0, 0))],
            out_specs=pl.BlockSpec(memory_space=pl.ANY),
            scratch_shapes=[pltpu.VMEM((2, MOE_TILE, dh), U32), pltpu.VMEM((2, MOE_TILE, dh), U32),
                            pltpu.VMEM((d, f2), BF16), pltpu.VMEM((f, d), BF16),
                            pltpu.SemaphoreType.DMA((2,)), pltpu.SemaphoreType.DMA((2,))]),
        out_shape=jax.ShapeDtypeStruct(xsorted.shape, U32),
        input_output_aliases={5: 0},
        compiler_params=_cparams(),
        name="moe",
    )(block_e, n_used, g_in3, g_next3, g_out3, xsorted, wgu, bgu, wd, bd)


def _combine_kernel(meta_ref, x1_ref, g_ref, eo_ref, y_ref):
    tm = x1_ref.shape[0]
    p_rows = eo_ref.shape[0]
    lane = lax.broadcasted_iota(I32, (tm, SORT_CHUNK), 1).astype(F32).astype(BF16)
    ws = [meta_ref[:, k:k + 1].astype(BF16) for k in range(TOP_K)]
    pos = [meta_ref[:, TOP_K + k:TOP_K + k + 1] for k in range(TOP_K)]
    acc = x1_ref[...]
    for r in range(p_rows // SORT_CHUNK):
        buf = _unpack_bf16_pairs(eo_ref[r * SORT_CHUNK:(r + 1) * SORT_CHUNK, :])
        lo_r = float(r * SORT_CHUNK)
        wm = jnp.zeros((tm, SORT_CHUNK), BF16)
        for k in range(TOP_K):
            rel = jnp.where((pos[k] >= lo_r) & (pos[k] < lo_r + SORT_CHUNK), pos[k] - lo_r, -1.0).astype(BF16)
            wm = jnp.where(lane == rel, ws[k], wm)
        acc = acc + jnp.dot(wm, buf, preferred_element_type=F32)
    y_ref[...] = _rms(acc, g_ref[...])


def _combine(meta, x1, g_final, eo, tile_off, n_tokens):
    d = x1.shape[1]
    tm = TOKEN_TILE
    p_rows = _sorted_rows(tm)
    return pl.pallas_call(
        _combine_kernel,
        grid=(n_tokens // tm,),
        in_specs=[pl.BlockSpec((tm, LANES), lambda i: (i + tile_off, 0)),
                  pl.BlockSpec((tm, d), lambda i: (i + tile_off, 0)),
                  pl.BlockSpec((1, d), lambda i: (0, 0)),
                  pl.BlockSpec((p_rows, d // 2), lambda i: (i + tile_off, 0))],
        out_specs=pl.BlockSpec((tm, d), lambda i: (i, 0)),
        out_shape=jax.ShapeDtypeStruct((n_tokens, d), F32),
        compiler_params=_cparams(),
        name="combine",
    )(meta, x1, g_final, eo)


def kernel(x_prompt, x_sample, g_mix, w_in, ssm_lam_re, ssm_lam_im, ssm_log_dt, ssm_b_re, ssm_b_im,
           ssm_c_re, ssm_c_im, ssm_d, w_glu, b_glu, sgu_ln_g, sgu_ln_b, sgu_w_s, sgu_b_s,
           g_out_ssm, g_out_sgu, w_out, g_ffn, w_router, b_router, w_gate_up, b_gate_up,
           w_down, b_down, g_final):
    assert g_mix.shape[0] == 1, "single-layer trunk"
    bp, lp, d = x_prompt.shape
    bs, ls, _ = x_sample.shape
    tp, ts = bp * lp, bs * ls
    t = tp + ts
    d_ssm = ssm_d.shape[1]
    d_sgu = sgu_ln_g.shape[1]
    n_exp = w_router.shape[2]
    assert lp % TOKEN_TILE == 0 and ls % TOKEN_TILE == 0 and TOKEN_TILE % CHUNK == 0
    assert SSM_CHUNK * SSM_GROUP == MXU_DIM and 2 * ssm_lam_re.shape[-1] == LANES
    assert d_sgu // SGU_HEADS == LANES // 2 and n_exp <= LANES
    assert n_exp * (SUBLANES - 1) <= MXU_DIM and _sorted_rows(TOKEN_TILE) % SORT_CHUNK == 0
    assert (2 * MOE_GROUPS + 1) * SUBLANES <= _sorted_rows(TOKEN_TILE)

    xp = x_prompt.reshape(tp, d)
    xs = x_sample.reshape(ts, d)
    row = lambda a: a.reshape(1, -1).astype(F32)

    w_in_b = w_in[0].astype(BF16)
    u_p, zuv_p = _inproj(xp, row(g_mix[0]), w_in_b, d_ssm)
    u_s, zuv_s = _inproj(xs, row(g_mix[0]), w_in_b, d_ssm)

    mats = _ssm_matrices(ssm_lam_re[0], ssm_lam_im[0], ssm_log_dt[0], ssm_b_re[0], ssm_b_im[0],
                         ssm_c_re[0], ssm_c_im[0], ssm_d[0])
    y_p = _ssm_trunk(u_p, bp, mats)
    y_s = _ssm_trunk(u_s, bs, mats)

    ws = sgu_w_s[0]
    ws_pairs = jnp.concatenate([ws[0::2], ws[1::2]], axis=2).astype(BF16)
    bias_s = jnp.repeat(sgu_b_s[0].T, d_sgu // SGU_HEADS, axis=1).astype(F32)
    mixed = _mix(y_p, y_s, zuv_p, zuv_s, w_glu[0].astype(BF16), row(b_glu[0]), row(g_out_ssm[0]),
                 row(sgu_ln_g[0]), row(sgu_ln_b[0]), ws_pairs, bias_s, row(g_out_sgu[0]))

    wr_pad = jnp.pad(w_router[0].astype(F32), ((0, 0), (0, LANES - n_exp)))
    wr_hi = wr_pad.astype(BF16)
    wr_lo = (wr_pad - wr_hi.astype(F32)).astype(BF16)
    br_pad = jnp.pad(b_router[0].astype(F32), (0, LANES - n_exp)).reshape(1, LANES)
    x1, xsorted, meta, c8 = _route(xp, xs, mixed, w_out[0].astype(BF16), row(g_ffn[0]),
                                   wr_hi, wr_lo, br_pad, n_exp)

    nt = t // TOKEN_TILE
    p_rows = _sorted_rows(TOKEN_TILE)
    max_rows = t * TOP_K + nt * n_exp * (SUBLANES - 1) + n_exp * (MOE_TILE - 1)
    n_blocks = -(-max_rows // MOE_TILE)
    block_e, n_used, g_in, g_out = _slot_tables(c8[:, :, 0].astype(I32), p_rows, n_blocks)
    eo = _moe(block_e, n_used.reshape(1), g_in, g_out, xsorted, w_gate_up[0],
              b_gate_up[0][:, None, :].astype(F32), w_down[0], b_down[0][:, None, :].astype(F32))

    gf = row(g_final)
    y_prompt = _combine(meta, x1, gf, eo, 0, tp)
    y_sample = _combine(meta, x1, gf, eo, tp // TOKEN_TILE, ts)
    return y_prompt.reshape(bp, lp, d), y_sample.reshape(bs, ls, d)
```

```python
import functools
import math

import jax
import jax.numpy as jnp
from jax import lax
from jax.experimental import pallas as pl
from jax.experimental.pallas import tpu as pltpu

F32 = jnp.float32
BF16 = jnp.bfloat16
I32 = jnp.int32
U32 = jnp.uint32

SSM_GROUP = 16
SGU_HEADS = 8
CHUNK = 128
TOP_K = 4
SWIGLU_LIMIT = 7.0
SWIGLU_ALPHA = 1.702
RMS_EPS = 1e-6
LN_EPS = 1e-5

LANES = 128
SUBLANES = 8
MXU_DIM = 256
SSM_CHUNK = MXU_DIM // SSM_GROUP
PAIR_W = 2 * SSM_CHUNK * SSM_GROUP
ATOM = 2 * SSM_GROUP
ATOMS = LANES // ATOM

TOKEN_TILE = 512
MOE_TILE = 512
MOE_GROUPS = MOE_TILE // SUBLANES
SORT_CHUNK = 256
CAST_ROWS = 128
SSM_ROW_TILE = 1024
SCAN_SEQS = 4
VMEM_LIMIT = 56 * 1024 * 1024


def _cparams(n_axes=1, vmem=None):
    return pltpu.CompilerParams(
        dimension_semantics=("arbitrary",) * n_axes,
        vmem_limit_bytes=vmem if vmem is not None else VMEM_LIMIT,
    )


def _rms(x, g):
    return x * lax.rsqrt(jnp.mean(x * x, axis=-1, keepdims=True) + RMS_EPS) * g


def _gelu(x):
    return x * (lax.erf(x * (1.0 / math.sqrt(2.0))) + 1.0) * 0.5


def _atom_masks(rows):
    lane = lax.broadcasted_iota(I32, (rows, LANES), 1)
    return [(lane >= a * ATOM) & (lane < (a + 1) * ATOM) for a in range(ATOMS)]


def _atom_transpose(src, masks):
    dst = []
    for b in range(ATOMS):
        acc = None
        for a in range(ATOMS):
            r = (a - b) % ATOMS
            piece = src[a] if r == 0 else pltpu.roll(src[a], ATOM * r, axis=1)
            acc = piece if acc is None else jnp.where(masks[a], piece, acc)
        dst.append(acc)
    return dst


def _pack_bf16_pairs(x):
    w = x.shape[1] // 2
    lo = lax.bitcast_convert_type(x[:, :w], U32) >> 16
    hi = lax.bitcast_convert_type(x[:, w:], U32) & jnp.uint32(0xFFFF0000)
    return hi | lo


def _unpack_bf16_pairs(u):
    lo = lax.bitcast_convert_type(u << 16, F32)
    hi = lax.bitcast_convert_type(u & jnp.uint32(0xFFFF0000), F32)
    return jnp.concatenate([lo.astype(BF16), hi.astype(BF16)], axis=1)


def _inproj_kernel(x_ref, g_ref, w_ref, u_ref, zuv_ref, za_scr, *, d_ssm):
    tm = x_ref.shape[0]
    nc = tm // SSM_CHUNK
    h = _rms(x_ref[...], g_ref[...])
    z = jnp.dot(h.astype(BF16), w_ref[...], preferred_element_type=F32)
    zuv_ref[...] = z[:, d_ssm:].astype(BF16)
    n_blk = d_ssm // LANES
    for b in range(n_blk):
        za_scr[b] = z[:, b * LANES:(b + 1) * LANES]
    masks = _atom_masks(nc)
    n_quads = SSM_CHUNK // ATOMS
    for b in range(n_blk):
        for v in range(n_quads):
            src = [za_scr[b, pl.ds(ATOMS * v + jj, nc, stride=SSM_CHUNK), :] for jj in range(ATOMS)]
            dst = _atom_transpose(src, masks)
            for pi in range(ATOMS):
                c0 = (ATOMS * b + pi) * PAIR_W + v * LANES
                u_ref[:, c0:c0 + LANES] = dst[pi].astype(BF16)


def _inproj(x, g_mix, w_in, d_ssm):
    t, d = x.shape
    tm = TOKEN_TILE
    d_in = w_in.shape[1]
    nc = tm // SSM_CHUNK
    uw = d_ssm * SSM_CHUNK
    return pl.pallas_call(
        functools.partial(_inproj_kernel, d_ssm=d_ssm),
        grid=(t // tm,),
        in_specs=[pl.BlockSpec((tm, d), lambda i: (i, 0)),
                  pl.BlockSpec((1, d), lambda i: (0, 0)),
                  pl.BlockSpec((d, d_in), lambda i: (0, 0))],
        out_specs=[pl.BlockSpec((nc, uw), lambda i: (i, 0)),
                   pl.BlockSpec((tm, d_in - d_ssm), lambda i: (i, 0))],
        out_shape=[jax.ShapeDtypeStruct((t // SSM_CHUNK, uw), BF16),
                   jax.ShapeDtypeStruct((t, d_in - d_ssm), BF16)],
        scratch_shapes=[pltpu.VMEM((d_ssm // LANES, tm, LANES), F32)],
        compiler_params=_cparams(),
        name="inproj",
    )(x, g_mix, w_in)


def _ssm_matrices(lam_re, lam_im, log_dt, b_re, b_im, c_re, c_im, d_skip):
    hp = lax.Precision.HIGHEST
    _, g, n = lam_re.shape
    p = b_re.shape[-1]
    lc = SSM_CHUNK
    np_ = g // 2
    dt = jnp.exp(log_dt)[..., None]
    mag = jnp.exp(lam_re * dt)
    ar = mag * jnp.cos(lam_im * dt)
    ai = mag * jnp.sin(lam_im * dt)
    den = lam_re * lam_re + lam_im * lam_im
    nr = ar - 1.0
    fr = (nr * lam_re + ai * lam_im) / den
    fi = (ai * lam_re - nr * lam_im) / den
    bbr = fr[..., None] * b_re - fi[..., None] * b_im
    bbi = fr[..., None] * b_im + fi[..., None] * b_re

    prs, pis = [jnp.ones_like(ar)], [jnp.zeros_like(ai)]
    for _ in range(lc):
        pr, pi = prs[-1], pis[-1]
        prs.append(pr * ar - pi * ai)
        pis.append(pr * ai + pi * ar)
    pw_r, pw_i = jnp.stack(prs), jnp.stack(pis)

    car = c_re[None] * pw_r[:lc, :, :, None, :] - c_im[None] * pw_i[:lc, :, :, None, :]
    cai = c_re[None] * pw_i[:lc, :, :, None, :] + c_im[None] * pw_r[:lc, :, :, None, :]
    kk = (jnp.einsum('tdgpn,dgnq->tdgpq', car, bbr, precision=hp)
          - jnp.einsum('tdgpn,dgnq->tdgpq', cai, bbi, precision=hp))
    kf, kb = kk[:, 0], kk[:, 1]
    eye = jnp.eye(2, dtype=F32)
    a2 = p * 2

    k_all = jnp.concatenate([kb[:0:-1], (kf[0] + kb[0])[None], kf[1:]], axis=0)
    k_all = jnp.transpose(k_all, (1, 3, 0, 2)).reshape(np_, 2, p, 2 * lc - 1, p)
    kflat = jnp.einsum('nsqup,st->nsqutp', k_all, eye).reshape(np_, a2, (2 * lc - 1) * a2)
    kflat = jnp.pad(kflat, ((0, 0), (0, 0), (0, a2)))

    def atoms_q(x):
        x = jnp.transpose(x, (0, 1, 3, 2)).reshape(2, np_, 2, p, n)
        return jnp.einsum('dnsqm,st->dnsqtm', x, eye).reshape(2, np_, a2, 2 * n)
    bq_r, bq_i = atoms_q(bbr), atoms_q(bbi)
    bq = jnp.stack([bq_r[0], bq_i[0], bq_r[1], bq_i[1]], axis=1)
    ef = lc - 1 - jnp.arange(lc)
    eb = jnp.arange(lc)
    rows_q = lambda w, e, d: jnp.transpose(w[e, d].reshape(lc, np_, 2 * n), (1, 0, 2))
    wq = jnp.stack([rows_q(pw_r, ef, 0), rows_q(pw_i, ef, 0),
                    rows_q(pw_r, eb, 1), rows_q(pw_i, eb, 1)], axis=1)

    def atoms_p(x):
        x = jnp.transpose(x, (0, 1, 3, 2)).reshape(2, np_, 2, n, p)
        return jnp.einsum('dnsmp,st->dnsmtp', x, eye).reshape(2, np_, 2 * n, a2)
    cp_r, cp_i = atoms_p(c_re), atoms_p(c_im)
    cp = jnp.stack([cp_r[0], cp_i[0], cp_r[1], cp_i[1]], axis=1)
    pf_e = jnp.arange(lc) + 1
    pb_e = lc - jnp.arange(lc)
    cols_p = lambda w, e, d: jnp.transpose(w[e, d].reshape(lc, np_, 2 * n), (1, 2, 0))
    wp = jnp.stack([cols_p(pw_r, pf_e, 0), cols_p(pw_i, pf_e, 0),
                    cols_p(pw_r, pb_e, 1), cols_p(pw_i, pb_e, 1)], axis=1)
    lane = jnp.arange(lc * a2)
    til = (lane[None, :] % a2 == jnp.arange(a2)[:, None]).astype(BF16)
    rep = (lane[None, :] // a2 == jnp.arange(lc)[:, None]).astype(BF16)

    al = jnp.stack([pw_r[lc, 0], pw_i[lc, 0], pw_r[lc, 1], pw_i[lc, 1]])
    alpha = jnp.transpose(al.reshape(4, np_, 2 * n), (1, 0, 2))
    skip = jnp.broadcast_to(d_skip.reshape(np_, 1, a2), (np_, lc, a2)).reshape(np_, 1, lc * a2)
    return kflat, bq, wq, cp, wp, til, rep, alpha, skip.astype(F32)


def _ssm_v_kernel(u_ref, bq_ref, wq_ref, v_ref, q_scr):
    a2 = bq_ref.shape[1]
    w = bq_ref.shape[2]

    @pl.when(pl.program_id(1) == 0)
    def _():
        for d in range(2):
            br, bi = bq_ref[2 * d], bq_ref[2 * d + 1]
            for j in range(SSM_CHUNK):
                wr = wq_ref[2 * d, j:j + 1, :]
                wi = wq_ref[2 * d + 1, j:j + 1, :]
                q_scr[j * a2:(j + 1) * a2, (2 * d) * w:(2 * d + 1) * w] = (wr * br - wi * bi).astype(BF16)
                q_scr[j * a2:(j + 1) * a2, (2 * d + 1) * w:(2 * d + 2) * w] = (wr * bi + wi * br).astype(BF16)

    v_ref[...] = jnp.dot(u_ref[...], q_scr[...], preferred_element_type=F32)


def _ssm_scan_kernel(v_ref, a_ref, s_ref, *, n_chunks, bsz):
    w = LANES
    sub = SUBLANES
    n_groups = n_chunks // sub
    row = lax.broadcasted_iota(I32, (sub, w), 0)

    def cmul(ar, ai, xr, xi):
        return ar * xr - ai * xi, ar * xi + ai * xr

    def powers(ar, ai):
        p2 = cmul(ar, ai, ar, ai)
        p4 = cmul(*p2, *p2)
        p8 = cmul(*p4, *p4)
        p3 = cmul(*p2, ar, ai)
        p5 = cmul(*p4, ar, ai)
        p6 = cmul(*p4, *p2)
        p7 = cmul(*p4, *p3)
        seq = [(jnp.ones_like(ar), jnp.zeros_like(ai)), (ar, ai), p2, p3, p4, p5, p6, p7]
        tr = jnp.concatenate([s[0] for s in seq], axis=0)
        ti = jnp.concatenate([s[1] for s in seq], axis=0)
        return (ar, ai), p2, p4, p8, (tr, ti)

    def tile_scan(vr, vi, pw, reverse):
        a1, a2, a4, _, _ = pw

        def shift(x, s):
            if reverse:
                return jnp.where(row < sub - s, pltpu.roll(x, sub - s, axis=0), 0.0)
            return jnp.where(row >= s, pltpu.roll(x, s, axis=0), 0.0)

        xr, xi = shift(vr, 1), shift(vi, 1)
        for s, (ar, ai) in ((1, a1), (2, a2), (4, a4)):
            mr, mi = cmul(ar, ai, shift(xr, s), shift(xi, s))
            xr, xi = xr + mr, xi + mi
        e = 0 if reverse else sub - 1
        lr, li = cmul(a1[0], a1[1], xr[e:e + 1], xi[e:e + 1])
        return xr, xi, lr + vr[e:e + 1], li + vi[e:e + 1]

    pw_f = powers(a_ref[0:1, :], a_ref[1:2, :])
    pw_b = powers(a_ref[2:3, :], a_ref[3:4, :])
    tbr = jnp.concatenate([pw_b[4][0][sub - 1 - k:sub - k] for k in range(sub)], axis=0)
    tbi = jnp.concatenate([pw_b[4][1][sub - 1 - k:sub - k] for k in range(sub)], axis=0)

    def step(g, carry, b0):
        out = []
        for b in range(b0, b0 + len(carry) // 4):
            cfr, cfi, cbr, cbi = carry[4 * (b - b0):4 * (b - b0) + 4]
            rf = pl.multiple_of(b * n_chunks + g * sub, sub)
            rb = pl.multiple_of(b * n_chunks + (n_groups - 1 - g) * sub, sub)
            xr, xi, lr, li = tile_scan(v_ref[pl.ds(rf, sub), 0:w], v_ref[pl.ds(rf, sub), w:2 * w], pw_f, False)
            mr, mi = cmul(pw_f[4][0], pw_f[4][1], cfr, cfi)
            s_ref[pl.ds(rf, sub), 0:w] = xr + mr
            s_ref[pl.ds(rf, sub), w:2 * w] = xi + mi
            nr, ni = cmul(pw_f[3][0], pw_f[3][1], cfr, cfi)
            out += [nr + lr, ni + li]
            xr, xi, lr, li = tile_scan(v_ref[pl.ds(rb, sub), 2 * w:3 * w], v_ref[pl.ds(rb, sub), 3 * w:4 * w],
                                       pw_b, True)
            mr, mi = cmul(tbr, tbi, cbr, cbi)
            s_ref[pl.ds(rb, sub), 2 * w:3 * w] = xr + mr
            s_ref[pl.ds(rb, sub), 3 * w:4 * w] = xi + mi
            nr, ni = cmul(pw_b[3][0], pw_b[3][1], cbr, cbi)
            out += [nr + lr, ni + li]
        return tuple(out)

    zero = jnp.zeros((1, w), F32)
    for b0 in range(0, bsz, SCAN_SEQS):
        nb = min(SCAN_SEQS, bsz - b0)
        lax.fori_loop(0, n_groups, functools.partial(step, b0=b0), (zero,) * (4 * nb))


def _ssm_y_kernel(u_ref, s_ref, kflat_ref, cp_ref, wp_ref, til_ref, rep_ref, d_ref, y_ref, m_scr, p_scr):
    a2 = kflat_ref.shape[0]
    w = cp_ref.shape[1]

    def expand(x, e_ref):
        hi = x.astype(BF16)
        lo = (x - hi.astype(F32)).astype(BF16)
        return (jnp.dot(hi, e_ref[...], preferred_element_type=F32)
                + jnp.dot(lo, e_ref[...], preferred_element_type=F32))

    @pl.when(pl.program_id(1) == 0)
    def _():
        kflat = kflat_ref[...]
        for j in range(SSM_CHUNK):
            off = (SSM_CHUNK - 1 - j) * a2
            m_scr[j * a2:(j + 1) * a2, :] = kflat[:, off:off + PAIR_W].astype(BF16)
        for d in range(2):
            cr, ci = expand(cp_ref[2 * d], til_ref), expand(cp_ref[2 * d + 1], til_ref)
            wr, wi = expand(wp_ref[2 * d], rep_ref), expand(wp_ref[2 * d + 1], rep_ref)
            p_scr[(2 * d) * w:(2 * d + 1) * w, :] = (cr * wr - ci * wi).astype(BF16)
            p_scr[(2 * d + 1) * w:(2 * d + 2) * w, :] = (-(cr * wi + ci * wr)).astype(BF16)

    u = u_ref[...]
    y = jnp.dot(u, m_scr[...], preferred_element_type=F32)
    y = y + jnp.dot(s_ref[...].astype(BF16), p_scr[...], preferred_element_type=F32)
    y_ref[...] = (y + d_ref[...] * u.astype(F32)).astype(BF16)


def _ssm_trunk(u, bsz, mats):
    kflat, bq, wq, cp, wp, til, rep, alpha, skip = mats
    rows = u.shape[0]
    nc = rows // bsz
    np_ = kflat.shape[0]
    pw = PAIR_W
    sw = 4 * bq.shape[3]
    tr = min(SSM_ROW_TILE, rows)
    per_pair = lambda a: pl.BlockSpec((None,) + a.shape[1:], lambda p, r: (p,) + (0,) * (a.ndim - 1))
    whole = lambda a: pl.BlockSpec(a.shape, lambda p, r: (0,) * a.ndim)
    v = pl.pallas_call(
        _ssm_v_kernel,
        grid=(np_, rows // tr),
        in_specs=[pl.BlockSpec((tr, pw), lambda p, r: (r, p)), per_pair(bq), per_pair(wq)],
        out_specs=pl.BlockSpec((tr, sw), lambda p, r: (r, p)),
        out_shape=jax.ShapeDtypeStruct((rows, np_ * sw), F32),
        scratch_shapes=[pltpu.VMEM((pw, sw), BF16)],
        compiler_params=_cparams(2),
        name="ssm_v",
    )(u, bq, wq)
    s = pl.pallas_call(
        functools.partial(_ssm_scan_kernel, n_chunks=nc, bsz=bsz),
        grid=(np_,),
        in_specs=[pl.BlockSpec((rows, sw), lambda p: (0, p)),
                  pl.BlockSpec((None, 4, LANES), lambda p: (p, 0, 0))],
        out_specs=pl.BlockSpec((rows, sw), lambda p: (0, p)),
        out_shape=jax.ShapeDtypeStruct((rows, np_ * sw), F32),
        compiler_params=_cparams(1),
        name="ssm_scan",
    )(v, alpha)
    y = pl.pallas_call(
        _ssm_y_kernel,
        grid=(np_, rows // tr),
        in_specs=[pl.BlockSpec((tr, pw), lambda p, r: (r, p)),
                  pl.BlockSpec((tr, sw), lambda p, r: (r, p)),
                  per_pair(kflat), per_pair(cp), per_pair(wp), whole(til), whole(rep), per_pair(skip)],
        out_specs=pl.BlockSpec((tr, pw), lambda p, r: (r, p)),
        out_shape=jax.ShapeDtypeStruct((rows, np_ * pw), BF16),
        scratch_shapes=[pltpu.VMEM((pw, pw), BF16), pltpu.VMEM((sw, pw), BF16)],
        compiler_params=_cparams(2),
        name="ssm_y",
    )(u, s, kflat, cp, wp, til, rep, skip)
    return y


def _mix_kernel(yp_ref, ys_ref, zp_ref, zs_ref, wglu_ref, bglu_ref, gssm_ref,
                lng_ref, lnb_ref, ws_ref, bs_ref, gsgu_ref, o_ref, ya_scr, *, n_prompt_tiles):
    i = pl.program_id(0)
    tm = o_ref.shape[0]
    n_blk = ya_scr.shape[0]
    d_ssm = n_blk * LANES
    nc = tm // SSM_CHUNK

    def body(y_ref, zuv_ref):
        masks = _atom_masks(nc)
        for b in range(n_blk):
            for v in range(SSM_CHUNK // ATOMS):
                src = [y_ref[:, (ATOMS * b + pi) * PAIR_W + v * LANES:
                             (ATOMS * b + pi) * PAIR_W + (v + 1) * LANES].astype(F32)
                       for pi in range(ATOMS)]
                dst = _atom_transpose(src, masks)
                for jj in range(ATOMS):
                    ya_scr[b, pl.ds(ATOMS * v + jj, nc, stride=SSM_CHUNK), :] = dst[jj]
        y = jnp.concatenate([ya_scr[b] for b in range(n_blk)], axis=1)
        gl = _gelu(y)
        gate = jnp.dot(gl.astype(BF16), wglu_ref[...], preferred_element_type=F32) + bglu_ref[...]
        o_ref[:, :d_ssm] = _rms(gl * jax.nn.sigmoid(gate), gssm_ref[...]).astype(BF16)
        d_sgu = zuv_ref.shape[1] // 2
        u = _gelu(zuv_ref[:, :d_sgu].astype(F32))
        gv = _gelu(zuv_ref[:, d_sgu:].astype(F32))
        xc = gv - jnp.mean(gv, axis=-1, keepdims=True)
        v = xc * lax.rsqrt(jnp.mean(xc * xc, axis=-1, keepdims=True) + LN_EPS)
        v = (v * lng_ref[...] + lnb_ref[...]).astype(BF16)
        lo = lax.broadcasted_iota(I32, (CHUNK, LANES), 1) < (LANES // 2)
        zero = jnp.zeros((CHUNK, LANES), BF16)
        rows = []
        for c in range(tm // CHUNK):
            cols = []
            for j in range(d_sgu // LANES):
                vp = v[c * CHUNK:(c + 1) * CHUNK, j * LANES:(j + 1) * LANES]
                rhs = jnp.concatenate([jnp.where(lo, vp, zero), jnp.where(lo, zero, vp)], axis=0)
                cols.append(jnp.dot(ws_ref[j], rhs, preferred_element_type=F32))
            rows.append(jnp.concatenate(cols, axis=1) + bs_ref[...])
        s = jnp.concatenate(rows, axis=0)
        o_ref[:, d_ssm:] = _rms(u * s, gsgu_ref[...]).astype(BF16)

    @pl.when(i < n_prompt_tiles)
    def _():
        body(yp_ref, zp_ref)

    @pl.when(i >= n_prompt_tiles)
    def _():
        body(ys_ref, zs_ref)


def _mix(y_p, y_s, zuv_p, zuv_s, wglu, bglu, gssm, lng, lnb, ws_pairs, bias_s, gsgu):
    tp, ts = zuv_p.shape[0], zuv_s.shape[0]
    d_sgu = zuv_p.shape[1] // 2
    d_ssm = wglu.shape[0]
    tm = TOKEN_TILE
    npt = tp // tm
    nc = tm // SSM_CHUNK
    uw = d_ssm * SSM_CHUNK
    ip = lambda i: jnp.minimum(i, npt - 1)
    isamp = lambda i: jnp.maximum(i - npt, 0)
    row = lambda w: pl.BlockSpec((1, w), lambda i: (0, 0))
    return pl.pallas_call(
        functools.partial(_mix_kernel, n_prompt_tiles=npt),
        grid=((tp + ts) // tm,),
        in_specs=[pl.BlockSpec((nc, uw), lambda i: (ip(i), 0)),
                  pl.BlockSpec((nc, uw), lambda i: (isamp(i), 0)),
                  pl.BlockSpec((tm, 2 * d_sgu), lambda i: (ip(i), 0)),
                  pl.BlockSpec((tm, 2 * d_sgu), lambda i: (isamp(i), 0)),
                  pl.BlockSpec((d_ssm, d_ssm), lambda i: (0, 0)),
                  row(d_ssm), row(d_ssm), row(d_sgu), row(d_sgu),
                  pl.BlockSpec(ws_pairs.shape, lambda i: (0, 0, 0)),
                  pl.BlockSpec(bias_s.shape, lambda i: (0, 0)),
                  row(d_sgu)],
        out_specs=pl.BlockSpec((tm, d_ssm + d_sgu), lambda i: (i, 0)),
        out_shape=jax.ShapeDtypeStruct((tp + ts, d_ssm + d_sgu), BF16),
        scratch_shapes=[pltpu.VMEM((d_ssm // LANES, tm, LANES), F32)],
        compiler_params=_cparams(),
        name="mix",
    )(y_p, y_s, zuv_p, zuv_s, wglu, bglu, gssm, lng, lnb, ws_pairs, bias_s, gsgu)


def _sorted_rows(tile):
    return TOP_K * tile + MXU_DIM


def _route_kernel(xp_ref, xs_ref, mix_ref, wout_ref, gffn_ref, whi_ref, wcat_ref, br_ref, tri_ref,
                  ltri_ref, x1_ref, xsort_ref, meta_ref, c8_ref, *, n_prompt_tiles, n_tiles, n_exp):
    i = pl.program_id(0)
    tm = mix_ref.shape[0]
    p_rows = xsort_ref.shape[0]

    def body(x_ref):
        x1 = x_ref[...] + jnp.dot(mix_ref[...], wout_ref[...], preferred_element_type=F32)
        x1_ref[...] = x1
        h2 = _rms(x1, gffn_ref[...])
        hi = h2.astype(BF16)
        lo = (h2 - hi.astype(F32)).astype(BF16)
        hw = jnp.dot(hi, wcat_ref[...], preferred_element_type=F32)
        lt = (hw[:, :LANES] + (jnp.dot(lo, whi_ref[...], preferred_element_type=F32) + hw[:, LANES:])
              + br_ref[...])
        logits = lt.T[:n_exp]
        eio = lax.broadcasted_iota(I32, (n_exp, tm), 0)
        vals, idxs = [], []
        l = logits
        for _ in range(TOP_K):
            m = jnp.max(l, axis=0, keepdims=True)
            idx = jnp.min(jnp.where(l == m, eio, n_exp), axis=0, keepdims=True)
            vals.append(m)
            idxs.append(idx)
            l = jnp.where(eio == idx, -jnp.inf, l)
        ex = [jnp.exp(v - vals[0]) for v in vals]
        den = ex[0] + ex[1] + ex[2] + ex[3]
        ws = [e / den for e in ex]
        hot = [eio == idx for idx in idxs]
        cnt = sum(h.astype(F32) for h in hot)
        prefix = jnp.dot(cnt.astype(BF16), tri_ref[...], preferred_element_type=F32)
        c = jnp.sum(cnt, axis=1, keepdims=True)
        c8 = jnp.floor((c + (SUBLANES - 1)) * (1.0 / SUBLANES)) * SUBLANES
        c8b = jnp.broadcast_to(c8, (n_exp, LANES))
        c8_ref[...] = c8b
        run0 = jnp.dot(ltri_ref[...], c8b.astype(BF16), preferred_element_type=F32)[:, 0:1]
        base = run0 + prefix
        pos = [jnp.sum(jnp.where(h, base, 0.0), axis=0, keepdims=True) for h in hot]
        meta = jnp.concatenate(ws + pos + [jnp.zeros((LANES - 2 * TOP_K, tm), F32)], axis=0)
        meta_ref[...] = meta.T
        rio = lax.broadcasted_iota(I32, (SORT_CHUNK, tm), 0).astype(F32).astype(BF16)
        one = jnp.ones((SORT_CHUNK, tm), BF16)
        zero = jnp.zeros((SORT_CHUNK, tm), BF16)
        for r in range(p_rows // SORT_CHUNK):
            lo_r = float(r * SORT_CHUNK)
            rel = [jnp.where((p >= lo_r) & (p < lo_r + SORT_CHUNK), p - lo_r, -1.0).astype(BF16) for p in pos]
            sel = (rio == rel[0]) | (rio == rel[1]) | (rio == rel[2]) | (rio == rel[3])
            srt = jnp.dot(jnp.where(sel, one, zero), hi, preferred_element_type=F32)
            xsort_ref[r * SORT_CHUNK:(r + 1) * SORT_CHUNK, :] = _pack_bf16_pairs(srt)

    @pl.when(i < n_prompt_tiles)
    def _():
        body(xp_ref)

    @pl.when((i >= n_prompt_tiles) & (i < n_tiles))
    def _():
        body(xs_ref)

    @pl.when(i == n_tiles)
    def _():
        xsort_ref[...] = jnp.zeros_like(xsort_ref)


def _dual_specs(tile, width, n_prompt_tiles, n_tiles):
    last_p = n_prompt_tiles - 1
    last_s = n_tiles - n_prompt_tiles - 1
    sp = pl.BlockSpec((tile, width), lambda i: (jnp.minimum(i, last_p), 0))
    ss = pl.BlockSpec((tile, width), lambda i: (jnp.clip(i - n_prompt_tiles, 0, last_s), 0))
    return sp, ss


def _route(xp, xs, mixed, w_out, g_ffn, wr_hi, wr_cat, br_pad, n_exp):
    tp, d = xp.shape
    t = mixed.shape[0]
    tm = TOKEN_TILE
    npt = tp // tm
    nt = t // tm
    p_rows = _sorted_rows(tm)
    iota = lambda n, ax: lax.broadcasted_iota(I32, (n, n), ax)
    tri = (iota(tm, 0) < iota(tm, 1)).astype(BF16)
    ltri = (iota(n_exp, 1) < iota(n_exp, 0)).astype(BF16)
    sp, ss = _dual_specs(tm, d, npt, nt)
    const = lambda shape: pl.BlockSpec(shape, lambda i: (0,) * len(shape))
    tile = lambda i: jnp.minimum(i, nt - 1)
    return pl.pallas_call(
        functools.partial(_route_kernel, n_prompt_tiles=npt, n_tiles=nt, n_exp=n_exp),
        grid=(nt + 1,),
        in_specs=[sp, ss,
                  pl.BlockSpec((tm, mixed.shape[1]), lambda i: (tile(i), 0)),
                  const(w_out.shape), const((1, d)), const(wr_hi.shape), const(wr_cat.shape),
                  const((1, LANES)), const((tm, tm)), const((n_exp, n_exp))],
        out_specs=[pl.BlockSpec((tm, d), lambda i: (tile(i), 0)),
                   pl.BlockSpec((p_rows, d // 2), lambda i: (i, 0)),
                   pl.BlockSpec((tm, LANES), lambda i: (tile(i), 0)),
                   pl.BlockSpec((None, n_exp, LANES), lambda i: (tile(i), 0, 0))],
        out_shape=[jax.ShapeDtypeStruct((t, d), F32),
                   jax.ShapeDtypeStruct(((nt + 1) * p_rows, d // 2), U32),
                   jax.ShapeDtypeStruct((t, LANES), F32),
                   jax.ShapeDtypeStruct((nt, n_exp, LANES), F32)],
        compiler_params=_cparams(),
        name="route",
    )(xp, xs, mixed, w_out, g_ffn, wr_hi, wr_cat, br_pad, tri, ltri)


def _slot_tables(c8, p_rows, n_blocks):
    nt, n_exp = c8.shape
    run0 = jnp.cumsum(c8, axis=1) - c8
    seg_len = c8.T
    cum = jnp.cumsum(seg_len, axis=1)
    tot = cum[:, -1]
    padded = (tot + MOE_TILE - 1) // MOE_TILE * MOE_TILE
    ends = jnp.cumsum(padded)
    starts = ends - padded
    n_used = (ends[-1] // MOE_TILE).astype(I32)
    blk = jnp.arange(n_blocks, dtype=I32)
    be = jnp.minimum(jnp.sum(ends[None, :] <= (blk * MOE_TILE)[:, None], axis=1), n_exp - 1).astype(I32)
    block_e = jnp.where(blk < n_used, be, be[jnp.maximum(n_used - 1, 0)])
    onehot = (be[:, None] == jnp.arange(n_exp, dtype=I32)[None, :]).astype(F32)
    pick = lambda tbl: jnp.dot(onehot, tbl.astype(F32), precision=lax.Precision.HIGHEST)
    run_end = pick(cum)
    run_beg = run_end - pick(seg_len)
    shift = pick((jnp.arange(nt, dtype=I32) * p_rows)[None, :] + run0.T) - run_beg
    grow = (blk[:, None] * MOE_TILE + jnp.arange(MOE_GROUPS, dtype=I32)[None, :] * SUBLANES).astype(F32)
    rel = grow - pick(starts[:, None])
    inside = (run_beg[:, None, :] <= rel[:, :, None]) & (rel[:, :, None] < run_end[:, None, :])
    src_row = (rel + jnp.sum(jnp.where(inside, shift[:, None, :], 0.0), axis=2)).astype(I32)
    valid = jnp.any(inside, axis=2) & (blk < n_used)[:, None]
    scratch0 = nt * p_rows // SUBLANES
    zero_group = scratch0 + 2 * MOE_GROUPS
    g_in = jnp.where(valid, src_row // SUBLANES, zero_group).astype(I32)
    g_scr = scratch0 + (blk[:, None] % 2) * MOE_GROUPS + jnp.arange(MOE_GROUPS, dtype=I32)[None, :]
    g_out = jnp.where(valid, src_row // SUBLANES, g_scr).astype(I32)
    return block_e, n_used, g_in, g_out


def _moe_kernel(be_ref, nu_ref, gin_ref, gnext_ref, gout_ref, xs_hbm, wgu_ref, bgu_ref, wd_ref, bd_ref,
                eo_hbm, xbuf, obuf, wgu_s, wd_s, isem, osem):
    j = pl.program_id(0)
    nu = nu_ref[0]
    f = wd_ref.shape[0]
    slot = lax.rem(j, 2)
    other = 1 - slot

    @pl.when((j < nu) & ((j == 0) | (be_ref[j] != be_ref[jnp.maximum(j - 1, 0)])))
    def _():
        for c in range(0, wgu_ref.shape[0], CAST_ROWS):
            wgu_s[c:c + CAST_ROWS, :] = wgu_ref[c:c + CAST_ROWS, :].astype(BF16)
        for c in range(0, wd_ref.shape[0], CAST_ROWS):
            wd_s[c:c + CAST_ROWS, :] = wd_ref[c:c + CAST_ROWS, :].astype(BF16)

    def in_copy(tbl_ref, r, sl):
        row = pl.multiple_of(tbl_ref[0, r] * SUBLANES, SUBLANES)
        return pltpu.make_async_copy(xs_hbm.at[pl.ds(row, SUBLANES), :],
                                     xbuf.at[sl, pl.ds(r * SUBLANES, SUBLANES), :], isem.at[sl])

    def out_copy(r, sl):
        row = pl.multiple_of(gout_ref[0, r] * SUBLANES, SUBLANES)
        return pltpu.make_async_copy(obuf.at[sl, pl.ds(r * SUBLANES, SUBLANES), :],
                                     eo_hbm.at[pl.ds(row, SUBLANES), :], osem.at[sl])

    def wait_in(sl):
        pltpu.make_async_copy(xs_hbm.at[pl.ds(0, MOE_TILE), :], xbuf.at[sl], isem.at[sl]).wait()

    def wait_out(sl):
        pltpu.make_async_copy(obuf.at[sl], eo_hbm.at[pl.ds(0, MOE_TILE), :], osem.at[sl]).wait()

    @pl.when(j == 0)
    def _():
        for r in range(MOE_GROUPS):
            in_copy(gin_ref, r, 0).start()

    @pl.when(j + 1 < nu)
    def _():
        for r in range(MOE_GROUPS):
            in_copy(gnext_ref, r, other).start()

    @pl.when(j < nu)
    def _():
        wait_in(slot)

        @pl.when(j >= 2)
        def _():
            wait_out(slot)

        x = _unpack_bf16_pairs(xbuf[slot])
        gu = jnp.dot(x, wgu_s[...], preferred_element_type=F32) + bgu_ref[...]
        gate = jnp.minimum(gu[:, :f], SWIGLU_LIMIT)
        up = jnp.clip(gu[:, f:], -SWIGLU_LIMIT, SWIGLU_LIMIT)
        act = gate * jax.nn.sigmoid(SWIGLU_ALPHA * gate) * (up + 1.0)
        out = jnp.dot(act.astype(BF16), wd_s[...], preferred_element_type=F32) + bd_ref[...]
        obuf[slot] = _pack_bf16_pairs(out.astype(BF16).astype(F32))
        for r in range(MOE_GROUPS):
            out_copy(r, slot).start()

        @pl.when(j == nu - 1)
        def _():
            wait_out(slot)

            @pl.when(j >= 1)
            def _():
                wait_out(other)


def _moe(block_e, n_used, g_in, g_out, xsorted, wgu, bgu, wd, bd):
    n_blocks = block_e.shape[0]
    dh = xsorted.shape[1]
    d = 2 * dh
    f2 = wgu.shape[2]
    f = wd.shape[1]
    g_in3 = g_in.reshape(n_blocks, 1, MOE_GROUPS)
    g_next3 = jnp.concatenate([g_in3[1:], g_in3[:1]], axis=0)
    g_out3 = g_out.reshape(n_blocks, 1, MOE_GROUPS)
    tbl = pl.BlockSpec((None, 1, MOE_GROUPS), lambda j, be, nu: (j, 0, 0), memory_space=pltpu.SMEM)
    return pl.pallas_call(
        _moe_kernel,
        grid_spec=pltpu.PrefetchScalarGridSpec(
            num_scalar_prefetch=2, grid=(n_blocks,),
            in_specs=[tbl, tbl, tbl,
                      pl.BlockSpec(memory_space=pl.ANY),
                      pl.BlockSpec((None, d, f2), lambda j, be, nu: (be[j], 0, 0)),
                      pl.BlockSpec((None, 1, f2), lambda j, be, nu: (be[j], 0, 0)),
                      pl.BlockSpec((None, f, d), lambda j, be, nu: (be[j], 0, 0)),
                      pl.BlockSpec((None, 1, d), lambda j, be, nu: (be[j], 0, 0))],
            out_specs=pl.BlockSpec(memory_space=pl.ANY),
            scratch_shapes=[pltpu.VMEM((2, MOE_TILE, dh), U32), pltpu.VMEM((2, MOE_TILE, dh), U32),
                            pltpu.VMEM((d, f2), BF16), pltpu.VMEM((f, d), BF16),
                            pltpu.SemaphoreType.DMA((2,)), pltpu.SemaphoreType.DMA((2,))]),
        out_shape=jax.ShapeDtypeStruct(xsorted.shape, U32),
        input_output_aliases={5: 0},
        compiler_params=_cparams(),
        name="moe",
    )(block_e, n_used, g_in3, g_next3, g_out3, xsorted, wgu, bgu, wd, bd)


def _combine_kernel(meta_ref, x1_ref, g_ref, eo_ref, y_ref):
    tm = x1_ref.shape[0]
    p_rows = eo_ref.shape[0]
    lane = lax.broadcasted_iota(I32, (tm, SORT_CHUNK), 1).astype(F32).astype(BF16)
    ws = [meta_ref[:, k:k + 1].astype(BF16) for k in range(TOP_K)]
    pos = [meta_ref[:, TOP_K + k:TOP_K + k + 1] for k in range(TOP_K)]
    acc = x1_ref[...]
    for r in range(p_rows // SORT_CHUNK):
        buf = _unpack_bf16_pairs(eo_ref[r * SORT_CHUNK:(r + 1) * SORT_CHUNK, :])
        lo_r = float(r * SORT_CHUNK)
        wm = jnp.zeros((tm, SORT_CHUNK), BF16)
        for k in range(TOP_K):
            rel = jnp.where((pos[k] >= lo_r) & (pos[k] < lo_r + SORT_CHUNK), pos[k] - lo_r, -1.0).astype(BF16)
            wm = jnp.where(lane == rel, ws[k], wm)
        acc = acc + jnp.dot(wm, buf, preferred_element_type=F32)
    y_ref[...] = _rms(acc, g_ref[...])


def _combine(meta, x1, g_final, eo, tile_off, n_tokens):
    d = x1.shape[1]
    tm = TOKEN_TILE
    p_rows = _sorted_rows(tm)
    return pl.pallas_call(
        _combine_kernel,
        grid=(n_tokens // tm,),
        in_specs=[pl.BlockSpec((tm, LANES), lambda i: (i + tile_off, 0)),
                  pl.BlockSpec((tm, d), lambda i: (i + tile_off, 0)),
                  pl.BlockSpec((1, d), lambda i: (0, 0)),
                  pl.BlockSpec((p_rows, d // 2), lambda i: (i + tile_off, 0))],
        out_specs=pl.BlockSpec((tm, d), lambda i: (i, 0)),
        out_shape=jax.ShapeDtypeStruct((n_tokens, d), F32),
        compiler_params=_cparams(),
        name="combine",
    )(meta, x1, g_final, eo)


def kernel(x_prompt, x_sample, g_mix, w_in, ssm_lam_re, ssm_lam_im, ssm_log_dt, ssm_b_re, ssm_b_im,
           ssm_c_re, ssm_c_im, ssm_d, w_glu, b_glu, sgu_ln_g, sgu_ln_b, sgu_w_s, sgu_b_s,
           g_out_ssm, g_out_sgu, w_out, g_ffn, w_router, b_router, w_gate_up, b_gate_up,
           w_down, b_down, g_final):
    assert g_mix.shape[0] == 1, "single-layer trunk"
    bp, lp, d = x_prompt.shape
    bs, ls, _ = x_sample.shape
    tp, ts = bp * lp, bs * ls
    t = tp + ts
    d_ssm = ssm_d.shape[1]
    d_sgu = sgu_ln_g.shape[1]
    n_exp = w_router.shape[2]
    assert lp % TOKEN_TILE == 0 and ls % TOKEN_TILE == 0 and TOKEN_TILE % CHUNK == 0
    assert SSM_CHUNK * SSM_GROUP == MXU_DIM and 2 * ssm_lam_re.shape[-1] == LANES
    assert d_sgu // SGU_HEADS == LANES // 2 and n_exp <= LANES
    assert n_exp * (SUBLANES - 1) <= MXU_DIM and _sorted_rows(TOKEN_TILE) % SORT_CHUNK == 0
    assert (2 * MOE_GROUPS + 1) * SUBLANES <= _sorted_rows(TOKEN_TILE)

    xp = x_prompt.reshape(tp, d)
    xs = x_sample.reshape(ts, d)
    row = lambda a: a.reshape(1, -1).astype(F32)

    w_in_b = w_in[0].astype(BF16)
    u_p, zuv_p = _inproj(xp, row(g_mix[0]), w_in_b, d_ssm)
    u_s, zuv_s = _inproj(xs, row(g_mix[0]), w_in_b, d_ssm)

    mats = _ssm_matrices(ssm_lam_re[0], ssm_lam_im[0], ssm_log_dt[0], ssm_b_re[0], ssm_b_im[0],
                         ssm_c_re[0], ssm_c_im[0], ssm_d[0])
    y_p = _ssm_trunk(u_p, bp, mats)
    y_s = _ssm_trunk(u_s, bs, mats)

    ws = sgu_w_s[0]
    ws_pairs = jnp.concatenate([ws[0::2], ws[1::2]], axis=2).astype(BF16)
    bias_s = jnp.repeat(sgu_b_s[0].T, d_sgu // SGU_HEADS, axis=1).astype(F32)
    mixed = _mix(y_p, y_s, zuv_p, zuv_s, w_glu[0].astype(BF16), row(b_glu[0]), row(g_out_ssm[0]),
                 row(sgu_ln_g[0]), row(sgu_ln_b[0]), ws_pairs, bias_s, row(g_out_sgu[0]))

    wr_pad = jnp.pad(w_router[0].astype(F32), ((0, 0), (0, LANES - n_exp)))
    wr_hi = wr_pad.astype(BF16)
    wr_cat = jnp.concatenate([wr_hi, (wr_pad - wr_hi.astype(F32)).astype(BF16)], axis=1)
    br_pad = jnp.pad(b_router[0].astype(F32), (0, LANES - n_exp)).reshape(1, LANES)
    x1, xsorted, meta, c8 = _route(xp, xs, mixed, w_out[0].astype(BF16), row(g_ffn[0]),
                                   wr_hi, wr_cat, br_pad, n_exp)

    nt = t // TOKEN_TILE
    p_rows = _sorted_rows(TOKEN_TILE)
    max_rows = t * TOP_K + nt * n_exp * (SUBLANES - 1) + n_exp * (MOE_TILE - 1)
    n_blocks = -(-max_rows // MOE_TILE)
    block_e, n_used, g_in, g_out = _slot_tables(c8[:, :, 0].astype(I32), p_rows, n_blocks)
    eo = _moe(block_e, n_used.reshape(1), g_in, g_out, xsorted, w_gate_up[0],
              b_gate_up[0][:, None, :].astype(F32), w_down[0], b_down[0][:, None, :].astype(F32))

    gf = row(g_final)
    y_prompt = _combine(meta, x1, gf, eo, 0, tp)
    y_sample = _combine(meta, x1, gf, eo, tp // TOKEN_TILE, ts)
    return y_prompt.reshape(bp, lp, d), y_sample.reshape(bs, ls, d)
```

```python
import functools
import math

import jax
import jax.numpy as jnp
from jax import lax
from jax.experimental import pallas as pl
from jax.experimental.pallas import tpu as pltpu

F32 = jnp.float32
BF16 = jnp.bfloat16
I32 = jnp.int32
U32 = jnp.uint32

SSM_GROUP = 16
SGU_HEADS = 8
CHUNK = 128
TOP_K = 4
SWIGLU_LIMIT = 7.0
SWIGLU_ALPHA = 1.702
RMS_EPS = 1e-6
LN_EPS = 1e-5

LANES = 128
SUBLANES = 8
MXU_DIM = 256
SSM_CHUNK = MXU_DIM // SSM_GROUP
PAIR_W = 2 * SSM_CHUNK * SSM_GROUP
ATOM = 2 * SSM_GROUP
ATOMS = LANES // ATOM

TOKEN_TILE = 512
MOE_TILE = 512
MOE_GROUPS = MOE_TILE // SUBLANES
SORT_CHUNK = 256
CAST_ROWS = 128
SSM_ROW_TILE = 2048
SCAN_SEQS = 4
VMEM_LIMIT = 56 * 1024 * 1024


def _cparams(n_axes=1, vmem=None):
    return pltpu.CompilerParams(
        dimension_semantics=("arbitrary",) * n_axes,
        vmem_limit_bytes=vmem if vmem is not None else VMEM_LIMIT,
    )


def _rms(x, g):
    return x * lax.rsqrt(jnp.mean(x * x, axis=-1, keepdims=True) + RMS_EPS) * g


def _gelu(x):
    return x * (lax.erf(x * (1.0 / math.sqrt(2.0))) + 1.0) * 0.5


def _atom_masks(rows):
    lane = lax.broadcasted_iota(I32, (rows, LANES), 1)
    return [(lane >= a * ATOM) & (lane < (a + 1) * ATOM) for a in range(ATOMS)]


def _atom_transpose(src, masks):
    dst = []
    for b in range(ATOMS):
        acc = None
        for a in range(ATOMS):
            r = (a - b) % ATOMS
            piece = src[a] if r == 0 else pltpu.roll(src[a], ATOM * r, axis=1)
            acc = piece if acc is None else jnp.where(masks[a], piece, acc)
        dst.append(acc)
    return dst


def _pack_bf16_pairs(x):
    w = x.shape[1] // 2
    lo = lax.bitcast_convert_type(x[:, :w], U32) >> 16
    hi = lax.bitcast_convert_type(x[:, w:], U32) & jnp.uint32(0xFFFF0000)
    return hi | lo


def _unpack_bf16_pairs(u):
    lo = lax.bitcast_convert_type(u << 16, F32)
    hi = lax.bitcast_convert_type(u & jnp.uint32(0xFFFF0000), F32)
    return jnp.concatenate([lo.astype(BF16), hi.astype(BF16)], axis=1)


def _inproj_kernel(x_ref, g_ref, w_ref, u_ref, zuv_ref, za_scr, *, d_ssm):
    tm = x_ref.shape[0]
    nc = tm // SSM_CHUNK
    h = _rms(x_ref[...], g_ref[...])
    z = jnp.dot(h.astype(BF16), w_ref[...], preferred_element_type=F32)
    zuv_ref[...] = z[:, d_ssm:].astype(BF16)
    n_blk = d_ssm // LANES
    for b in range(n_blk):
        za_scr[b] = z[:, b * LANES:(b + 1) * LANES]
    masks = _atom_masks(nc)
    n_quads = SSM_CHUNK // ATOMS
    for b in range(n_blk):
        for v in range(n_quads):
            src = [za_scr[b, pl.ds(ATOMS * v + jj, nc, stride=SSM_CHUNK), :] for jj in range(ATOMS)]
            dst = _atom_transpose(src, masks)
            for pi in range(ATOMS):
                c0 = (ATOMS * b + pi) * PAIR_W + v * LANES
                u_ref[:, c0:c0 + LANES] = dst[pi].astype(BF16)


def _inproj(x, g_mix, w_in, d_ssm):
    t, d = x.shape
    tm = TOKEN_TILE
    d_in = w_in.shape[1]
    nc = tm // SSM_CHUNK
    uw = d_ssm * SSM_CHUNK
    return pl.pallas_call(
        functools.partial(_inproj_kernel, d_ssm=d_ssm),
        grid=(t // tm,),
        in_specs=[pl.BlockSpec((tm, d), lambda i: (i, 0)),
                  pl.BlockSpec((1, d), lambda i: (0, 0)),
                  pl.BlockSpec((d, d_in), lambda i: (0, 0))],
        out_specs=[pl.BlockSpec((nc, uw), lambda i: (i, 0)),
                   pl.BlockSpec((tm, d_in - d_ssm), lambda i: (i, 0))],
        out_shape=[jax.ShapeDtypeStruct((t // SSM_CHUNK, uw), BF16),
                   jax.ShapeDtypeStruct((t, d_in - d_ssm), BF16)],
        scratch_shapes=[pltpu.VMEM((d_ssm // LANES, tm, LANES), F32)],
        compiler_params=_cparams(),
        name="inproj",
    )(x, g_mix, w_in)


def _ssm_matrices(lam_re, lam_im, log_dt, b_re, b_im, c_re, c_im, d_skip):
    hp = lax.Precision.HIGHEST
    _, g, n = lam_re.shape
    p = b_re.shape[-1]
    lc = SSM_CHUNK
    np_ = g // 2
    dt = jnp.exp(log_dt)[..., None]
    mag = jnp.exp(lam_re * dt)
    ar = mag * jnp.cos(lam_im * dt)
    ai = mag * jnp.sin(lam_im * dt)
    den = lam_re * lam_re + lam_im * lam_im
    nr = ar - 1.0
    fr = (nr * lam_re + ai * lam_im) / den
    fi = (ai * lam_re - nr * lam_im) / den
    bbr = fr[..., None] * b_re - fi[..., None] * b_im
    bbi = fr[..., None] * b_im + fi[..., None] * b_re

    prs, pis = [jnp.ones_like(ar)], [jnp.zeros_like(ai)]
    for _ in range(lc):
        pr, pi = prs[-1], pis[-1]
        prs.append(pr * ar - pi * ai)
        pis.append(pr * ai + pi * ar)
    pw_r, pw_i = jnp.stack(prs), jnp.stack(pis)

    lane_k = jnp.arange(lc * p)
    til_k = (lane_k[None, :] % p == jnp.arange(p)[:, None]).astype(F32)
    rep_k = (lane_k[None, :] // p == jnp.arange(lc)[:, None]).astype(F32)
    ex = lambda x, e: jnp.einsum('dgnk,kx->dgnx', x, e, precision=hp)
    ct_r = ex(jnp.transpose(c_re, (0, 1, 3, 2)), til_k)
    ct_i = ex(jnp.transpose(c_im, (0, 1, 3, 2)), til_k)
    at_r = ex(jnp.transpose(pw_r[:lc], (1, 2, 3, 0)), rep_k)
    at_i = ex(jnp.transpose(pw_i[:lc], (1, 2, 3, 0)), rep_k)
    kk = (jnp.einsum('dgnq,dgnx->dgqx', bbr, ct_r * at_r - ct_i * at_i, precision=hp)
          - jnp.einsum('dgnq,dgnx->dgqx', bbi, ct_r * at_i + ct_i * at_r, precision=hp))
    kk = kk.reshape(2, g, p, lc, p)
    kf, kb = kk[0], kk[1]
    eye = jnp.eye(2, dtype=F32)
    a2 = p * 2

    k_all = jnp.concatenate([kb[:, :, :0:-1], (kf[:, :, :1] + kb[:, :, :1]), kf[:, :, 1:]], axis=2)
    k_all = k_all.reshape(np_, 2, p, 2 * lc - 1, p)
    kflat = jnp.einsum('nsqup,st->nsqutp', k_all, eye).reshape(np_, a2, (2 * lc - 1) * a2)
    kflat = jnp.pad(kflat, ((0, 0), (0, 0), (0, a2)))

    def atoms_q(x):
        x = jnp.transpose(x, (0, 1, 3, 2)).reshape(2, np_, 2, p, n)
        return jnp.einsum('dnsqm,st->dnsqtm', x, eye).reshape(2, np_, a2, 2 * n)
    bq_r, bq_i = atoms_q(bbr), atoms_q(bbi)
    bq = jnp.stack([bq_r[0], bq_i[0], bq_r[1], bq_i[1]], axis=1)
    ef = lc - 1 - jnp.arange(lc)
    eb = jnp.arange(lc)
    rows_q = lambda w, e, d: jnp.transpose(w[e, d].reshape(lc, np_, 2 * n), (1, 0, 2))
    wq = jnp.stack([rows_q(pw_r, ef, 0), rows_q(pw_i, ef, 0),
                    rows_q(pw_r, eb, 1), rows_q(pw_i, eb, 1)], axis=1)

    def atoms_p(x):
        x = jnp.transpose(x, (0, 1, 3, 2)).reshape(2, np_, 2, n, p)
        return jnp.einsum('dnsmp,st->dnsmtp', x, eye).reshape(2, np_, 2 * n, a2)
    cp_r, cp_i = atoms_p(c_re), atoms_p(c_im)
    cp = jnp.stack([cp_r[0], cp_i[0], cp_r[1], cp_i[1]], axis=1)
    pf_e = jnp.arange(lc) + 1
    pb_e = lc - jnp.arange(lc)
    cols_p = lambda w, e, d: jnp.transpose(w[e, d].reshape(lc, np_, 2 * n), (1, 2, 0))
    wp = jnp.stack([cols_p(pw_r, pf_e, 0), cols_p(pw_i, pf_e, 0),
                    cols_p(pw_r, pb_e, 1), cols_p(pw_i, pb_e, 1)], axis=1)
    lane = jnp.arange(lc * a2)
    til = (lane[None, :] % a2 == jnp.arange(a2)[:, None]).astype(BF16)
    rep = (lane[None, :] // a2 == jnp.arange(lc)[:, None]).astype(BF16)

    al = jnp.stack([pw_r[lc, 0], pw_i[lc, 0], pw_r[lc, 1], pw_i[lc, 1]])
    alpha = jnp.transpose(al.reshape(4, np_, 2 * n), (1, 0, 2))
    skip = jnp.broadcast_to(d_skip.reshape(np_, 1, a2), (np_, lc, a2)).reshape(np_, 1, lc * a2)
    return kflat, bq, wq, cp, wp, til, rep, alpha, skip.astype(F32)


def _ssm_v_kernel(u_ref, bq_ref, wq_ref, v_ref, q_scr):
    a2 = bq_ref.shape[1]
    w = bq_ref.shape[2]

    @pl.when(pl.program_id(1) == 0)
    def _():
        for d in range(2):
            br, bi = bq_ref[2 * d], bq_ref[2 * d + 1]
            for j in range(SSM_CHUNK):
                wr = wq_ref[2 * d, j:j + 1, :]
                wi = wq_ref[2 * d + 1, j:j + 1, :]
                q_scr[j * a2:(j + 1) * a2, (2 * d) * w:(2 * d + 1) * w] = (wr * br - wi * bi).astype(BF16)
                q_scr[j * a2:(j + 1) * a2, (2 * d + 1) * w:(2 * d + 2) * w] = (wr * bi + wi * br).astype(BF16)

    v_ref[...] = jnp.dot(u_ref[...], q_scr[...], preferred_element_type=F32)


def _ssm_scan_kernel(v_ref, a_ref, s_ref, *, n_chunks, bsz):
    w = LANES
    sub = SUBLANES
    n_groups = n_chunks // sub
    row = lax.broadcasted_iota(I32, (sub, w), 0)

    def cmul(ar, ai, xr, xi):
        return ar * xr - ai * xi, ar * xi + ai * xr

    def powers(ar, ai):
        p2 = cmul(ar, ai, ar, ai)
        p4 = cmul(*p2, *p2)
        p8 = cmul(*p4, *p4)
        p3 = cmul(*p2, ar, ai)
        p5 = cmul(*p4, ar, ai)
        p6 = cmul(*p4, *p2)
        p7 = cmul(*p4, *p3)
        seq = [(jnp.ones_like(ar), jnp.zeros_like(ai)), (ar, ai), p2, p3, p4, p5, p6, p7]
        tr = jnp.concatenate([s[0] for s in seq], axis=0)
        ti = jnp.concatenate([s[1] for s in seq], axis=0)
        return (ar, ai), p2, p4, p8, (tr, ti)

    def tile_scan(vr, vi, pw, reverse):
        a1, a2, a4, _, _ = pw

        def shift(x, s):
            if reverse:
                return jnp.where(row < sub - s, pltpu.roll(x, sub - s, axis=0), 0.0)
            return jnp.where(row >= s, pltpu.roll(x, s, axis=0), 0.0)

        xr, xi = shift(vr, 1), shift(vi, 1)
        for s, (ar, ai) in ((1, a1), (2, a2), (4, a4)):
            mr, mi = cmul(ar, ai, shift(xr, s), shift(xi, s))
            xr, xi = xr + mr, xi + mi
        e = 0 if reverse else sub - 1
        lr, li = cmul(a1[0], a1[1], xr[e:e + 1], xi[e:e + 1])
        return xr, xi, lr + vr[e:e + 1], li + vi[e:e + 1]

    pw_f = powers(a_ref[0:1, :], a_ref[1:2, :])
    pw_b = powers(a_ref[2:3, :], a_ref[3:4, :])
    tbr = jnp.concatenate([pw_b[4][0][sub - 1 - k:sub - k] for k in range(sub)], axis=0)
    tbi = jnp.concatenate([pw_b[4][1][sub - 1 - k:sub - k] for k in range(sub)], axis=0)

    def step(g, carry, b0):
        out = []
        for b in range(b0, b0 + len(carry) // 4):
            cfr, cfi, cbr, cbi = carry[4 * (b - b0):4 * (b - b0) + 4]
            rf = pl.multiple_of(b * n_chunks + g * sub, sub)
            rb = pl.multiple_of(b * n_chunks + (n_groups - 1 - g) * sub, sub)
            xr, xi, lr, li = tile_scan(v_ref[pl.ds(rf, sub), 0:w], v_ref[pl.ds(rf, sub), w:2 * w], pw_f, False)
            mr, mi = cmul(pw_f[4][0], pw_f[4][1], cfr, cfi)
            s_ref[pl.ds(rf, sub), 0:w] = xr + mr
            s_ref[pl.ds(rf, sub), w:2 * w] = xi + mi
            nr, ni = cmul(pw_f[3][0], pw_f[3][1], cfr, cfi)
            out += [nr + lr, ni + li]
            xr, xi, lr, li = tile_scan(v_ref[pl.ds(rb, sub), 2 * w:3 * w], v_ref[pl.ds(rb, sub), 3 * w:4 * w],
                                       pw_b, True)
            mr, mi = cmul(tbr, tbi, cbr, cbi)
            s_ref[pl.ds(rb, sub), 2 * w:3 * w] = xr + mr
            s_ref[pl.ds(rb, sub), 3 * w:4 * w] = xi + mi
            nr, ni = cmul(pw_b[3][0], pw_b[3][1], cbr, cbi)
            out += [nr + lr, ni + li]
        return tuple(out)

    zero = jnp.zeros((1, w), F32)
    for b0 in range(0, bsz, SCAN_SEQS):
        nb = min(SCAN_SEQS, bsz - b0)
        lax.fori_loop(0, n_groups, functools.partial(step, b0=b0), (zero,) * (4 * nb))


def _ssm_y_kernel(u_ref, s_ref, kflat_ref, cp_ref, wp_ref, til_ref, rep_ref, d_ref, y_ref, m_scr, p_scr):
    a2 = kflat_ref.shape[0]
    w = cp_ref.shape[1]

    def expand(x, e_ref):
        hi = x.astype(BF16)
        lo = (x - hi.astype(F32)).astype(BF16)
        return (jnp.dot(hi, e_ref[...], preferred_element_type=F32)
                + jnp.dot(lo, e_ref[...], preferred_element_type=F32))

    @pl.when(pl.program_id(1) == 0)
    def _():
        kflat = kflat_ref[...]
        for j in range(SSM_CHUNK):
            off = (SSM_CHUNK - 1 - j) * a2
            m_scr[j * a2:(j + 1) * a2, :] = kflat[:, off:off + PAIR_W].astype(BF16)
        for d in range(2):
            cr, ci = expand(cp_ref[2 * d], til_ref), expand(cp_ref[2 * d + 1], til_ref)
            wr, wi = expand(wp_ref[2 * d], rep_ref), expand(wp_ref[2 * d + 1], rep_ref)
            p_scr[(2 * d) * w:(2 * d + 1) * w, :] = (cr * wr - ci * wi).astype(BF16)
            p_scr[(2 * d + 1) * w:(2 * d + 2) * w, :] = (-(cr * wi + ci * wr)).astype(BF16)

    u = u_ref[...]
    y = jnp.dot(u, m_scr[...], preferred_element_type=F32)
    y = y + jnp.dot(s_ref[...].astype(BF16), p_scr[...], preferred_element_type=F32)
    y_ref[...] = (y + d_ref[...] * u.astype(F32)).astype(BF16)


def _ssm_trunk(u, bsz, mats):
    kflat, bq, wq, cp, wp, til, rep, alpha, skip = mats
    rows = u.shape[0]
    nc = rows // bsz
    np_ = kflat.shape[0]
    pw = PAIR_W
    sw = 4 * bq.shape[3]
    tr = min(SSM_ROW_TILE, rows)
    per_pair = lambda a: pl.BlockSpec((None,) + a.shape[1:], lambda p, r: (p,) + (0,) * (a.ndim - 1))
    whole = lambda a: pl.BlockSpec(a.shape, lambda p, r: (0,) * a.ndim)
    v = pl.pallas_call(
        _ssm_v_kernel,
        grid=(np_, rows // tr),
        in_specs=[pl.BlockSpec((tr, pw), lambda p, r: (r, p)), per_pair(bq), per_pair(wq)],
        out_specs=pl.BlockSpec((tr, sw), lambda p, r: (r, p)),
        out_shape=jax.ShapeDtypeStruct((rows, np_ * sw), F32),
        scratch_shapes=[pltpu.VMEM((pw, sw), BF16)],
        compiler_params=_cparams(2),
        name="ssm_v",
    )(u, bq, wq)
    s = pl.pallas_call(
        functools.partial(_ssm_scan_kernel, n_chunks=nc, bsz=bsz),
        grid=(np_,),
        in_specs=[pl.BlockSpec((rows, sw), lambda p: (0, p)),
                  pl.BlockSpec((None, 4, LANES), lambda p: (p, 0, 0))],
        out_specs=pl.BlockSpec((rows, sw), lambda p: (0, p)),
        out_shape=jax.ShapeDtypeStruct((rows, np_ * sw), F32),
        compiler_params=_cparams(1),
        name="ssm_scan",
    )(v, alpha)
    y = pl.pallas_call(
        _ssm_y_kernel,
        grid=(np_, rows // tr),
        in_specs=[pl.BlockSpec((tr, pw), lambda p, r: (r, p)),
                  pl.BlockSpec((tr, sw), lambda p, r: (r, p)),
                  per_pair(kflat), per_pair(cp), per_pair(wp), whole(til), whole(rep), per_pair(skip)],
        out_specs=pl.BlockSpec((tr, pw), lambda p, r: (r, p)),
        out_shape=jax.ShapeDtypeStruct((rows, np_ * pw), BF16),
        scratch_shapes=[pltpu.VMEM((pw, pw), BF16), pltpu.VMEM((sw, pw), BF16)],
        compiler_params=_cparams(2),
        name="ssm_y",
    )(u, s, kflat, cp, wp, til, rep, skip)
    return y


def _mix_kernel(yp_ref, ys_ref, zp_ref, zs_ref, wglu_ref, bglu_ref, gssm_ref,
                lng_ref, lnb_ref, ws_ref, bs_ref, gsgu_ref, o_ref, ya_scr, *, n_prompt_tiles):
    i = pl.program_id(0)
    tm = o_ref.shape[0]
    n_blk = ya_scr.shape[0]
    d_ssm = n_blk * LANES
    nc = tm // SSM_CHUNK

    def body(y_ref, zuv_ref):
        masks = _atom_masks(nc)
        for b in range(n_blk):
            for v in range(SSM_CHUNK // ATOMS):
                src = [y_ref[:, (ATOMS * b + pi) * PAIR_W + v * LANES:
                             (ATOMS * b + pi) * PAIR_W + (v + 1) * LANES].astype(F32)
                       for pi in range(ATOMS)]
                dst = _atom_transpose(src, masks)
                for jj in range(ATOMS):
                    ya_scr[b, pl.ds(ATOMS * v + jj, nc, stride=SSM_CHUNK), :] = dst[jj]
        y = jnp.concatenate([ya_scr[b] for b in range(n_blk)], axis=1)
        gl = _gelu(y)
        gate = jnp.dot(gl.astype(BF16), wglu_ref[...], preferred_element_type=F32) + bglu_ref[...]
        o_ref[:, :d_ssm] = _rms(gl * jax.nn.sigmoid(gate), gssm_ref[...]).astype(BF16)
        d_sgu = zuv_ref.shape[1] // 2
        u = _gelu(zuv_ref[:, :d_sgu].astype(F32))
        gv = _gelu(zuv_ref[:, d_sgu:].astype(F32))
        xc = gv - jnp.mean(gv, axis=-1, keepdims=True)
        v = xc * lax.rsqrt(jnp.mean(xc * xc, axis=-1, keepdims=True) + LN_EPS)
        v = (v * lng_ref[...] + lnb_ref[...]).astype(BF16)
        lo = lax.broadcasted_iota(I32, (CHUNK, LANES), 1) < (LANES // 2)
        zero = jnp.zeros((CHUNK, LANES), BF16)
        rows = []
        for c in range(tm // CHUNK):
            cols = []
            for j in range(d_sgu // LANES):
                vp = v[c * CHUNK:(c + 1) * CHUNK, j * LANES:(j + 1) * LANES]
                rhs = jnp.concatenate([jnp.where(lo, vp, zero), jnp.where(lo, zero, vp)], axis=0)
                cols.append(jnp.dot(ws_ref[j], rhs, preferred_element_type=F32))
            rows.append(jnp.concatenate(cols, axis=1) + bs_ref[...])
        s = jnp.concatenate(rows, axis=0)
        o_ref[:, d_ssm:] = _rms(u * s, gsgu_ref[...]).astype(BF16)

    @pl.when(i < n_prompt_tiles)
    def _():
        body(yp_ref, zp_ref)

    @pl.when(i >= n_prompt_tiles)
    def _():
        body(ys_ref, zs_ref)


def _mix(y_p, y_s, zuv_p, zuv_s, wglu, bglu, gssm, lng, lnb, ws_pairs, bias_s, gsgu):
    tp, ts = zuv_p.shape[0], zuv_s.shape[0]
    d_sgu = zuv_p.shape[1] // 2
    d_ssm = wglu.shape[0]
    tm = TOKEN_TILE
    npt = tp // tm
    nc = tm // SSM_CHUNK
    uw = d_ssm * SSM_CHUNK
    ip = lambda i: jnp.minimum(i, npt - 1)
    isamp = lambda i: jnp.maximum(i - npt, 0)
    row = lambda w: pl.BlockSpec((1, w), lambda i: (0, 0))
    return pl.pallas_call(
        functools.partial(_mix_kernel, n_prompt_tiles=npt),
        grid=((tp + ts) // tm,),
        in_specs=[pl.BlockSpec((nc, uw), lambda i: (ip(i), 0)),
                  pl.BlockSpec((nc, uw), lambda i: (isamp(i), 0)),
                  pl.BlockSpec((tm, 2 * d_sgu), lambda i: (ip(i), 0)),
                  pl.BlockSpec((tm, 2 * d_sgu), lambda i: (isamp(i), 0)),
                  pl.BlockSpec((d_ssm, d_ssm), lambda i: (0, 0)),
                  row(d_ssm), row(d_ssm), row(d_sgu), row(d_sgu),
                  pl.BlockSpec(ws_pairs.shape, lambda i: (0, 0, 0)),
                  pl.BlockSpec(bias_s.shape, lambda i: (0, 0)),
                  row(d_sgu)],
        out_specs=pl.BlockSpec((tm, d_ssm + d_sgu), lambda i: (i, 0)),
        out_shape=jax.ShapeDtypeStruct((tp + ts, d_ssm + d_sgu), BF16),
        scratch_shapes=[pltpu.VMEM((d_ssm // LANES, tm, LANES), F32)],
        compiler_params=_cparams(),
        name="mix",
    )(y_p, y_s, zuv_p, zuv_s, wglu, bglu, gssm, lng, lnb, ws_pairs, bias_s, gsgu)


def _sorted_rows(tile):
    return TOP_K * tile + MXU_DIM


def _route_kernel(xp_ref, xs_ref, mix_ref, wout_ref, gffn_ref, whi_ref, wcat_ref, br_ref, tri_ref,
                  ltri_ref, x1_ref, xsort_ref, meta_ref, c8_ref, *, n_prompt_tiles, n_tiles, n_exp):
    i = pl.program_id(0)
    tm = mix_ref.shape[0]
    p_rows = xsort_ref.shape[0]

    def body(x_ref):
        x1 = x_ref[...] + jnp.dot(mix_ref[...], wout_ref[...], preferred_element_type=F32)
        x1_ref[...] = x1
        h2 = _rms(x1, gffn_ref[...])
        hi = h2.astype(BF16)
        lo = (h2 - hi.astype(F32)).astype(BF16)
        hw = jnp.dot(hi, wcat_ref[...], preferred_element_type=F32)
        lt = (hw[:, :LANES] + (jnp.dot(lo, whi_ref[...], preferred_element_type=F32) + hw[:, LANES:])
              + br_ref[...])
        logits = lt.T[:n_exp]
        eio = lax.broadcasted_iota(I32, (n_exp, tm), 0)
        vals, idxs = [], []
        l = logits
        for _ in range(TOP_K):
            m = jnp.max(l, axis=0, keepdims=True)
            idx = jnp.min(jnp.where(l == m, eio, n_exp), axis=0, keepdims=True)
            vals.append(m)
            idxs.append(idx)
            l = jnp.where(eio == idx, -jnp.inf, l)
        ex = [jnp.exp(v - vals[0]) for v in vals]
        den = ex[0] + ex[1] + ex[2] + ex[3]
        ws = [e / den for e in ex]
        hot = [eio == idx for idx in idxs]
        cnt = sum(h.astype(F32) for h in hot)
        prefix = jnp.dot(cnt.astype(BF16), tri_ref[...], preferred_element_type=F32)
        c = jnp.sum(cnt, axis=1, keepdims=True)
        c8 = jnp.floor((c + (SUBLANES - 1)) * (1.0 / SUBLANES)) * SUBLANES
        c8b = jnp.broadcast_to(c8, (n_exp, LANES))
        c8_ref[...] = c8b
        run0 = jnp.dot(ltri_ref[...], c8b.astype(BF16), preferred_element_type=F32)[:, 0:1]
        base = run0 + prefix
        pos = [jnp.sum(jnp.where(h, base, 0.0), axis=0, keepdims=True) for h in hot]
        meta = jnp.concatenate(ws + pos + [jnp.zeros((LANES - 2 * TOP_K, tm), F32)], axis=0)
        meta_ref[...] = meta.T
        rio = lax.broadcasted_iota(I32, (SORT_CHUNK, tm), 0).astype(F32).astype(BF16)
        one = jnp.ones((SORT_CHUNK, tm), BF16)
        zero = jnp.zeros((SORT_CHUNK, tm), BF16)
        for r in range(p_rows // SORT_CHUNK):
            lo_r = float(r * SORT_CHUNK)
            rel = [jnp.where((p >= lo_r) & (p < lo_r + SORT_CHUNK), p - lo_r, -1.0).astype(BF16) for p in pos]
            sel = (rio == rel[0]) | (rio == rel[1]) | (rio == rel[2]) | (rio == rel[3])
            srt = jnp.dot(jnp.where(sel, one, zero), hi, preferred_element_type=F32)
            xsort_ref[r * SORT_CHUNK:(r + 1) * SORT_CHUNK, :] = _pack_bf16_pairs(srt)

    @pl.when(i < n_prompt_tiles)
    def _():
        body(xp_ref)

    @pl.when((i >= n_prompt_tiles) & (i < n_tiles))
    def _():
        body(xs_ref)

    @pl.when(i == n_tiles)
    def _():
        xsort_ref[...] = jnp.zeros_like(xsort_ref)


def _dual_specs(tile, width, n_prompt_tiles, n_tiles):
    last_p = n_prompt_tiles - 1
    last_s = n_tiles - n_prompt_tiles - 1
    sp = pl.BlockSpec((tile, width), lambda i: (jnp.minimum(i, last_p), 0))
    ss = pl.BlockSpec((tile, width), lambda i: (jnp.clip(i - n_prompt_tiles, 0, last_s), 0))
    return sp, ss


def _route(xp, xs, mixed, w_out, g_ffn, wr_hi, wr_cat, br_pad, n_exp):
    tp, d = xp.shape
    t = mixed.shape[0]
    tm = TOKEN_TILE
    npt = tp // tm
    nt = t // tm
    p_rows = _sorted_rows(tm)
    iota = lambda n, ax: lax.broadcasted_iota(I32, (n, n), ax)
    tri = (iota(tm, 0) < iota(tm, 1)).astype(BF16)
    ltri = (iota(n_exp, 1) < iota(n_exp, 0)).astype(BF16)
    sp, ss = _dual_specs(tm, d, npt, nt)
    const = lambda shape: pl.BlockSpec(shape, lambda i: (0,) * len(shape))
    tile = lambda i: jnp.minimum(i, nt - 1)
    return pl.pallas_call(
        functools.partial(_route_kernel, n_prompt_tiles=npt, n_tiles=nt, n_exp=n_exp),
        grid=(nt + 1,),
        in_specs=[sp, ss,
                  pl.BlockSpec((tm, mixed.shape[1]), lambda i: (tile(i), 0)),
                  const(w_out.shape), const((1, d)), const(wr_hi.shape), const(wr_cat.shape),
                  const((1, LANES)), const((tm, tm)), const((n_exp, n_exp))],
        out_specs=[pl.BlockSpec((tm, d), lambda i: (tile(i), 0)),
                   pl.BlockSpec((p_rows, d // 2), lambda i: (i, 0)),
                   pl.BlockSpec((tm, LANES), lambda i: (tile(i), 0)),
                   pl.BlockSpec((None, n_exp, LANES), lambda i: (tile(i), 0, 0))],
        out_shape=[jax.ShapeDtypeStruct((t, d), F32),
                   jax.ShapeDtypeStruct(((nt + 1) * p_rows, d // 2), U32),
                   jax.ShapeDtypeStruct((t, LANES), F32),
                   jax.ShapeDtypeStruct((nt, n_exp, LANES), F32)],
        compiler_params=_cparams(),
        name="route",
    )(xp, xs, mixed, w_out, g_ffn, wr_hi, wr_cat, br_pad, tri, ltri)


def _slot_tables(c8, p_rows, n_blocks):
    nt, n_exp = c8.shape
    run0 = jnp.cumsum(c8, axis=1) - c8
    seg_len = c8.T
    cum = jnp.cumsum(seg_len, axis=1)
    tot = cum[:, -1]
    padded = (tot + MOE_TILE - 1) // MOE_TILE * MOE_TILE
    ends = jnp.cumsum(padded)
    starts = ends - padded
    n_used = (ends[-1] // MOE_TILE).astype(I32)
    blk = jnp.arange(n_blocks, dtype=I32)
    be = jnp.minimum(jnp.sum(ends[None, :] <= (blk * MOE_TILE)[:, None], axis=1), n_exp - 1).astype(I32)
    block_e = jnp.where(blk < n_used, be, be[jnp.maximum(n_used - 1, 0)])
    onehot = (be[:, None] == jnp.arange(n_exp, dtype=I32)[None, :]).astype(F32)
    pick = lambda tbl: jnp.dot(onehot, tbl.astype(F32), precision=lax.Precision.HIGHEST)
    run_end = pick(cum)
    run_beg = run_end - pick(seg_len)
    shift = pick((jnp.arange(nt, dtype=I32) * p_rows)[None, :] + run0.T) - run_beg
    grow = (blk[:, None] * MOE_TILE + jnp.arange(MOE_GROUPS, dtype=I32)[None, :] * SUBLANES).astype(F32)
    rel = grow - pick(starts[:, None])
    inside = (run_beg[:, None, :] <= rel[:, :, None]) & (rel[:, :, None] < run_end[:, None, :])
    src_row = (rel + jnp.sum(jnp.where(inside, shift[:, None, :], 0.0), axis=2)).astype(I32)
    valid = jnp.any(inside, axis=2) & (blk < n_used)[:, None]
    scratch0 = nt * p_rows // SUBLANES
    zero_group = scratch0 + 2 * MOE_GROUPS
    g_in = jnp.where(valid, src_row // SUBLANES, zero_group).astype(I32)
    g_scr = scratch0 + (blk[:, None] % 2) * MOE_GROUPS + jnp.arange(MOE_GROUPS, dtype=I32)[None, :]
    g_out = jnp.where(valid, src_row // SUBLANES, g_scr).astype(I32)
    return block_e, n_used, g_in, g_out


def _moe_kernel(be_ref, nu_ref, gin_ref, gnext_ref, gout_ref, xs_hbm, wgu_ref, bgu_ref, wd_ref, bd_ref,
                eo_hbm, xbuf, obuf, wgu_s, wd_s, isem, osem):
    j = pl.program_id(0)
    nu = nu_ref[0]
    f = wd_ref.shape[0]
    slot = lax.rem(j, 2)
    other = 1 - slot

    @pl.when((j < nu) & ((j == 0) | (be_ref[j] != be_ref[jnp.maximum(j - 1, 0)])))
    def _():
        for c in range(0, wgu_ref.shape[0], CAST_ROWS):
            wgu_s[c:c + CAST_ROWS, :] = wgu_ref[c:c + CAST_ROWS, :].astype(BF16)
        for c in range(0, wd_ref.shape[0], CAST_ROWS):
            wd_s[c:c + CAST_ROWS, :] = wd_ref[c:c + CAST_ROWS, :].astype(BF16)

    def in_copy(tbl_ref, r, sl):
        row = pl.multiple_of(tbl_ref[0, r] * SUBLANES, SUBLANES)
        return pltpu.make_async_copy(xs_hbm.at[pl.ds(row, SUBLANES), :],
                                     xbuf.at[sl, pl.ds(r * SUBLANES, SUBLANES), :], isem.at[sl])

    def out_copy(r, sl):
        row = pl.multiple_of(gout_ref[0, r] * SUBLANES, SUBLANES)
        return pltpu.make_async_copy(obuf.at[sl, pl.ds(r * SUBLANES, SUBLANES), :],
                                     eo_hbm.at[pl.ds(row, SUBLANES), :], osem.at[sl])

    def wait_in(sl):
        pltpu.make_async_copy(xs_hbm.at[pl.ds(0, MOE_TILE), :], xbuf.at[sl], isem.at[sl]).wait()

    def wait_out(sl):
        pltpu.make_async_copy(obuf.at[sl], eo_hbm.at[pl.ds(0, MOE_TILE), :], osem.at[sl]).wait()

    @pl.when(j == 0)
    def _():
        for r in range(MOE_GROUPS):
            in_copy(gin_ref, r, 0).start()

    @pl.when(j + 1 < nu)
    def _():
        for r in range(MOE_GROUPS):
            in_copy(gnext_ref, r, other).start()

    @pl.when(j < nu)
    def _():
        wait_in(slot)

        @pl.when(j >= 2)
        def _():
            wait_out(slot)

        x = _unpack_bf16_pairs(xbuf[slot])
        gate = jnp.dot(x, wgu_s[:, :f], preferred_element_type=F32) + bgu_ref[:, :f]
        gate = jnp.minimum(gate, SWIGLU_LIMIT)
        sg = gate * jax.nn.sigmoid(SWIGLU_ALPHA * gate)
        up = jnp.dot(x, wgu_s[:, f:], preferred_element_type=F32) + bgu_ref[:, f:]
        act = sg * (jnp.clip(up, -SWIGLU_LIMIT, SWIGLU_LIMIT) + 1.0)
        out = jnp.dot(act.astype(BF16), wd_s[...], preferred_element_type=F32) + bd_ref[...]
        obuf[slot] = _pack_bf16_pairs(out.astype(BF16).astype(F32))
        for r in range(MOE_GROUPS):
            out_copy(r, slot).start()

        @pl.when(j == nu - 1)
        def _():
            wait_out(slot)

            @pl.when(j >= 1)
            def _():
                wait_out(other)


def _moe(block_e, n_used, g_in, g_out, xsorted, wgu, bgu, wd, bd):
    n_blocks = block_e.shape[0]
    dh = xsorted.shape[1]
    d = 2 * dh
    f2 = wgu.shape[2]
    f = wd.shape[1]
    g_in3 = g_in.reshape(n_blocks, 1, MOE_GROUPS)
    g_next3 = jnp.concatenate([g_in3[1:], g_in3[:1]], axis=0)
    g_out3 = g_out.reshape(n_blocks, 1, MOE_GROUPS)
    tbl = pl.BlockSpec((None, 1, MOE_GROUPS), lambda j, be, nu: (j, 0, 0), memory_space=pltpu.SMEM)
    return pl.pallas_call(
        _moe_kernel,
        grid_spec=pltpu.PrefetchScalarGridSpec(
            num_scalar_prefetch=2, grid=(n_blocks,),
            in_specs=[tbl, tbl, tbl,
                      pl.BlockSpec(memory_space=pl.ANY),
                      pl.BlockSpec((None, d, f2), lambda j, be, nu: (be[j], 0, 0)),
                      pl.BlockSpec((None, 1, f2), lambda j, be, nu: (be[j], 0, 0)),
                      pl.BlockSpec((None, f, d), lambda j, be, nu: (be[j], 0, 0)),
                      pl.BlockSpec((None, 1, d), lambda j, be, nu: (be[j], 0, 0))],
            out_specs=pl.BlockSpec(memory_space=pl.ANY),
            scratch_shapes=[pltpu.VMEM((2, MOE_TILE, dh), U32), pltpu.VMEM((2, MOE_TILE, dh), U32),
                            pltpu.VMEM((d, f2), BF16), pltpu.VMEM((f, d), BF16),
                            pltpu.SemaphoreType.DMA((2,)), pltpu.SemaphoreType.DMA((2,))]),
        out_shape=jax.ShapeDtypeStruct(xsorted.shape, U32),
        input_output_aliases={5: 0},
        compiler_params=_cparams(),
        name="moe",
    )(block_e, n_used, g_in3, g_next3, g_out3, xsorted, wgu, bgu, wd, bd)


def _combine_kernel(meta_ref, x1_ref, g_ref, eo_ref, y_ref):
    tm = x1_ref.shape[0]
    p_rows = eo_ref.shape[0]
    lane = lax.broadcasted_iota(I32, (tm, SORT_CHUNK), 1).astype(F32).astype(BF16)
    ws = [meta_ref[:, k:k + 1].astype(BF16) for k in range(TOP_K)]
    pos = [meta_ref[:, TOP_K + k:TOP_K + k + 1] for k in range(TOP_K)]
    acc = x1_ref[...]
    for r in range(p_rows // SORT_CHUNK):
        buf = _unpack_bf16_pairs(eo_ref[r * SORT_CHUNK:(r + 1) * SORT_CHUNK, :])
        lo_r = float(r * SORT_CHUNK)
        wm = jnp.zeros((tm, SORT_CHUNK), BF16)
        for k in range(TOP_K):
            rel = jnp.where((pos[k] >= lo_r) & (pos[k] < lo_r + SORT_CHUNK), pos[k] - lo_r, -1.0).astype(BF16)
            wm = jnp.where(lane == rel, ws[k], wm)
        acc = acc + jnp.dot(wm, buf, preferred_element_type=F32)
    y_ref[...] = _rms(acc, g_ref[...])


def _combine(meta, x1, g_final, eo, tile_off, n_tokens):
    d = x1.shape[1]
    tm = TOKEN_TILE
    p_rows = _sorted_rows(tm)
    return pl.pallas_call(
        _combine_kernel,
        grid=(n_tokens // tm,),
        in_specs=[pl.BlockSpec((tm, LANES), lambda i: (i + tile_off, 0)),
                  pl.BlockSpec((tm, d), lambda i: (i + tile_off, 0)),
                  pl.BlockSpec((1, d), lambda i: (0, 0)),
                  pl.BlockSpec((p_rows, d // 2), lambda i: (i + tile_off, 0))],
        out_specs=pl.BlockSpec((tm, d), lambda i: (i, 0)),
        out_shape=jax.ShapeDtypeStruct((n_tokens, d), F32),
        compiler_params=_cparams(),
        name="combine",
    )(meta, x1, g_final, eo)


def kernel(x_prompt, x_sample, g_mix, w_in, ssm_lam_re, ssm_lam_im, ssm_log_dt, ssm_b_re, ssm_b_im,
           ssm_c_re, ssm_c_im, ssm_d, w_glu, b_glu, sgu_ln_g, sgu_ln_b, sgu_w_s, sgu_b_s,
           g_out_ssm, g_out_sgu, w_out, g_ffn, w_router, b_router, w_gate_up, b_gate_up,
           w_down, b_down, g_final):
    assert g_mix.shape[0] == 1, "single-layer trunk"
    bp, lp, d = x_prompt.shape
    bs, ls, _ = x_sample.shape
    tp, ts = bp * lp, bs * ls
    t = tp + ts
    d_ssm = ssm_d.shape[1]
    d_sgu = sgu_ln_g.shape[1]
    n_exp = w_router.shape[2]
    assert lp % TOKEN_TILE == 0 and ls % TOKEN_TILE == 0 and TOKEN_TILE % CHUNK == 0
    assert SSM_CHUNK * SSM_GROUP == MXU_DIM and 2 * ssm_lam_re.shape[-1] == LANES
    assert d_sgu // SGU_HEADS == LANES // 2 and n_exp <= LANES
    assert n_exp * (SUBLANES - 1) <= MXU_DIM and _sorted_rows(TOKEN_TILE) % SORT_CHUNK == 0
    assert (2 * MOE_GROUPS + 1) * SUBLANES <= _sorted_rows(TOKEN_TILE)

    xp = x_prompt.reshape(tp, d)
    xs = x_sample.reshape(ts, d)
    row = lambda a: a.reshape(1, -1).astype(F32)

    w_in_b = w_in[0].astype(BF16)
    u_p, zuv_p = _inproj(xp, row(g_mix[0]), w_in_b, d_ssm)
    u_s, zuv_s = _inproj(xs, row(g_mix[0]), w_in_b, d_ssm)

    mats = _ssm_matrices(ssm_lam_re[0], ssm_lam_im[0], ssm_log_dt[0], ssm_b_re[0], ssm_b_im[0],
                         ssm_c_re[0], ssm_c_im[0], ssm_d[0])
    y_p = _ssm_trunk(u_p, bp, mats)
    y_s = _ssm_trunk(u_s, bs, mats)

    ws = sgu_w_s[0]
    ws_pairs = jnp.concatenate([ws[0::2], ws[1::2]], axis=2).astype(BF16)
    bias_s = jnp.repeat(sgu_b_s[0].T, d_sgu // SGU_HEADS, axis=1).astype(F32)
    mixed = _mix(y_p, y_s, zuv_p, zuv_s, w_glu[0].astype(BF16), row(b_glu[0]), row(g_out_ssm[0]),
                 row(sgu_ln_g[0]), row(sgu_ln_b[0]), ws_pairs, bias_s, row(g_out_sgu[0]))

    wr_pad = jnp.pad(w_router[0].astype(F32), ((0, 0), (0, LANES - n_exp)))
    wr_hi = wr_pad.astype(BF16)
    wr_cat = jnp.concatenate([wr_hi, (wr_pad - wr_hi.astype(F32)).astype(BF16)], axis=1)
    br_pad = jnp.pad(b_router[0].astype(F32), (0, LANES - n_exp)).reshape(1, LANES)
    x1, xsorted, meta, c8 = _route(xp, xs, mixed, w_out[0].astype(BF16), row(g_ffn[0]),
                                   wr_hi, wr_cat, br_pad, n_exp)

    nt = t // TOKEN_TILE
    p_rows = _sorted_rows(TOKEN_TILE)
    max_rows = t * TOP_K + nt * n_exp * (SUBLANES - 1) + n_exp * (MOE_TILE - 1)
    n_blocks = -(-max_rows // MOE_TILE)
    block_e, n_used, g_in, g_out = _slot_tables(c8[:, :, 0].astype(I32), p_rows, n_blocks)
    eo = _moe(block_e, n_used.reshape(1), g_in, g_out, xsorted, w_gate_up[0],
              b_gate_up[0][:, None, :].astype(F32), w_down[0], b_down[0][:, None, :].astype(F32))

    gf = row(g_final)
    y_prompt = _combine(meta, x1, gf, eo, 0, tp)
    y_sample = _combine(meta, x1, gf, eo, tp // TOKEN_TILE, ts)
    return y_prompt.reshape(bp, lp, d), y_sample.reshape(bs, ls, d)
```

```python
import functools
import math

import jax
import jax.numpy as jnp
from jax import lax
from jax.experimental import pallas as pl
from jax.experimental.pallas import tpu as pltpu

F32 = jnp.float32
BF16 = jnp.bfloat16
I32 = jnp.int32
U32 = jnp.uint32

SSM_GROUP = 16
SGU_HEADS = 8
CHUNK = 128
TOP_K = 4
SWIGLU_LIMIT = 7.0
SWIGLU_ALPHA = 1.702
RMS_EPS = 1e-6
LN_EPS = 1e-5

LANES = 128
SUBLANES = 8
MXU_DIM = 256
SSM_CHUNK = MXU_DIM // SSM_GROUP
PAIR_W = 2 * SSM_CHUNK * SSM_GROUP
ATOM = 2 * SSM_GROUP
ATOMS = LANES // ATOM

TOKEN_TILE = 512
MOE_TILE = 512
MOE_GROUPS = MOE_TILE // SUBLANES
SORT_CHUNK = 256
CAST_ROWS = 128
SSM_ROW_TILE = 2048
SCAN_SEQS = 4
VMEM_LIMIT = 56 * 1024 * 1024


def _cparams(n_axes=1, vmem=None):
    return pltpu.CompilerParams(
        dimension_semantics=("arbitrary",) * n_axes,
        vmem_limit_bytes=vmem if vmem is not None else VMEM_LIMIT,
    )


def _rms(x, g):
    return x * lax.rsqrt(jnp.mean(x * x, axis=-1, keepdims=True) + RMS_EPS) * g


def _gelu(x):
    return x * (lax.erf(x * (1.0 / math.sqrt(2.0))) + 1.0) * 0.5


def _atom_masks(rows):
    lane = lax.broadcasted_iota(I32, (rows, LANES), 1)
    return [(lane >= a * ATOM) & (lane < (a + 1) * ATOM) for a in range(ATOMS)]


def _atom_transpose(src, masks):
    dst = []
    for b in range(ATOMS):
        acc = None
        for a in range(ATOMS):
            r = (a - b) % ATOMS
            piece = src[a] if r == 0 else pltpu.roll(src[a], ATOM * r, axis=1)
            acc = piece if acc is None else jnp.where(masks[a], piece, acc)
        dst.append(acc)
    return dst


def _pack_bf16_pairs(x):
    w = x.shape[1] // 2
    lo = lax.bitcast_convert_type(x[:, :w], U32) >> 16
    hi = lax.bitcast_convert_type(x[:, w:], U32) & jnp.uint32(0xFFFF0000)
    return hi | lo


def _unpack_bf16_pairs(u):
    lo = lax.bitcast_convert_type(u << 16, F32)
    hi = lax.bitcast_convert_type(u & jnp.uint32(0xFFFF0000), F32)
    return jnp.concatenate([lo.astype(BF16), hi.astype(BF16)], axis=1)


def _inproj_kernel(x_ref, g_ref, w_ref, u_ref, zuv_ref, za_scr, *, d_ssm):
    tm = x_ref.shape[0]
    nc = tm // SSM_CHUNK
    h = _rms(x_ref[...], g_ref[...])
    z = jnp.dot(h.astype(BF16), w_ref[...], preferred_element_type=F32)
    zuv_ref[...] = z[:, d_ssm:].astype(BF16)
    n_blk = d_ssm // LANES
    for b in range(n_blk):
        za_scr[b] = z[:, b * LANES:(b + 1) * LANES]
    masks = _atom_masks(nc)
    n_quads = SSM_CHUNK // ATOMS
    for b in range(n_blk):
        for v in range(n_quads):
            src = [za_scr[b, pl.ds(ATOMS * v + jj, nc, stride=SSM_CHUNK), :] for jj in range(ATOMS)]
            dst = _atom_transpose(src, masks)
            for pi in range(ATOMS):
                c0 = (ATOMS * b + pi) * PAIR_W + v * LANES
                u_ref[:, c0:c0 + LANES] = dst[pi].astype(BF16)


def _inproj(x, g_mix, w_in, d_ssm):
    t, d = x.shape
    tm = TOKEN_TILE
    d_in = w_in.shape[1]
    nc = tm // SSM_CHUNK
    uw = d_ssm * SSM_CHUNK
    return pl.pallas_call(
        functools.partial(_inproj_kernel, d_ssm=d_ssm),
        grid=(t // tm,),
        in_specs=[pl.BlockSpec((tm, d), lambda i: (i, 0)),
                  pl.BlockSpec((1, d), lambda i: (0, 0)),
                  pl.BlockSpec((d, d_in), lambda i: (0, 0))],
        out_specs=[pl.BlockSpec((nc, uw), lambda i: (i, 0)),
                   pl.BlockSpec((tm, d_in - d_ssm), lambda i: (i, 0))],
        out_shape=[jax.ShapeDtypeStruct((t // SSM_CHUNK, uw), BF16),
                   jax.ShapeDtypeStruct((t, d_in - d_ssm), BF16)],
        scratch_shapes=[pltpu.VMEM((d_ssm // LANES, tm, LANES), F32)],
        compiler_params=_cparams(),
        name="inproj",
    )(x, g_mix, w_in)


def _ssm_matrices(lam_re, lam_im, log_dt, b_re, b_im, c_re, c_im, d_skip):
    hp = lax.Precision.HIGHEST
    _, g, n = lam_re.shape
    p = b_re.shape[-1]
    lc = SSM_CHUNK
    np_ = g // 2
    dt = jnp.exp(log_dt)[..., None]
    mag = jnp.exp(lam_re * dt)
    ar = mag * jnp.cos(lam_im * dt)
    ai = mag * jnp.sin(lam_im * dt)
    den = lam_re * lam_re + lam_im * lam_im
    nr = ar - 1.0
    fr = (nr * lam_re + ai * lam_im) / den
    fi = (ai * lam_re - nr * lam_im) / den
    bbr = fr[..., None] * b_re - fi[..., None] * b_im
    bbi = fr[..., None] * b_im + fi[..., None] * b_re

    prs, pis = [jnp.ones_like(ar)], [jnp.zeros_like(ai)]
    for _ in range(lc):
        pr, pi = prs[-1], pis[-1]
        prs.append(pr * ar - pi * ai)
        pis.append(pr * ai + pi * ar)
    pw_r, pw_i = jnp.stack(prs), jnp.stack(pis)

    lane_k = jnp.arange(lc * p)
    til_k = (lane_k[None, :] % p == jnp.arange(p)[:, None]).astype(F32)
    rep_k = (lane_k[None, :] // p == jnp.arange(lc)[:, None]).astype(F32)
    ex = lambda x, e: jnp.einsum('dgnk,kx->dgnx', x, e, precision=hp)
    ct_r = ex(jnp.transpose(c_re, (0, 1, 3, 2)), til_k)
    ct_i = ex(jnp.transpose(c_im, (0, 1, 3, 2)), til_k)
    at_r = ex(jnp.transpose(pw_r[:lc], (1, 2, 3, 0)), rep_k)
    at_i = ex(jnp.transpose(pw_i[:lc], (1, 2, 3, 0)), rep_k)
    kk = (jnp.einsum('dgnq,dgnx->dgqx', bbr, ct_r * at_r - ct_i * at_i, precision=hp)
          - jnp.einsum('dgnq,dgnx->dgqx', bbi, ct_r * at_i + ct_i * at_r, precision=hp))
    kk = kk.reshape(2, g, p, lc, p)
    kf, kb = kk[0], kk[1]
    eye = jnp.eye(2, dtype=F32)
    a2 = p * 2

    k_all = jnp.concatenate([kb[:, :, :0:-1], (kf[:, :, :1] + kb[:, :, :1]), kf[:, :, 1:]], axis=2)
    k_all = k_all.reshape(np_, 2, p, 2 * lc - 1, p)
    kflat = jnp.einsum('nsqup,st->nsqutp', k_all, eye).reshape(np_, a2, (2 * lc - 1) * a2)
    kflat = jnp.pad(kflat, ((0, 0), (0, 0), (0, a2)))

    def atoms_q(x):
        x = jnp.transpose(x, (0, 1, 3, 2)).reshape(2, np_, 2, p, n)
        return jnp.einsum('dnsqm,st->dnsqtm', x, eye).reshape(2, np_, a2, 2 * n)
    bq_r, bq_i = atoms_q(bbr), atoms_q(bbi)
    bq = jnp.stack([bq_r[0], bq_i[0], bq_r[1], bq_i[1]], axis=1)
    ef = lc - 1 - jnp.arange(lc)
    eb = jnp.arange(lc)
    rows_q = lambda w, e, d: jnp.transpose(w[e, d].reshape(lc, np_, 2 * n), (1, 0, 2))
    wq = jnp.stack([rows_q(pw_r, ef, 0), rows_q(pw_i, ef, 0),
                    rows_q(pw_r, eb, 1), rows_q(pw_i, eb, 1)], axis=1)

    def atoms_p(x):
        x = jnp.transpose(x, (0, 1, 3, 2)).reshape(2, np_, 2, n, p)
        return jnp.einsum('dnsmp,st->dnsmtp', x, eye).reshape(2, np_, 2 * n, a2)
    cp_r, cp_i = atoms_p(c_re), atoms_p(c_im)
    cp = jnp.stack([cp_r[0], cp_i[0], cp_r[1], cp_i[1]], axis=1)
    pf_e = jnp.arange(lc) + 1
    pb_e = lc - jnp.arange(lc)
    cols_p = lambda w, e, d: jnp.transpose(w[e, d].reshape(lc, np_, 2 * n), (1, 2, 0))
    wp = jnp.stack([cols_p(pw_r, pf_e, 0), cols_p(pw_i, pf_e, 0),
                    cols_p(pw_r, pb_e, 1), cols_p(pw_i, pb_e, 1)], axis=1)
    lane = jnp.arange(lc * a2)
    til = (lane[None, :] % a2 == jnp.arange(a2)[:, None]).astype(BF16)
    rep = (lane[None, :] // a2 == jnp.arange(lc)[:, None]).astype(BF16)

    al = jnp.stack([pw_r[lc, 0], pw_i[lc, 0], pw_r[lc, 1], pw_i[lc, 1]])
    alpha = jnp.transpose(al.reshape(4, np_, 2 * n), (1, 0, 2))
    skip = jnp.broadcast_to(d_skip.reshape(np_, 1, a2), (np_, lc, a2)).reshape(np_, 1, lc * a2)
    return kflat, bq, wq, cp, wp, til, rep, alpha, skip.astype(F32)


def _ssm_v_kernel(u_ref, bq_ref, wq_ref, v_ref, q_scr):
    a2 = bq_ref.shape[1]
    w = bq_ref.shape[2]

    @pl.when(pl.program_id(1) == 0)
    def _():
        for d in range(2):
            br, bi = bq_ref[2 * d], bq_ref[2 * d + 1]
            for j in range(SSM_CHUNK):
                wr = wq_ref[2 * d, j:j + 1, :]
                wi = wq_ref[2 * d + 1, j:j + 1, :]
                q_scr[j * a2:(j + 1) * a2, (2 * d) * w:(2 * d + 1) * w] = (wr * br - wi * bi).astype(BF16)
                q_scr[j * a2:(j + 1) * a2, (2 * d + 1) * w:(2 * d + 2) * w] = (wr * bi + wi * br).astype(BF16)

    v_ref[...] = jnp.dot(u_ref[...], q_scr[...], preferred_element_type=F32)


def _ssm_scan_kernel(v_ref, a_ref, s_ref, *, n_chunks, bsz):
    w = LANES
    sub = SUBLANES
    n_groups = n_chunks // sub
    row = lax.broadcasted_iota(I32, (sub, w), 0)

    def cmul(ar, ai, xr, xi):
        return ar * xr - ai * xi, ar * xi + ai * xr

    def powers(ar, ai):
        p2 = cmul(ar, ai, ar, ai)
        p4 = cmul(*p2, *p2)
        p8 = cmul(*p4, *p4)
        p3 = cmul(*p2, ar, ai)
        p5 = cmul(*p4, ar, ai)
        p6 = cmul(*p4, *p2)
        p7 = cmul(*p4, *p3)
        seq = [(jnp.ones_like(ar), jnp.zeros_like(ai)), (ar, ai), p2, p3, p4, p5, p6, p7]
        tr = jnp.concatenate([s[0] for s in seq], axis=0)
        ti = jnp.concatenate([s[1] for s in seq], axis=0)
        return (ar, ai), p2, p4, p8, (tr, ti)

    def tile_scan(vr, vi, pw, reverse):
        a1, a2, a4, _, _ = pw

        def shift(x, s):
            if reverse:
                return jnp.where(row < sub - s, pltpu.roll(x, sub - s, axis=0), 0.0)
            return jnp.where(row >= s, pltpu.roll(x, s, axis=0), 0.0)

        xr, xi = shift(vr, 1), shift(vi, 1)
        for s, (ar, ai) in ((1, a1), (2, a2), (4, a4)):
            mr, mi = cmul(ar, ai, shift(xr, s), shift(xi, s))
            xr, xi = xr + mr, xi + mi
        e = 0 if reverse else sub - 1
        lr, li = cmul(a1[0], a1[1], xr[e:e + 1], xi[e:e + 1])
        return xr, xi, lr + vr[e:e + 1], li + vi[e:e + 1]

    pw_f = powers(a_ref[0:1, :], a_ref[1:2, :])
    pw_b = powers(a_ref[2:3, :], a_ref[3:4, :])
    tbr = jnp.concatenate([pw_b[4][0][sub - 1 - k:sub - k] for k in range(sub)], axis=0)
    tbi = jnp.concatenate([pw_b[4][1][sub - 1 - k:sub - k] for k in range(sub)], axis=0)

    def step(g, carry, b0):
        out = []
        for b in range(b0, b0 + len(carry) // 4):
            cfr, cfi, cbr, cbi = carry[4 * (b - b0):4 * (b - b0) + 4]
            rf = pl.multiple_of(b * n_chunks + g * sub, sub)
            rb = pl.multiple_of(b * n_chunks + (n_groups - 1 - g) * sub, sub)
            xr, xi, lr, li = tile_scan(v_ref[pl.ds(rf, sub), 0:w], v_ref[pl.ds(rf, sub), w:2 * w], pw_f, False)
            mr, mi = cmul(pw_f[4][0], pw_f[4][1], cfr, cfi)
            s_ref[pl.ds(rf, sub), 0:w] = xr + mr
            s_ref[pl.ds(rf, sub), w:2 * w] = xi + mi
            nr, ni = cmul(pw_f[3][0], pw_f[3][1], cfr, cfi)
            out += [nr + lr, ni + li]
            xr, xi, lr, li = tile_scan(v_ref[pl.ds(rb, sub), 2 * w:3 * w], v_ref[pl.ds(rb, sub), 3 * w:4 * w],
                                       pw_b, True)
            mr, mi = cmul(tbr, tbi, cbr, cbi)
            s_ref[pl.ds(rb, sub), 2 * w:3 * w] = xr + mr
            s_ref[pl.ds(rb, sub), 3 * w:4 * w] = xi + mi
            nr, ni = cmul(pw_b[3][0], pw_b[3][1], cbr, cbi)
            out += [nr + lr, ni + li]
        return tuple(out)

    zero = jnp.zeros((1, w), F32)
    for b0 in range(0, bsz, SCAN_SEQS):
        nb = min(SCAN_SEQS, bsz - b0)
        lax.fori_loop(0, n_groups, functools.partial(step, b0=b0), (zero,) * (4 * nb))


def _ssm_y_kernel(u_ref, s_ref, kflat_ref, cp_ref, wp_ref, til_ref, rep_ref, d_ref, y_ref, m_scr, p_scr):
    a2 = kflat_ref.shape[0]
    w = cp_ref.shape[1]

    def expand(x, e_ref):
        hi = x.astype(BF16)
        lo = (x - hi.astype(F32)).astype(BF16)
        return (jnp.dot(hi, e_ref[...], preferred_element_type=F32)
                + jnp.dot(lo, e_ref[...], preferred_element_type=F32))

    @pl.when(pl.program_id(1) == 0)
    def _():
        kflat = kflat_ref[...]
        for j in range(SSM_CHUNK):
            off = (SSM_CHUNK - 1 - j) * a2
            m_scr[j * a2:(j + 1) * a2, :] = kflat[:, off:off + PAIR_W].astype(BF16)
        for d in range(2):
            cr, ci = expand(cp_ref[2 * d], til_ref), expand(cp_ref[2 * d + 1], til_ref)
            wr, wi = expand(wp_ref[2 * d], rep_ref), expand(wp_ref[2 * d + 1], rep_ref)
            p_scr[(2 * d) * w:(2 * d + 1) * w, :] = (cr * wr - ci * wi).astype(BF16)
            p_scr[(2 * d + 1) * w:(2 * d + 2) * w, :] = (-(cr * wi + ci * wr)).astype(BF16)

    u = u_ref[...]
    y = jnp.dot(u, m_scr[...], preferred_element_type=F32)
    y = y + jnp.dot(s_ref[...].astype(BF16), p_scr[...], preferred_element_type=F32)
    y_ref[...] = (y + d_ref[...] * u.astype(F32)).astype(BF16)


def _ssm_trunk(u, bsz, mats):
    kflat, bq, wq, cp, wp, til, rep, alpha, skip = mats
    rows = u.shape[0]
    nc = rows // bsz
    np_ = kflat.shape[0]
    pw = PAIR_W
    sw = 4 * bq.shape[3]
    tr = min(SSM_ROW_TILE, rows)
    per_pair = lambda a: pl.BlockSpec((None,) + a.shape[1:], lambda p, r: (p,) + (0,) * (a.ndim - 1))
    whole = lambda a: pl.BlockSpec(a.shape, lambda p, r: (0,) * a.ndim)
    v = pl.pallas_call(
        _ssm_v_kernel,
        grid=(np_, rows // tr),
        in_specs=[pl.BlockSpec((tr, pw), lambda p, r: (r, p)), per_pair(bq), per_pair(wq)],
        out_specs=pl.BlockSpec((tr, sw), lambda p, r: (r, p)),
        out_shape=jax.ShapeDtypeStruct((rows, np_ * sw), F32),
        scratch_shapes=[pltpu.VMEM((pw, sw), BF16)],
        compiler_params=_cparams(2),
        name="ssm_v",
    )(u, bq, wq)
    s = pl.pallas_call(
        functools.partial(_ssm_scan_kernel, n_chunks=nc, bsz=bsz),
        grid=(np_,),
        in_specs=[pl.BlockSpec((rows, sw), lambda p: (0, p)),
                  pl.BlockSpec((None, 4, LANES), lambda p: (p, 0, 0))],
        out_specs=pl.BlockSpec((rows, sw), lambda p: (0, p)),
        out_shape=jax.ShapeDtypeStruct((rows, np_ * sw), F32),
        compiler_params=_cparams(1),
        name="ssm_scan",
    )(v, alpha)
    y = pl.pallas_call(
        _ssm_y_kernel,
        grid=(np_, rows // tr),
        in_specs=[pl.BlockSpec((tr, pw), lambda p, r: (r, p)),
                  pl.BlockSpec((tr, sw), lambda p, r: (r, p)),
                  per_pair(kflat), per_pair(cp), per_pair(wp), whole(til), whole(rep), per_pair(skip)],
        out_specs=pl.BlockSpec((tr, pw), lambda p, r: (r, p)),
        out_shape=jax.ShapeDtypeStruct((rows, np_ * pw), BF16),
        scratch_shapes=[pltpu.VMEM((pw, pw), BF16), pltpu.VMEM((sw, pw), BF16)],
        compiler_params=_cparams(2),
        name="ssm_y",
    )(u, s, kflat, cp, wp, til, rep, skip)
    return y


def _mix_kernel(yp_ref, ys_ref, zp_ref, zs_ref, wglu_ref, bglu_ref, gssm_ref,
                lng_ref, lnb_ref, ws_ref, bs_ref, gsgu_ref, o_ref, ya_scr, *, n_prompt_tiles):
    i = pl.program_id(0)
    tm = o_ref.shape[0]
    n_blk = ya_scr.shape[0]
    d_ssm = n_blk * LANES
    nc = tm // SSM_CHUNK

    def body(y_ref, zuv_ref):
        masks = _atom_masks(nc)
        for b in range(n_blk):
            for v in range(SSM_CHUNK // ATOMS):
                src = [y_ref[:, (ATOMS * b + pi) * PAIR_W + v * LANES:
                             (ATOMS * b + pi) * PAIR_W + (v + 1) * LANES].astype(F32)
                       for pi in range(ATOMS)]
                dst = _atom_transpose(src, masks)
                for jj in range(ATOMS):
                    ya_scr[b, pl.ds(ATOMS * v + jj, nc, stride=SSM_CHUNK), :] = dst[jj]
        y = jnp.concatenate([ya_scr[b] for b in range(n_blk)], axis=1)
        gl = _gelu(y)
        gate = jnp.dot(gl.astype(BF16), wglu_ref[...], preferred_element_type=F32) + bglu_ref[...]
        o_ref[:, :d_ssm] = _rms(gl * jax.nn.sigmoid(gate), gssm_ref[...]).astype(BF16)
        d_sgu = zuv_ref.shape[1] // 2
        u = _gelu(zuv_ref[:, :d_sgu].astype(F32))
        gv = _gelu(zuv_ref[:, d_sgu:].astype(F32))
        xc = gv - jnp.mean(gv, axis=-1, keepdims=True)
        v = xc * lax.rsqrt(jnp.mean(xc * xc, axis=-1, keepdims=True) + LN_EPS)
        v = (v * lng_ref[...] + lnb_ref[...]).astype(BF16)
        lo = lax.broadcasted_iota(I32, (CHUNK, LANES), 1) < (LANES // 2)
        zero = jnp.zeros((CHUNK, LANES), BF16)
        rows = []
        for c in range(tm // CHUNK):
            cols = []
            for j in range(d_sgu // LANES):
                vp = v[c * CHUNK:(c + 1) * CHUNK, j * LANES:(j + 1) * LANES]
                rhs = jnp.concatenate([jnp.where(lo, vp, zero), jnp.where(lo, zero, vp)], axis=0)
                cols.append(jnp.dot(ws_ref[j], rhs, preferred_element_type=F32))
            rows.append(jnp.concatenate(cols, axis=1) + bs_ref[...])
        s = jnp.concatenate(rows, axis=0)
        o_ref[:, d_ssm:] = _rms(u * s, gsgu_ref[...]).astype(BF16)

    @pl.when(i < n_prompt_tiles)
    def _():
        body(yp_ref, zp_ref)

    @pl.when(i >= n_prompt_tiles)
    def _():
        body(ys_ref, zs_ref)


def _mix(y_p, y_s, zuv_p, zuv_s, wglu, bglu, gssm, lng, lnb, ws_pairs, bias_s, gsgu):
    tp, ts = zuv_p.shape[0], zuv_s.shape[0]
    d_sgu = zuv_p.shape[1] // 2
    d_ssm = wglu.shape[0]
    tm = TOKEN_TILE
    npt = tp // tm
    nc = tm // SSM_CHUNK
    uw = d_ssm * SSM_CHUNK
    ip = lambda i: jnp.minimum(i, npt - 1)
    isamp = lambda i: jnp.maximum(i - npt, 0)
    row = lambda w: pl.BlockSpec((1, w), lambda i: (0, 0))
    return pl.pallas_call(
        functools.partial(_mix_kernel, n_prompt_tiles=npt),
        grid=((tp + ts) // tm,),
        in_specs=[pl.BlockSpec((nc, uw), lambda i: (ip(i), 0)),
                  pl.BlockSpec((nc, uw), lambda i: (isamp(i), 0)),
                  pl.BlockSpec((tm, 2 * d_sgu), lambda i: (ip(i), 0)),
                  pl.BlockSpec((tm, 2 * d_sgu), lambda i: (isamp(i), 0)),
                  pl.BlockSpec((d_ssm, d_ssm), lambda i: (0, 0)),
                  row(d_ssm), row(d_ssm), row(d_sgu), row(d_sgu),
                  pl.BlockSpec(ws_pairs.shape, lambda i: (0, 0, 0)),
                  pl.BlockSpec(bias_s.shape, lambda i: (0, 0)),
                  row(d_sgu)],
        out_specs=pl.BlockSpec((tm, d_ssm + d_sgu), lambda i: (i, 0)),
        out_shape=jax.ShapeDtypeStruct((tp + ts, d_ssm + d_sgu), BF16),
        scratch_shapes=[pltpu.VMEM((d_ssm // LANES, tm, LANES), F32)],
        compiler_params=_cparams(),
        name="mix",
    )(y_p, y_s, zuv_p, zuv_s, wglu, bglu, gssm, lng, lnb, ws_pairs, bias_s, gsgu)


def _sorted_rows(tile):
    return TOP_K * tile + MXU_DIM


def _route_kernel(xp_ref, xs_ref, mix_ref, wout_ref, gffn_ref, whi_ref, wcat_ref, br_ref, tri_ref,
                  ltri_ref, x1_ref, xsort_ref, meta_ref, c8_ref, *, n_prompt_tiles, n_tiles, n_exp):
    i = pl.program_id(0)
    tm = mix_ref.shape[0]
    p_rows = xsort_ref.shape[0]

    def body(x_ref):
        x1 = x_ref[...] + jnp.dot(mix_ref[...], wout_ref[...], preferred_element_type=F32)
        x1_ref[...] = x1
        h2 = _rms(x1, gffn_ref[...])
        hi = h2.astype(BF16)
        lo = (h2 - hi.astype(F32)).astype(BF16)
        hw = jnp.dot(hi, wcat_ref[...], preferred_element_type=F32)
        lt = (hw[:, :LANES] + (jnp.dot(lo, whi_ref[...], preferred_element_type=F32) + hw[:, LANES:])
              + br_ref[...])
        logits = lt.T[:n_exp]
        eio = lax.broadcasted_iota(I32, (n_exp, tm), 0)
        vals, idxs = [], []
        l = logits
        for _ in range(TOP_K):
            m = jnp.max(l, axis=0, keepdims=True)
            idx = jnp.min(jnp.where(l == m, eio, n_exp), axis=0, keepdims=True)
            vals.append(m)
            idxs.append(idx)
            l = jnp.where(eio == idx, -jnp.inf, l)
        ex = [jnp.exp(v - vals[0]) for v in vals]
        den = ex[0] + ex[1] + ex[2] + ex[3]
        ws = [e / den for e in ex]
        hot = [eio == idx for idx in idxs]
        cnt = sum(h.astype(F32) for h in hot)
        prefix = jnp.dot(cnt.astype(BF16), tri_ref[...], preferred_element_type=F32)
        c = jnp.sum(cnt, axis=1, keepdims=True)
        c8 = jnp.floor((c + (SUBLANES - 1)) * (1.0 / SUBLANES)) * SUBLANES
        c8b = jnp.broadcast_to(c8, (n_exp, LANES))
        c8_ref[...] = c8b
        run0 = jnp.dot(ltri_ref[...], c8b.astype(BF16), preferred_element_type=F32)[:, 0:1]
        base = run0 + prefix
        pos = [jnp.sum(jnp.where(h, base, 0.0), axis=0, keepdims=True) for h in hot]
        meta = jnp.concatenate(ws + pos + [jnp.zeros((LANES - 2 * TOP_K, tm), F32)], axis=0)
        meta_ref[...] = meta.T
        rio = lax.broadcasted_iota(I32, (SORT_CHUNK, tm), 0).astype(F32).astype(BF16)
        one = jnp.ones((SORT_CHUNK, tm), BF16)
        zero = jnp.zeros((SORT_CHUNK, tm), BF16)
        for r in range(p_rows // SORT_CHUNK):
            lo_r = float(r * SORT_CHUNK)
            rel = [jnp.where((p >= lo_r) & (p < lo_r + SORT_CHUNK), p - lo_r, -1.0).astype(BF16) for p in pos]
            sel = (rio == rel[0]) | (rio == rel[1]) | (rio == rel[2]) | (rio == rel[3])
            srt = jnp.dot(jnp.where(sel, one, zero), hi, preferred_element_type=F32)
            xsort_ref[r * SORT_CHUNK:(r + 1) * SORT_CHUNK, :] = _pack_bf16_pairs(srt)

    @pl.when(i < n_prompt_tiles)
    def _():
        body(xp_ref)

    @pl.when((i >= n_prompt_tiles) & (i < n_tiles))
    def _():
        body(xs_ref)

    @pl.when(i == n_tiles)
    def _():
        xsort_ref[...] = jnp.zeros_like(xsort_ref)


def _dual_specs(tile, width, n_prompt_tiles, n_tiles):
    last_p = n_prompt_tiles - 1
    last_s = n_tiles - n_prompt_tiles - 1
    sp = pl.BlockSpec((tile, width), lambda i: (jnp.minimum(i, last_p), 0))
    ss = pl.BlockSpec((tile, width), lambda i: (jnp.clip(i - n_prompt_tiles, 0, last_s), 0))
    return sp, ss


def _route(xp, xs, mixed, w_out, g_ffn, wr_hi, wr_cat, br_pad, n_exp):
    tp, d = xp.shape
    t = mixed.shape[0]
    tm = TOKEN_TILE
    npt = tp // tm
    nt = t // tm
    p_rows = _sorted_rows(tm)
    iota = lambda n, ax: lax.broadcasted_iota(I32, (n, n), ax)
    tri = (iota(tm, 0) < iota(tm, 1)).astype(BF16)
    ltri = (iota(n_exp, 1) < iota(n_exp, 0)).astype(BF16)
    sp, ss = _dual_specs(tm, d, npt, nt)
    const = lambda shape: pl.BlockSpec(shape, lambda i: (0,) * len(shape))
    tile = lambda i: jnp.minimum(i, nt - 1)
    return pl.pallas_call(
        functools.partial(_route_kernel, n_prompt_tiles=npt, n_tiles=nt, n_exp=n_exp),
        grid=(nt + 1,),
        in_specs=[sp, ss,
                  pl.BlockSpec((tm, mixed.shape[1]), lambda i: (tile(i), 0)),
                  const(w_out.shape), const((1, d)), const(wr_hi.shape), const(wr_cat.shape),
                  const((1, LANES)), const((tm, tm)), const((n_exp, n_exp))],
        out_specs=[pl.BlockSpec((tm, d), lambda i: (tile(i), 0)),
                   pl.BlockSpec((p_rows, d // 2), lambda i: (i, 0)),
                   pl.BlockSpec((tm, LANES), lambda i: (tile(i), 0)),
                   pl.BlockSpec((None, n_exp, LANES), lambda i: (tile(i), 0, 0))],
        out_shape=[jax.ShapeDtypeStruct((t, d), F32),
                   jax.ShapeDtypeStruct(((nt + 1) * p_rows, d // 2), U32),
                   jax.ShapeDtypeStruct((t, LANES), F32),
                   jax.ShapeDtypeStruct((nt, n_exp, LANES), F32)],
        compiler_params=_cparams(),
        name="route",
    )(xp, xs, mixed, w_out, g_ffn, wr_hi, wr_cat, br_pad, tri, ltri)


def _slot_tables(c8, p_rows, n_blocks):
    nt, n_exp = c8.shape
    run0 = jnp.cumsum(c8, axis=1) - c8
    seg_len = c8.T
    cum = jnp.cumsum(seg_len, axis=1)
    tot = cum[:, -1]
    padded = (tot + MOE_TILE - 1) // MOE_TILE * MOE_TILE
    ends = jnp.cumsum(padded)
    starts = ends - padded
    n_used = (ends[-1] // MOE_TILE).astype(I32)
    blk = jnp.arange(n_blocks, dtype=I32)
    be = jnp.minimum(jnp.sum(ends[None, :] <= (blk * MOE_TILE)[:, None], axis=1), n_exp - 1).astype(I32)
    block_e = jnp.where(blk < n_used, be, be[jnp.maximum(n_used - 1, 0)])
    onehot = (be[:, None] == jnp.arange(n_exp, dtype=I32)[None, :]).astype(F32)
    pick = lambda tbl: jnp.dot(onehot, tbl.astype(F32), precision=lax.Precision.HIGHEST)
    run_end = pick(cum)
    run_beg = run_end - pick(seg_len)
    shift = pick((jnp.arange(nt, dtype=I32) * p_rows)[None, :] + run0.T) - run_beg
    grow = (blk[:, None] * MOE_TILE + jnp.arange(MOE_GROUPS, dtype=I32)[None, :] * SUBLANES).astype(F32)
    rel = grow - pick(starts[:, None])
    inside = (run_beg[:, None, :] <= rel[:, :, None]) & (rel[:, :, None] < run_end[:, None, :])
    src_row = (rel + jnp.sum(jnp.where(inside, shift[:, None, :], 0.0), axis=2)).astype(I32)
    valid = jnp.any(inside, axis=2) & (blk < n_used)[:, None]
    scratch0 = nt * p_rows // SUBLANES
    zero_group = scratch0 + 2 * MOE_GROUPS
    g_in = jnp.where(valid, src_row // SUBLANES, zero_group).astype(I32)
    g_scr = scratch0 + (blk[:, None] % 2) * MOE_GROUPS + jnp.arange(MOE_GROUPS, dtype=I32)[None, :]
    g_out = jnp.where(valid, src_row // SUBLANES, g_scr).astype(I32)
    return block_e, n_used, g_in, g_out


def _moe_kernel(be_ref, nu_ref, gin_ref, gnext_ref, gout_ref, xs_hbm, wgu_ref, bgu_ref, wd_ref, bd_ref,
                eo_hbm, xbuf, obuf, wgu_s, wd_s, isem, osem):
    j = pl.program_id(0)
    nu = nu_ref[0]
    f = wd_ref.shape[0]

    @pl.when((j < nu) & ((j == 0) | (be_ref[j] != be_ref[jnp.maximum(j - 1, 0)])))
    def _():
        for c in range(0, wgu_ref.shape[0], CAST_ROWS):
            wgu_s[c:c + CAST_ROWS, :] = wgu_ref[c:c + CAST_ROWS, :].astype(BF16)
        for c in range(0, wd_ref.shape[0], CAST_ROWS):
            wd_s[c:c + CAST_ROWS, :] = wd_ref[c:c + CAST_ROWS, :].astype(BF16)

    def in_copy(tbl_ref, r, sl):
        return pltpu.make_async_copy(xs_hbm.at[tbl_ref[0, r]], xbuf.at[sl, r], isem.at[sl])

    def out_copy(r, sl):
        return pltpu.make_async_copy(obuf.at[sl, r], eo_hbm.at[gout_ref[0, r]], osem.at[sl])

    def wait_in(sl):
        pltpu.make_async_copy(xs_hbm.at[pl.ds(0, MOE_GROUPS)], xbuf.at[sl], isem.at[sl]).wait()

    def wait_out(sl):
        pltpu.make_async_copy(obuf.at[sl], eo_hbm.at[pl.ds(0, MOE_GROUPS)], osem.at[sl]).wait()

    @pl.when(j == 0)
    def _():
        for r in range(MOE_GROUPS):
            in_copy(gin_ref, r, 0).start()

    def step(slot):
        other = 1 - slot

        @pl.when(j + 1 < nu)
        def _():
            for r in range(MOE_GROUPS):
                in_copy(gnext_ref, r, other).start()

        wait_in(slot)

        @pl.when(j >= 2)
        def _():
            wait_out(slot)

        x = _unpack_bf16_pairs(xbuf[slot].reshape(MOE_TILE, xbuf.shape[-1]))
        gate = jnp.dot(x, wgu_s[:, :f], preferred_element_type=F32) + bgu_ref[:, :f]
        gate = jnp.minimum(gate, SWIGLU_LIMIT)
        sg = gate * jax.nn.sigmoid(SWIGLU_ALPHA * gate)
        up = jnp.dot(x, wgu_s[:, f:], preferred_element_type=F32) + bgu_ref[:, f:]
        act = sg * (jnp.clip(up, -SWIGLU_LIMIT, SWIGLU_LIMIT) + 1.0)
        out = jnp.dot(act.astype(BF16), wd_s[...], preferred_element_type=F32) + bd_ref[...]
        obuf[slot] = _pack_bf16_pairs(out.astype(BF16).astype(F32)).reshape(obuf.shape[1:])
        for r in range(MOE_GROUPS):
            out_copy(r, slot).start()

        @pl.when(j == nu - 1)
        def _():
            wait_out(slot)

            @pl.when(j >= 1)
            def _():
                wait_out(other)

    for parity in range(2):
        pl.when((j < nu) & (lax.rem(j, 2) == parity))(functools.partial(step, parity))


def _moe(block_e, n_used, g_in, g_out, xsorted, wgu, bgu, wd, bd):
    n_blocks = block_e.shape[0]
    dh = xsorted.shape[1]
    d = 2 * dh
    f2 = wgu.shape[2]
    f = wd.shape[1]
    g_in3 = g_in.reshape(n_blocks, 1, MOE_GROUPS)
    g_next3 = jnp.concatenate([g_in3[1:], g_in3[:1]], axis=0)
    g_out3 = g_out.reshape(n_blocks, 1, MOE_GROUPS)
    xs3 = xsorted.reshape(xsorted.shape[0] // SUBLANES, SUBLANES, dh)
    tbl =pl.BlockSpec((None, 1, MOE_GROUPS), lambda j, be, nu: (j, 0, 0), memory_space=pltpu.SMEM)
    return pl.pallas_call(
        _moe_kernel,
        grid_spec=pltpu.PrefetchScalarGridSpec(
            num_scalar_prefetch=2, grid=(n_blocks,),
            in_specs=[tbl, tbl, tbl,
                      pl.BlockSpec(memory_space=pl.ANY),
                      pl.BlockSpec((None, d, f2), lambda j, be, nu: (be[j], 0, 0)),
                      pl.BlockSpec((None, 1, f2), lambda j, be, nu: (be[j], 0, 0)),
                      pl.BlockSpec((None, f, d), lambda j, be, nu: (be[j], 0, 0)),
                      pl.BlockSpec((None, 1, d), lambda j, be, nu: (be[j], 0, 0))],
            out_specs=pl.BlockSpec(memory_space=pl.ANY),
            scratch_shapes=[pltpu.VMEM((2, MOE_GROUPS, SUBLANES, dh), U32),
                            pltpu.VMEM((2, MOE_GROUPS, SUBLANES, dh), U32),
                            pltpu.VMEM((d, f2), BF16), pltpu.VMEM((f, d), BF16),
                            pltpu.SemaphoreType.DMA((2,)), pltpu.SemaphoreType.DMA((2,))]),
        out_shape=jax.ShapeDtypeStruct(xs3.shape, U32),
        input_output_aliases={5: 0},
        compiler_params=_cparams(),
        name="moe",
    )(block_e, n_used, g_in3, g_next3, g_out3, xs3, wgu, bgu, wd, bd).reshape(xsorted.shape)


def _combine_kernel(meta_ref, x1_ref, g_ref, eo_ref, y_ref):
    tm = x1_ref.shape[0]
    p_rows = eo_ref.shape[0]
    lane = lax.broadcasted_iota(I32, (tm, SORT_CHUNK), 1).astype(F32).astype(BF16)
    ws = [meta_ref[:, k:k + 1].astype(BF16) for k in range(TOP_K)]
    pos = [meta_ref[:, TOP_K + k:TOP_K + k + 1] for k in range(TOP_K)]
    acc = x1_ref[...]
    for r in range(p_rows // SORT_CHUNK):
        buf = _unpack_bf16_pairs(eo_ref[r * SORT_CHUNK:(r + 1) * SORT_CHUNK, :])
        lo_r = float(r * SORT_CHUNK)
        wm = jnp.zeros((tm, SORT_CHUNK), BF16)
        for k in range(TOP_K):
            rel = jnp.where((pos[k] >= lo_r) & (pos[k] < lo_r + SORT_CHUNK), pos[k] - lo_r, -1.0).astype(BF16)
            wm = jnp.where(lane == rel, ws[k], wm)
        acc = acc + jnp.dot(wm, buf, preferred_element_type=F32)
    y_ref[...] = _rms(acc, g_ref[...])


def _combine(meta, x1, g_final, eo, tile_off, n_tokens):
    d = x1.shape[1]
    tm = TOKEN_TILE
    p_rows = _sorted_rows(tm)
    return pl.pallas_call(
        _combine_kernel,
        grid=(n_tokens // tm,),
        in_specs=[pl.BlockSpec((tm, LANES), lambda i: (i + tile_off, 0)),
                  pl.BlockSpec((tm, d), lambda i: (i + tile_off, 0)),
                  pl.BlockSpec((1, d), lambda i: (0, 0)),
                  pl.BlockSpec((p_rows, d // 2), lambda i: (i + tile_off, 0))],
        out_specs=pl.BlockSpec((tm, d), lambda i: (i, 0)),
        out_shape=jax.ShapeDtypeStruct((n_tokens, d), F32),
        compiler_params=_cparams(),
        name="combine",
    )(meta, x1, g_final, eo)


def kernel(x_prompt, x_sample, g_mix, w_in, ssm_lam_re, ssm_lam_im, ssm_log_dt, ssm_b_re, ssm_b_im,
           ssm_c_re, ssm_c_im, ssm_d, w_glu, b_glu, sgu_ln_g, sgu_ln_b, sgu_w_s, sgu_b_s,
           g_out_ssm, g_out_sgu, w_out, g_ffn, w_router, b_router, w_gate_up, b_gate_up,
           w_down, b_down, g_final):
    assert g_mix.shape[0] == 1, "single-layer trunk"
    bp, lp, d = x_prompt.shape
    bs, ls, _ = x_sample.shape
    tp, ts = bp * lp, bs * ls
    t = tp + ts
    d_ssm = ssm_d.shape[1]
    d_sgu = sgu_ln_g.shape[1]
    n_exp = w_router.shape[2]
    assert lp % TOKEN_TILE == 0 and ls % TOKEN_TILE == 0 and TOKEN_TILE % CHUNK == 0
    assert SSM_CHUNK * SSM_GROUP == MXU_DIM and 2 * ssm_lam_re.shape[-1] == LANES
    assert d_sgu // SGU_HEADS == LANES // 2 and n_exp <= LANES
    assert n_exp * (SUBLANES - 1) <= MXU_DIM and _sorted_rows(TOKEN_TILE) % SORT_CHUNK == 0
    assert (2 * MOE_GROUPS + 1) * SUBLANES <= _sorted_rows(TOKEN_TILE)

    xp = x_prompt.reshape(tp, d)
    xs = x_sample.reshape(ts, d)
    row = lambda a: a.reshape(1, -1).astype(F32)

    w_in_b = w_in[0].astype(BF16)
    u_p, zuv_p = _inproj(xp, row(g_mix[0]), w_in_b, d_ssm)
    u_s, zuv_s = _inproj(xs, row(g_mix[0]), w_in_b, d_ssm)

    mats = _ssm_matrices(ssm_lam_re[0], ssm_lam_im[0], ssm_log_dt[0], ssm_b_re[0], ssm_b_im[0],
                         ssm_c_re[0], ssm_c_im[0], ssm_d[0])
    y_p = _ssm_trunk(u_p, bp, mats)
    y_s = _ssm_trunk(u_s, bs, mats)

    ws = sgu_w_s[0]
    ws_pairs = jnp.concatenate([ws[0::2], ws[1::2]], axis=2).astype(BF16)
    bias_s = jnp.repeat(sgu_b_s[0].T, d_sgu // SGU_HEADS, axis=1).astype(F32)
    mixed = _mix(y_p, y_s, zuv_p, zuv_s, w_glu[0].astype(BF16), row(b_glu[0]), row(g_out_ssm[0]),
                 row(sgu_ln_g[0]), row(sgu_ln_b[0]), ws_pairs, bias_s, row(g_out_sgu[0]))

    wr_pad = jnp.pad(w_router[0].astype(F32), ((0, 0), (0, LANES - n_exp)))
    wr_hi = wr_pad.astype(BF16)
    wr_cat = jnp.concatenate([wr_hi, (wr_pad - wr_hi.astype(F32)).astype(BF16)], axis=1)
    br_pad = jnp.pad(b_router[0].astype(F32), (0, LANES - n_exp)).reshape(1, LANES)
    x1, xsorted, meta, c8 = _route(xp, xs, mixed, w_out[0].astype(BF16), row(g_ffn[0]),
                                   wr_hi, wr_cat, br_pad, n_exp)

    nt = t // TOKEN_TILE
    p_rows = _sorted_rows(TOKEN_TILE)
    max_rows = t * TOP_K + nt * n_exp * (SUBLANES - 1) + n_exp * (MOE_TILE - 1)
    n_blocks = -(-max_rows // MOE_TILE)
    block_e, n_used, g_in, g_out = _slot_tables(c8[:, :, 0].astype(I32), p_rows, n_blocks)
    eo = _moe(block_e, n_used.reshape(1), g_in, g_out, xsorted, w_gate_up[0],
              b_gate_up[0][:, None, :].astype(F32), w_down[0], b_down[0][:, None, :].astype(F32))

    gf = row(g_final)
    y_prompt = _combine(meta, x1, gf, eo, 0, tp)
    y_sample = _combine(meta, x1, gf, eo, tp // TOKEN_TILE, ts)
    return y_prompt.reshape(bp, lp, d), y_sample.reshape(bs, ls, d)
```

```python
import functools
import math

import jax
import jax.numpy as jnp
from jax import lax
from jax.experimental import pallas as pl
from jax.experimental.pallas import tpu as pltpu

F32 = jnp.float32
BF16 = jnp.bfloat16
I32 = jnp.int32
U32 = jnp.uint32

SSM_GROUP = 16
SGU_HEADS = 8
CHUNK = 128
TOP_K = 4
SWIGLU_LIMIT = 7.0
SWIGLU_ALPHA = 1.702
RMS_EPS = 1e-6
LN_EPS = 1e-5

LANES = 128
SUBLANES = 8
MXU_DIM = 256
SSM_CHUNK = MXU_DIM // SSM_GROUP
PAIR_W = 2 * SSM_CHUNK * SSM_GROUP
ATOM = 2 * SSM_GROUP
ATOMS = LANES // ATOM

TOKEN_TILE = 512
SEQ_TILE = 1024
MOE_TILE = 512
MOE_GROUPS = MOE_TILE // SUBLANES
SORT_CHUNK = 256
CAST_ROWS = 128
SSM_ROW_TILE = 2048
SCAN_SEQS = 4
VMEM_LIMIT = 56 * 1024 * 1024


def _cparams(n_axes=1, vmem=None):
    return pltpu.CompilerParams(
        dimension_semantics=("arbitrary",) * n_axes,
        vmem_limit_bytes=vmem if vmem is not None else VMEM_LIMIT,
    )


def _rms(x, g):
    return x * lax.rsqrt(jnp.mean(x * x, axis=-1, keepdims=True) + RMS_EPS) * g


def _gelu(x):
    return x * (lax.erf(x * (1.0 / math.sqrt(2.0))) + 1.0) * 0.5


def _atom_masks(rows):
    lane = lax.broadcasted_iota(I32, (rows, LANES), 1)
    return [(lane >= a * ATOM) & (lane < (a + 1) * ATOM) for a in range(ATOMS)]


def _atom_transpose(src, masks):
    dst = []
    for b in range(ATOMS):
        acc = None
        for a in range(ATOMS):
            r = (a - b) % ATOMS
            piece = src[a] if r == 0 else pltpu.roll(src[a], ATOM * r, axis=1)
            acc = piece if acc is None else jnp.where(masks[a], piece, acc)
        dst.append(acc)
    return dst


def _pack_bf16_pairs(x):
    w = x.shape[1] // 2
    lo = lax.bitcast_convert_type(x[:, :w], U32) >> 16
    hi = lax.bitcast_convert_type(x[:, w:], U32) & jnp.uint32(0xFFFF0000)
    return hi | lo


def _unpack_bf16_pairs(u):
    lo = lax.bitcast_convert_type(u << 16, F32)
    hi = lax.bitcast_convert_type(u & jnp.uint32(0xFFFF0000), F32)
    return jnp.concatenate([lo.astype(BF16), hi.astype(BF16)], axis=1)


def _inproj_kernel(x_ref, g_ref, w_ref, u_ref, zuv_ref, za_scr, *, d_ssm):
    tm = x_ref.shape[0]
    nc = tm // SSM_CHUNK
    h = _rms(x_ref[...], g_ref[...])
    z = jnp.dot(h.astype(BF16), w_ref[...], preferred_element_type=F32)
    zuv_ref[...] = z[:, d_ssm:].astype(BF16)
    n_blk = d_ssm // LANES
    for b in range(n_blk):
        za_scr[b] = z[:, b * LANES:(b + 1) * LANES]
    masks = _atom_masks(nc)
    n_quads = SSM_CHUNK // ATOMS
    for b in range(n_blk):
        for v in range(n_quads):
            src = [za_scr[b, pl.ds(ATOMS * v + jj, nc, stride=SSM_CHUNK), :] for jj in range(ATOMS)]
            dst = _atom_transpose(src, masks)
            for pi in range(ATOMS):
                c0 = (ATOMS * b + pi) * PAIR_W + v * LANES
                u_ref[:, c0:c0 + LANES] = dst[pi].astype(BF16)


def _inproj(x, g_mix, w_in, d_ssm):
    t, d = x.shape
    tm = SEQ_TILE
    d_in = w_in.shape[1]
    nc = tm // SSM_CHUNK
    uw = d_ssm * SSM_CHUNK
    return pl.pallas_call(
        functools.partial(_inproj_kernel, d_ssm=d_ssm),
        grid=(t // tm,),
        in_specs=[pl.BlockSpec((tm, d), lambda i: (i, 0)),
                  pl.BlockSpec((1, d), lambda i: (0, 0)),
                  pl.BlockSpec((d, d_in), lambda i: (0, 0))],
        out_specs=[pl.BlockSpec((nc, uw), lambda i: (i, 0)),
                   pl.BlockSpec((tm, d_in - d_ssm), lambda i: (i, 0))],
        out_shape=[jax.ShapeDtypeStruct((t // SSM_CHUNK, uw), BF16),
                   jax.ShapeDtypeStruct((t, d_in - d_ssm), BF16)],
        scratch_shapes=[pltpu.VMEM((d_ssm // LANES, tm, LANES), F32)],
        compiler_params=_cparams(),
        name="inproj",
    )(x, g_mix, w_in)


def _ssm_matrices(lam_re, lam_im, log_dt, b_re, b_im, c_re, c_im, d_skip):
    hp = lax.Precision.HIGHEST
    _, g, n = lam_re.shape
    p = b_re.shape[-1]
    lc = SSM_CHUNK
    np_ = g // 2
    dt = jnp.exp(log_dt)[..., None]
    mag = jnp.exp(lam_re * dt)
    ar = mag * jnp.cos(lam_im * dt)
    ai = mag * jnp.sin(lam_im * dt)
    den = lam_re * lam_re + lam_im * lam_im
    nr = ar - 1.0
    fr = (nr * lam_re + ai * lam_im) / den
    fi = (ai * lam_re - nr * lam_im) / den
    bbr = fr[..., None] * b_re - fi[..., None] * b_im
    bbi = fr[..., None] * b_im + fi[..., None] * b_re

    prs, pis = [jnp.ones_like(ar)], [jnp.zeros_like(ai)]
    for _ in range(lc):
        pr, pi = prs[-1], pis[-1]
        prs.append(pr * ar - pi * ai)
        pis.append(pr * ai + pi * ar)
    pw_r, pw_i = jnp.stack(prs), jnp.stack(pis)

    lane_k = jnp.arange(lc * p)
    til_k = (lane_k[None, :] % p == jnp.arange(p)[:, None]).astype(F32)
    rep_k = (lane_k[None, :] // p == jnp.arange(lc)[:, None]).astype(F32)
    ex = lambda x, e: jnp.einsum('dgnk,kx->dgnx', x, e, precision=hp)
    ct_r = ex(jnp.transpose(c_re, (0, 1, 3, 2)), til_k)
    ct_i = ex(jnp.transpose(c_im, (0, 1, 3, 2)), til_k)
    at_r = ex(jnp.transpose(pw_r[:lc], (1, 2, 3, 0)), rep_k)
    at_i = ex(jnp.transpose(pw_i[:lc], (1, 2, 3, 0)), rep_k)
    kk = (jnp.einsum('dgnq,dgnx->dgqx', bbr, ct_r * at_r - ct_i * at_i, precision=hp)
          - jnp.einsum('dgnq,dgnx->dgqx', bbi, ct_r * at_i + ct_i * at_r, precision=hp))
    kk = kk.reshape(2, g, p, lc, p)
    kf, kb = kk[0], kk[1]
    eye = jnp.eye(2, dtype=F32)
    a2 = p * 2

    k_all = jnp.concatenate([kb[:, :, :0:-1], (kf[:, :, :1] + kb[:, :, :1]), kf[:, :, 1:]], axis=2)
    k_all = k_all.reshape(np_, 2, p, 2 * lc - 1, p)
    kflat = jnp.einsum('nsqup,st->nsqutp', k_all, eye).reshape(np_, a2, (2 * lc - 1) * a2)
    kflat = jnp.pad(kflat, ((0, 0), (0, 0), (0, a2)))

    def atoms_q(x):
        x = jnp.transpose(x, (0, 1, 3, 2)).reshape(2, np_, 2, p, n)
        return jnp.einsum('dnsqm,st->dnsqtm', x, eye).reshape(2, np_, a2, 2 * n)
    bq_r, bq_i = atoms_q(bbr), atoms_q(bbi)
    bq = jnp.stack([bq_r[0], bq_i[0], bq_r[1], bq_i[1]], axis=1)
    ef = lc - 1 - jnp.arange(lc)
    eb = jnp.arange(lc)
    rows_q = lambda w, e, d: jnp.transpose(w[e, d].reshape(lc, np_, 2 * n), (1, 0, 2))
    wq = jnp.stack([rows_q(pw_r, ef, 0), rows_q(pw_i, ef, 0),
                    rows_q(pw_r, eb, 1), rows_q(pw_i, eb, 1)], axis=1)

    def atoms_p(x):
        x = jnp.transpose(x, (0, 1, 3, 2)).reshape(2, np_, 2, n, p)
        return jnp.einsum('dnsmp,st->dnsmtp', x, eye).reshape(2, np_, 2 * n, a2)
    cp_r, cp_i = atoms_p(c_re), atoms_p(c_im)
    cp = jnp.stack([cp_r[0], cp_i[0], cp_r[1], cp_i[1]], axis=1)
    pf_e = jnp.arange(lc) + 1
    pb_e = lc - jnp.arange(lc)
    cols_p = lambda w, e, d: jnp.transpose(w[e, d].reshape(lc, np_, 2 * n), (1, 2, 0))
    wp = jnp.stack([cols_p(pw_r, pf_e, 0), cols_p(pw_i, pf_e, 0),
                    cols_p(pw_r, pb_e, 1), cols_p(pw_i, pb_e, 1)], axis=1)
    lane = jnp.arange(lc * a2)
    til = (lane[None, :] % a2 == jnp.arange(a2)[:, None]).astype(BF16)
    rep = (lane[None, :] // a2 == jnp.arange(lc)[:, None]).astype(BF16)

    al = jnp.stack([pw_r[lc, 0], pw_i[lc, 0], pw_r[lc, 1], pw_i[lc, 1]])
    alpha = jnp.transpose(al.reshape(4, np_, 2 * n), (1, 0, 2))
    skip = jnp.broadcast_to(d_skip.reshape(np_, 1, a2), (np_, lc, a2)).reshape(np_, 1, lc * a2)
    return kflat, bq, wq, cp, wp, til, rep, alpha, skip.astype(F32)


def _ssm_v_kernel(u_ref, bq_ref, wq_ref, v_ref, q_scr):
    a2 = bq_ref.shape[1]
    w = bq_ref.shape[2]

    @pl.when(pl.program_id(1) == 0)
    def _():
        for d in range(2):
            br, bi = bq_ref[2 * d], bq_ref[2 * d + 1]
            for j in range(SSM_CHUNK):
                wr = wq_ref[2 * d, j:j + 1, :]
                wi = wq_ref[2 * d + 1, j:j + 1, :]
                q_scr[j * a2:(j + 1) * a2, (2 * d) * w:(2 * d + 1) * w] = (wr * br - wi * bi).astype(BF16)
                q_scr[j * a2:(j + 1) * a2, (2 * d + 1) * w:(2 * d + 2) * w] = (wr * bi + wi * br).astype(BF16)

    v_ref[...] = jnp.dot(u_ref[...], q_scr[...], preferred_element_type=F32)


def _ssm_scan_kernel(v_ref, a_ref, s_ref, *, n_chunks, bsz):
    w = LANES
    sub = SUBLANES
    n_groups = n_chunks // sub
    row = lax.broadcasted_iota(I32, (sub, w), 0)

    def cmul(ar, ai, xr, xi):
        return ar * xr - ai * xi, ar * xi + ai * xr

    def powers(ar, ai):
        p2 = cmul(ar, ai, ar, ai)
        p4 = cmul(*p2, *p2)
        p8 = cmul(*p4, *p4)
        p3 = cmul(*p2, ar, ai)
        p5 = cmul(*p4, ar, ai)
        p6 = cmul(*p4, *p2)
        p7 = cmul(*p4, *p3)
        seq = [(jnp.ones_like(ar), jnp.zeros_like(ai)), (ar, ai), p2, p3, p4, p5, p6, p7]
        tr = jnp.concatenate([s[0] for s in seq], axis=0)
        ti = jnp.concatenate([s[1] for s in seq], axis=0)
        return (ar, ai), p2, p4, p8, (tr, ti)

    def tile_scan(vr, vi, pw, reverse):
        a1, a2, a4, _, _ = pw

        def shift(x, s):
            if reverse:
                return jnp.where(row < sub - s, pltpu.roll(x, sub - s, axis=0), 0.0)
            return jnp.where(row >= s, pltpu.roll(x, s, axis=0), 0.0)

        xr, xi = shift(vr, 1), shift(vi, 1)
        for s, (ar, ai) in ((1, a1), (2, a2), (4, a4)):
            mr, mi = cmul(ar, ai, shift(xr, s), shift(xi, s))
            xr, xi = xr + mr, xi + mi
        e = 0 if reverse else sub - 1
        lr, li = cmul(a1[0], a1[1], xr[e:e + 1], xi[e:e + 1])
        return xr, xi, lr + vr[e:e + 1], li + vi[e:e + 1]

    pw_f = powers(a_ref[0:1, :], a_ref[1:2, :])
    pw_b = powers(a_ref[2:3, :], a_ref[3:4, :])
    tbr = jnp.concatenate([pw_b[4][0][sub - 1 - k:sub - k] for k in range(sub)], axis=0)
    tbi = jnp.concatenate([pw_b[4][1][sub - 1 - k:sub - k] for k in range(sub)], axis=0)

    def step(g, carry, b0):
        out = []
        for b in range(b0, b0 + len(carry) // 4):
            cfr, cfi, cbr, cbi = carry[4 * (b - b0):4 * (b - b0) + 4]
            rf = pl.multiple_of(b * n_chunks + g * sub, sub)
            rb = pl.multiple_of(b * n_chunks + (n_groups - 1 - g) * sub, sub)
            xr, xi, lr, li = tile_scan(v_ref[pl.ds(rf, sub), 0:w], v_ref[pl.ds(rf, sub), w:2 * w], pw_f, False)
            mr, mi = cmul(pw_f[4][0], pw_f[4][1], cfr, cfi)
            s_ref[pl.ds(rf, sub), 0:w] = xr + mr
            s_ref[pl.ds(rf, sub), w:2 * w] = xi + mi
            nr, ni = cmul(pw_f[3][0], pw_f[3][1], cfr, cfi)
            out += [nr + lr, ni + li]
            xr, xi, lr, li = tile_scan(v_ref[pl.ds(rb, sub), 2 * w:3 * w], v_ref[pl.ds(rb, sub), 3 * w:4 * w],
                                       pw_b, True)
            mr, mi = cmul(tbr, tbi, cbr, cbi)
            s_ref[pl.ds(rb, sub), 2 * w:3 * w] = xr + mr
            s_ref[pl.ds(rb, sub), 3 * w:4 * w] = xi + mi
            nr, ni = cmul(pw_b[3][0], pw_b[3][1], cbr, cbi)
            out += [nr + lr, ni + li]
        return tuple(out)

    zero = jnp.zeros((1, w), F32)
    for b0 in range(0, bsz, SCAN_SEQS):
        nb = min(SCAN_SEQS, bsz - b0)
        lax.fori_loop(0, n_groups, functools.partial(step, b0=b0), (zero,) * (4 * nb))


def _ssm_y_kernel(u_ref, s_ref, kflat_ref, cp_ref, wp_ref, til_ref, rep_ref, d_ref, y_ref, m_scr, p_scr):
    a2 = kflat_ref.shape[0]
    w = cp_ref.shape[1]

    def expand(x, e_ref):
        hi = x.astype(BF16)
        lo = (x - hi.astype(F32)).astype(BF16)
        return (jnp.dot(hi, e_ref[...], preferred_element_type=F32)
                + jnp.dot(lo, e_ref[...], preferred_element_type=F32))

    @pl.when(pl.program_id(1) == 0)
    def _():
        kflat = kflat_ref[...]
        for j in range(SSM_CHUNK):
            off = (SSM_CHUNK - 1 - j) * a2
            m_scr[j * a2:(j + 1) * a2, :] = kflat[:, off:off + PAIR_W].astype(BF16)
        for d in range(2):
            cr, ci = expand(cp_ref[2 * d], til_ref), expand(cp_ref[2 * d + 1], til_ref)
            wr, wi = expand(wp_ref[2 * d], rep_ref), expand(wp_ref[2 * d + 1], rep_ref)
            p_scr[(2 * d) * w:(2 * d + 1) * w, :] = (cr * wr - ci * wi).astype(BF16)
            p_scr[(2 * d + 1) * w:(2 * d + 2) * w, :] = (-(cr * wi + ci * wr)).astype(BF16)

    u = u_ref[...]
    y = jnp.dot(u, m_scr[...], preferred_element_type=F32)
    y = y + jnp.dot(s_ref[...].astype(BF16), p_scr[...], preferred_element_type=F32)
    y_ref[...] = (y + d_ref[...] * u.astype(F32)).astype(BF16)


def _ssm_trunk(u, bsz, mats):
    kflat, bq, wq, cp, wp, til, rep, alpha, skip = mats
    rows = u.shape[0]
    nc = rows // bsz
    np_ = kflat.shape[0]
    pw = PAIR_W
    sw = 4 * bq.shape[3]
    tr = min(SSM_ROW_TILE, rows)
    per_pair = lambda a: pl.BlockSpec((None,) + a.shape[1:], lambda p, r: (p,) + (0,) * (a.ndim - 1))
    whole = lambda a: pl.BlockSpec(a.shape, lambda p, r: (0,) * a.ndim)
    v = pl.pallas_call(
        _ssm_v_kernel,
        grid=(np_, rows // tr),
        in_specs=[pl.BlockSpec((tr, pw), lambda p, r: (r, p)), per_pair(bq), per_pair(wq)],
        out_specs=pl.BlockSpec((tr, sw), lambda p, r: (r, p)),
        out_shape=jax.ShapeDtypeStruct((rows, np_ * sw), F32),
        scratch_shapes=[pltpu.VMEM((pw, sw), BF16)],
        compiler_params=_cparams(2),
        name="ssm_v",
    )(u, bq, wq)
    s = pl.pallas_call(
        functools.partial(_ssm_scan_kernel, n_chunks=nc, bsz=bsz),
        grid=(np_,),
        in_specs=[pl.BlockSpec((rows, sw), lambda p: (0, p)),
                  pl.BlockSpec((None, 4, LANES), lambda p: (p, 0, 0))],
        out_specs=pl.BlockSpec((rows, sw), lambda p: (0, p)),
        out_shape=jax.ShapeDtypeStruct((rows, np_ * sw), F32),
        compiler_params=_cparams(1),
        name="ssm_scan",
    )(v, alpha)
    y = pl.pallas_call(
        _ssm_y_kernel,
        grid=(np_, rows // tr),
        in_specs=[pl.BlockSpec((tr, pw), lambda p, r: (r, p)),
                  pl.BlockSpec((tr, sw), lambda p, r: (r, p)),
                  per_pair(kflat), per_pair(cp), per_pair(wp), whole(til), whole(rep), per_pair(skip)],
        out_specs=pl.BlockSpec((tr, pw), lambda p, r: (r, p)),
        out_shape=jax.ShapeDtypeStruct((rows, np_ * pw), BF16),
        scratch_shapes=[pltpu.VMEM((pw, pw), BF16), pltpu.VMEM((sw, pw), BF16)],
        compiler_params=_cparams(2),
        name="ssm_y",
    )(u, s, kflat, cp, wp, til, rep, skip)
    return y


def _mix_kernel(yp_ref, ys_ref, zp_ref, zs_ref, wglu_ref, bglu_ref, gssm_ref,
                lng_ref, lnb_ref, ws_ref, bs_ref, gsgu_ref, o_ref, ya_scr, *, n_prompt_tiles):
    i = pl.program_id(0)
    tm = o_ref.shape[0]
    n_blk = ya_scr.shape[0]
    d_ssm = n_blk * LANES
    nc = tm // SSM_CHUNK

    def body(y_ref, zuv_ref):
        masks = _atom_masks(nc)
        for b in range(n_blk):
            for v in range(SSM_CHUNK // ATOMS):
                src = [y_ref[:, (ATOMS * b + pi) * PAIR_W + v * LANES:
                             (ATOMS * b + pi) * PAIR_W + (v + 1) * LANES].astype(F32)
                       for pi in range(ATOMS)]
                dst = _atom_transpose(src, masks)
                for jj in range(ATOMS):
                    ya_scr[b, pl.ds(ATOMS * v + jj, nc, stride=SSM_CHUNK), :] = dst[jj]
        y = jnp.concatenate([ya_scr[b] for b in range(n_blk)], axis=1)
        gl = _gelu(y)
        gate = jnp.dot(gl.astype(BF16), wglu_ref[...], preferred_element_type=F32) + bglu_ref[...]
        o_ref[:, :d_ssm] = _rms(gl * jax.nn.sigmoid(gate), gssm_ref[...]).astype(BF16)
        d_sgu = zuv_ref.shape[1] // 2
        u = _gelu(zuv_ref[:, :d_sgu].astype(F32))
        gv = _gelu(zuv_ref[:, d_sgu:].astype(F32))
        xc = gv - jnp.mean(gv, axis=-1, keepdims=True)
        v = xc * lax.rsqrt(jnp.mean(xc * xc, axis=-1, keepdims=True) + LN_EPS)
        v = (v * lng_ref[...] + lnb_ref[...]).astype(BF16)
        lo = lax.broadcasted_iota(I32, (CHUNK, LANES), 1) < (LANES // 2)
        zero = jnp.zeros((CHUNK, LANES), BF16)
        rows = []
        for c in range(tm // CHUNK):
            cols = []
            for j in range(d_sgu // LANES):
                vp = v[c * CHUNK:(c + 1) * CHUNK, j * LANES:(j + 1) * LANES]
                rhs = jnp.concatenate([jnp.where(lo, vp, zero), jnp.where(lo, zero, vp)], axis=0)
                cols.append(jnp.dot(ws_ref[j], rhs, preferred_element_type=F32))
            rows.append(jnp.concatenate(cols, axis=1) + bs_ref[...])
        s = jnp.concatenate(rows, axis=0)
        o_ref[:, d_ssm:] = _rms(u * s, gsgu_ref[...]).astype(BF16)

    @pl.when(i < n_prompt_tiles)
    def _():
        body(yp_ref, zp_ref)

    @pl.when(i >= n_prompt_tiles)
    def _():
        body(ys_ref, zs_ref)


def _mix(y_p, y_s, zuv_p, zuv_s, wglu, bglu, gssm, lng, lnb, ws_pairs, bias_s, gsgu):
    tp, ts = zuv_p.shape[0], zuv_s.shape[0]
    d_sgu = zuv_p.shape[1] // 2
    d_ssm = wglu.shape[0]
    tm = SEQ_TILE
    npt = tp // tm
    nc = tm // SSM_CHUNK
    uw = d_ssm * SSM_CHUNK
    ip = lambda i: jnp.minimum(i, npt - 1)
    isamp = lambda i: jnp.maximum(i - npt, 0)
    row = lambda w: pl.BlockSpec((1, w), lambda i: (0, 0))
    return pl.pallas_call(
        functools.partial(_mix_kernel, n_prompt_tiles=npt),
        grid=((tp + ts) // tm,),
        in_specs=[pl.BlockSpec((nc, uw), lambda i: (ip(i), 0)),
                  pl.BlockSpec((nc, uw), lambda i: (isamp(i), 0)),
                  pl.BlockSpec((tm, 2 * d_sgu), lambda i: (ip(i), 0)),
                  pl.BlockSpec((tm, 2 * d_sgu), lambda i: (isamp(i), 0)),
                  pl.BlockSpec((d_ssm, d_ssm), lambda i: (0, 0)),
                  row(d_ssm), row(d_ssm), row(d_sgu), row(d_sgu),
                  pl.BlockSpec(ws_pairs.shape, lambda i: (0, 0, 0)),
                  pl.BlockSpec(bias_s.shape, lambda i: (0, 0)),
                  row(d_sgu)],
        out_specs=pl.BlockSpec((tm, d_ssm + d_sgu), lambda i: (i, 0)),
        out_shape=jax.ShapeDtypeStruct((tp + ts, d_ssm + d_sgu), BF16),
        scratch_shapes=[pltpu.VMEM((d_ssm // LANES, tm, LANES), F32)],
        compiler_params=_cparams(),
        name="mix",
    )(y_p, y_s, zuv_p, zuv_s, wglu, bglu, gssm, lng, lnb, ws_pairs, bias_s, gsgu)


def _sorted_rows(tile):
    return TOP_K * tile + MXU_DIM


def _route_kernel(xp_ref, xs_ref, mix_ref, wout_ref, gffn_ref, whi_ref, wcat_ref, br_ref, tri_ref,
                  ltri_ref, x1_ref, xsort_ref, meta_ref, c8_ref, *, n_prompt_tiles, n_tiles, n_exp):
    i = pl.program_id(0)
    tm = mix_ref.shape[0]
    p_rows = xsort_ref.shape[0]

    def body(x_ref):
        x1 = x_ref[...] + jnp.dot(mix_ref[...], wout_ref[...], preferred_element_type=F32)
        x1_ref[...] = x1
        h2 = _rms(x1, gffn_ref[...])
        hi = h2.astype(BF16)
        lo = (h2 - hi.astype(F32)).astype(BF16)
        hw = jnp.dot(hi, wcat_ref[...], preferred_element_type=F32)
        lt = (hw[:, :LANES] + (jnp.dot(lo, whi_ref[...], preferred_element_type=F32) + hw[:, LANES:])
              + br_ref[...])
        logits = lt.T[:n_exp]
        eio = lax.broadcasted_iota(I32, (n_exp, tm), 0)
        vals, idxs = [], []
        l = logits
        for _ in range(TOP_K):
            m = jnp.max(l, axis=0, keepdims=True)
            idx = jnp.min(jnp.where(l == m, eio, n_exp), axis=0, keepdims=True)
            vals.append(m)
            idxs.append(idx)
            l = jnp.where(eio == idx, -jnp.inf, l)
        ex = [jnp.exp(v - vals[0]) for v in vals]
        den = ex[0] + ex[1] + ex[2] + ex[3]
        ws = [e / den for e in ex]
        hot = [eio == idx for idx in idxs]
        cnt = sum(h.astype(F32) for h in hot)
        prefix = jnp.dot(cnt.astype(BF16), tri_ref[...], preferred_element_type=F32)
        c = jnp.sum(cnt, axis=1, keepdims=True)
        c8 = jnp.floor((c + (SUBLANES - 1)) * (1.0 / SUBLANES)) * SUBLANES
        c8b = jnp.broadcast_to(c8, (n_exp, LANES))
        c8_ref[...] = c8b
        run0 = jnp.dot(ltri_ref[...], c8b.astype(BF16), preferred_element_type=F32)[:, 0:1]
        base = run0 + prefix
        pos = [jnp.sum(jnp.where(h, base, 0.0), axis=0, keepdims=True) for h in hot]
        meta = jnp.concatenate(ws + pos + [jnp.zeros((LANES - 2 * TOP_K, tm), F32)], axis=0)
        meta_ref[...] = meta.T
        rio = lax.broadcasted_iota(I32, (SORT_CHUNK, tm), 0).astype(F32).astype(BF16)
        one = jnp.ones((SORT_CHUNK, tm), BF16)
        zero = jnp.zeros((SORT_CHUNK, tm), BF16)
        for r in range(p_rows // SORT_CHUNK):
            lo_r = float(r * SORT_CHUNK)
            rel = [jnp.where((p >= lo_r) & (p < lo_r + SORT_CHUNK), p - lo_r, -1.0).astype(BF16) for p in pos]
            sel = (rio == rel[0]) | (rio == rel[1]) | (rio == rel[2]) | (rio == rel[3])
            srt = jnp.dot(jnp.where(sel, one, zero), hi, preferred_element_type=F32)
            xsort_ref[r * SORT_CHUNK:(r + 1) * SORT_CHUNK, :] = _pack_bf16_pairs(srt)

    @pl.when(i < n_prompt_tiles)
    def _():
        body(xp_ref)

    @pl.when((i >= n_prompt_tiles) & (i < n_tiles))
    def _():
        body(xs_ref)

    @pl.when(i == n_tiles)
    def _():
        xsort_ref[...] = jnp.zeros_like(xsort_ref)


def _dual_specs(tile, width, n_prompt_tiles, n_tiles):
    last_p = n_prompt_tiles - 1
    last_s = n_tiles - n_prompt_tiles - 1
    sp = pl.BlockSpec((tile, width), lambda i: (jnp.minimum(i, last_p), 0))
    ss = pl.BlockSpec((tile, width), lambda i: (jnp.clip(i - n_prompt_tiles, 0, last_s), 0))
    return sp, ss


def _route(xp, xs, mixed, w_out, g_ffn, wr_hi, wr_cat, br_pad, n_exp):
    tp, d = xp.shape
    t = mixed.shape[0]
    tm = TOKEN_TILE
    npt = tp // tm
    nt = t // tm
    p_rows = _sorted_rows(tm)
    iota = lambda n, ax: lax.broadcasted_iota(I32, (n, n), ax)
    tri = (iota(tm, 0) < iota(tm, 1)).astype(BF16)
    ltri = (iota(n_exp, 1) < iota(n_exp, 0)).astype(BF16)
    sp, ss = _dual_specs(tm, d, npt, nt)
    const = lambda shape: pl.BlockSpec(shape, lambda i: (0,) * len(shape))
    tile = lambda i: jnp.minimum(i, nt - 1)
    return pl.pallas_call(
        functools.partial(_route_kernel, n_prompt_tiles=npt, n_tiles=nt, n_exp=n_exp),
        grid=(nt + 1,),
        in_specs=[sp, ss,
                  pl.BlockSpec((tm, mixed.shape[1]), lambda i: (tile(i), 0)),
                  const(w_out.shape), const((1, d)), const(wr_hi.shape), const(wr_cat.shape),
                  const((1, LANES)), const((tm, tm)), const((n_exp, n_exp))],
        out_specs=[pl.BlockSpec((tm, d), lambda i: (tile(i), 0)),
                   pl.BlockSpec((p_rows, d // 2), lambda i: (i, 0)),
                   pl.BlockSpec((tm, LANES), lambda i: (tile(i), 0)),
                   pl.BlockSpec((None, n_exp, LANES), lambda i: (tile(i), 0, 0))],
        out_shape=[jax.ShapeDtypeStruct((t, d), F32),
                   jax.ShapeDtypeStruct(((nt + 1) * p_rows, d // 2), U32),
                   jax.ShapeDtypeStruct((t, LANES), F32),
                   jax.ShapeDtypeStruct((nt, n_exp, LANES), F32)],
        compiler_params=_cparams(),
        name="route",
    )(xp, xs, mixed, w_out, g_ffn, wr_hi, wr_cat, br_pad, tri, ltri)


def _slot_tables(c8, p_rows, n_blocks):
    nt, n_exp = c8.shape
    run0 = jnp.cumsum(c8, axis=1) - c8
    seg_len = c8.T
    cum = jnp.cumsum(seg_len, axis=1)
    tot = cum[:, -1]
    padded = (tot + MOE_TILE - 1) // MOE_TILE * MOE_TILE
    ends = jnp.cumsum(padded)
    starts = ends - padded
    n_used = (ends[-1] // MOE_TILE).astype(I32)
    blk = jnp.arange(n_blocks, dtype=I32)
    be = jnp.minimum(jnp.sum(ends[None, :] <= (blk * MOE_TILE)[:, None], axis=1), n_exp - 1).astype(I32)
    block_e = jnp.where(blk < n_used, be, be[jnp.maximum(n_used - 1, 0)])
    onehot = (be[:, None] == jnp.arange(n_exp, dtype=I32)[None, :]).astype(F32)
    pick = lambda tbl: jnp.dot(onehot, tbl.astype(F32), precision=lax.Precision.HIGHEST)
    run_end = pick(cum)
    run_beg = run_end - pick(seg_len)
    shift = pick((jnp.arange(nt, dtype=I32) * p_rows)[None, :] + run0.T) - run_beg
    grow = (blk[:, None] * MOE_TILE + jnp.arange(MOE_GROUPS, dtype=I32)[None, :] * SUBLANES).astype(F32)
    rel = grow - pick(starts[:, None])
    inside = (run_beg[:, None, :] <= rel[:, :, None]) & (rel[:, :, None] < run_end[:, None, :])
    src_row = (rel + jnp.sum(jnp.where(inside, shift[:, None, :], 0.0), axis=2)).astype(I32)
    valid = jnp.any(inside, axis=2) & (blk < n_used)[:, None]
    scratch0 = nt * p_rows // SUBLANES
    zero_group = scratch0 + 2 * MOE_GROUPS
    g_in = jnp.where(valid, src_row // SUBLANES, zero_group).astype(I32)
    g_scr = scratch0 + (blk[:, None] % 2) * MOE_GROUPS + jnp.arange(MOE_GROUPS, dtype=I32)[None, :]
    g_out = jnp.where(valid, src_row // SUBLANES, g_scr).astype(I32)
    return block_e, n_used, g_in, g_out


def _moe_kernel(be_ref, nu_ref, gin_ref, gnext_ref, gout_ref, xs_hbm, wgu_ref, bgu_ref, wd_ref, bd_ref,
                eo_hbm, xbuf, obuf, wgu_s, wd_s, isem, osem):
    j = pl.program_id(0)
    nu = nu_ref[0]
    f = wd_ref.shape[0]

    @pl.when((j < nu) & ((j == 0) | (be_ref[j] != be_ref[jnp.maximum(j - 1, 0)])))
    def _():
        for c in range(0, wgu_ref.shape[0], CAST_ROWS):
            wgu_s[c:c + CAST_ROWS, :] = wgu_ref[c:c + CAST_ROWS, :].astype(BF16)
        for c in range(0, wd_ref.shape[0], CAST_ROWS):
            wd_s[c:c + CAST_ROWS, :] = wd_ref[c:c + CAST_ROWS, :].astype(BF16)

    def in_copy(tbl_ref, r, sl):
        return pltpu.make_async_copy(xs_hbm.at[tbl_ref[0, r]], xbuf.at[sl, r], isem.at[sl])

    def out_copy(r, sl):
        return pltpu.make_async_copy(obuf.at[sl, r], eo_hbm.at[gout_ref[0, r]], osem.at[sl])

    def wait_in(sl):
        pltpu.make_async_copy(xs_hbm.at[pl.ds(0, MOE_GROUPS)], xbuf.at[sl], isem.at[sl]).wait()

    def wait_out(sl):
        pltpu.make_async_copy(obuf.at[sl], eo_hbm.at[pl.ds(0, MOE_GROUPS)], osem.at[sl]).wait()

    @pl.when(j == 0)
    def _():
        for r in range(MOE_GROUPS):
            in_copy(gin_ref, r, 0).start()

    def step(slot):
        other = 1 - slot

        @pl.when(j + 1 < nu)
        def _():
            for r in range(MOE_GROUPS):
                in_copy(gnext_ref, r, other).start()

        wait_in(slot)

        @pl.when(j >= 2)
        def _():
            wait_out(slot)

        x = _unpack_bf16_pairs(xbuf[slot].reshape(MOE_TILE, xbuf.shape[-1]))
        gate = jnp.dot(x, wgu_s[:, :f], preferred_element_type=F32) + bgu_ref[:, :f]
        gate = jnp.minimum(gate, SWIGLU_LIMIT)
        sg = gate * jax.nn.sigmoid(SWIGLU_ALPHA * gate)
        up = jnp.dot(x, wgu_s[:, f:], preferred_element_type=F32) + bgu_ref[:, f:]
        act = sg * (jnp.clip(up, -SWIGLU_LIMIT, SWIGLU_LIMIT) + 1.0)
        out = jnp.dot(act.astype(BF16), wd_s[...], preferred_element_type=F32) + bd_ref[...]
        obuf[slot] = _pack_bf16_pairs(out.astype(BF16).astype(F32)).reshape(obuf.shape[1:])
        for r in range(MOE_GROUPS):
            out_copy(r, slot).start()

        @pl.when(j == nu - 1)
        def _():
            wait_out(slot)

            @pl.when(j >= 1)
            def _():
                wait_out(other)

    for parity in range(2):
        pl.when((j < nu) & (lax.rem(j, 2) == parity))(functools.partial(step, parity))


def _moe(block_e, n_used, g_in, g_out, xsorted, wgu, bgu, wd, bd):
    n_blocks = block_e.shape[0]
    dh = xsorted.shape[1]
    d = 2 * dh
    f2 = wgu.shape[2]
    f = wd.shape[1]
    g_in3 = g_in.reshape(n_blocks, 1, MOE_GROUPS)
    g_next3 = jnp.concatenate([g_in3[1:], g_in3[:1]], axis=0)
    g_out3 = g_out.reshape(n_blocks, 1, MOE_GROUPS)
    xs3 = xsorted.reshape(xsorted.shape[0] // SUBLANES, SUBLANES, dh)
    tbl =pl.BlockSpec((None, 1, MOE_GROUPS), lambda j, be, nu: (j, 0, 0), memory_space=pltpu.SMEM)
    return pl.pallas_call(
        _moe_kernel,
        grid_spec=pltpu.PrefetchScalarGridSpec(
            num_scalar_prefetch=2, grid=(n_blocks,),
            in_specs=[tbl, tbl, tbl,
                      pl.BlockSpec(memory_space=pl.ANY),
                      pl.BlockSpec((None, d, f2), lambda j, be, nu: (be[j], 0, 0)),
                      pl.BlockSpec((None, 1, f2), lambda j, be, nu: (be[j], 0, 0)),
                      pl.BlockSpec((None, f, d), lambda j, be, nu: (be[j], 0, 0)),
                      pl.BlockSpec((None, 1, d), lambda j, be, nu: (be[j], 0, 0))],
            out_specs=pl.BlockSpec(memory_space=pl.ANY),
            scratch_shapes=[pltpu.VMEM((2, MOE_GROUPS, SUBLANES, dh), U32),
                            pltpu.VMEM((2, MOE_GROUPS, SUBLANES, dh), U32),
                            pltpu.VMEM((d, f2), BF16), pltpu.VMEM((f, d), BF16),
                            pltpu.SemaphoreType.DMA((2,)), pltpu.SemaphoreType.DMA((2,))]),
        out_shape=jax.ShapeDtypeStruct(xs3.shape, U32),
        input_output_aliases={5: 0},
        compiler_params=_cparams(),
        name="moe",
    )(block_e, n_used, g_in3, g_next3, g_out3, xs3, wgu, bgu, wd, bd).reshape(xsorted.shape)


def _combine_kernel(meta_ref, x1_ref, g_ref, eo_ref, y_ref):
    tm = x1_ref.shape[0]
    p_rows = eo_ref.shape[0]
    lane = lax.broadcasted_iota(I32, (tm, SORT_CHUNK), 1).astype(F32).astype(BF16)
    ws = [meta_ref[:, k:k + 1].astype(BF16) for k in range(TOP_K)]
    pos = [meta_ref[:, TOP_K + k:TOP_K + k + 1] for k in range(TOP_K)]
    acc = x1_ref[...]
    for r in range(p_rows // SORT_CHUNK):
        buf = _unpack_bf16_pairs(eo_ref[r * SORT_CHUNK:(r + 1) * SORT_CHUNK, :])
        lo_r = float(r * SORT_CHUNK)
        wm = jnp.zeros((tm, SORT_CHUNK), BF16)
        for k in range(TOP_K):
            rel = jnp.where((pos[k] >= lo_r) & (pos[k] < lo_r + SORT_CHUNK), pos[k] - lo_r, -1.0).astype(BF16)
            wm = jnp.where(lane == rel, ws[k], wm)
        acc = acc + jnp.dot(wm, buf, preferred_element_type=F32)
    y_ref[...] = _rms(acc, g_ref[...])


def _combine(meta, x1, g_final, eo, tile_off, n_tokens):
    d = x1.shape[1]
    tm = TOKEN_TILE
    p_rows = _sorted_rows(tm)
    return pl.pallas_call(
        _combine_kernel,
        grid=(n_tokens // tm,),
        in_specs=[pl.BlockSpec((tm, LANES), lambda i: (i + tile_off, 0)),
                  pl.BlockSpec((tm, d), lambda i: (i + tile_off, 0)),
                  pl.BlockSpec((1, d), lambda i: (0, 0)),
                  pl.BlockSpec((p_rows, d // 2), lambda i: (i + tile_off, 0))],
        out_specs=pl.BlockSpec((tm, d), lambda i: (i, 0)),
        out_shape=jax.ShapeDtypeStruct((n_tokens, d), F32),
        compiler_params=_cparams(),
        name="combine",
    )(meta, x1, g_final, eo)


def kernel(x_prompt, x_sample, g_mix, w_in, ssm_lam_re, ssm_lam_im, ssm_log_dt, ssm_b_re, ssm_b_im,
           ssm_c_re, ssm_c_im, ssm_d, w_glu, b_glu, sgu_ln_g, sgu_ln_b, sgu_w_s, sgu_b_s,
           g_out_ssm, g_out_sgu, w_out, g_ffn, w_router, b_router, w_gate_up, b_gate_up,
           w_down, b_down, g_final):
    assert g_mix.shape[0] == 1, "single-layer trunk"
    bp, lp, d = x_prompt.shape
    bs, ls, _ = x_sample.shape
    tp, ts = bp * lp, bs * ls
    t = tp + ts
    d_ssm = ssm_d.shape[1]
    d_sgu = sgu_ln_g.shape[1]
    n_exp = w_router.shape[2]
    assert all(n % tile == 0 for n in (tp, ts) for tile in (SEQ_TILE, TOKEN_TILE))
    assert lp % CHUNK == 0 and ls % CHUNK == 0 and SEQ_TILE % CHUNK == 0 and CHUNK % SSM_CHUNK == 0
    assert SSM_CHUNK * SSM_GROUP == MXU_DIM and 2 * ssm_lam_re.shape[-1] == LANES
    assert d_sgu // SGU_HEADS == LANES // 2 and n_exp <= LANES
    assert n_exp * (SUBLANES - 1) <= MXU_DIM and _sorted_rows(TOKEN_TILE) % SORT_CHUNK == 0
    assert (2 * MOE_GROUPS + 1) * SUBLANES <= _sorted_rows(TOKEN_TILE)

    xp = x_prompt.reshape(tp, d)
    xs = x_sample.reshape(ts, d)
    row = lambda a: a.reshape(1, -1).astype(F32)

    w_in_b = w_in[0].astype(BF16)
    u_p, zuv_p = _inproj(xp, row(g_mix[0]), w_in_b, d_ssm)
    u_s, zuv_s = _inproj(xs, row(g_mix[0]), w_in_b, d_ssm)

    mats = _ssm_matrices(ssm_lam_re[0], ssm_lam_im[0], ssm_log_dt[0], ssm_b_re[0], ssm_b_im[0],
                         ssm_c_re[0], ssm_c_im[0], ssm_d[0])
    y_p = _ssm_trunk(u_p, bp, mats)
    y_s = _ssm_trunk(u_s, bs, mats)

    ws = sgu_w_s[0]
    ws_pairs = jnp.concatenate([ws[0::2], ws[1::2]], axis=2).astype(BF16)
    bias_s = jnp.repeat(sgu_b_s[0].T, d_sgu // SGU_HEADS, axis=1).astype(F32)
    mixed = _mix(y_p, y_s, zuv_p, zuv_s, w_glu[0].astype(BF16), row(b_glu[0]), row(g_out_ssm[0]),
                 row(sgu_ln_g[0]), row(sgu_ln_b[0]), ws_pairs, bias_s, row(g_out_sgu[0]))

    wr_pad = jnp.pad(w_router[0].astype(F32), ((0, 0), (0, LANES - n_exp)))
    wr_hi = wr_pad.astype(BF16)
    wr_cat = jnp.concatenate([wr_hi, (wr_pad - wr_hi.astype(F32)).astype(BF16)], axis=1)
    br_pad = jnp.pad(b_router[0].astype(F32), (0, LANES - n_exp)).reshape(1, LANES)
    x1, xsorted, meta, c8 = _route(xp, xs, mixed, w_out[0].astype(BF16), row(g_ffn[0]),
                                   wr_hi, wr_cat, br_pad, n_exp)

    nt = t // TOKEN_TILE
    p_rows = _sorted_rows(TOKEN_TILE)
    max_rows = t * TOP_K + nt * n_exp * (SUBLANES - 1) + n_exp * (MOE_TILE - 1)
    n_blocks = -(-max_rows // MOE_TILE)
    block_e, n_used, g_in, g_out = _slot_tables(c8[:, :, 0].astype(I32), p_rows, n_blocks)
    eo = _moe(block_e, n_used.reshape(1), g_in, g_out, xsorted, w_gate_up[0],
              b_gate_up[0][:, None, :].astype(F32), w_down[0], b_down[0][:, None, :].astype(F32))

    gf = row(g_final)
    y_prompt = _combine(meta, x1, gf, eo, 0, tp)
    y_sample = _combine(meta, x1, gf, eo, tp // TOKEN_TILE, ts)
    return y_prompt.reshape(bp, lp, d), y_sample.reshape(bs, ls, d)
```

```python
import functools
import math

import jax
import jax.numpy as jnp
from jax import lax
from jax.experimental import pallas as pl
from jax.experimental.pallas import tpu as pltpu

F32 = jnp.float32
BF16 = jnp.bfloat16
I32 = jnp.int32
U32 = jnp.uint32

SSM_GROUP = 16
SGU_HEADS = 8
CHUNK = 128
TOP_K = 4
SWIGLU_LIMIT = 7.0
SWIGLU_ALPHA = 1.702
RMS_EPS = 1e-6
LN_EPS = 1e-5

LANES = 128
SUBLANES = 8
MXU_DIM = 256
SSM_CHUNK = MXU_DIM // SSM_GROUP
PAIR_W = 2 * SSM_CHUNK * SSM_GROUP
ATOM = 2 * SSM_GROUP
ATOMS = LANES // ATOM

TOKEN_TILE = 512
SEQ_TILE = 1024
MOE_TILE = 512
MOE_GROUPS = MOE_TILE // SUBLANES
SORT_CHUNK = 256
CAST_ROWS = 128
SSM_ROW_TILE = 2048
SCAN_SEQS = 4
VMEM_LIMIT = 56 * 1024 * 1024


def _cparams(n_axes=1, vmem=None):
    return pltpu.CompilerParams(
        dimension_semantics=("arbitrary",) * n_axes,
        vmem_limit_bytes=vmem if vmem is not None else VMEM_LIMIT,
    )


def _rms(x, g):
    return x * lax.rsqrt(jnp.mean(x * x, axis=-1, keepdims=True) + RMS_EPS) * g


def _gelu(x):
    return x * (lax.erf(x * (1.0 / math.sqrt(2.0))) + 1.0) * 0.5


def _atom_masks(rows):
    lane = lax.broadcasted_iota(I32, (rows, LANES), 1)
    return [(lane >= a * ATOM) & (lane < (a + 1) * ATOM) for a in range(ATOMS)]


def _atom_transpose(src, masks):
    dst = []
    for b in range(ATOMS):
        acc = None
        for a in range(ATOMS):
            r = (a - b) % ATOMS
            piece = src[a] if r == 0 else pltpu.roll(src[a], ATOM * r, axis=1)
            acc = piece if acc is None else jnp.where(masks[a], piece, acc)
        dst.append(acc)
    return dst


def _pack_bf16_pairs(x):
    w = x.shape[1] // 2
    lo = lax.bitcast_convert_type(x[:, :w], U32) >> 16
    hi = lax.bitcast_convert_type(x[:, w:], U32) & jnp.uint32(0xFFFF0000)
    return hi | lo


def _unpack_bf16_pairs(u):
    lo = lax.bitcast_convert_type(u << 16, F32)
    hi = lax.bitcast_convert_type(u & jnp.uint32(0xFFFF0000), F32)
    return jnp.concatenate([lo.astype(BF16), hi.astype(BF16)], axis=1)


def _inproj_kernel(x_ref, g_ref, w_ref, u_ref, zuv_ref, za_scr, *, d_ssm):
    tm = x_ref.shape[0]
    nc = tm // SSM_CHUNK
    h = _rms(x_ref[...], g_ref[...])
    z = jnp.dot(h.astype(BF16), w_ref[...], preferred_element_type=F32)
    zuv_ref[...] = z[:, d_ssm:].astype(BF16)
    n_blk = d_ssm // LANES
    for b in range(n_blk):
        za_scr[b] = z[:, b * LANES:(b + 1) * LANES]
    masks = _atom_masks(nc)
    n_quads = SSM_CHUNK // ATOMS
    for b in range(n_blk):
        for v in range(n_quads):
            src = [za_scr[b, pl.ds(ATOMS * v + jj, nc, stride=SSM_CHUNK), :] for jj in range(ATOMS)]
            dst = _atom_transpose(src, masks)
            for pi in range(ATOMS):
                c0 = (ATOMS * b + pi) * PAIR_W + v * LANES
                u_ref[:, c0:c0 + LANES] = dst[pi].astype(BF16)


def _inproj(x, g_mix, w_in, d_ssm):
    t, d = x.shape
    tm = SEQ_TILE
    d_in = w_in.shape[1]
    nc = tm // SSM_CHUNK
    uw = d_ssm * SSM_CHUNK
    return pl.pallas_call(
        functools.partial(_inproj_kernel, d_ssm=d_ssm),
        grid=(t // tm,),
        in_specs=[pl.BlockSpec((tm, d), lambda i: (i, 0)),
                  pl.BlockSpec((1, d), lambda i: (0, 0)),
                  pl.BlockSpec((d, d_in), lambda i: (0, 0))],
        out_specs=[pl.BlockSpec((nc, uw), lambda i: (i, 0)),
                   pl.BlockSpec((tm, d_in - d_ssm), lambda i: (i, 0))],
        out_shape=[jax.ShapeDtypeStruct((t // SSM_CHUNK, uw), BF16),
                   jax.ShapeDtypeStruct((t, d_in - d_ssm), BF16)],
        scratch_shapes=[pltpu.VMEM((d_ssm // LANES, tm, LANES), F32)],
        compiler_params=_cparams(),
        name="inproj",
    )(x, g_mix, w_in)


def _ssm_matrices(lam_re, lam_im, log_dt, b_re, b_im, c_re, c_im, d_skip):
    hp = lax.Precision.HIGHEST
    _, g, n = lam_re.shape
    p = b_re.shape[-1]
    lc = SSM_CHUNK
    np_ = g // 2
    dt = jnp.exp(log_dt)[..., None]
    mag = jnp.exp(lam_re * dt)
    ar = mag * jnp.cos(lam_im * dt)
    ai = mag * jnp.sin(lam_im * dt)
    den = lam_re * lam_re + lam_im * lam_im
    nr = ar - 1.0
    fr = (nr * lam_re + ai * lam_im) / den
    fi = (ai * lam_re - nr * lam_im) / den
    bbr = fr[..., None] * b_re - fi[..., None] * b_im
    bbi = fr[..., None] * b_im + fi[..., None] * b_re

    prs, pis = [jnp.ones_like(ar)], [jnp.zeros_like(ai)]
    for _ in range(lc):
        pr, pi = prs[-1], pis[-1]
        prs.append(pr * ar - pi * ai)
        pis.append(pr * ai + pi * ar)
    pw_r, pw_i = jnp.stack(prs), jnp.stack(pis)

    lane_k = jnp.arange(lc * p)
    til_k = (lane_k[None, :] % p == jnp.arange(p)[:, None]).astype(F32)
    rep_k = (lane_k[None, :] // p == jnp.arange(lc)[:, None]).astype(F32)
    ex = lambda x, e: jnp.einsum('dgnk,kx->dgnx', x, e, precision=hp)
    ct_r = ex(jnp.transpose(c_re, (0, 1, 3, 2)), til_k)
    ct_i = ex(jnp.transpose(c_im, (0, 1, 3, 2)), til_k)
    at_r = ex(jnp.transpose(pw_r[:lc], (1, 2, 3, 0)), rep_k)
    at_i = ex(jnp.transpose(pw_i[:lc], (1, 2, 3, 0)), rep_k)
    kk = (jnp.einsum('dgnq,dgnx->dgqx', bbr, ct_r * at_r - ct_i * at_i, precision=hp)
          - jnp.einsum('dgnq,dgnx->dgqx', bbi, ct_r * at_i + ct_i * at_r, precision=hp))
    kk = kk.reshape(2, g, p, lc, p)
    kf, kb = kk[0], kk[1]
    eye = jnp.eye(2, dtype=F32)
    a2 = p * 2

    k_all = jnp.concatenate([kb[:, :, :0:-1], (kf[:, :, :1] + kb[:, :, :1]), kf[:, :, 1:]], axis=2)
    k_all = k_all.reshape(np_, 2, p, 2 * lc - 1, p)
    kflat = jnp.einsum('nsqup,st->nsqutp', k_all, eye).reshape(np_, a2, (2 * lc - 1) * a2)
    kflat = jnp.pad(kflat, ((0, 0), (0, 0), (0, a2)))

    def atoms_q(x):
        x = jnp.transpose(x, (0, 1, 3, 2)).reshape(2, np_, 2, p, n)
        return jnp.einsum('dnsqm,st->dnsqtm', x, eye).reshape(2, np_, a2, 2 * n)
    bq_r, bq_i = atoms_q(bbr), atoms_q(bbi)
    bq = jnp.stack([bq_r[0], bq_i[0], bq_r[1], bq_i[1]], axis=1)
    ef = lc - 1 - jnp.arange(lc)
    eb = jnp.arange(lc)
    rows_q = lambda w, e, d: jnp.transpose(w[e, d].reshape(lc, np_, 2 * n), (1, 0, 2))
    wq = jnp.stack([rows_q(pw_r, ef, 0), rows_q(pw_i, ef, 0),
                    rows_q(pw_r, eb, 1), rows_q(pw_i, eb, 1)], axis=1)

    def atoms_p(x):
        x = jnp.transpose(x, (0, 1, 3, 2)).reshape(2, np_, 2, n, p)
        return jnp.einsum('dnsmp,st->dnsmtp', x, eye).reshape(2, np_, 2 * n, a2)
    cp_r, cp_i = atoms_p(c_re), atoms_p(c_im)
    cp = jnp.stack([cp_r[0], cp_i[0], cp_r[1], cp_i[1]], axis=1)
    pf_e = jnp.arange(lc) + 1
    pb_e = lc - jnp.arange(lc)
    cols_p = lambda w, e, d: jnp.transpose(w[e, d].reshape(lc, np_, 2 * n), (1, 2, 0))
    wp = jnp.stack([cols_p(pw_r, pf_e, 0), cols_p(pw_i, pf_e, 0),
                    cols_p(pw_r, pb_e, 1), cols_p(pw_i, pb_e, 1)], axis=1)
    lane = jnp.arange(lc * a2)
    til = (lane[None, :] % a2 == jnp.arange(a2)[:, None]).astype(BF16)
    rep = (lane[None, :] // a2 == jnp.arange(lc)[:, None]).astype(BF16)

    al = jnp.stack([pw_r[lc, 0], pw_i[lc, 0], pw_r[lc, 1], pw_i[lc, 1]])
    alpha = jnp.transpose(al.reshape(4, np_, 2 * n), (1, 0, 2))
    skip = jnp.broadcast_to(d_skip.reshape(np_, 1, a2), (np_, lc, a2)).reshape(np_, 1, lc * a2)
    return kflat, bq, wq, cp, wp, til, rep, alpha, skip.astype(F32)


def _ssm_v_kernel(u_ref, bq_ref, wq_ref, v_ref, q_scr):
    a2 = bq_ref.shape[1]
    w = bq_ref.shape[2]

    @pl.when(pl.program_id(1) == 0)
    def _():
        for d in range(2):
            br, bi = bq_ref[2 * d], bq_ref[2 * d + 1]
            for j in range(SSM_CHUNK):
                wr = wq_ref[2 * d, j:j + 1, :]
                wi = wq_ref[2 * d + 1, j:j + 1, :]
                q_scr[j * a2:(j + 1) * a2, (2 * d) * w:(2 * d + 1) * w] = (wr * br - wi * bi).astype(BF16)
                q_scr[j * a2:(j + 1) * a2, (2 * d + 1) * w:(2 * d + 2) * w] = (wr * bi + wi * br).astype(BF16)

    v_ref[...] = jnp.dot(u_ref[...], q_scr[...], preferred_element_type=F32)


def _ssm_scan_kernel(v_ref, a_ref, s_ref, *, n_chunks, bsz):
    w = LANES
    sub = SUBLANES
    n_groups = n_chunks // sub
    row = lax.broadcasted_iota(I32, (sub, w), 0)

    def cmul(ar, ai, xr, xi):
        return ar * xr - ai * xi, ar * xi + ai * xr

    def powers(ar, ai):
        p2 = cmul(ar, ai, ar, ai)
        p4 = cmul(*p2, *p2)
        p8 = cmul(*p4, *p4)
        p3 = cmul(*p2, ar, ai)
        p5 = cmul(*p4, ar, ai)
        p6 = cmul(*p4, *p2)
        p7 = cmul(*p4, *p3)
        seq = [(jnp.ones_like(ar), jnp.zeros_like(ai)), (ar, ai), p2, p3, p4, p5, p6, p7]
        tr = jnp.concatenate([s[0] for s in seq], axis=0)
        ti = jnp.concatenate([s[1] for s in seq], axis=0)
        return (ar, ai), p2, p4, p8, (tr, ti)

    def tile_scan(vr, vi, pw, reverse):
        a1, a2, a4, _, _ = pw

        def shift(x, s):
            if reverse:
                return jnp.where(row < sub - s, pltpu.roll(x, sub - s, axis=0), 0.0)
            return jnp.where(row >= s, pltpu.roll(x, s, axis=0), 0.0)

        xr, xi = shift(vr, 1), shift(vi, 1)
        for s, (ar, ai) in ((1, a1), (2, a2), (4, a4)):
            mr, mi = cmul(ar, ai, shift(xr, s), shift(xi, s))
            xr, xi = xr + mr, xi + mi
        e = 0 if reverse else sub - 1
        lr, li = cmul(a1[0], a1[1], xr[e:e + 1], xi[e:e + 1])
        return xr, xi, lr + vr[e:e + 1], li + vi[e:e + 1]

    pw_f = powers(a_ref[0:1, :], a_ref[1:2, :])
    pw_b = powers(a_ref[2:3, :], a_ref[3:4, :])
    tbr = jnp.concatenate([pw_b[4][0][sub - 1 - k:sub - k] for k in range(sub)], axis=0)
    tbi = jnp.concatenate([pw_b[4][1][sub - 1 - k:sub - k] for k in range(sub)], axis=0)

    def one_tile(row0, col0, pw, table, cr, ci, reverse):
        rows = pl.ds(pl.multiple_of(row0, sub), sub)
        xr, xi, lr, li = tile_scan(v_ref[rows, col0:col0 + w], v_ref[rows, col0 + w:col0 + 2 * w], pw, reverse)
        mr, mi = cmul(table[0], table[1], cr, ci)
        nr, ni = cmul(pw[3][0], pw[3][1], cr, ci)
        return xr + mr, xi + mi, nr + lr, ni + li

    pair = 2 * sub
    n_pairs = n_groups // 2

    def step(g, carry, b0):
        out = []
        for b in range(b0, b0 + len(carry) // 4):
            cfr, cfi, cbr, cbi = carry[4 * (b - b0):4 * (b - b0) + 4]
            rf = pl.multiple_of(b * n_chunks + g * pair, pair)
            rb = pl.multiple_of(b * n_chunks + (n_pairs - 1 - g) * pair, pair)
            ar, ai, cfr, cfi = one_tile(rf, 0, pw_f, pw_f[4], cfr, cfi, False)
            br, bi, cfr, cfi = one_tile(rf + sub, 0, pw_f, pw_f[4], cfr, cfi, False)
            s_ref[pl.ds(rf, pair), 0:w] = jnp.concatenate([ar, br], axis=0).astype(BF16)
            s_ref[pl.ds(rf, pair), w:2 * w] = jnp.concatenate([ai, bi], axis=0).astype(BF16)
            out += [cfr, cfi]
            br, bi, cbr, cbi = one_tile(rb + sub, 2 * w, pw_b, (tbr, tbi), cbr, cbi, True)
            ar, ai, cbr, cbi = one_tile(rb, 2 * w, pw_b, (tbr, tbi), cbr, cbi, True)
            s_ref[pl.ds(rb, pair), 2 * w:3 * w] = jnp.concatenate([ar, br], axis=0).astype(BF16)
            s_ref[pl.ds(rb, pair), 3 * w:4 * w] = jnp.concatenate([ai, bi], axis=0).astype(BF16)
            out += [cbr, cbi]
        return tuple(out)

    zero = jnp.zeros((1, w), F32)
    for b0 in range(0, bsz, SCAN_SEQS):
        nb = min(SCAN_SEQS, bsz - b0)
        lax.fori_loop(0, n_pairs, functools.partial(step, b0=b0), (zero,) * (4 * nb))


def _ssm_y_kernel(u_ref, s_ref, kflat_ref, cp_ref, wp_ref, til_ref, rep_ref, d_ref, y_ref, m_scr, p_scr):
    a2 = kflat_ref.shape[0]
    w = cp_ref.shape[1]

    def expand(x, e_ref):
        hi = x.astype(BF16)
        lo = (x - hi.astype(F32)).astype(BF16)
        return (jnp.dot(hi, e_ref[...], preferred_element_type=F32)
                + jnp.dot(lo, e_ref[...], preferred_element_type=F32))

    @pl.when(pl.program_id(1) == 0)
    def _():
        kflat = kflat_ref[...]
        for j in range(SSM_CHUNK):
            off = (SSM_CHUNK - 1 - j) * a2
            m_scr[j * a2:(j + 1) * a2, :] = kflat[:, off:off + PAIR_W].astype(BF16)
        for d in range(2):
            cr, ci = expand(cp_ref[2 * d], til_ref), expand(cp_ref[2 * d + 1], til_ref)
            wr, wi = expand(wp_ref[2 * d], rep_ref), expand(wp_ref[2 * d + 1], rep_ref)
            p_scr[(2 * d) * w:(2 * d + 1) * w, :] = (cr * wr - ci * wi).astype(BF16)
            p_scr[(2 * d + 1) * w:(2 * d + 2) * w, :] = (-(cr * wi + ci * wr)).astype(BF16)

    u = u_ref[...]
    y = jnp.dot(u, m_scr[...], preferred_element_type=F32)
    y = y + jnp.dot(s_ref[...], p_scr[...], preferred_element_type=F32)
    y_ref[...] = (y + d_ref[...] * u.astype(F32)).astype(BF16)


def _ssm_trunk(u, bsz, mats):
    kflat, bq, wq, cp, wp, til, rep, alpha, skip = mats
    rows = u.shape[0]
    nc = rows // bsz
    np_ = kflat.shape[0]
    pw = PAIR_W
    sw = 4 * bq.shape[3]
    tr = min(SSM_ROW_TILE, rows)
    per_pair = lambda a: pl.BlockSpec((None,) + a.shape[1:], lambda p, r: (p,) + (0,) * (a.ndim - 1))
    whole = lambda a: pl.BlockSpec(a.shape, lambda p, r: (0,) * a.ndim)
    v = pl.pallas_call(
        _ssm_v_kernel,
        grid=(np_, rows // tr),
        in_specs=[pl.BlockSpec((tr, pw), lambda p, r: (r, p)), per_pair(bq), per_pair(wq)],
        out_specs=pl.BlockSpec((tr, sw), lambda p, r: (r, p)),
        out_shape=jax.ShapeDtypeStruct((rows, np_ * sw), F32),
        scratch_shapes=[pltpu.VMEM((pw, sw), BF16)],
        compiler_params=_cparams(2),
        name="ssm_v",
    )(u, bq, wq)
    s = pl.pallas_call(
        functools.partial(_ssm_scan_kernel, n_chunks=nc, bsz=bsz),
        grid=(np_,),
        in_specs=[pl.BlockSpec((rows, sw), lambda p: (0, p)),
                  pl.BlockSpec((None, 4, LANES), lambda p: (p, 0, 0))],
        out_specs=pl.BlockSpec((rows, sw), lambda p: (0, p)),
        out_shape=jax.ShapeDtypeStruct((rows, np_ * sw), BF16),
        compiler_params=_cparams(1),
        name="ssm_scan",
    )(v, alpha)
    y = pl.pallas_call(
        _ssm_y_kernel,
        grid=(np_, rows // tr),
        in_specs=[pl.BlockSpec((tr, pw), lambda p, r: (r, p)),
                  pl.BlockSpec((tr, sw), lambda p, r: (r, p)),
                  per_pair(kflat), per_pair(cp), per_pair(wp), whole(til), whole(rep), per_pair(skip)],
        out_specs=pl.BlockSpec((tr, pw), lambda p, r: (r, p)),
        out_shape=jax.ShapeDtypeStruct((rows, np_ * pw), BF16),
        scratch_shapes=[pltpu.VMEM((pw, pw), BF16), pltpu.VMEM((sw, pw), BF16)],
        compiler_params=_cparams(2),
        name="ssm_y",
    )(u, s, kflat, cp, wp, til, rep, skip)
    return y


def _mix_kernel(yp_ref, ys_ref, zp_ref, zs_ref, wglu_ref, bglu_ref, gssm_ref,
                lng_ref, lnb_ref, ws_ref, bs_ref, gsgu_ref, o_ref, ya_scr, *, n_prompt_tiles):
    i = pl.program_id(0)
    tm = o_ref.shape[0]
    n_blk = ya_scr.shape[0]
    d_ssm = n_blk * LANES
    nc = tm // SSM_CHUNK

    def body(y_ref, zuv_ref):
        masks = _atom_masks(nc)
        for b in range(n_blk):
            for v in range(SSM_CHUNK // ATOMS):
                src = [y_ref[:, (ATOMS * b + pi) * PAIR_W + v * LANES:
                             (ATOMS * b + pi) * PAIR_W + (v + 1) * LANES].astype(F32)
                       for pi in range(ATOMS)]
                dst = _atom_transpose(src, masks)
                for jj in range(ATOMS):
                    ya_scr[b, pl.ds(ATOMS * v + jj, nc, stride=SSM_CHUNK), :] = dst[jj]
        y = jnp.concatenate([ya_scr[b] for b in range(n_blk)], axis=1)
        gl = _gelu(y)
        gate = jnp.dot(gl.astype(BF16), wglu_ref[...], preferred_element_type=F32) + bglu_ref[...]
        o_ref[:, :d_ssm] = _rms(gl * jax.nn.sigmoid(gate), gssm_ref[...]).astype(BF16)
        d_sgu = zuv_ref.shape[1] // 2
        u = _gelu(zuv_ref[:, :d_sgu].astype(F32))
        gv = _gelu(zuv_ref[:, d_sgu:].astype(F32))
        xc = gv - jnp.mean(gv, axis=-1, keepdims=True)
        v = xc * lax.rsqrt(jnp.mean(xc * xc, axis=-1, keepdims=True) + LN_EPS)
        v = (v * lng_ref[...] + lnb_ref[...]).astype(BF16)
        lo = lax.broadcasted_iota(I32, (CHUNK, LANES), 1) < (LANES // 2)
        zero = jnp.zeros((CHUNK, LANES), BF16)
        rows = []
        for c in range(tm // CHUNK):
            cols = []
            for j in range(d_sgu // LANES):
                vp = v[c * CHUNK:(c + 1) * CHUNK, j * LANES:(j + 1) * LANES]
                rhs = jnp.concatenate([jnp.where(lo, vp, zero), jnp.where(lo, zero, vp)], axis=0)
                cols.append(jnp.dot(ws_ref[j], rhs, preferred_element_type=F32))
            rows.append(jnp.concatenate(cols, axis=1) + bs_ref[...])
        s = jnp.concatenate(rows, axis=0)
        o_ref[:, d_ssm:] = _rms(u * s, gsgu_ref[...]).astype(BF16)

    @pl.when(i < n_prompt_tiles)
    def _():
        body(yp_ref, zp_ref)

    @pl.when(i >= n_prompt_tiles)
    def _():
        body(ys_ref, zs_ref)


def _mix(y_p, y_s, zuv_p, zuv_s, wglu, bglu, gssm, lng, lnb, ws_pairs, bias_s, gsgu):
    tp, ts = zuv_p.shape[0], zuv_s.shape[0]
    d_sgu = zuv_p.shape[1] // 2
    d_ssm = wglu.shape[0]
    tm = SEQ_TILE
    npt = tp // tm
    nc = tm // SSM_CHUNK
    uw = d_ssm * SSM_CHUNK
    ip = lambda i: jnp.minimum(i, npt - 1)
    isamp = lambda i: jnp.maximum(i - npt, 0)
    row = lambda w: pl.BlockSpec((1, w), lambda i: (0, 0))
    return pl.pallas_call(
        functools.partial(_mix_kernel, n_prompt_tiles=npt),
        grid=((tp + ts) // tm,),
        in_specs=[pl.BlockSpec((nc, uw), lambda i: (ip(i), 0)),
                  pl.BlockSpec((nc, uw), lambda i: (isamp(i), 0)),
                  pl.BlockSpec((tm, 2 * d_sgu), lambda i: (ip(i), 0)),
                  pl.BlockSpec((tm, 2 * d_sgu), lambda i: (isamp(i), 0)),
                  pl.BlockSpec((d_ssm, d_ssm), lambda i: (0, 0)),
                  row(d_ssm), row(d_ssm), row(d_sgu), row(d_sgu),
                  pl.BlockSpec(ws_pairs.shape, lambda i: (0, 0, 0)),
                  pl.BlockSpec(bias_s.shape, lambda i: (0, 0)),
                  row(d_sgu)],
        out_specs=pl.BlockSpec((tm, d_ssm + d_sgu), lambda i: (i, 0)),
        out_shape=jax.ShapeDtypeStruct((tp + ts, d_ssm + d_sgu), BF16),
        scratch_shapes=[pltpu.VMEM((d_ssm // LANES, tm, LANES), F32)],
        compiler_params=_cparams(),
        name="mix",
    )(y_p, y_s, zuv_p, zuv_s, wglu, bglu, gssm, lng, lnb, ws_pairs, bias_s, gsgu)


def _sorted_rows(tile):
    return TOP_K * tile + MXU_DIM


def _route_kernel(xp_ref, xs_ref, mix_ref, wout_ref, gffn_ref, whi_ref, wcat_ref, br_ref, tri_ref,
                  ltri_ref, x1_ref, xsort_ref, meta_ref, c8_ref, *, n_prompt_tiles, n_tiles, n_exp):
    i = pl.program_id(0)
    tm = mix_ref.shape[0]
    p_rows = xsort_ref.shape[0]

    def body(x_ref):
        x1 = x_ref[...] + jnp.dot(mix_ref[...], wout_ref[...], preferred_element_type=F32)
        x1_ref[...] = x1
        h2 = _rms(x1, gffn_ref[...])
        hi = h2.astype(BF16)
        lo = (h2 - hi.astype(F32)).astype(BF16)
        hw = jnp.dot(hi, wcat_ref[...], preferred_element_type=F32)
        lt = (hw[:, :LANES] + (jnp.dot(lo, whi_ref[...], preferred_element_type=F32) + hw[:, LANES:])
              + br_ref[...])
        logits = lt.T[:n_exp]
        eio = lax.broadcasted_iota(I32, (n_exp, tm), 0)
        vals, idxs = [], []
        l = logits
        for _ in range(TOP_K):
            m = jnp.max(l, axis=0, keepdims=True)
            idx = jnp.min(jnp.where(l == m, eio, n_exp), axis=0, keepdims=True)
            vals.append(m)
            idxs.append(idx)
            l = jnp.where(eio == idx, -jnp.inf, l)
        ex = [jnp.exp(v - vals[0]) for v in vals]
        den = ex[0] + ex[1] + ex[2] + ex[3]
        ws = [e / den for e in ex]
        hot = [eio == idx for idx in idxs]
        cnt = sum(h.astype(F32) for h in hot)
        prefix = jnp.dot(cnt.astype(BF16), tri_ref[...], preferred_element_type=F32)
        c = jnp.sum(cnt, axis=1, keepdims=True)
        c8 = jnp.floor((c + (SUBLANES - 1)) * (1.0 / SUBLANES)) * SUBLANES
        c8b = jnp.broadcast_to(c8, (n_exp, LANES))
        c8_ref[...] = c8b
        run0 = jnp.dot(ltri_ref[...], c8b.astype(BF16), preferred_element_type=F32)[:, 0:1]
        base = run0 + prefix
        pos = [jnp.sum(jnp.where(h, base, 0.0), axis=0, keepdims=True) for h in hot]
        meta = jnp.concatenate(ws + pos + [jnp.zeros((LANES - 2 * TOP_K, tm), F32)], axis=0)
        meta_ref[...] = meta.T
        rio = lax.broadcasted_iota(I32, (SORT_CHUNK, tm), 0).astype(F32).astype(BF16)
        one = jnp.ones((SORT_CHUNK, tm), BF16)
        zero = jnp.zeros((SORT_CHUNK, tm), BF16)
        for r in range(p_rows // SORT_CHUNK):
            lo_r = float(r * SORT_CHUNK)
            rel = [jnp.where((p >= lo_r) & (p < lo_r + SORT_CHUNK), p - lo_r, -1.0).astype(BF16) for p in pos]
            sel = (rio == rel[0]) | (rio == rel[1]) | (rio == rel[2]) | (rio == rel[3])
            srt = jnp.dot(jnp.where(sel, one, zero), hi, preferred_element_type=F32)
            xsort_ref[r * SORT_CHUNK:(r + 1) * SORT_CHUNK, :] = _pack_bf16_pairs(srt)

    @pl.when(i < n_prompt_tiles)
    def _():
        body(xp_ref)

    @pl.when((i >= n_prompt_tiles) & (i < n_tiles))
    def _():
        body(xs_ref)

    @pl.when(i == n_tiles)
    def _():
        xsort_ref[...] = jnp.zeros_like(xsort_ref)


def _dual_specs(tile, width, n_prompt_tiles, n_tiles):
    last_p = n_prompt_tiles - 1
    last_s = n_tiles - n_prompt_tiles - 1
    sp = pl.BlockSpec((tile, width), lambda i: (jnp.minimum(i, last_p), 0))
    ss = pl.BlockSpec((tile, width), lambda i: (jnp.clip(i - n_prompt_tiles, 0, last_s), 0))
    return sp, ss


def _route(xp, xs, mixed, w_out, g_ffn, wr_hi, wr_cat, br_pad, n_exp):
    tp, d = xp.shape
    t = mixed.shape[0]
    tm = TOKEN_TILE
    npt = tp // tm
    nt = t // tm
    p_rows = _sorted_rows(tm)
    iota = lambda n, ax: lax.broadcasted_iota(I32, (n, n), ax)
    tri = (iota(tm, 0) < iota(tm, 1)).astype(BF16)
    ltri = (iota(n_exp, 1) < iota(n_exp, 0)).astype(BF16)
    sp, ss = _dual_specs(tm, d, npt, nt)
    const = lambda shape: pl.BlockSpec(shape, lambda i: (0,) * len(shape))
    tile = lambda i: jnp.minimum(i, nt - 1)
    return pl.pallas_call(
        functools.partial(_route_kernel, n_prompt_tiles=npt, n_tiles=nt, n_exp=n_exp),
        grid=(nt + 1,),
        in_specs=[sp, ss,
                  pl.BlockSpec((tm, mixed.shape[1]), lambda i: (tile(i), 0)),
                  const(w_out.shape), const((1, d)), const(wr_hi.shape), const(wr_cat.shape),
                  const((1, LANES)), const((tm, tm)), const((n_exp, n_exp))],
        out_specs=[pl.BlockSpec((tm, d), lambda i: (tile(i), 0)),
                   pl.BlockSpec((p_rows, d // 2), lambda i: (i, 0)),
                   pl.BlockSpec((tm, LANES), lambda i: (tile(i), 0)),
                   pl.BlockSpec((None, n_exp, LANES), lambda i: (tile(i), 0, 0))],
        out_shape=[jax.ShapeDtypeStruct((t, d), F32),
                   jax.ShapeDtypeStruct(((nt + 1) * p_rows, d // 2), U32),
                   jax.ShapeDtypeStruct((t, LANES), F32),
                   jax.ShapeDtypeStruct((nt, n_exp, LANES), F32)],
        compiler_params=_cparams(),
        name="route",
    )(xp, xs, mixed, w_out, g_ffn, wr_hi, wr_cat, br_pad, tri, ltri)


def _slot_tables(c8, p_rows, n_blocks):
    nt, n_exp = c8.shape
    run0 = jnp.cumsum(c8, axis=1) - c8
    seg_len = c8.T
    cum = jnp.cumsum(seg_len, axis=1)
    tot = cum[:, -1]
    padded = (tot + MOE_TILE - 1) // MOE_TILE * MOE_TILE
    ends = jnp.cumsum(padded)
    starts = ends - padded
    n_used = (ends[-1] // MOE_TILE).astype(I32)
    blk = jnp.arange(n_blocks, dtype=I32)
    be = jnp.minimum(jnp.sum(ends[None, :] <= (blk * MOE_TILE)[:, None], axis=1), n_exp - 1).astype(I32)
    block_e = jnp.where(blk < n_used, be, be[jnp.maximum(n_used - 1, 0)])
    onehot = (be[:, None] == jnp.arange(n_exp, dtype=I32)[None, :]).astype(F32)
    pick = lambda tbl: jnp.dot(onehot, tbl.astype(F32), precision=lax.Precision.HIGHEST)
    run_end = pick(cum)
    run_beg = run_end - pick(seg_len)
    shift = pick((jnp.arange(nt, dtype=I32) * p_rows)[None, :] + run0.T) - run_beg
    grow = (blk[:, None] * MOE_TILE + jnp.arange(MOE_GROUPS, dtype=I32)[None, :] * SUBLANES).astype(F32)
    rel = grow - pick(starts[:, None])
    inside = (run_beg[:, None, :] <= rel[:, :, None]) & (rel[:, :, None] < run_end[:, None, :])
    src_row = (rel + jnp.sum(jnp.where(inside, shift[:, None, :], 0.0), axis=2)).astype(I32)
    valid = jnp.any(inside, axis=2) & (blk < n_used)[:, None]
    scratch0 = nt * p_rows // SUBLANES
    zero_group = scratch0 + 2 * MOE_GROUPS
    g_in = jnp.where(valid, src_row // SUBLANES, zero_group).astype(I32)
    g_scr = scratch0 + (blk[:, None] % 2) * MOE_GROUPS + jnp.arange(MOE_GROUPS, dtype=I32)[None, :]
    g_out = jnp.where(valid, src_row // SUBLANES, g_scr).astype(I32)
    return block_e, n_used, g_in, g_out


def _moe_kernel(be_ref, nu_ref, gin_ref, gnext_ref, gout_ref, xs_hbm, wgu_ref, bgu_ref, wd_ref, bd_ref,
                eo_hbm, xbuf, obuf, wgu_s, wd_s, isem, osem):
    j = pl.program_id(0)
    nu = nu_ref[0]
    f = wd_ref.shape[0]

    @pl.when((j < nu) & ((j == 0) | (be_ref[j] != be_ref[jnp.maximum(j - 1, 0)])))
    def _():
        for c in range(0, wgu_ref.shape[0], CAST_ROWS):
            wgu_s[c:c + CAST_ROWS, :] = wgu_ref[c:c + CAST_ROWS, :].astype(BF16)
        for c in range(0, wd_ref.shape[0], CAST_ROWS):
            wd_s[c:c + CAST_ROWS, :] = wd_ref[c:c + CAST_ROWS, :].astype(BF16)

    def in_copy(tbl_ref, r, sl):
        return pltpu.make_async_copy(xs_hbm.at[tbl_ref[0, r]], xbuf.at[sl, r], isem.at[sl])

    def out_copy(r, sl):
        return pltpu.make_async_copy(obuf.at[sl, r], eo_hbm.at[gout_ref[0, r]], osem.at[sl])

    def wait_in(sl):
        pltpu.make_async_copy(xs_hbm.at[pl.ds(0, MOE_GROUPS)], xbuf.at[sl], isem.at[sl]).wait()

    def wait_out(sl):
        pltpu.make_async_copy(obuf.at[sl], eo_hbm.at[pl.ds(0, MOE_GROUPS)], osem.at[sl]).wait()

    @pl.when(j == 0)
    def _():
        for r in range(MOE_GROUPS):
            in_copy(gin_ref, r, 0).start()

    def step(slot):
        other = 1 - slot

        @pl.when(j + 1 < nu)
        def _():
            for r in range(MOE_GROUPS):
                in_copy(gnext_ref, r, other).start()

        wait_in(slot)

        @pl.when(j >= 2)
        def _():
            wait_out(slot)

        x = _unpack_bf16_pairs(xbuf[slot].reshape(MOE_TILE, xbuf.shape[-1]))
        gate = jnp.dot(x, wgu_s[:, :f], preferred_element_type=F32) + bgu_ref[:, :f]
        gate = jnp.minimum(gate, SWIGLU_LIMIT)
        sg = gate * jax.nn.sigmoid(SWIGLU_ALPHA * gate)
        up = jnp.dot(x, wgu_s[:, f:], preferred_element_type=F32) + bgu_ref[:, f:]
        act = sg * (jnp.clip(up, -SWIGLU_LIMIT, SWIGLU_LIMIT) + 1.0)
        out = jnp.dot(act.astype(BF16), wd_s[...], preferred_element_type=F32) + bd_ref[...]
        obuf[slot] = _pack_bf16_pairs(out.astype(BF16).astype(F32)).reshape(obuf.shape[1:])
        for r in range(MOE_GROUPS):
            out_copy(r, slot).start()

        @pl.when(j == nu - 1)
        def _():
            wait_out(slot)

            @pl.when(j >= 1)
            def _():
                wait_out(other)

    for parity in range(2):
        pl.when((j < nu) & (lax.rem(j, 2) == parity))(functools.partial(step, parity))


def _moe(block_e, n_used, g_in, g_out, xsorted, wgu, bgu, wd, bd):
    n_blocks = block_e.shape[0]
    dh = xsorted.shape[1]
    d = 2 * dh
    f2 = wgu.shape[2]
    f = wd.shape[1]
    g_in3 = g_in.reshape(n_blocks, 1, MOE_GROUPS)
    g_next3 = jnp.concatenate([g_in3[1:], g_in3[:1]], axis=0)
    g_out3 = g_out.reshape(n_blocks, 1, MOE_GROUPS)
    xs3 = xsorted.reshape(xsorted.shape[0] // SUBLANES, SUBLANES, dh)
    tbl = pl.BlockSpec((None, 1, MOE_GROUPS), lambda j, be, nu: (j, 0, 0), memory_space=pltpu.SMEM)
    return pl.pallas_call(
        _moe_kernel,
        grid_spec=pltpu.PrefetchScalarGridSpec(
            num_scalar_prefetch=2, grid=(n_blocks,),
            in_specs=[tbl, tbl, tbl,
                      pl.BlockSpec(memory_space=pl.ANY),
                      pl.BlockSpec((None, d, f2), lambda j, be, nu: (be[j], 0, 0)),
                      pl.BlockSpec((None, 1, f2), lambda j, be, nu: (be[j], 0, 0)),
                      pl.BlockSpec((None, f, d), lambda j, be, nu: (be[j], 0, 0)),
                      pl.BlockSpec((None, 1, d), lambda j, be, nu: (be[j], 0, 0))],
            out_specs=pl.BlockSpec(memory_space=pl.ANY),
            scratch_shapes=[pltpu.VMEM((2, MOE_GROUPS, SUBLANES, dh), U32),
                            pltpu.VMEM((2, MOE_GROUPS, SUBLANES, dh), U32),
                            pltpu.VMEM((d, f2), BF16), pltpu.VMEM((f, d), BF16),
                            pltpu.SemaphoreType.DMA((2,)), pltpu.SemaphoreType.DMA((2,))]),
        out_shape=jax.ShapeDtypeStruct(xs3.shape, U32),
        input_output_aliases={5: 0},
        compiler_params=_cparams(),
        name="moe",
    )(block_e, n_used, g_in3, g_next3, g_out3, xs3, wgu, bgu, wd, bd).reshape(xsorted.shape)


def _combine_kernel(meta_ref, x1_ref, g_ref, eo_ref, y_ref):
    tm = x1_ref.shape[0]
    p_rows = eo_ref.shape[0]
    lane = lax.broadcasted_iota(I32, (tm, SORT_CHUNK), 1).astype(F32).astype(BF16)
    ws = [meta_ref[:, k:k + 1].astype(BF16) for k in range(TOP_K)]
    pos = [meta_ref[:, TOP_K + k:TOP_K + k + 1] for k in range(TOP_K)]
    acc = x1_ref[...]
    for r in range(p_rows // SORT_CHUNK):
        buf = _unpack_bf16_pairs(eo_ref[r * SORT_CHUNK:(r + 1) * SORT_CHUNK, :])
        lo_r = float(r * SORT_CHUNK)
        wm = jnp.zeros((tm, SORT_CHUNK), BF16)
        for k in range(TOP_K):
            rel = jnp.where((pos[k] >= lo_r) & (pos[k] < lo_r + SORT_CHUNK), pos[k] - lo_r, -1.0).astype(BF16)
            wm = jnp.where(lane == rel, ws[k], wm)
        acc = acc + jnp.dot(wm, buf, preferred_element_type=F32)
    y_ref[...] = _rms(acc, g_ref[...])


def _combine(meta, x1, g_final, eo, tile_off, n_tokens):
    d = x1.shape[1]
    tm = TOKEN_TILE
    p_rows = _sorted_rows(tm)
    return pl.pallas_call(
        _combine_kernel,
        grid=(n_tokens // tm,),
        in_specs=[pl.BlockSpec((tm, LANES), lambda i: (i + tile_off, 0)),
                  pl.BlockSpec((tm, d), lambda i: (i + tile_off, 0)),
                  pl.BlockSpec((1, d), lambda i: (0, 0)),
                  pl.BlockSpec((p_rows, d // 2), lambda i: (i + tile_off, 0))],
        out_specs=pl.BlockSpec((tm, d), lambda i: (i, 0)),
        out_shape=jax.ShapeDtypeStruct((n_tokens, d), F32),
        compiler_params=_cparams(),
        name="combine",
    )(meta, x1, g_final, eo)


def kernel(x_prompt, x_sample, g_mix, w_in, ssm_lam_re, ssm_lam_im, ssm_log_dt, ssm_b_re, ssm_b_im,
           ssm_c_re, ssm_c_im, ssm_d, w_glu, b_glu, sgu_ln_g, sgu_ln_b, sgu_w_s, sgu_b_s,
           g_out_ssm, g_out_sgu, w_out, g_ffn, w_router, b_router, w_gate_up, b_gate_up,
           w_down, b_down, g_final):
    assert g_mix.shape[0] == 1, "single-layer trunk"
    bp, lp, d = x_prompt.shape
    bs, ls, _ = x_sample.shape
    tp, ts = bp * lp, bs * ls
    t = tp + ts
    d_ssm = ssm_d.shape[1]
    d_sgu = sgu_ln_g.shape[1]
    n_exp = w_router.shape[2]
    assert all(n % tile == 0 for n in (tp, ts) for tile in (SEQ_TILE, TOKEN_TILE))
    assert lp % CHUNK == 0 and ls % CHUNK == 0 and SEQ_TILE % CHUNK == 0 and CHUNK % SSM_CHUNK == 0
    assert SSM_CHUNK * SSM_GROUP == MXU_DIM and 2 * ssm_lam_re.shape[-1] == LANES
    assert d_sgu // SGU_HEADS == LANES // 2 and n_exp <= LANES
    assert n_exp * (SUBLANES - 1) <= MXU_DIM and _sorted_rows(TOKEN_TILE) % SORT_CHUNK == 0
    assert (2 * MOE_GROUPS + 1) * SUBLANES <= _sorted_rows(TOKEN_TILE)

    xp = x_prompt.reshape(tp, d)
    xs = x_sample.reshape(ts, d)
    row = lambda a: a.reshape(1, -1).astype(F32)

    w_in_b = w_in[0].astype(BF16)
    u_p, zuv_p = _inproj(xp, row(g_mix[0]), w_in_b, d_ssm)
    u_s, zuv_s = _inproj(xs, row(g_mix[0]), w_in_b, d_ssm)

    mats = _ssm_matrices(ssm_lam_re[0], ssm_lam_im[0], ssm_log_dt[0], ssm_b_re[0], ssm_b_im[0],
                         ssm_c_re[0], ssm_c_im[0], ssm_d[0])
    y_p = _ssm_trunk(u_p, bp, mats)
    y_s = _ssm_trunk(u_s, bs, mats)

    ws = sgu_w_s[0]
    ws_pairs = jnp.concatenate([ws[0::2], ws[1::2]], axis=2).astype(BF16)
    bias_s = jnp.repeat(sgu_b_s[0].T, d_sgu // SGU_HEADS, axis=1).astype(F32)
    mixed = _mix(y_p, y_s, zuv_p, zuv_s, w_glu[0].astype(BF16), row(b_glu[0]), row(g_out_ssm[0]),
                 row(sgu_ln_g[0]), row(sgu_ln_b[0]), ws_pairs, bias_s, row(g_out_sgu[0]))

    wr_pad = jnp.pad(w_router[0].astype(F32), ((0, 0), (0, LANES - n_exp)))
    wr_hi = wr_pad.astype(BF16)
    wr_cat = jnp.concatenate([wr_hi, (wr_pad - wr_hi.astype(F32)).astype(BF16)], axis=1)
    br_pad = jnp.pad(b_router[0].astype(F32), (0, LANES - n_exp)).reshape(1, LANES)
    x1, xsorted, meta, c8 = _route(xp, xs, mixed, w_out[0].astype(BF16), row(g_ffn[0]),
                                   wr_hi, wr_cat, br_pad, n_exp)

    nt = t // TOKEN_TILE
    p_rows = _sorted_rows(TOKEN_TILE)
    max_rows = t * TOP_K + nt * n_exp * (SUBLANES - 1) + n_exp * (MOE_TILE - 1)
    n_blocks = -(-max_rows // MOE_TILE)
    block_e, n_used, g_in, g_out = _slot_tables(c8[:, :, 0].astype(I32), p_rows, n_blocks)
    eo = _moe(block_e, n_used.reshape(1), g_in, g_out, xsorted, w_gate_up[0],
              b_gate_up[0][:, None, :].astype(F32), w_down[0], b_down[0][:, None, :].astype(F32))

    gf = row(g_final)
    y_prompt = _combine(meta, x1, gf, eo, 0, tp)
    y_sample = _combine(meta, x1, gf, eo, tp // TOKEN_TILE, ts)
    return y_prompt.reshape(bp, lp, d), y_sample.reshape(bs, ls, d)
```

```python
import functools
import math

import jax
import jax.numpy as jnp
from jax import lax
from jax.experimental import pallas as pl
from jax.experimental.pallas import tpu as pltpu

F32 = jnp.float32
BF16 = jnp.bfloat16
I32 = jnp.int32
U32 = jnp.uint32

SSM_GROUP = 16
SGU_HEADS = 8
CHUNK = 128
TOP_K = 4
SWIGLU_LIMIT = 7.0
SWIGLU_ALPHA = 1.702
RMS_EPS = 1e-6
LN_EPS = 1e-5

LANES = 128
SUBLANES = 8
MXU_DIM = 256
SSM_CHUNK = MXU_DIM // SSM_GROUP
PAIR_W = 2 * SSM_CHUNK * SSM_GROUP
ATOM = 2 * SSM_GROUP
ATOMS = LANES // ATOM

TOKEN_TILE = 512
SEQ_TILE = 1024
MOE_TILE = 512
MOE_GROUPS = MOE_TILE // SUBLANES
SORT_CHUNK = 256
CAST_ROWS = 128
SSM_ROW_TILE = 2048
SCAN_SEQS = 4
VMEM_LIMIT = 56 * 1024 * 1024


def _cparams(n_axes=1, vmem=None):
    return pltpu.CompilerParams(
        dimension_semantics=("arbitrary",) * n_axes,
        vmem_limit_bytes=vmem if vmem is not None else VMEM_LIMIT,
    )


def _rms(x, g):
    return x * lax.rsqrt(jnp.mean(x * x, axis=-1, keepdims=True) + RMS_EPS) * g


def _gelu(x):
    return x * (lax.erf(x * (1.0 / math.sqrt(2.0))) + 1.0) * 0.5


def _atom_masks(rows):
    lane = lax.broadcasted_iota(I32, (rows, LANES), 1)
    return [(lane >= a * ATOM) & (lane < (a + 1) * ATOM) for a in range(ATOMS)]


def _atom_transpose(src, masks):
    dst = []
    for b in range(ATOMS):
        acc = None
        for a in range(ATOMS):
            r = (a - b) % ATOMS
            piece = src[a] if r == 0 else pltpu.roll(src[a], ATOM * r, axis=1)
            acc = piece if acc is None else jnp.where(masks[a], piece, acc)
        dst.append(acc)
    return dst


def _pack_bf16_pairs(x):
    w = x.shape[1] // 2
    lo = lax.bitcast_convert_type(x[:, :w], U32) >> 16
    hi = lax.bitcast_convert_type(x[:, w:], U32) & jnp.uint32(0xFFFF0000)
    return hi | lo


def _unpack_bf16_pairs(u):
    lo = lax.bitcast_convert_type(u << 16, F32)
    hi = lax.bitcast_convert_type(u & jnp.uint32(0xFFFF0000), F32)
    return jnp.concatenate([lo.astype(BF16), hi.astype(BF16)], axis=1)


def _inproj_kernel(x_ref, g_ref, w_ref, u_ref, zuv_ref, za_scr, *, d_ssm):
    tm = x_ref.shape[0]
    nc = tm // SSM_CHUNK
    h = _rms(x_ref[...], g_ref[...])
    z = jnp.dot(h.astype(BF16), w_ref[...], preferred_element_type=F32)
    zuv_ref[...] = _gelu(z[:, d_ssm:]).astype(BF16)
    n_blk = d_ssm // LANES
    for b in range(n_blk):
        za_scr[b] = z[:, b * LANES:(b + 1) * LANES]
    masks = _atom_masks(nc)
    n_quads = SSM_CHUNK // ATOMS
    for b in range(n_blk):
        for v in range(n_quads):
            src = [za_scr[b, pl.ds(ATOMS * v + jj, nc, stride=SSM_CHUNK), :] for jj in range(ATOMS)]
            dst = _atom_transpose(src, masks)
            for pi in range(ATOMS):
                c0 = (ATOMS * b + pi) * PAIR_W + v * LANES
                u_ref[:, c0:c0 + LANES] = dst[pi].astype(BF16)


def _inproj(x, g_mix, w_in, d_ssm):
    t, d = x.shape
    tm = SEQ_TILE
    d_in = w_in.shape[1]
    nc = tm // SSM_CHUNK
    uw = d_ssm * SSM_CHUNK
    return pl.pallas_call(
        functools.partial(_inproj_kernel, d_ssm=d_ssm),
        grid=(t // tm,),
        in_specs=[pl.BlockSpec((tm, d), lambda i: (i, 0)),
                  pl.BlockSpec((1, d), lambda i: (0, 0)),
                  pl.BlockSpec((d, d_in), lambda i: (0, 0))],
        out_specs=[pl.BlockSpec((nc, uw), lambda i: (i, 0)),
                   pl.BlockSpec((tm, d_in - d_ssm), lambda i: (i, 0))],
        out_shape=[jax.ShapeDtypeStruct((t // SSM_CHUNK, uw), BF16),
                   jax.ShapeDtypeStruct((t, d_in - d_ssm), BF16)],
        scratch_shapes=[pltpu.VMEM((d_ssm // LANES, tm, LANES), F32)],
        compiler_params=_cparams(),
        name="inproj",
    )(x, g_mix, w_in)


def _ssm_matrices(lam_re, lam_im, log_dt, b_re, b_im, c_re, c_im, d_skip):
    hp = lax.Precision.HIGHEST
    _, g, n = lam_re.shape
    p = b_re.shape[-1]
    lc = SSM_CHUNK
    np_ = g // 2
    dt = jnp.exp(log_dt)[..., None]
    mag = jnp.exp(lam_re * dt)
    ar = mag * jnp.cos(lam_im * dt)
    ai = mag * jnp.sin(lam_im * dt)
    den = lam_re * lam_re + lam_im * lam_im
    nr = ar - 1.0
    fr = (nr * lam_re + ai * lam_im) / den
    fi = (ai * lam_re - nr * lam_im) / den
    bbr = fr[..., None] * b_re - fi[..., None] * b_im
    bbi = fr[..., None] * b_im + fi[..., None] * b_re

    prs, pis = [jnp.ones_like(ar)], [jnp.zeros_like(ai)]
    for _ in range(lc):
        pr, pi = prs[-1], pis[-1]
        prs.append(pr * ar - pi * ai)
        pis.append(pr * ai + pi * ar)
    pw_r, pw_i = jnp.stack(prs), jnp.stack(pis)

    lane_k = jnp.arange(lc * p)
    til_k = (lane_k[None, :] % p == jnp.arange(p)[:, None]).astype(F32)
    rep_k = (lane_k[None, :] // p == jnp.arange(lc)[:, None]).astype(F32)
    ex = lambda x, e: jnp.einsum('dgnk,kx->dgnx', x, e, precision=hp)
    ct_r = ex(jnp.transpose(c_re, (0, 1, 3, 2)), til_k)
    ct_i = ex(jnp.transpose(c_im, (0, 1, 3, 2)), til_k)
    at_r = ex(jnp.transpose(pw_r[:lc], (1, 2, 3, 0)), rep_k)
    at_i = ex(jnp.transpose(pw_i[:lc], (1, 2, 3, 0)), rep_k)
    kk = (jnp.einsum('dgnq,dgnx->dgqx', bbr, ct_r * at_r - ct_i * at_i, precision=hp)
          - jnp.einsum('dgnq,dgnx->dgqx', bbi, ct_r * at_i + ct_i * at_r, precision=hp))
    kk = kk.reshape(2, g, p, lc, p)
    kf, kb = kk[0], kk[1]
    eye = jnp.eye(2, dtype=F32)
    a2 = p * 2

    k_all = jnp.concatenate([kb[:, :, :0:-1], (kf[:, :, :1] + kb[:, :, :1]), kf[:, :, 1:]], axis=2)
    k_all = k_all.reshape(np_, 2, p, 2 * lc - 1, p)
    kflat = jnp.einsum('nsqup,st->nsqutp', k_all, eye).reshape(np_, a2, (2 * lc - 1) * a2)
    kflat = jnp.pad(kflat, ((0, 0), (0, 0), (0, a2)))

    def atoms_q(x):
        x = jnp.transpose(x, (0, 1, 3, 2)).reshape(2, np_, 2, p, n)
        return jnp.einsum('dnsqm,st->dnsqtm', x, eye).reshape(2, np_, a2, 2 * n)
    bq_r, bq_i = atoms_q(bbr), atoms_q(bbi)
    bq = jnp.stack([bq_r[0], bq_i[0], bq_r[1], bq_i[1]], axis=1)
    ef = lc - 1 - jnp.arange(lc)
    eb = jnp.arange(lc)
    rows_q = lambda w, e, d: jnp.transpose(w[e, d].reshape(lc, np_, 2 * n), (1, 0, 2))
    wq = jnp.stack([rows_q(pw_r, ef, 0), rows_q(pw_i, ef, 0),
                    rows_q(pw_r, eb, 1), rows_q(pw_i, eb, 1)], axis=1)

    def atoms_p(x):
        x = jnp.transpose(x, (0, 1, 3, 2)).reshape(2, np_, 2, n, p)
        return jnp.einsum('dnsmp,st->dnsmtp', x, eye).reshape(2, np_, 2 * n, a2)
    cp_r, cp_i = atoms_p(c_re), atoms_p(c_im)
    cp = jnp.stack([cp_r[0], cp_i[0], cp_r[1], cp_i[1]], axis=1)
    pf_e = jnp.arange(lc) + 1
    pb_e = lc - jnp.arange(lc)
    cols_p = lambda w, e, d: jnp.transpose(w[e, d].reshape(lc, np_, 2 * n), (1, 2, 0))
    wp = jnp.stack([cols_p(pw_r, pf_e, 0), cols_p(pw_i, pf_e, 0),
                    cols_p(pw_r, pb_e, 1), cols_p(pw_i, pb_e, 1)], axis=1)
    lane = jnp.arange(lc * a2)
    til = (lane[None, :] % a2 == jnp.arange(a2)[:, None]).astype(BF16)
    rep = (lane[None, :] // a2 == jnp.arange(lc)[:, None]).astype(BF16)

    al = jnp.stack([pw_r[lc, 0], pw_i[lc, 0], pw_r[lc, 1], pw_i[lc, 1]])
    alpha = jnp.transpose(al.reshape(4, np_, 2 * n), (1, 0, 2))
    skip = jnp.broadcast_to(d_skip.reshape(np_, 1, a2), (np_, lc, a2)).reshape(np_, 1, lc * a2)
    return kflat, bq, wq, cp, wp, til, rep, alpha, skip.astype(F32)


def _ssm_v_kernel(u_ref, bq_ref, wq_ref, v_ref, q_scr):
    a2 = bq_ref.shape[1]
    w = bq_ref.shape[2]

    @pl.when(pl.program_id(1) == 0)
    def _():
        for d in range(2):
            br, bi = bq_ref[2 * d], bq_ref[2 * d + 1]
            for j in range(SSM_CHUNK):
                wr = wq_ref[2 * d, j:j + 1, :]
                wi = wq_ref[2 * d + 1, j:j + 1, :]
                q_scr[j * a2:(j + 1) * a2, (2 * d) * w:(2 * d + 1) * w] = (wr * br - wi * bi).astype(BF16)
                q_scr[j * a2:(j + 1) * a2, (2 * d + 1) * w:(2 * d + 2) * w] = (wr * bi + wi * br).astype(BF16)

    v_ref[...] = jnp.dot(u_ref[...], q_scr[...], preferred_element_type=F32)


def _ssm_scan_kernel(v_ref, a_ref, s_ref, *, n_chunks, bsz):
    w = LANES
    sub = SUBLANES
    n_groups = n_chunks // sub
    row = lax.broadcasted_iota(I32, (sub, w), 0)

    def cmul(ar, ai, xr, xi):
        return ar * xr - ai * xi, ar * xi + ai * xr

    def powers(ar, ai):
        p2 = cmul(ar, ai, ar, ai)
        p4 = cmul(*p2, *p2)
        p8 = cmul(*p4, *p4)
        p3 = cmul(*p2, ar, ai)
        p5 = cmul(*p4, ar, ai)
        p6 = cmul(*p4, *p2)
        p7 = cmul(*p4, *p3)
        seq = [(jnp.ones_like(ar), jnp.zeros_like(ai)), (ar, ai), p2, p3, p4, p5, p6, p7]
        tr = jnp.concatenate([s[0] for s in seq], axis=0)
        ti = jnp.concatenate([s[1] for s in seq], axis=0)
        return (ar, ai), p2, p4, p8, (tr, ti)

    def tile_scan(vr, vi, pw, reverse):
        a1, a2, a4, _, _ = pw

        def shift(x, s):
            if reverse:
                return jnp.where(row < sub - s, pltpu.roll(x, sub - s, axis=0), 0.0)
            return jnp.where(row >= s, pltpu.roll(x, s, axis=0), 0.0)

        xr, xi = shift(vr, 1), shift(vi, 1)
        for s, (ar, ai) in ((1, a1), (2, a2), (4, a4)):
            mr, mi = cmul(ar, ai, shift(xr, s), shift(xi, s))
            xr, xi = xr + mr, xi + mi
        e = 0 if reverse else sub - 1
        lr, li = cmul(a1[0], a1[1], xr[e:e + 1], xi[e:e + 1])
        return xr, xi, lr + vr[e:e + 1], li + vi[e:e + 1]

    pw_f = powers(a_ref[0:1, :], a_ref[1:2, :])
    pw_b = powers(a_ref[2:3, :], a_ref[3:4, :])
    tbr = jnp.concatenate([pw_b[4][0][sub - 1 - k:sub - k] for k in range(sub)], axis=0)
    tbi = jnp.concatenate([pw_b[4][1][sub - 1 - k:sub - k] for k in range(sub)], axis=0)

    def one_tile(row0, col0, pw, table, cr, ci, reverse):
        rows = pl.ds(pl.multiple_of(row0, sub), sub)
        xr, xi, lr, li = tile_scan(v_ref[rows, col0:col0 + w], v_ref[rows, col0 + w:col0 + 2 * w], pw, reverse)
        mr, mi = cmul(table[0], table[1], cr, ci)
        nr, ni = cmul(pw[3][0], pw[3][1], cr, ci)
        return xr + mr, xi + mi, nr + lr, ni + li

    pair = 2 * sub
    n_pairs = n_groups // 2

    def step(g, carry, b0):
        out = []
        for b in range(b0, b0 + len(carry) // 4):
            cfr, cfi, cbr, cbi = carry[4 * (b - b0):4 * (b - b0) + 4]
            rf = pl.multiple_of(b * n_chunks + g * pair, pair)
            rb = pl.multiple_of(b * n_chunks + (n_pairs - 1 - g) * pair, pair)
            ar, ai, cfr, cfi = one_tile(rf, 0, pw_f, pw_f[4], cfr, cfi, False)
            br, bi, cfr, cfi = one_tile(rf + sub, 0, pw_f, pw_f[4], cfr, cfi, False)
            s_ref[pl.ds(rf, pair), 0:w] = jnp.concatenate([ar, br], axis=0).astype(BF16)
            s_ref[pl.ds(rf, pair), w:2 * w] = jnp.concatenate([ai, bi], axis=0).astype(BF16)
            out += [cfr, cfi]
            br, bi, cbr, cbi = one_tile(rb + sub, 2 * w, pw_b, (tbr, tbi), cbr, cbi, True)
            ar, ai, cbr, cbi = one_tile(rb, 2 * w, pw_b, (tbr, tbi), cbr, cbi, True)
            s_ref[pl.ds(rb, pair), 2 * w:3 * w] = jnp.concatenate([ar, br], axis=0).astype(BF16)
            s_ref[pl.ds(rb, pair), 3 * w:4 * w] = jnp.concatenate([ai, bi], axis=0).astype(BF16)
            out += [cbr, cbi]
        return tuple(out)

    zero = jnp.zeros((1, w), F32)
    for b0 in range(0, bsz, SCAN_SEQS):
        nb = min(SCAN_SEQS, bsz - b0)
        lax.fori_loop(0, n_pairs, functools.partial(step, b0=b0), (zero,) * (4 * nb))


def _ssm_y_kernel(u_ref, s_ref, kflat_ref, cp_ref, wp_ref, til_ref, rep_ref, d_ref, y_ref, m_scr, p_scr):
    a2 = kflat_ref.shape[0]
    w = cp_ref.shape[1]

    def expand(x, e_ref):
        hi = x.astype(BF16)
        lo = (x - hi.astype(F32)).astype(BF16)
        return (jnp.dot(hi, e_ref[...], preferred_element_type=F32)
                + jnp.dot(lo, e_ref[...], preferred_element_type=F32))

    @pl.when(pl.program_id(1) == 0)
    def _():
        kflat = kflat_ref[...]
        for j in range(SSM_CHUNK):
            off = (SSM_CHUNK - 1 - j) * a2
            m_scr[j * a2:(j + 1) * a2, :] = kflat[:, off:off + PAIR_W].astype(BF16)
        for d in range(2):
            cr, ci = expand(cp_ref[2 * d], til_ref), expand(cp_ref[2 * d + 1], til_ref)
            wr, wi = expand(wp_ref[2 * d], rep_ref), expand(wp_ref[2 * d + 1], rep_ref)
            p_scr[(2 * d) * w:(2 * d + 1) * w, :] = (cr * wr - ci * wi).astype(BF16)
            p_scr[(2 * d + 1) * w:(2 * d + 2) * w, :] = (-(cr * wi + ci * wr)).astype(BF16)

    u = u_ref[...]
    y = jnp.dot(u, m_scr[...], preferred_element_type=F32)
    y = y + jnp.dot(s_ref[...], p_scr[...], preferred_element_type=F32)
    y_ref[...] = _gelu(y + d_ref[...] * u.astype(F32)).astype(BF16)


def _ssm_trunk(u, bsz, mats):
    kflat, bq, wq, cp, wp, til, rep, alpha, skip = mats
    rows = u.shape[0]
    nc = rows // bsz
    np_ = kflat.shape[0]
    pw = PAIR_W
    sw = 4 * bq.shape[3]
    tr = min(SSM_ROW_TILE, rows)
    per_pair = lambda a: pl.BlockSpec((None,) + a.shape[1:], lambda p, r: (p,) + (0,) * (a.ndim - 1))
    whole = lambda a: pl.BlockSpec(a.shape, lambda p, r: (0,) * a.ndim)
    v = pl.pallas_call(
        _ssm_v_kernel,
        grid=(np_, rows // tr),
        in_specs=[pl.BlockSpec((tr, pw), lambda p, r: (r, p)), per_pair(bq), per_pair(wq)],
        out_specs=pl.BlockSpec((tr, sw), lambda p, r: (r, p)),
        out_shape=jax.ShapeDtypeStruct((rows, np_ * sw), F32),
        scratch_shapes=[pltpu.VMEM((pw, sw), BF16)],
        compiler_params=_cparams(2),
        name="ssm_v",
    )(u, bq, wq)
    s = pl.pallas_call(
        functools.partial(_ssm_scan_kernel, n_chunks=nc, bsz=bsz),
        grid=(np_,),
        in_specs=[pl.BlockSpec((rows, sw), lambda p: (0, p)),
                  pl.BlockSpec((None, 4, LANES), lambda p: (p, 0, 0))],
        out_specs=pl.BlockSpec((rows, sw), lambda p: (0, p)),
        out_shape=jax.ShapeDtypeStruct((rows, np_ * sw), BF16),
        compiler_params=_cparams(1),
        name="ssm_scan",
    )(v, alpha)
    y = pl.pallas_call(
        _ssm_y_kernel,
        grid=(np_, rows // tr),
        in_specs=[pl.BlockSpec((tr, pw), lambda p, r: (r, p)),
                  pl.BlockSpec((tr, sw), lambda p, r: (r, p)),
                  per_pair(kflat), per_pair(cp), per_pair(wp), whole(til), whole(rep), per_pair(skip)],
        out_specs=pl.BlockSpec((tr, pw), lambda p, r: (r, p)),
        out_shape=jax.ShapeDtypeStruct((rows, np_ * pw), BF16),
        scratch_shapes=[pltpu.VMEM((pw, pw), BF16), pltpu.VMEM((sw, pw), BF16)],
        compiler_params=_cparams(2),
        name="ssm_y",
    )(u, s, kflat, cp, wp, til, rep, skip)
    return y


def _mix_kernel(yp_ref, ys_ref, zp_ref, zs_ref, wglu_ref, bglu_ref, gssm_ref,
                lng_ref, lnb_ref, ws_ref, bs_ref, gsgu_ref, o_ref, ya_scr, *, n_prompt_tiles):
    i = pl.program_id(0)
    tm = o_ref.shape[0]
    n_blk = ya_scr.shape[0]
    d_ssm = n_blk * LANES
    nc = tm // SSM_CHUNK

    def body(y_ref, zuv_ref):
        masks = _atom_masks(nc)
        for b in range(n_blk):
            for v in range(SSM_CHUNK // ATOMS):
                src = [y_ref[:, (ATOMS * b + pi) * PAIR_W + v * LANES:
                             (ATOMS * b + pi) * PAIR_W + (v + 1) * LANES].astype(F32)
                       for pi in range(ATOMS)]
                dst = _atom_transpose(src, masks)
                for jj in range(ATOMS):
                    ya_scr[b, pl.ds(ATOMS * v + jj, nc, stride=SSM_CHUNK), :] = dst[jj]
        y = jnp.concatenate([ya_scr[b] for b in range(n_blk)], axis=1)
        gl = y
        gate = jnp.dot(gl.astype(BF16), wglu_ref[...], preferred_element_type=F32) + bglu_ref[...]
        o_ref[:, :d_ssm] = _rms(gl * jax.nn.sigmoid(gate), gssm_ref[...]).astype(BF16)
        d_sgu = zuv_ref.shape[1] // 2
        u = zuv_ref[:, :d_sgu].astype(F32)
        gv = zuv_ref[:, d_sgu:].astype(F32)
        xc = gv - jnp.mean(gv, axis=-1, keepdims=True)
        v = xc * lax.rsqrt(jnp.mean(xc * xc, axis=-1, keepdims=True) + LN_EPS)
        v = (v * lng_ref[...] + lnb_ref[...]).astype(BF16)
        lo = lax.broadcasted_iota(I32, (CHUNK, LANES), 1) < (LANES // 2)
        zero = jnp.zeros((CHUNK, LANES), BF16)
        rows = []
        for c in range(tm // CHUNK):
            cols = []
            for j in range(d_sgu // LANES):
                vp = v[c * CHUNK:(c + 1) * CHUNK, j * LANES:(j + 1) * LANES]
                rhs = jnp.concatenate([jnp.where(lo, vp, zero), jnp.where(lo, zero, vp)], axis=0)
                cols.append(jnp.dot(ws_ref[j], rhs, preferred_element_type=F32))
            rows.append(jnp.concatenate(cols, axis=1) + bs_ref[...])
        s = jnp.concatenate(rows, axis=0)
        o_ref[:, d_ssm:] = _rms(u * s, gsgu_ref[...]).astype(BF16)

    @pl.when(i < n_prompt_tiles)
    def _():
        body(yp_ref, zp_ref)

    @pl.when(i >= n_prompt_tiles)
    def _():
        body(ys_ref, zs_ref)


def _mix(y_p, y_s, zuv_p, zuv_s, wglu, bglu, gssm, lng, lnb, ws_pairs, bias_s, gsgu):
    tp, ts = zuv_p.shape[0], zuv_s.shape[0]
    d_sgu = zuv_p.shape[1] // 2
    d_ssm = wglu.shape[0]
    tm = SEQ_TILE
    npt = tp // tm
    nc = tm // SSM_CHUNK
    uw = d_ssm * SSM_CHUNK
    ip = lambda i: jnp.minimum(i, npt - 1)
    isamp = lambda i: jnp.maximum(i - npt, 0)
    row = lambda w: pl.BlockSpec((1, w), lambda i: (0, 0))
    return pl.pallas_call(
        functools.partial(_mix_kernel, n_prompt_tiles=npt),
        grid=((tp + ts) // tm,),
        in_specs=[pl.BlockSpec((nc, uw), lambda i: (ip(i), 0)),
                  pl.BlockSpec((nc, uw), lambda i: (isamp(i), 0)),
                  pl.BlockSpec((tm, 2 * d_sgu), lambda i: (ip(i), 0)),
                  pl.BlockSpec((tm, 2 * d_sgu), lambda i: (isamp(i), 0)),
                  pl.BlockSpec((d_ssm, d_ssm), lambda i: (0, 0)),
                  row(d_ssm), row(d_ssm), row(d_sgu), row(d_sgu),
                  pl.BlockSpec(ws_pairs.shape, lambda i: (0, 0, 0)),
                  pl.BlockSpec(bias_s.shape, lambda i: (0, 0)),
                  row(d_sgu)],
        out_specs=pl.BlockSpec((tm, d_ssm + d_sgu), lambda i: (i, 0)),
        out_shape=jax.ShapeDtypeStruct((tp + ts, d_ssm + d_sgu), BF16),
        scratch_shapes=[pltpu.VMEM((d_ssm // LANES, tm, LANES), F32)],
        compiler_params=_cparams(),
        name="mix",
    )(y_p, y_s, zuv_p, zuv_s, wglu, bglu, gssm, lng, lnb, ws_pairs, bias_s, gsgu)


def _sorted_rows(tile):
    return TOP_K * tile + MXU_DIM


def _route_kernel(xp_ref, xs_ref, mix_ref, wout_ref, gffn_ref, whi_ref, wcat_ref, br_ref, tri_ref,
                  ltri_ref, x1_ref, xsort_ref, meta_ref, c8_ref, *, n_prompt_tiles, n_tiles, n_exp):
    i = pl.program_id(0)
    tm = mix_ref.shape[0]
    p_rows = xsort_ref.shape[0]

    def body(x_ref):
        x1 = x_ref[...] + jnp.dot(mix_ref[...], wout_ref[...], preferred_element_type=F32)
        x1_ref[...] = x1
        h2 = _rms(x1, gffn_ref[...])
        hi = h2.astype(BF16)
        lo = (h2 - hi.astype(F32)).astype(BF16)
        hw = jnp.dot(hi, wcat_ref[...], preferred_element_type=F32)
        lt = (hw[:, :LANES] + (jnp.dot(lo, whi_ref[...], preferred_element_type=F32) + hw[:, LANES:])
              + br_ref[...])
        logits = lt.T[:n_exp]
        eio = lax.broadcasted_iota(I32, (n_exp, tm), 0)
        vals, idxs = [], []
        l = logits
        for _ in range(TOP_K):
            m = jnp.max(l, axis=0, keepdims=True)
            idx = jnp.min(jnp.where(l == m, eio, n_exp), axis=0, keepdims=True)
            vals.append(m)
            idxs.append(idx)
            l = jnp.where(eio == idx, -jnp.inf, l)
        ex = [jnp.exp(v - vals[0]) for v in vals]
        den = ex[0] + ex[1] + ex[2] + ex[3]
        ws = [e / den for e in ex]
        hot = [eio == idx for idx in idxs]
        cnt = sum(h.astype(F32) for h in hot)
        prefix = jnp.dot(cnt.astype(BF16), tri_ref[...], preferred_element_type=F32)
        c = jnp.sum(cnt, axis=1, keepdims=True)
        c8 = jnp.floor((c + (SUBLANES - 1)) * (1.0 / SUBLANES)) * SUBLANES
        c8b = jnp.broadcast_to(c8, (n_exp, LANES))
        c8_ref[...] = c8b
        run0 = jnp.dot(ltri_ref[...], c8b.astype(BF16), preferred_element_type=F32)[:, 0:1]
        base = run0 + prefix
        pos = [jnp.sum(jnp.where(h, base, 0.0), axis=0, keepdims=True) for h in hot]
        meta = jnp.concatenate(ws + pos + [jnp.zeros((LANES - 2 * TOP_K, tm), F32)], axis=0)
        meta_ref[...] = meta.T
        rio = lax.broadcasted_iota(I32, (SORT_CHUNK, tm), 0).astype(F32).astype(BF16)
        one = jnp.ones((SORT_CHUNK, tm), BF16)
        zero = jnp.zeros((SORT_CHUNK, tm), BF16)
        for r in range(p_rows // SORT_CHUNK):
            lo_r = float(r * SORT_CHUNK)
            rel = [jnp.where((p >= lo_r) & (p < lo_r + SORT_CHUNK), p - lo_r, -1.0).astype(BF16) for p in pos]
            sel = (rio == rel[0]) | (rio == rel[1]) | (rio == rel[2]) | (rio == rel[3])
            srt = jnp.dot(jnp.where(sel, one, zero), hi, preferred_element_type=F32)
            xsort_ref[r * SORT_CHUNK:(r + 1) * SORT_CHUNK, :] = _pack_bf16_pairs(srt)

    @pl.when(i < n_prompt_tiles)
    def _():
        body(xp_ref)

    @pl.when((i >= n_prompt_tiles) & (i < n_tiles))
    def _():
        body(xs_ref)

    @pl.when(i == n_tiles)
    def _():
        xsort_ref[...] = jnp.zeros_like(xsort_ref)


def _dual_specs(tile, width, n_prompt_tiles, n_tiles):
    last_p = n_prompt_tiles - 1
    last_s = n_tiles - n_prompt_tiles - 1
    sp = pl.BlockSpec((tile, width), lambda i: (jnp.minimum(i, last_p), 0))
    ss = pl.BlockSpec((tile, width), lambda i: (jnp.clip(i - n_prompt_tiles, 0, last_s), 0))
    return sp, ss


def _route(xp, xs, mixed, w_out, g_ffn, wr_hi, wr_cat, br_pad, n_exp):
    tp, d = xp.shape
    t = mixed.shape[0]
    tm = TOKEN_TILE
    npt = tp // tm
    nt = t // tm
    p_rows = _sorted_rows(tm)
    iota = lambda n, ax: lax.broadcasted_iota(I32, (n, n), ax)
    tri = (iota(tm, 0) < iota(tm, 1)).astype(BF16)
    ltri = (iota(n_exp, 1) < iota(n_exp, 0)).astype(BF16)
    sp, ss = _dual_specs(tm, d, npt, nt)
    const = lambda shape: pl.BlockSpec(shape, lambda i: (0,) * len(shape))
    tile = lambda i: jnp.minimum(i, nt - 1)
    return pl.pallas_call(
        functools.partial(_route_kernel, n_prompt_tiles=npt, n_tiles=nt, n_exp=n_exp),
        grid=(nt + 1,),
        in_specs=[sp, ss,
                  pl.BlockSpec((tm, mixed.shape[1]), lambda i: (tile(i), 0)),
                  const(w_out.shape), const((1, d)), const(wr_hi.shape), const(wr_cat.shape),
                  const((1, LANES)), const((tm, tm)), const((n_exp, n_exp))],
        out_specs=[pl.BlockSpec((tm, d), lambda i: (tile(i), 0)),
                   pl.BlockSpec((p_rows, d // 2), lambda i: (i, 0)),
                   pl.BlockSpec((tm, LANES), lambda i: (tile(i), 0)),
                   pl.BlockSpec((None, n_exp, LANES), lambda i: (tile(i), 0, 0))],
        out_shape=[jax.ShapeDtypeStruct((t, d), F32),
                   jax.ShapeDtypeStruct(((nt + 1) * p_rows, d // 2), U32),
                   jax.ShapeDtypeStruct((t, LANES), F32),
                   jax.ShapeDtypeStruct((nt, n_exp, LANES), F32)],
        compiler_params=_cparams(),
        name="route",
    )(xp, xs, mixed, w_out, g_ffn, wr_hi, wr_cat, br_pad, tri, ltri)


def _slot_tables(c8, p_rows, n_blocks):
    nt, n_exp = c8.shape
    run0 = jnp.cumsum(c8, axis=1) - c8
    seg_len = c8.T
    cum = jnp.cumsum(seg_len, axis=1)
    tot = cum[:, -1]
    padded = (tot + MOE_TILE - 1) // MOE_TILE * MOE_TILE
    ends = jnp.cumsum(padded)
    starts = ends - padded
    n_used = (ends[-1] // MOE_TILE).astype(I32)
    blk = jnp.arange(n_blocks, dtype=I32)
    be = jnp.minimum(jnp.sum(ends[None, :] <= (blk * MOE_TILE)[:, None], axis=1), n_exp - 1).astype(I32)
    block_e = jnp.where(blk < n_used, be, be[jnp.maximum(n_used - 1, 0)])
    onehot = (be[:, None] == jnp.arange(n_exp, dtype=I32)[None, :]).astype(F32)
    pick = lambda tbl: jnp.dot(onehot, tbl.astype(F32), precision=lax.Precision.HIGHEST)
    run_end = pick(cum)
    run_beg = run_end - pick(seg_len)
    shift = pick((jnp.arange(nt, dtype=I32) * p_rows)[None, :] + run0.T) - run_beg
    grow = (blk[:, None] * MOE_TILE + jnp.arange(MOE_GROUPS, dtype=I32)[None, :] * SUBLANES).astype(F32)
    rel = grow - pick(starts[:, None])
    inside = (run_beg[:, None, :] <= rel[:, :, None]) & (rel[:, :, None] < run_end[:, None, :])
    src_row = (rel + jnp.sum(jnp.where(inside, shift[:, None, :], 0.0), axis=2)).astype(I32)
    valid = jnp.any(inside, axis=2) & (blk < n_used)[:, None]
    scratch0 = nt * p_rows // SUBLANES
    zero_group = scratch0 + 2 * MOE_GROUPS
    g_in = jnp.where(valid, src_row // SUBLANES, zero_group).astype(I32)
    g_scr = scratch0 + (blk[:, None] % 2) * MOE_GROUPS + jnp.arange(MOE_GROUPS, dtype=I32)[None, :]
    g_out = jnp.where(valid, src_row // SUBLANES, g_scr).astype(I32)
    return block_e, n_used, g_in, g_out


def _moe_kernel(be_ref, nu_ref, gin_ref, gnext_ref, gout_ref, xs_hbm, wgu_ref, bgu_ref, wd_ref, bd_ref,
                eo_hbm, xbuf, obuf, wgu_s, wd_s, isem, osem):
    j = pl.program_id(0)
    nu = nu_ref[0]
    f = wd_ref.shape[0]

    @pl.when((j < nu) & ((j == 0) | (be_ref[j] != be_ref[jnp.maximum(j - 1, 0)])))
    def _():
        for c in range(0, wgu_ref.shape[0], CAST_ROWS):
            wgu_s[c:c + CAST_ROWS, :] = wgu_ref[c:c + CAST_ROWS, :].astype(BF16)
        for c in range(0, wd_ref.shape[0], CAST_ROWS):
            wd_s[c:c + CAST_ROWS, :] = wd_ref[c:c + CAST_ROWS, :].astype(BF16)

    def in_copy(tbl_ref, r, sl):
        return pltpu.make_async_copy(xs_hbm.at[tbl_ref[0, r]], xbuf.at[sl, r], isem.at[sl])

    def out_copy(r, sl):
        return pltpu.make_async_copy(obuf.at[sl, r], eo_hbm.at[gout_ref[0, r]], osem.at[sl])

    def wait_in(sl):
        pltpu.make_async_copy(xs_hbm.at[pl.ds(0, MOE_GROUPS)], xbuf.at[sl], isem.at[sl]).wait()

    def wait_out(sl):
        pltpu.make_async_copy(obuf.at[sl], eo_hbm.at[pl.ds(0, MOE_GROUPS)], osem.at[sl]).wait()

    @pl.when(j == 0)
    def _():
        for r in range(MOE_GROUPS):
            in_copy(gin_ref, r, 0).start()

    def step(slot):
        other = 1 - slot

        @pl.when(j + 1 < nu)
        def _():
            for r in range(MOE_GROUPS):
                in_copy(gnext_ref, r, other).start()

        wait_in(slot)

        @pl.when(j >= 2)
        def _():
            wait_out(slot)

        x = _unpack_bf16_pairs(xbuf[slot].reshape(MOE_TILE, xbuf.shape[-1]))
        gate = jnp.dot(x, wgu_s[:, :f], preferred_element_type=F32) + bgu_ref[:, :f]
        gate = jnp.minimum(gate, SWIGLU_LIMIT)
        sg = gate * jax.nn.sigmoid(SWIGLU_ALPHA * gate)
        up = jnp.dot(x, wgu_s[:, f:], preferred_element_type=F32) + bgu_ref[:, f:]
        act = sg * (jnp.clip(up, -SWIGLU_LIMIT, SWIGLU_LIMIT) + 1.0)
        out = jnp.dot(act.astype(BF16), wd_s[...], preferred_element_type=F32) + bd_ref[...]
        obuf[slot] = _pack_bf16_pairs(out.astype(BF16).astype(F32)).reshape(obuf.shape[1:])
        for r in range(MOE_GROUPS):
            out_copy(r, slot).start()

        @pl.when(j == nu - 1)
        def _():
            wait_out(slot)

            @pl.when(j >= 1)
            def _():
                wait_out(other)

    for parity in range(2):
        pl.when((j < nu) & (lax.rem(j, 2) == parity))(functools.partial(step, parity))


def _moe(block_e, n_used, g_in, g_out, xsorted, wgu, bgu, wd, bd):
    n_blocks = block_e.shape[0]
    dh = xsorted.shape[1]
    d = 2 * dh
    f2 = wgu.shape[2]
    f = wd.shape[1]
    g_in3 = g_in.reshape(n_blocks, 1, MOE_GROUPS)
    g_next3 = jnp.concatenate([g_in3[1:], g_in3[:1]], axis=0)
    g_out3 = g_out.reshape(n_blocks, 1, MOE_GROUPS)
    xs3 = xsorted.reshape(xsorted.shape[0] // SUBLANES, SUBLANES, dh)
    tbl = pl.BlockSpec((None, 1, MOE_GROUPS), lambda j, be, nu: (j, 0, 0), memory_space=pltpu.SMEM)
    return pl.pallas_call(
        _moe_kernel,
        grid_spec=pltpu.PrefetchScalarGridSpec(
            num_scalar_prefetch=2, grid=(n_blocks,),
            in_specs=[tbl, tbl, tbl,
                      pl.BlockSpec(memory_space=pl.ANY),
                      pl.BlockSpec((None, d, f2), lambda j, be, nu: (be[j], 0, 0)),
                      pl.BlockSpec((None, 1, f2), lambda j, be, nu: (be[j], 0, 0)),
                      pl.BlockSpec((None, f, d), lambda j, be, nu: (be[j], 0, 0)),
                      pl.BlockSpec((None, 1, d), lambda j, be, nu: (be[j], 0, 0))],
            out_specs=pl.BlockSpec(memory_space=pl.ANY),
            scratch_shapes=[pltpu.VMEM((2, MOE_GROUPS, SUBLANES, dh), U32),
                            pltpu.VMEM((2, MOE_GROUPS, SUBLANES, dh), U32),
                            pltpu.VMEM((d, f2), BF16), pltpu.VMEM((f, d), BF16),
                            pltpu.SemaphoreType.DMA((2,)), pltpu.SemaphoreType.DMA((2,))]),
        out_shape=jax.ShapeDtypeStruct(xs3.shape, U32),
        input_output_aliases={5: 0},
        compiler_params=_cparams(),
        name="moe",
    )(block_e, n_used, g_in3, g_next3, g_out3, xs3, wgu, bgu, wd, bd).reshape(xsorted.shape)


def _combine_kernel(meta_ref, x1_ref, g_ref, eo_ref, y_ref):
    tm = x1_ref.shape[0]
    p_rows = eo_ref.shape[0]
    lane = lax.broadcasted_iota(I32, (tm, SORT_CHUNK), 1).astype(F32).astype(BF16)
    ws = [meta_ref[:, k:k + 1].astype(BF16) for k in range(TOP_K)]
    pos = [meta_ref[:, TOP_K + k:TOP_K + k + 1] for k in range(TOP_K)]
    acc = x1_ref[...]
    for r in range(p_rows // SORT_CHUNK):
        buf = _unpack_bf16_pairs(eo_ref[r * SORT_CHUNK:(r + 1) * SORT_CHUNK, :])
        lo_r = float(r * SORT_CHUNK)
        wm = jnp.zeros((tm, SORT_CHUNK), BF16)
        for k in range(TOP_K):
            rel = jnp.where((pos[k] >= lo_r) & (pos[k] < lo_r + SORT_CHUNK), pos[k] - lo_r, -1.0).astype(BF16)
            wm = jnp.where(lane == rel, ws[k], wm)
        acc = acc + jnp.dot(wm, buf, preferred_element_type=F32)
    y_ref[...] = _rms(acc, g_ref[...])


def _combine(meta, x1, g_final, eo, tile_off, n_tokens):
    d = x1.shape[1]
    tm = TOKEN_TILE
    p_rows = _sorted_rows(tm)
    return pl.pallas_call(
        _combine_kernel,
        grid=(n_tokens // tm,),
        in_specs=[pl.BlockSpec((tm, LANES), lambda i: (i + tile_off, 0)),
                  pl.BlockSpec((tm, d), lambda i: (i + tile_off, 0)),
                  pl.BlockSpec((1, d), lambda i: (0, 0)),
                  pl.BlockSpec((p_rows, d // 2), lambda i: (i + tile_off, 0))],
        out_specs=pl.BlockSpec((tm, d), lambda i: (i, 0)),
        out_shape=jax.ShapeDtypeStruct((n_tokens, d), F32),
        compiler_params=_cparams(),
        name="combine",
    )(meta, x1, g_final, eo)


def kernel(x_prompt, x_sample, g_mix, w_in, ssm_lam_re, ssm_lam_im, ssm_log_dt, ssm_b_re, ssm_b_im,
           ssm_c_re, ssm_c_im, ssm_d, w_glu, b_glu, sgu_ln_g, sgu_ln_b, sgu_w_s, sgu_b_s,
           g_out_ssm, g_out_sgu, w_out, g_ffn, w_router, b_router, w_gate_up, b_gate_up,
           w_down, b_down, g_final):
    assert g_mix.shape[0] == 1, "single-layer trunk"
    bp, lp, d = x_prompt.shape
    bs, ls, _ = x_sample.shape
    tp, ts = bp * lp, bs * ls
    t = tp + ts
    d_ssm = ssm_d.shape[1]
    d_sgu = sgu_ln_g.shape[1]
    n_exp = w_router.shape[2]
    assert all(n % tile == 0 for n in (tp, ts) for tile in (SEQ_TILE, TOKEN_TILE))
    assert lp % CHUNK == 0 and ls % CHUNK == 0 and SEQ_TILE % CHUNK == 0 and CHUNK % SSM_CHUNK == 0
    assert SSM_CHUNK * SSM_GROUP == MXU_DIM and 2 * ssm_lam_re.shape[-1] == LANES
    assert d_sgu // SGU_HEADS == LANES // 2 and n_exp <= LANES
    assert n_exp * (SUBLANES - 1) <= MXU_DIM and _sorted_rows(TOKEN_TILE) % SORT_CHUNK == 0
    assert (2 * MOE_GROUPS + 1) * SUBLANES <= _sorted_rows(TOKEN_TILE)

    xp = x_prompt.reshape(tp, d)
    xs = x_sample.reshape(ts, d)
    row = lambda a: a.reshape(1, -1).astype(F32)

    w_in_b = w_in[0].astype(BF16)
    u_p, zuv_p = _inproj(xp, row(g_mix[0]), w_in_b, d_ssm)
    u_s, zuv_s = _inproj(xs, row(g_mix[0]), w_in_b, d_ssm)

    mats = _ssm_matrices(ssm_lam_re[0], ssm_lam_im[0], ssm_log_dt[0], ssm_b_re[0], ssm_b_im[0],
                         ssm_c_re[0], ssm_c_im[0], ssm_d[0])
    y_p = _ssm_trunk(u_p, bp, mats)
    y_s = _ssm_trunk(u_s, bs, mats)

    ws = sgu_w_s[0]
    ws_pairs = jnp.concatenate([ws[0::2], ws[1::2]], axis=2).astype(BF16)
    bias_s = jnp.repeat(sgu_b_s[0].T, d_sgu // SGU_HEADS, axis=1).astype(F32)
    mixed = _mix(y_p, y_s, zuv_p, zuv_s, w_glu[0].astype(BF16), row(b_glu[0]), row(g_out_ssm[0]),
                 row(sgu_ln_g[0]), row(sgu_ln_b[0]), ws_pairs, bias_s, row(g_out_sgu[0]))

    wr_pad = jnp.pad(w_router[0].astype(F32), ((0, 0), (0, LANES - n_exp)))
    wr_hi = wr_pad.astype(BF16)
    wr_cat = jnp.concatenate([wr_hi, (wr_pad - wr_hi.astype(F32)).astype(BF16)], axis=1)
    br_pad = jnp.pad(b_router[0].astype(F32), (0, LANES - n_exp)).reshape(1, LANES)
    x1, xsorted, meta, c8 = _route(xp, xs, mixed, w_out[0].astype(BF16), row(g_ffn[0]),
                                   wr_hi, wr_cat, br_pad, n_exp)

    nt = t // TOKEN_TILE
    p_rows = _sorted_rows(TOKEN_TILE)
    max_rows = t * TOP_K + nt * n_exp * (SUBLANES - 1) + n_exp * (MOE_TILE - 1)
    n_blocks = -(-max_rows // MOE_TILE)
    block_e, n_used, g_in, g_out = _slot_tables(c8[:, :, 0].astype(I32), p_rows, n_blocks)
    eo = _moe(block_e, n_used.reshape(1), g_in, g_out, xsorted, w_gate_up[0],
              b_gate_up[0][:, None, :].astype(F32), w_down[0], b_down[0][:, None, :].astype(F32))

    gf = row(g_final)
    y_prompt = _combine(meta, x1, gf, eo, 0, tp)
    y_sample = _combine(meta, x1, gf, eo, tp // TOKEN_TILE, ts)
    return y_prompt.reshape(bp, lp, d), y_sample.reshape(bs, ls, d)
```

```python
import functools
import math

import jax
import jax.numpy as jnp
from jax import lax
from jax.experimental import pallas as pl
from jax.experimental.pallas import tpu as pltpu

F32 = jnp.float32
BF16 = jnp.bfloat16
I32 = jnp.int32
U32 = jnp.uint32

SSM_GROUP = 16
SGU_HEADS = 8
CHUNK = 128
TOP_K = 4
SWIGLU_LIMIT = 7.0
SWIGLU_ALPHA = 1.702
RMS_EPS = 1e-6
LN_EPS = 1e-5

LANES = 128
SUBLANES = 8
MXU_DIM = 256
SSM_CHUNK = MXU_DIM // SSM_GROUP
PAIR_W = 2 * SSM_CHUNK * SSM_GROUP
ATOM = 2 * SSM_GROUP
ATOMS = LANES // ATOM

TOKEN_TILE = 512
SEQ_TILE = 1024
MOE_TILE = 512
MOE_GROUPS = MOE_TILE // SUBLANES
SORT_CHUNK = 256
CAST_ROWS = 128
SSM_ROW_TILE = 2048
SCAN_SEQS = 4
VMEM_LIMIT = 56 * 1024 * 1024


def _cparams(n_axes=1, vmem=None):
    return pltpu.CompilerParams(
        dimension_semantics=("arbitrary",) * n_axes,
        vmem_limit_bytes=vmem if vmem is not None else VMEM_LIMIT,
    )


def _rms(x, g):
    return x * lax.rsqrt(jnp.mean(x * x, axis=-1, keepdims=True) + RMS_EPS) * g


def _gelu(x):
    return x * (lax.erf(x * (1.0 / math.sqrt(2.0))) + 1.0) * 0.5


def _atom_masks(rows):
    lane = lax.broadcasted_iota(I32, (rows, LANES), 1)
    return [(lane >= a * ATOM) & (lane < (a + 1) * ATOM) for a in range(ATOMS)]


def _atom_transpose(src, masks):
    dst = []
    for b in range(ATOMS):
        acc = None
        for a in range(ATOMS):
            r = (a - b) % ATOMS
            piece = src[a] if r == 0 else pltpu.roll(src[a], ATOM * r, axis=1)
            acc = piece if acc is None else jnp.where(masks[a], piece, acc)
        dst.append(acc)
    return dst


def _pack_bf16_pairs(x):
    w = x.shape[1] // 2
    lo = lax.bitcast_convert_type(x[:, :w], U32) >> 16
    hi = lax.bitcast_convert_type(x[:, w:], U32) & jnp.uint32(0xFFFF0000)
    return hi | lo


def _unpack_bf16_pairs(u):
    lo = lax.bitcast_convert_type(u << 16, F32)
    hi = lax.bitcast_convert_type(u & jnp.uint32(0xFFFF0000), F32)
    return jnp.concatenate([lo.astype(BF16), hi.astype(BF16)], axis=1)


def _inproj_kernel(x_ref, g_ref, w_ref, u_ref, zuv_ref, za_scr, *, d_ssm):
    tm = x_ref.shape[0]
    nc = tm // SSM_CHUNK
    h = _rms(x_ref[...], g_ref[...])
    z = jnp.dot(h.astype(BF16), w_ref[...], preferred_element_type=F32)
    zuv_ref[...] = _gelu(z[:, d_ssm:]).astype(BF16)
    n_blk = d_ssm // LANES
    for b in range(n_blk):
        za_scr[b] = z[:, b * LANES:(b + 1) * LANES]
    masks = _atom_masks(nc)
    n_quads = SSM_CHUNK // ATOMS
    for b in range(n_blk):
        for v in range(n_quads):
            src = [za_scr[b, pl.ds(ATOMS * v + jj, nc, stride=SSM_CHUNK), :] for jj in range(ATOMS)]
            dst = _atom_transpose(src, masks)
            for pi in range(ATOMS):
                c0 = (ATOMS * b + pi) * PAIR_W + v * LANES
                u_ref[:, c0:c0 + LANES] = dst[pi].astype(BF16)


def _inproj(x, g_mix, w_in, d_ssm):
    t, d = x.shape
    tm = SEQ_TILE
    d_in = w_in.shape[1]
    nc = tm // SSM_CHUNK
    uw = d_ssm * SSM_CHUNK
    return pl.pallas_call(
        functools.partial(_inproj_kernel, d_ssm=d_ssm),
        grid=(t // tm,),
        in_specs=[pl.BlockSpec((tm, d), lambda i: (i, 0)),
                  pl.BlockSpec((1, d), lambda i: (0, 0)),
                  pl.BlockSpec((d, d_in), lambda i: (0, 0))],
        out_specs=[pl.BlockSpec((nc, uw), lambda i: (i, 0)),
                   pl.BlockSpec((tm, d_in - d_ssm), lambda i: (i, 0))],
        out_shape=[jax.ShapeDtypeStruct((t // SSM_CHUNK, uw), BF16),
                   jax.ShapeDtypeStruct((t, d_in - d_ssm), BF16)],
        scratch_shapes=[pltpu.VMEM((d_ssm // LANES, tm, LANES), F32)],
        compiler_params=_cparams(),
        name="inproj",
    )(x, g_mix, w_in)


def _ssm_matrices(lam_re, lam_im, log_dt, b_re, b_im, c_re, c_im, d_skip):
    hp = lax.Precision.HIGHEST
    _, g, n = lam_re.shape
    p = b_re.shape[-1]
    lc = SSM_CHUNK
    np_ = g // 2
    dt = jnp.exp(log_dt)[..., None]
    mag = jnp.exp(lam_re * dt)
    ar = mag * jnp.cos(lam_im * dt)
    ai = mag * jnp.sin(lam_im * dt)
    den = lam_re * lam_re + lam_im * lam_im
    nr = ar - 1.0
    fr = (nr * lam_re + ai * lam_im) / den
    fi = (ai * lam_re - nr * lam_im) / den
    bbr = fr[..., None] * b_re - fi[..., None] * b_im
    bbi = fr[..., None] * b_im + fi[..., None] * b_re

    prs, pis = [jnp.ones_like(ar)], [jnp.zeros_like(ai)]
    for _ in range(lc):
        pr, pi = prs[-1], pis[-1]
        prs.append(pr * ar - pi * ai)
        pis.append(pr * ai + pi * ar)
    pw_r, pw_i = jnp.stack(prs), jnp.stack(pis)

    lane_k = jnp.arange(lc * p)
    til_k = (lane_k[None, :] % p == jnp.arange(p)[:, None]).astype(F32)
    rep_k = (lane_k[None, :] // p == jnp.arange(lc)[:, None]).astype(F32)
    ex = lambda x, e: jnp.einsum('dgnk,kx->dgnx', x, e, precision=hp)
    ct_r = ex(jnp.transpose(c_re, (0, 1, 3, 2)), til_k)
    ct_i = ex(jnp.transpose(c_im, (0, 1, 3, 2)), til_k)
    at_r = ex(jnp.transpose(pw_r[:lc], (1, 2, 3, 0)), rep_k)
    at_i = ex(jnp.transpose(pw_i[:lc], (1, 2, 3, 0)), rep_k)
    kk = (jnp.einsum('dgnq,dgnx->dgqx', bbr, ct_r * at_r - ct_i * at_i, precision=hp)
          - jnp.einsum('dgnq,dgnx->dgqx', bbi, ct_r * at_i + ct_i * at_r, precision=hp))
    kk = kk.reshape(2, g, p, lc, p)
    kf, kb = kk[0], kk[1]
    eye = jnp.eye(2, dtype=F32)
    a2 = p * 2

    k_all = jnp.concatenate([kb[:, :, :0:-1], (kf[:, :, :1] + kb[:, :, :1]), kf[:, :, 1:]], axis=2)
    k_all = k_all.reshape(np_, 2, p, 2 * lc - 1, p)
    kflat = jnp.einsum('nsqup,st->nsqutp', k_all, eye).reshape(np_, a2, (2 * lc - 1) * a2)
    kflat = jnp.pad(kflat, ((0, 0), (0, 0), (0, a2)))

    def atoms_q(x):
        x = jnp.transpose(x, (0, 1, 3, 2)).reshape(2, np_, 2, p, n)
        return jnp.einsum('dnsqm,st->dnsqtm', x, eye).reshape(2, np_, a2, 2 * n)
    bq_r, bq_i = atoms_q(bbr), atoms_q(bbi)
    bq = jnp.stack([bq_r[0], bq_i[0], bq_r[1], bq_i[1]], axis=1)
    ef = lc - 1 - jnp.arange(lc)
    eb = jnp.arange(lc)
    rows_q = lambda w, e, d: jnp.transpose(w[e, d].reshape(lc, np_, 2 * n), (1, 0, 2))
    wq = jnp.stack([rows_q(pw_r, ef, 0), rows_q(pw_i, ef, 0),
                    rows_q(pw_r, eb, 1), rows_q(pw_i, eb, 1)], axis=1)

    def atoms_p(x):
        x = jnp.transpose(x, (0, 1, 3, 2)).reshape(2, np_, 2, n, p)
        return jnp.einsum('dnsmp,st->dnsmtp', x, eye).reshape(2, np_, 2 * n, a2)
    cp_r, cp_i = atoms_p(c_re), atoms_p(c_im)
    cp = jnp.stack([cp_r[0], cp_i[0], cp_r[1], cp_i[1]], axis=1)
    pf_e = jnp.arange(lc) + 1
    pb_e = lc - jnp.arange(lc)
    cols_p = lambda w, e, d: jnp.transpose(w[e, d].reshape(lc, np_, 2 * n), (1, 2, 0))
    wp = jnp.stack([cols_p(pw_r, pf_e, 0), cols_p(pw_i, pf_e, 0),
                    cols_p(pw_r, pb_e, 1), cols_p(pw_i, pb_e, 1)], axis=1)
    lane = jnp.arange(lc * a2)
    til = (lane[None, :] % a2 == jnp.arange(a2)[:, None]).astype(BF16)
    rep = (lane[None, :] // a2 == jnp.arange(lc)[:, None]).astype(BF16)

    al = jnp.stack([pw_r[lc, 0], pw_i[lc, 0], pw_r[lc, 1], pw_i[lc, 1]])
    alpha = jnp.transpose(al.reshape(4, np_, 2 * n), (1, 0, 2))
    skip = jnp.broadcast_to(d_skip.reshape(np_, 1, a2), (np_, lc, a2)).reshape(np_, 1, lc * a2)
    return kflat, bq, wq, cp, wp, til, rep, alpha, skip.astype(F32)


def _ssm_v_kernel(u_ref, bq_ref, wq_ref, v_ref, q_scr):
    a2 = bq_ref.shape[1]
    w = bq_ref.shape[2]

    @pl.when(pl.program_id(1) == 0)
    def _():
        for d in range(2):
            br, bi = bq_ref[2 * d], bq_ref[2 * d + 1]
            for j in range(SSM_CHUNK):
                wr = wq_ref[2 * d, j:j + 1, :]
                wi = wq_ref[2 * d + 1, j:j + 1, :]
                q_scr[j * a2:(j + 1) * a2, (2 * d) * w:(2 * d + 1) * w] = (wr * br - wi * bi).astype(BF16)
                q_scr[j * a2:(j + 1) * a2, (2 * d + 1) * w:(2 * d + 2) * w] = (wr * bi + wi * br).astype(BF16)

    v_ref[...] = jnp.dot(u_ref[...], q_scr[...], preferred_element_type=F32)


def _ssm_scan_kernel(v_ref, a_ref, s_ref, *, n_chunks, bsz):
    w = LANES
    sub = SUBLANES
    n_groups = n_chunks // sub
    row = lax.broadcasted_iota(I32, (sub, w), 0)

    def cmul(ar, ai, xr, xi):
        return ar * xr - ai * xi, ar * xi + ai * xr

    def powers(ar, ai):
        p2 = cmul(ar, ai, ar, ai)
        p4 = cmul(*p2, *p2)
        p8 = cmul(*p4, *p4)
        p3 = cmul(*p2, ar, ai)
        p5 = cmul(*p4, ar, ai)
        p6 = cmul(*p4, *p2)
        p7 = cmul(*p4, *p3)
        seq = [(jnp.ones_like(ar), jnp.zeros_like(ai)), (ar, ai), p2, p3, p4, p5, p6, p7]
        tr = jnp.concatenate([s[0] for s in seq], axis=0)
        ti = jnp.concatenate([s[1] for s in seq], axis=0)
        return (ar, ai), p2, p4, p8, (tr, ti)

    def tile_scan(vr, vi, pw, reverse):
        a1, a2, a4, _, _ = pw

        def shift(x, s):
            if reverse:
                return jnp.where(row < sub - s, pltpu.roll(x, sub - s, axis=0), 0.0)
            return jnp.where(row >= s, pltpu.roll(x, s, axis=0), 0.0)

        xr, xi = shift(vr, 1), shift(vi, 1)
        for s, (ar, ai) in ((1, a1), (2, a2), (4, a4)):
            mr, mi = cmul(ar, ai, shift(xr, s), shift(xi, s))
            xr, xi = xr + mr, xi + mi
        e = 0 if reverse else sub - 1
        lr, li = cmul(a1[0], a1[1], xr[e:e + 1], xi[e:e + 1])
        return xr, xi, lr + vr[e:e + 1], li + vi[e:e + 1]

    pw_f = powers(a_ref[0:1, :], a_ref[1:2, :])
    pw_b = powers(a_ref[2:3, :], a_ref[3:4, :])
    tbr = jnp.concatenate([pw_b[4][0][sub - 1 - k:sub - k] for k in range(sub)], axis=0)
    tbi = jnp.concatenate([pw_b[4][1][sub - 1 - k:sub - k] for k in range(sub)], axis=0)

    def one_tile(row0, col0, pw, table, cr, ci, reverse):
        rows = pl.ds(pl.multiple_of(row0, sub), sub)
        xr, xi, lr, li = tile_scan(v_ref[rows, col0:col0 + w], v_ref[rows, col0 + w:col0 + 2 * w], pw, reverse)
        mr, mi = cmul(table[0], table[1], cr, ci)
        nr, ni = cmul(pw[3][0], pw[3][1], cr, ci)
        return xr + mr, xi + mi, nr + lr, ni + li

    pair = 2 * sub
    n_pairs = n_groups // 2

    def step(g, carry, b0):
        out = []
        for b in range(b0, b0 + len(carry) // 4):
            cfr, cfi, cbr, cbi = carry[4 * (b - b0):4 * (b - b0) + 4]
            rf = pl.multiple_of(b * n_chunks + g * pair, pair)
            rb = pl.multiple_of(b * n_chunks + (n_pairs - 1 - g) * pair, pair)
            ar, ai, cfr, cfi = one_tile(rf, 0, pw_f, pw_f[4], cfr, cfi, False)
            br, bi, cfr, cfi = one_tile(rf + sub, 0, pw_f, pw_f[4], cfr, cfi, False)
            s_ref[pl.ds(rf, pair), 0:w] = jnp.concatenate([ar, br], axis=0).astype(BF16)
            s_ref[pl.ds(rf, pair), w:2 * w] = jnp.concatenate([ai, bi], axis=0).astype(BF16)
            out += [cfr, cfi]
            br, bi, cbr, cbi = one_tile(rb + sub, 2 * w, pw_b, (tbr, tbi), cbr, cbi, True)
            ar, ai, cbr, cbi = one_tile(rb, 2 * w, pw_b, (tbr, tbi), cbr, cbi, True)
            s_ref[pl.ds(rb, pair), 2 * w:3 * w] = jnp.concatenate([ar, br], axis=0).astype(BF16)
            s_ref[pl.ds(rb, pair), 3 * w:4 * w] = jnp.concatenate([ai, bi], axis=0).astype(BF16)
            out += [cbr, cbi]
        return tuple(out)

    zero = jnp.zeros((1, w), F32)
    for b0 in range(0, bsz, SCAN_SEQS):
        nb = min(SCAN_SEQS, bsz - b0)
        lax.fori_loop(0, n_pairs, functools.partial(step, b0=b0), (zero,) * (4 * nb))


def _ssm_y_kernel(u_ref, s_ref, kflat_ref, cp_ref, wp_ref, til_ref, rep_ref, d_ref, y_ref, m_scr, p_scr):
    a2 = kflat_ref.shape[0]
    w = cp_ref.shape[1]

    def expand(x, e_ref):
        hi = x.astype(BF16)
        lo = (x - hi.astype(F32)).astype(BF16)
        return (jnp.dot(hi, e_ref[...], preferred_element_type=F32)
                + jnp.dot(lo, e_ref[...], preferred_element_type=F32))

    @pl.when(pl.program_id(1) == 0)
    def _():
        kflat = kflat_ref[...]
        for j in range(SSM_CHUNK):
            off = (SSM_CHUNK - 1 - j) * a2
            m_scr[j * a2:(j + 1) * a2, :] = kflat[:, off:off + PAIR_W].astype(BF16)
        for d in range(2):
            cr, ci = expand(cp_ref[2 * d], til_ref), expand(cp_ref[2 * d + 1], til_ref)
            wr, wi = expand(wp_ref[2 * d], rep_ref), expand(wp_ref[2 * d + 1], rep_ref)
            p_scr[(2 * d) * w:(2 * d + 1) * w, :] = (cr * wr - ci * wi).astype(BF16)
            p_scr[(2 * d + 1) * w:(2 * d + 2) * w, :] = (-(cr * wi + ci * wr)).astype(BF16)

    u = u_ref[...]
    y = jnp.dot(u, m_scr[...], preferred_element_type=F32)
    y = y + jnp.dot(s_ref[...], p_scr[...], preferred_element_type=F32)
    y_ref[...] = _gelu(y + d_ref[...] * u.astype(F32)).astype(BF16)


def _ssm_trunk(u, bsz, mats):
    kflat, bq, wq, cp, wp, til, rep, alpha, skip = mats
    rows = u.shape[0]
    nc = rows // bsz
    np_ = kflat.shape[0]
    pw = PAIR_W
    sw = 4 * bq.shape[3]
    tr = min(SSM_ROW_TILE, rows)
    per_pair = lambda a: pl.BlockSpec((None,) + a.shape[1:], lambda p, r: (p,) + (0,) * (a.ndim - 1))
    whole = lambda a: pl.BlockSpec(a.shape, lambda p, r: (0,) * a.ndim)
    v = pl.pallas_call(
        _ssm_v_kernel,
        grid=(np_, rows // tr),
        in_specs=[pl.BlockSpec((tr, pw), lambda p, r: (r, p)), per_pair(bq), per_pair(wq)],
        out_specs=pl.BlockSpec((tr, sw), lambda p, r: (r, p)),
        out_shape=jax.ShapeDtypeStruct((rows, np_ * sw), F32),
        scratch_shapes=[pltpu.VMEM((pw, sw), BF16)],
        compiler_params=_cparams(2),
        name="ssm_v",
    )(u, bq, wq)
    s = pl.pallas_call(
        functools.partial(_ssm_scan_kernel, n_chunks=nc, bsz=bsz),
        grid=(np_,),
        in_specs=[pl.BlockSpec((rows, sw), lambda p: (0, p)),
                  pl.BlockSpec((None, 4, LANES), lambda p: (p, 0, 0))],
        out_specs=pl.BlockSpec((rows, sw), lambda p: (0, p)),
        out_shape=jax.ShapeDtypeStruct((rows, np_ * sw), BF16),
        compiler_params=_cparams(1),
        name="ssm_scan",
    )(v, alpha)
    y = pl.pallas_call(
        _ssm_y_kernel,
        grid=(np_, rows // tr),
        in_specs=[pl.BlockSpec((tr, pw), lambda p, r: (r, p)),
                  pl.BlockSpec((tr, sw), lambda p, r: (r, p)),
                  per_pair(kflat), per_pair(cp), per_pair(wp), whole(til), whole(rep), per_pair(skip)],
        out_specs=pl.BlockSpec((tr, pw), lambda p, r: (r, p)),
        out_shape=jax.ShapeDtypeStruct((rows, np_ * pw), BF16),
        scratch_shapes=[pltpu.VMEM((pw, pw), BF16), pltpu.VMEM((sw, pw), BF16)],
        compiler_params=_cparams(2),
        name="ssm_y",
    )(u, s, kflat, cp, wp, til, rep, skip)
    return y


def _mixer(y_ref, zuv_ref, wglu_ref, bglu_ref, gssm_ref, lng_ref, lnb_ref, ws_ref, bs_ref, gsgu_ref, ya_scr):
    tm = zuv_ref.shape[0]
    n_blk = ya_scr.shape[0]
    nc = tm // SSM_CHUNK
    masks = _atom_masks(nc)
    for b in range(n_blk):
        for v in range(SSM_CHUNK // ATOMS):
            src = [y_ref[:, (ATOMS * b + pi) * PAIR_W + v * LANES:
                         (ATOMS * b + pi) * PAIR_W + (v + 1) * LANES].astype(F32)
                   for pi in range(ATOMS)]
            dst = _atom_transpose(src, masks)
            for jj in range(ATOMS):
                ya_scr[b, pl.ds(ATOMS * v + jj, nc, stride=SSM_CHUNK), :] = dst[jj]
    gl = jnp.concatenate([ya_scr[b] for b in range(n_blk)], axis=1)
    gate = jnp.dot(gl.astype(BF16), wglu_ref[...], preferred_element_type=F32) + bglu_ref[...]
    ra = _rms(gl * jax.nn.sigmoid(gate), gssm_ref[...]).astype(BF16)
    d_sgu = zuv_ref.shape[1] // 2
    u = zuv_ref[:, :d_sgu].astype(F32)
    gv = zuv_ref[:, d_sgu:].astype(F32)
    xc = gv - jnp.mean(gv, axis=-1, keepdims=True)
    v = xc * lax.rsqrt(jnp.mean(xc * xc, axis=-1, keepdims=True) + LN_EPS)
    v = (v * lng_ref[...] + lnb_ref[...]).astype(BF16)
    lo = lax.broadcasted_iota(I32, (CHUNK, LANES), 1) < (LANES // 2)
    zero = jnp.zeros((CHUNK, LANES), BF16)
    rows = []
    for c in range(tm // CHUNK):
        cols = []
        for j in range(d_sgu // LANES):
            vp = v[c * CHUNK:(c + 1) * CHUNK, j * LANES:(j + 1) * LANES]
            rhs = jnp.concatenate([jnp.where(lo, vp, zero), jnp.where(lo, zero, vp)], axis=0)
            cols.append(jnp.dot(ws_ref[j], rhs, preferred_element_type=F32))
        rows.append(jnp.concatenate(cols, axis=1) + bs_ref[...])
    s = jnp.concatenate(rows, axis=0)
    rb = _rms(u * s, gsgu_ref[...]).astype(BF16)
    return jnp.concatenate([ra, rb], axis=1)


def _sorted_rows(tile):
    return TOP_K * tile + MXU_DIM


def _route_kernel(xp_ref, xs_ref, yp_ref, ys_ref, zp_ref, zs_ref, wglu_ref, bglu_ref, gssm_ref, lng_ref,
                  lnb_ref, ws_ref, bs_ref, gsgu_ref, wout_ref, gffn_ref, whi_ref, wcat_ref, br_ref, tri_ref,
                  ltri_ref, x1_ref, xsort_ref, meta_ref, c8_ref, ya_scr, *, n_prompt_tiles, n_tiles, n_exp):
    i = pl.program_id(0)
    tm = x1_ref.shape[0]
    p_rows = xsort_ref.shape[0]

    def body(x_ref, y_ref, zuv_ref):
        mixed = _mixer(y_ref, zuv_ref, wglu_ref, bglu_ref, gssm_ref, lng_ref, lnb_ref, ws_ref, bs_ref,
                       gsgu_ref, ya_scr)
        x1 = x_ref[...] + jnp.dot(mixed, wout_ref[...], preferred_element_type=F32)
        x1_ref[...] = x1
        h2 = _rms(x1, gffn_ref[...])
        hi = h2.astype(BF16)
        lo = (h2 - hi.astype(F32)).astype(BF16)
        hw = jnp.dot(hi, wcat_ref[...], preferred_element_type=F32)
        lt = (hw[:, :LANES] + (jnp.dot(lo, whi_ref[...], preferred_element_type=F32) + hw[:, LANES:])
              + br_ref[...])
        logits = lt.T[:n_exp]
        eio = lax.broadcasted_iota(I32, (n_exp, tm), 0)
        vals, idxs = [], []
        l = logits
        for _ in range(TOP_K):
            m = jnp.max(l, axis=0, keepdims=True)
            idx = jnp.min(jnp.where(l == m, eio, n_exp), axis=0, keepdims=True)
            vals.append(m)
            idxs.append(idx)
            l = jnp.where(eio == idx, -jnp.inf, l)
        ex = [jnp.exp(v - vals[0]) for v in vals]
        den = ex[0] + ex[1] + ex[2] + ex[3]
        ws = [e / den for e in ex]
        hot = [eio == idx for idx in idxs]
        cnt = sum(h.astype(F32) for h in hot)
        prefix = jnp.dot(cnt.astype(BF16), tri_ref[...], preferred_element_type=F32)
        c = jnp.sum(cnt, axis=1, keepdims=True)
        c8 = jnp.floor((c + (SUBLANES - 1)) * (1.0 / SUBLANES)) * SUBLANES
        c8b = jnp.broadcast_to(c8, (n_exp, LANES))
        c8_ref[...] = c8b
        run0 = jnp.dot(ltri_ref[...], c8b.astype(BF16), preferred_element_type=F32)[:, 0:1]
        base = run0 + prefix
        pos = [jnp.sum(jnp.where(h, base, 0.0), axis=0, keepdims=True) for h in hot]
        meta = jnp.concatenate(ws + pos + [jnp.zeros((LANES - 2 * TOP_K, tm), F32)], axis=0)
        meta_ref[...] = meta.T
        rio = lax.broadcasted_iota(I32, (SORT_CHUNK, tm), 0).astype(F32).astype(BF16)
        one = jnp.ones((SORT_CHUNK, tm), BF16)
        zero = jnp.zeros((SORT_CHUNK, tm), BF16)
        for r in range(p_rows // SORT_CHUNK):
            lo_r = float(r * SORT_CHUNK)
            rel = [jnp.where((p >= lo_r) & (p < lo_r + SORT_CHUNK), p - lo_r, -1.0).astype(BF16) for p in pos]
            sel = (rio == rel[0]) | (rio == rel[1]) | (rio == rel[2]) | (rio == rel[3])
            srt = jnp.dot(jnp.where(sel, one, zero), hi, preferred_element_type=F32)
            xsort_ref[r * SORT_CHUNK:(r + 1) * SORT_CHUNK, :] = _pack_bf16_pairs(srt)

    @pl.when(i < n_prompt_tiles)
    def _():
        body(xp_ref, yp_ref, zp_ref)

    @pl.when((i >= n_prompt_tiles) & (i < n_tiles))
    def _():
        body(xs_ref, ys_ref, zs_ref)

    @pl.when(i == n_tiles)
    def _():
        xsort_ref[...] = jnp.zeros_like(xsort_ref)


def _dual_specs(tile, width, n_prompt_tiles, n_tiles):
    last_p = n_prompt_tiles - 1
    last_s = n_tiles - n_prompt_tiles - 1
    sp = pl.BlockSpec((tile, width), lambda i: (jnp.minimum(i, last_p), 0))
    ss = pl.BlockSpec((tile, width), lambda i: (jnp.clip(i - n_prompt_tiles, 0, last_s), 0))
    return sp, ss


def _route(xp, xs, y_p, y_s, zuv_p, zuv_s, mix_params, w_out, g_ffn, wr_hi, wr_cat, br_pad, n_exp):
    tp, d = xp.shape
    t = tp + xs.shape[0]
    tm = TOKEN_TILE
    npt = tp // tm
    nt = t // tm
    p_rows = _sorted_rows(tm)
    d_ssm = mix_params[0].shape[0]
    iota = lambda n, ax: lax.broadcasted_iota(I32, (n, n), ax)
    tri = (iota(tm, 0) < iota(tm, 1)).astype(BF16)
    ltri = (iota(n_exp, 1) < iota(n_exp, 0)).astype(BF16)
    const = lambda shape: pl.BlockSpec(shape, lambda i: (0,) * len(shape))
    tile = lambda i: jnp.minimum(i, nt - 1)
    dual = lambda rows, width: _dual_specs(rows, width, npt, nt)
    return pl.pallas_call(
        functools.partial(_route_kernel, n_prompt_tiles=npt, n_tiles=nt, n_exp=n_exp),
        grid=(nt + 1,),
        in_specs=[*dual(tm, d), *dual(tm // SSM_CHUNK, y_p.shape[1]), *dual(tm, zuv_p.shape[1]),
                  *[const(a.shape) for a in mix_params],
                  const(w_out.shape), const((1, d)), const(wr_hi.shape), const(wr_cat.shape),
                  const((1, LANES)), const((tm, tm)), const((n_exp, n_exp))],
        out_specs=[pl.BlockSpec((tm, d), lambda i: (tile(i), 0)),
                   pl.BlockSpec((p_rows, d // 2), lambda i: (i, 0)),
                   pl.BlockSpec((tm, LANES), lambda i: (tile(i), 0)),
                   pl.BlockSpec((None, n_exp, LANES), lambda i: (tile(i), 0, 0))],
        out_shape=[jax.ShapeDtypeStruct((t, d), F32),
                   jax.ShapeDtypeStruct(((nt + 1) * p_rows, d // 2), U32),
                   jax.ShapeDtypeStruct((t, LANES), F32),
                   jax.ShapeDtypeStruct((nt, n_exp, LANES), F32)],
        scratch_shapes=[pltpu.VMEM((d_ssm // LANES, tm, LANES), F32)],
        compiler_params=_cparams(),
        name="route",
    )(xp, xs, y_p, y_s, zuv_p, zuv_s, *mix_params, w_out, g_ffn, wr_hi, wr_cat, br_pad, tri, ltri)


def _slot_tables(c8, p_rows, n_blocks):
    nt, n_exp = c8.shape
    run0 = jnp.cumsum(c8, axis=1) - c8
    seg_len = c8.T
    cum = jnp.cumsum(seg_len, axis=1)
    tot = cum[:, -1]
    padded = (tot + MOE_TILE - 1) // MOE_TILE * MOE_TILE
    ends = jnp.cumsum(padded)
    starts = ends - padded
    n_used = (ends[-1] // MOE_TILE).astype(I32)
    blk = jnp.arange(n_blocks, dtype=I32)
    be = jnp.minimum(jnp.sum(ends[None, :] <= (blk * MOE_TILE)[:, None], axis=1), n_exp - 1).astype(I32)
    block_e = jnp.where(blk < n_used, be, be[jnp.maximum(n_used - 1, 0)])
    onehot = (be[:, None] == jnp.arange(n_exp, dtype=I32)[None, :]).astype(F32)
    pick = lambda tbl: jnp.dot(onehot, tbl.astype(F32), precision=lax.Precision.HIGHEST)
    run_end = pick(cum)
    run_beg = run_end - pick(seg_len)
    shift = pick((jnp.arange(nt, dtype=I32) * p_rows)[None, :] + run0.T) - run_beg
    grow = (blk[:, None] * MOE_TILE + jnp.arange(MOE_GROUPS, dtype=I32)[None, :] * SUBLANES).astype(F32)
    rel = grow - pick(starts[:, None])
    inside = (run_beg[:, None, :] <= rel[:, :, None]) & (rel[:, :, None] < run_end[:, None, :])
    src_row = (rel + jnp.sum(jnp.where(inside, shift[:, None, :], 0.0), axis=2)).astype(I32)
    valid = jnp.any(inside, axis=2) & (blk < n_used)[:, None]
    scratch0 = nt * p_rows // SUBLANES
    zero_group = scratch0 + 2 * MOE_GROUPS
    g_in = jnp.where(valid, src_row // SUBLANES, zero_group).astype(I32)
    g_scr = scratch0 + (blk[:, None] % 2) * MOE_GROUPS + jnp.arange(MOE_GROUPS, dtype=I32)[None, :]
    g_out = jnp.where(valid, src_row // SUBLANES, g_scr).astype(I32)
    return block_e, n_used, g_in, g_out


def _moe_kernel(be_ref, nu_ref, gin_ref, gnext_ref, gout_ref, xs_hbm, wgu_ref, bgu_ref, wd_ref, bd_ref,
                eo_hbm, xbuf, obuf, wgu_s, wd_s, isem, osem):
    j = pl.program_id(0)
    nu = nu_ref[0]
    f = wd_ref.shape[0]

    @pl.when((j < nu) & ((j == 0) | (be_ref[j] != be_ref[jnp.maximum(j - 1, 0)])))
    def _():
        for c in range(0, wgu_ref.shape[0], CAST_ROWS):
            wgu_s[c:c + CAST_ROWS, :] = wgu_ref[c:c + CAST_ROWS, :].astype(BF16)
        for c in range(0, wd_ref.shape[0], CAST_ROWS):
            wd_s[c:c + CAST_ROWS, :] = wd_ref[c:c + CAST_ROWS, :].astype(BF16)

    def in_copy(tbl_ref, r, sl):
        return pltpu.make_async_copy(xs_hbm.at[tbl_ref[0, r]], xbuf.at[sl, r], isem.at[sl])

    def out_copy(r, sl):
        return pltpu.make_async_copy(obuf.at[sl, r], eo_hbm.at[gout_ref[0, r]], osem.at[sl])

    def wait_in(sl):
        pltpu.make_async_copy(xs_hbm.at[pl.ds(0, MOE_GROUPS)], xbuf.at[sl], isem.at[sl]).wait()

    def wait_out(sl):
        pltpu.make_async_copy(obuf.at[sl], eo_hbm.at[pl.ds(0, MOE_GROUPS)], osem.at[sl]).wait()

    @pl.when(j == 0)
    def _():
        for r in range(MOE_GROUPS):
            in_copy(gin_ref, r, 0).start()

    def step(slot):
        other = 1 - slot

        @pl.when(j + 1 < nu)
        def _():
            for r in range(MOE_GROUPS):
                in_copy(gnext_ref, r, other).start()

        wait_in(slot)

        @pl.when(j >= 2)
        def _():
            wait_out(slot)

        x = _unpack_bf16_pairs(xbuf[slot].reshape(MOE_TILE, xbuf.shape[-1]))
        gate = jnp.dot(x, wgu_s[:, :f], preferred_element_type=F32) + bgu_ref[:, :f]
        gate = jnp.minimum(gate, SWIGLU_LIMIT)
        sg = gate * jax.nn.sigmoid(SWIGLU_ALPHA * gate)
        up = jnp.dot(x, wgu_s[:, f:], preferred_element_type=F32) + bgu_ref[:, f:]
        act = sg * (jnp.clip(up, -SWIGLU_LIMIT, SWIGLU_LIMIT) + 1.0)
        out = jnp.dot(act.astype(BF16), wd_s[...], preferred_element_type=F32) + bd_ref[...]
        obuf[slot] = _pack_bf16_pairs(out.astype(BF16).astype(F32)).reshape(obuf.shape[1:])
        for r in range(MOE_GROUPS):
            out_copy(r, slot).start()

        @pl.when(j == nu - 1)
        def _():
            wait_out(slot)

            @pl.when(j >= 1)
            def _():
                wait_out(other)

    for parity in range(2):
        pl.when((j < nu) & (lax.rem(j, 2) == parity))(functools.partial(step, parity))


def _moe(block_e, n_used, g_in, g_out, xsorted, wgu, bgu, wd, bd):
    n_blocks = block_e.shape[0]
    dh = xsorted.shape[1]
    d = 2 * dh
    f2 = wgu.shape[2]
    f = wd.shape[1]
    g_in3 = g_in.reshape(n_blocks, 1, MOE_GROUPS)
    g_next3 = jnp.concatenate([g_in3[1:], g_in3[:1]], axis=0)
    g_out3 = g_out.reshape(n_blocks, 1, MOE_GROUPS)
    xs3 = xsorted.reshape(xsorted.shape[0] // SUBLANES, SUBLANES, dh)
    tbl = pl.BlockSpec((None, 1, MOE_GROUPS), lambda j, be, nu: (j, 0, 0), memory_space=pltpu.SMEM)
    return pl.pallas_call(
        _moe_kernel,
        grid_spec=pltpu.PrefetchScalarGridSpec(
            num_scalar_prefetch=2, grid=(n_blocks,),
            in_specs=[tbl, tbl, tbl,
                      pl.BlockSpec(memory_space=pl.ANY),
                      pl.BlockSpec((None, d, f2), lambda j, be, nu: (be[j], 0, 0)),
                      pl.BlockSpec((None, 1, f2), lambda j, be, nu: (be[j], 0, 0)),
                      pl.BlockSpec((None, f, d), lambda j, be, nu: (be[j], 0, 0)),
                      pl.BlockSpec((None, 1, d), lambda j, be, nu: (be[j], 0, 0))],
            out_specs=pl.BlockSpec(memory_space=pl.ANY),
            scratch_shapes=[pltpu.VMEM((2, MOE_GROUPS, SUBLANES, dh), U32),
                            pltpu.VMEM((2, MOE_GROUPS, SUBLANES, dh), U32),
                            pltpu.VMEM((d, f2), BF16), pltpu.VMEM((f, d), BF16),
                            pltpu.SemaphoreType.DMA((2,)), pltpu.SemaphoreType.DMA((2,))]),
        out_shape=jax.ShapeDtypeStruct(xs3.shape, U32),
        input_output_aliases={5: 0},
        compiler_params=_cparams(),
        name="moe",
    )(block_e, n_used, g_in3, g_next3, g_out3, xs3, wgu, bgu, wd, bd).reshape(xsorted.shape)


def _combine_kernel(meta_ref, x1_ref, g_ref, eo_ref, y_ref):
    tm = x1_ref.shape[0]
    p_rows = eo_ref.shape[0]
    lane = lax.broadcasted_iota(I32, (tm, SORT_CHUNK), 1).astype(F32).astype(BF16)
    ws = [meta_ref[:, k:k + 1].astype(BF16) for k in range(TOP_K)]
    pos = [meta_ref[:, TOP_K + k:TOP_K + k + 1] for k in range(TOP_K)]
    acc = x1_ref[...]
    for r in range(p_rows // SORT_CHUNK):
        buf = _unpack_bf16_pairs(eo_ref[r * SORT_CHUNK:(r + 1) * SORT_CHUNK, :])
        lo_r = float(r * SORT_CHUNK)
        wm = jnp.zeros((tm, SORT_CHUNK), BF16)
        for k in range(TOP_K):
            rel = jnp.where((pos[k] >= lo_r) & (pos[k] < lo_r + SORT_CHUNK), pos[k] - lo_r, -1.0).astype(BF16)
            wm = jnp.where(lane == rel, ws[k], wm)
        acc = acc + jnp.dot(wm, buf, preferred_element_type=F32)
    y_ref[...] = _rms(acc, g_ref[...])


def _combine(meta, x1, g_final, eo, tile_off, n_tokens):
    d = x1.shape[1]
    tm = TOKEN_TILE
    p_rows = _sorted_rows(tm)
    return pl.pallas_call(
        _combine_kernel,
        grid=(n_tokens // tm,),
        in_specs=[pl.BlockSpec((tm, LANES), lambda i: (i + tile_off, 0)),
                  pl.BlockSpec((tm, d), lambda i: (i + tile_off, 0)),
                  pl.BlockSpec((1, d), lambda i: (0, 0)),
                  pl.BlockSpec((p_rows, d // 2), lambda i: (i + tile_off, 0))],
        out_specs=pl.BlockSpec((tm, d), lambda i: (i, 0)),
        out_shape=jax.ShapeDtypeStruct((n_tokens, d), F32),
        compiler_params=_cparams(),
        name="combine",
    )(meta, x1, g_final, eo)


def kernel(x_prompt, x_sample, g_mix, w_in, ssm_lam_re, ssm_lam_im, ssm_log_dt, ssm_b_re, ssm_b_im,
           ssm_c_re, ssm_c_im, ssm_d, w_glu, b_glu, sgu_ln_g, sgu_ln_b, sgu_w_s, sgu_b_s,
           g_out_ssm, g_out_sgu, w_out, g_ffn, w_router, b_router, w_gate_up, b_gate_up,
           w_down, b_down, g_final):
    assert g_mix.shape[0] == 1, "single-layer trunk"
    bp, lp, d = x_prompt.shape
    bs, ls, _ = x_sample.shape
    tp, ts = bp * lp, bs * ls
    t = tp + ts
    d_ssm = ssm_d.shape[1]
    d_sgu = sgu_ln_g.shape[1]
    n_exp = w_router.shape[2]
    assert all(n % tile == 0 for n in (tp, ts) for tile in (SEQ_TILE, TOKEN_TILE))
    assert lp % CHUNK == 0 and ls % CHUNK == 0 and TOKEN_TILE % CHUNK == 0 and CHUNK % SSM_CHUNK == 0
    assert SEQ_TILE % SSM_CHUNK == 0
    assert SSM_CHUNK * SSM_GROUP == MXU_DIM and 2 * ssm_lam_re.shape[-1] == LANES
    assert d_sgu // SGU_HEADS == LANES // 2 and n_exp <= LANES
    assert n_exp * (SUBLANES - 1) <= MXU_DIM and _sorted_rows(TOKEN_TILE) % SORT_CHUNK == 0
    assert (2 * MOE_GROUPS + 1) * SUBLANES <= _sorted_rows(TOKEN_TILE)

    xp = x_prompt.reshape(tp, d)
    xs = x_sample.reshape(ts, d)
    row = lambda a: a.reshape(1, -1).astype(F32)

    w_in_b = w_in[0].astype(BF16)
    u_p, zuv_p = _inproj(xp, row(g_mix[0]), w_in_b, d_ssm)
    u_s, zuv_s = _inproj(xs, row(g_mix[0]), w_in_b, d_ssm)

    mats = _ssm_matrices(ssm_lam_re[0], ssm_lam_im[0], ssm_log_dt[0], ssm_b_re[0], ssm_b_im[0],
                         ssm_c_re[0], ssm_c_im[0], ssm_d[0])
    y_p = _ssm_trunk(u_p, bp, mats)
    y_s = _ssm_trunk(u_s, bs, mats)

    ws = sgu_w_s[0]
    ws_pairs = jnp.concatenate([ws[0::2], ws[1::2]], axis=2).astype(BF16)
    bias_s = jnp.repeat(sgu_b_s[0].T, d_sgu // SGU_HEADS, axis=1).astype(F32)
    mix_params = (w_glu[0].astype(BF16), row(b_glu[0]), row(g_out_ssm[0]), row(sgu_ln_g[0]), row(sgu_ln_b[0]),
                  ws_pairs, bias_s, row(g_out_sgu[0]))

    wr_pad = jnp.pad(w_router[0].astype(F32), ((0, 0), (0, LANES - n_exp)))
    wr_hi = wr_pad.astype(BF16)
    wr_cat = jnp.concatenate([wr_hi, (wr_pad - wr_hi.astype(F32)).astype(BF16)], axis=1)
    br_pad = jnp.pad(b_router[0].astype(F32), (0, LANES - n_exp)).reshape(1, LANES)
    x1, xsorted, meta, c8 = _route(xp, xs, y_p, y_s, zuv_p, zuv_s, mix_params, w_out[0].astype(BF16),
                                   row(g_ffn[0]), wr_hi, wr_cat, br_pad, n_exp)

    nt = t // TOKEN_TILE
    p_rows = _sorted_rows(TOKEN_TILE)
    max_rows = t * TOP_K + nt * n_exp * (SUBLANES - 1) + n_exp * (MOE_TILE - 1)
    n_blocks = -(-max_rows // MOE_TILE)
    block_e, n_used, g_in, g_out = _slot_tables(c8[:, :, 0].astype(I32), p_rows, n_blocks)
    eo = _moe(block_e, n_used.reshape(1), g_in, g_out, xsorted, w_gate_up[0],
              b_gate_up[0][:, None, :].astype(F32), w_down[0], b_down[0][:, None, :].astype(F32))

    gf = row(g_final)
    y_prompt = _combine(meta, x1, gf, eo, 0, tp)
    y_sample = _combine(meta, x1, gf, eo, tp // TOKEN_TILE, ts)
    return y_prompt.reshape(bp, lp, d), y_sample.reshape(bs, ls, d)
```

```python
import functools
import math

import jax
import jax.numpy as jnp
from jax import lax
from jax.experimental import pallas as pl
from jax.experimental.pallas import tpu as pltpu

F32 = jnp.float32
BF16 = jnp.bfloat16
I32 = jnp.int32
U32 = jnp.uint32

SSM_GROUP = 16
SGU_HEADS = 8
CHUNK = 128
TOP_K = 4
SWIGLU_LIMIT = 7.0
SWIGLU_ALPHA = 1.702
RMS_EPS = 1e-6
LN_EPS = 1e-5

LANES = 128
SUBLANES = 8
MXU_DIM = 256
SSM_CHUNK = MXU_DIM // SSM_GROUP
PAIR_W = 2 * SSM_CHUNK * SSM_GROUP
ATOM = 2 * SSM_GROUP
ATOMS = LANES // ATOM

TOKEN_TILE = 512
SEQ_TILE = 1024
MOE_TILE = 512
MOE_GROUPS = MOE_TILE // SUBLANES
SORT_CHUNK = 256
CAST_ROWS = 128
SSM_ROW_TILE = 2048
SCAN_SEQS = 4
VMEM_LIMIT = 56 * 1024 * 1024


def _cparams(n_axes=1, vmem=None):
    return pltpu.CompilerParams(
        dimension_semantics=("arbitrary",) * n_axes,
        vmem_limit_bytes=vmem if vmem is not None else VMEM_LIMIT,
    )


def _rms(x, g):
    return x * lax.rsqrt(jnp.mean(x * x, axis=-1, keepdims=True) + RMS_EPS) * g


def _gelu(x):
    return x * (lax.erf(x * (1.0 / math.sqrt(2.0))) + 1.0) * 0.5


def _atom_masks(rows):
    lane = lax.broadcasted_iota(I32, (rows, LANES), 1)
    return [(lane >= a * ATOM) & (lane < (a + 1) * ATOM) for a in range(ATOMS)]


def _atom_transpose(src, masks):
    dst = []
    for b in range(ATOMS):
        acc = None
        for a in range(ATOMS):
            r = (a - b) % ATOMS
            piece = src[a] if r == 0 else pltpu.roll(src[a], ATOM * r, axis=1)
            acc = piece if acc is None else jnp.where(masks[a], piece, acc)
        dst.append(acc)
    return dst


def _pack_bf16_pairs(x):
    w = x.shape[1] // 2
    lo = lax.bitcast_convert_type(x[:, :w], U32) >> 16
    hi = lax.bitcast_convert_type(x[:, w:], U32) & jnp.uint32(0xFFFF0000)
    return hi | lo


def _unpack_bf16_pairs(u):
    lo = lax.bitcast_convert_type(u << 16, F32)
    hi = lax.bitcast_convert_type(u & jnp.uint32(0xFFFF0000), F32)
    return jnp.concatenate([lo.astype(BF16), hi.astype(BF16)], axis=1)


def _inproj_kernel(x_ref, g_ref, w_ref, u_ref, zuv_ref, za_scr, *, d_ssm):
    tm = x_ref.shape[0]
    nc = tm // SSM_CHUNK
    h = _rms(x_ref[...], g_ref[...])
    z = jnp.dot(h.astype(BF16), w_ref[...], preferred_element_type=F32)
    zuv_ref[...] = _gelu(z[:, d_ssm:]).astype(BF16)
    n_blk = d_ssm // LANES
    for b in range(n_blk):
        za_scr[b] = z[:, b * LANES:(b + 1) * LANES]
    masks = _atom_masks(nc)
    n_quads = SSM_CHUNK // ATOMS
    for b in range(n_blk):
        for v in range(n_quads):
            src = [za_scr[b, pl.ds(ATOMS * v + jj, nc, stride=SSM_CHUNK), :] for jj in range(ATOMS)]
            dst = _atom_transpose(src, masks)
            for pi in range(ATOMS):
                c0 = (ATOMS * b + pi) * PAIR_W + v * LANES
                u_ref[:, c0:c0 + LANES] = dst[pi].astype(BF16)


def _inproj(x, g_mix, w_in, d_ssm):
    t, d = x.shape
    tm = SEQ_TILE
    d_in = w_in.shape[1]
    nc = tm // SSM_CHUNK
    uw = d_ssm * SSM_CHUNK
    return pl.pallas_call(
        functools.partial(_inproj_kernel, d_ssm=d_ssm),
        grid=(t // tm,),
        in_specs=[pl.BlockSpec((tm, d), lambda i: (i, 0)),
                  pl.BlockSpec((1, d), lambda i: (0, 0)),
                  pl.BlockSpec((d, d_in), lambda i: (0, 0))],
        out_specs=[pl.BlockSpec((nc, uw), lambda i: (i, 0)),
                   pl.BlockSpec((tm, d_in - d_ssm), lambda i: (i, 0))],
        out_shape=[jax.ShapeDtypeStruct((t // SSM_CHUNK, uw), BF16),
                   jax.ShapeDtypeStruct((t, d_in - d_ssm), BF16)],
        scratch_shapes=[pltpu.VMEM((d_ssm // LANES, tm, LANES), F32)],
        compiler_params=_cparams(),
        name="inproj",
    )(x, g_mix, w_in)


def _ssm_matrices(lam_re, lam_im, log_dt, b_re, b_im, c_re, c_im, d_skip):
    hp = lax.Precision.HIGHEST
    _, g, n = lam_re.shape
    p = b_re.shape[-1]
    lc = SSM_CHUNK
    np_ = g // 2
    dt = jnp.exp(log_dt)[..., None]
    mag = jnp.exp(lam_re * dt)
    ar = mag * jnp.cos(lam_im * dt)
    ai = mag * jnp.sin(lam_im * dt)
    den = lam_re * lam_re + lam_im * lam_im
    nr = ar - 1.0
    fr = (nr * lam_re + ai * lam_im) / den
    fi = (ai * lam_re - nr * lam_im) / den
    bbr = fr[..., None] * b_re - fi[..., None] * b_im
    bbi = fr[..., None] * b_im + fi[..., None] * b_re

    kk_ = jnp.arange(lc + 1, dtype=F32)[:, None, None, None]
    mag_k = jnp.exp(kk_ * (lam_re * dt)[None])
    pw_r = mag_k * jnp.cos(kk_ * (lam_im * dt)[None])
    pw_i = mag_k * jnp.sin(kk_ * (lam_im * dt)[None])

    lane_k = jnp.arange(lc * p)
    til_k = (lane_k[None, :] % p == jnp.arange(p)[:, None]).astype(F32)
    rep_k = (lane_k[None, :] // p == jnp.arange(lc)[:, None]).astype(F32)
    ex = lambda x, e: jnp.einsum('dgnk,kx->dgnx', x, e, precision=hp)
    ct_r = ex(jnp.transpose(c_re, (0, 1, 3, 2)), til_k)
    ct_i = ex(jnp.transpose(c_im, (0, 1, 3, 2)), til_k)
    at_r = ex(jnp.transpose(pw_r[:lc], (1, 2, 3, 0)), rep_k)
    at_i = ex(jnp.transpose(pw_i[:lc], (1, 2, 3, 0)), rep_k)
    kk = (jnp.einsum('dgnq,dgnx->dgqx', bbr, ct_r * at_r - ct_i * at_i, precision=hp)
          - jnp.einsum('dgnq,dgnx->dgqx', bbi, ct_r * at_i + ct_i * at_r, precision=hp))
    kk = kk.reshape(2, g, p, lc, p)
    kf, kb = kk[0], kk[1]
    eye = jnp.eye(2, dtype=F32)
    a2 = p * 2

    k_all = jnp.concatenate([kb[:, :, :0:-1], (kf[:, :, :1] + kb[:, :, :1]), kf[:, :, 1:]], axis=2)
    k_all = k_all.reshape(np_, 2, p, 2 * lc - 1, p)
    kflat = jnp.einsum('nsqup,st->nsqutp', k_all, eye).reshape(np_, a2, (2 * lc - 1) * a2)
    kflat = jnp.pad(kflat, ((0, 0), (0, 0), (0, a2)))

    def atoms_q(x):
        x = jnp.transpose(x, (0, 1, 3, 2)).reshape(2, np_, 2, p, n)
        return jnp.einsum('dnsqm,st->dnsqtm', x, eye).reshape(2, np_, a2, 2 * n)
    bq_r, bq_i = atoms_q(bbr), atoms_q(bbi)
    bq = jnp.stack([bq_r[0], bq_i[0], bq_r[1], bq_i[1]], axis=1)
    ef = lc - 1 - jnp.arange(lc)
    eb = jnp.arange(lc)
    rows_q = lambda w, e, d: jnp.transpose(w[e, d].reshape(lc, np_, 2 * n), (1, 0, 2))
    wq = jnp.stack([rows_q(pw_r, ef, 0), rows_q(pw_i, ef, 0),
                    rows_q(pw_r, eb, 1), rows_q(pw_i, eb, 1)], axis=1)

    def atoms_p(x):
        x = jnp.transpose(x, (0, 1, 3, 2)).reshape(2, np_, 2, n, p)
        return jnp.einsum('dnsmp,st->dnsmtp', x, eye).reshape(2, np_, 2 * n, a2)
    cp_r, cp_i = atoms_p(c_re), atoms_p(c_im)
    cp = jnp.stack([cp_r[0], cp_i[0], cp_r[1], cp_i[1]], axis=1)
    pf_e = jnp.arange(lc) + 1
    pb_e = lc - jnp.arange(lc)
    cols_p = lambda w, e, d: jnp.transpose(w[e, d].reshape(lc, np_, 2 * n), (1, 2, 0))
    wp = jnp.stack([cols_p(pw_r, pf_e, 0), cols_p(pw_i, pf_e, 0),
                    cols_p(pw_r, pb_e, 1), cols_p(pw_i, pb_e, 1)], axis=1)
    lane = jnp.arange(lc * a2)
    til = (lane[None, :] % a2 == jnp.arange(a2)[:, None]).astype(BF16)
    rep = (lane[None, :] // a2 == jnp.arange(lc)[:, None]).astype(BF16)

    al = jnp.stack([pw_r[lc, 0], pw_i[lc, 0], pw_r[lc, 1], pw_i[lc, 1]])
    alpha = jnp.transpose(al.reshape(4, np_, 2 * n), (1, 0, 2))
    skip = jnp.broadcast_to(d_skip.reshape(np_, 1, a2), (np_, lc, a2)).reshape(np_, 1, lc * a2)
    return kflat, bq, wq, cp, wp, til, rep, alpha, skip.astype(F32)


def _ssm_v_kernel(u_ref, bq_ref, wq_ref, v_ref, q_scr):
    a2 = bq_ref.shape[1]
    w = bq_ref.shape[2]

    @pl.when(pl.program_id(1) == 0)
    def _():
        for d in range(2):
            br, bi = bq_ref[2 * d], bq_ref[2 * d + 1]
            for j in range(SSM_CHUNK):
                wr = wq_ref[2 * d, j:j + 1, :]
                wi = wq_ref[2 * d + 1, j:j + 1, :]
                q_scr[j * a2:(j + 1) * a2, (2 * d) * w:(2 * d + 1) * w] = (wr * br - wi * bi).astype(BF16)
                q_scr[j * a2:(j + 1) * a2, (2 * d + 1) * w:(2 * d + 2) * w] = (wr * bi + wi * br).astype(BF16)

    v_ref[...] = jnp.dot(u_ref[...], q_scr[...], preferred_element_type=F32)


def _ssm_scan_kernel(v_ref, a_ref, s_ref, *, n_chunks, bsz):
    w = LANES
    sub = SUBLANES
    n_groups = n_chunks // sub
    row = lax.broadcasted_iota(I32, (sub, w), 0)

    def cmul(ar, ai, xr, xi):
        return ar * xr - ai * xi, ar * xi + ai * xr

    def powers(ar, ai):
        p2 = cmul(ar, ai, ar, ai)
        p4 = cmul(*p2, *p2)
        p8 = cmul(*p4, *p4)
        p3 = cmul(*p2, ar, ai)
        p5 = cmul(*p4, ar, ai)
        p6 = cmul(*p4, *p2)
        p7 = cmul(*p4, *p3)
        seq = [(jnp.ones_like(ar), jnp.zeros_like(ai)), (ar, ai), p2, p3, p4, p5, p6, p7]
        tr = jnp.concatenate([s[0] for s in seq], axis=0)
        ti = jnp.concatenate([s[1] for s in seq], axis=0)
        return (ar, ai), p2, p4, p8, (tr, ti)

    def tile_scan(vr, vi, pw, reverse):
        a1, a2, a4, _, _ = pw

        def shift(x, s):
            if reverse:
                return jnp.where(row < sub - s, pltpu.roll(x, sub - s, axis=0), 0.0)
            return jnp.where(row >= s, pltpu.roll(x, s, axis=0), 0.0)

        xr, xi = shift(vr, 1), shift(vi, 1)
        for s, (ar, ai) in ((1, a1), (2, a2), (4, a4)):
            mr, mi = cmul(ar, ai, shift(xr, s), shift(xi, s))
            xr, xi = xr + mr, xi + mi
        e = 0 if reverse else sub - 1
        lr, li = cmul(a1[0], a1[1], xr[e:e + 1], xi[e:e + 1])
        return xr, xi, lr + vr[e:e + 1], li + vi[e:e + 1]

    pw_f = powers(a_ref[0:1, :], a_ref[1:2, :])
    pw_b = powers(a_ref[2:3, :], a_ref[3:4, :])
    tbr = jnp.concatenate([pw_b[4][0][sub - 1 - k:sub - k] for k in range(sub)], axis=0)
    tbi = jnp.concatenate([pw_b[4][1][sub - 1 - k:sub - k] for k in range(sub)], axis=0)

    def one_tile(row0, col0, pw, table, cr, ci, reverse):
        rows = pl.ds(pl.multiple_of(row0, sub), sub)
        xr, xi, lr, li = tile_scan(v_ref[rows, col0:col0 + w], v_ref[rows, col0 + w:col0 + 2 * w], pw, reverse)
        mr, mi = cmul(table[0], table[1], cr, ci)
        nr, ni = cmul(pw[3][0], pw[3][1], cr, ci)
        return xr + mr, xi + mi, nr + lr, ni + li

    pair = 2 * sub
    n_pairs = n_groups // 2

    def step(g, carry, b0):
        out = []
        for b in range(b0, b0 + len(carry) // 4):
            cfr, cfi, cbr, cbi = carry[4 * (b - b0):4 * (b - b0) + 4]
            rf = pl.multiple_of(b * n_chunks + g * pair, pair)
            rb = pl.multiple_of(b * n_chunks + (n_pairs - 1 - g) * pair, pair)
            ar, ai, cfr, cfi = one_tile(rf, 0, pw_f, pw_f[4], cfr, cfi, False)
            br, bi, cfr, cfi = one_tile(rf + sub, 0, pw_f, pw_f[4], cfr, cfi, False)
            s_ref[pl.ds(rf, pair), 0:w] = jnp.concatenate([ar, br], axis=0).astype(BF16)
            s_ref[pl.ds(rf, pair), w:2 * w] = jnp.concatenate([ai, bi], axis=0).astype(BF16)
            out += [cfr, cfi]
            br, bi, cbr, cbi = one_tile(rb + sub, 2 * w, pw_b, (tbr, tbi), cbr, cbi, True)
            ar, ai, cbr, cbi = one_tile(rb, 2 * w, pw_b, (tbr, tbi), cbr, cbi, True)
            s_ref[pl.ds(rb, pair), 2 * w:3 * w] = jnp.concatenate([ar, br], axis=0).astype(BF16)
            s_ref[pl.ds(rb, pair), 3 * w:4 * w] = jnp.concatenate([ai, bi], axis=0).astype(BF16)
            out += [cbr, cbi]
        return tuple(out)

    zero = jnp.zeros((1, w), F32)
    for b0 in range(0, bsz, SCAN_SEQS):
        nb = min(SCAN_SEQS, bsz - b0)
        lax.fori_loop(0, n_pairs, functools.partial(step, b0=b0), (zero,) * (4 * nb))


def _ssm_y_kernel(u_ref, s_ref, kflat_ref, cp_ref, wp_ref, til_ref, rep_ref, d_ref, y_ref, m_scr, p_scr):
    a2 = kflat_ref.shape[0]
    w = cp_ref.shape[1]

    def expand(x, e_ref):
        hi = x.astype(BF16)
        lo = (x - hi.astype(F32)).astype(BF16)
        return (jnp.dot(hi, e_ref[...], preferred_element_type=F32)
                + jnp.dot(lo, e_ref[...], preferred_element_type=F32))

    @pl.when(pl.program_id(1) == 0)
    def _():
        kflat = kflat_ref[...]
        for j in range(SSM_CHUNK):
            off = (SSM_CHUNK - 1 - j) * a2
            m_scr[j * a2:(j + 1) * a2, :] = kflat[:, off:off + PAIR_W].astype(BF16)
        for d in range(2):
            cr, ci = expand(cp_ref[2 * d], til_ref), expand(cp_ref[2 * d + 1], til_ref)
            wr, wi = expand(wp_ref[2 * d], rep_ref), expand(wp_ref[2 * d + 1], rep_ref)
            p_scr[(2 * d) * w:(2 * d + 1) * w, :] = (cr * wr - ci * wi).astype(BF16)
            p_scr[(2 * d + 1) * w:(2 * d + 2) * w, :] = (-(cr * wi + ci * wr)).astype(BF16)

    u = u_ref[...]
    y = jnp.dot(u, m_scr[...], preferred_element_type=F32)
    y = y + jnp.dot(s_ref[...], p_scr[...], preferred_element_type=F32)
    y_ref[...] = _gelu(y + d_ref[...] * u.astype(F32)).astype(BF16)


def _ssm_trunk(u, bsz, mats):
    kflat, bq, wq, cp, wp, til, rep, alpha, skip = mats
    rows = u.shape[0]
    nc = rows // bsz
    np_ = kflat.shape[0]
    pw = PAIR_W
    sw = 4 * bq.shape[3]
    tr = min(SSM_ROW_TILE, rows)
    per_pair = lambda a: pl.BlockSpec((None,) + a.shape[1:], lambda p, r: (p,) + (0,) * (a.ndim - 1))
    whole = lambda a: pl.BlockSpec(a.shape, lambda p, r: (0,) * a.ndim)
    v = pl.pallas_call(
        _ssm_v_kernel,
        grid=(np_, rows // tr),
        in_specs=[pl.BlockSpec((tr, pw), lambda p, r: (r, p)), per_pair(bq), per_pair(wq)],
        out_specs=pl.BlockSpec((tr, sw), lambda p, r: (r, p)),
        out_shape=jax.ShapeDtypeStruct((rows, np_ * sw), F32),
        scratch_shapes=[pltpu.VMEM((pw, sw), BF16)],
        compiler_params=_cparams(2),
        name="ssm_v",
    )(u, bq, wq)
    s = pl.pallas_call(
        functools.partial(_ssm_scan_kernel, n_chunks=nc, bsz=bsz),
        grid=(np_,),
        in_specs=[pl.BlockSpec((rows, sw), lambda p: (0, p)),
                  pl.BlockSpec((None, 4, LANES), lambda p: (p, 0, 0))],
        out_specs=pl.BlockSpec((rows, sw), lambda p: (0, p)),
        out_shape=jax.ShapeDtypeStruct((rows, np_ * sw), BF16),
        compiler_params=_cparams(1),
        name="ssm_scan",
    )(v, alpha)
    y = pl.pallas_call(
        _ssm_y_kernel,
        grid=(np_, rows // tr),
        in_specs=[pl.BlockSpec((tr, pw), lambda p, r: (r, p)),
                  pl.BlockSpec((tr, sw), lambda p, r: (r, p)),
                  per_pair(kflat), per_pair(cp), per_pair(wp), whole(til), whole(rep), per_pair(skip)],
        out_specs=pl.BlockSpec((tr, pw), lambda p, r: (r, p)),
        out_shape=jax.ShapeDtypeStruct((rows, np_ * pw), BF16),
        scratch_shapes=[pltpu.VMEM((pw, pw), BF16), pltpu.VMEM((sw, pw), BF16)],
        compiler_params=_cparams(2),
        name="ssm_y",
    )(u, s, kflat, cp, wp, til, rep, skip)
    return y


def _mixer(y_ref, zuv_ref, wglu_ref, bglu_ref, gssm_ref, lng_ref, lnb_ref, ws_ref, bs_ref, gsgu_ref, ya_scr):
    tm = zuv_ref.shape[0]
    n_blk = ya_scr.shape[0]
    nc = tm // SSM_CHUNK
    masks = _atom_masks(nc)
    for b in range(n_blk):
        for v in range(SSM_CHUNK // ATOMS):
            src = [y_ref[:, (ATOMS * b + pi) * PAIR_W + v * LANES:
                         (ATOMS * b + pi) * PAIR_W + (v + 1) * LANES].astype(F32)
                   for pi in range(ATOMS)]
            dst = _atom_transpose(src, masks)
            for jj in range(ATOMS):
                ya_scr[b, pl.ds(ATOMS * v + jj, nc, stride=SSM_CHUNK), :] = dst[jj]
    gl = jnp.concatenate([ya_scr[b] for b in range(n_blk)], axis=1)
    gate = jnp.dot(gl.astype(BF16), wglu_ref[...], preferred_element_type=F32) + bglu_ref[...]
    ra = _rms(gl * jax.nn.sigmoid(gate), gssm_ref[...]).astype(BF16)
    d_sgu = zuv_ref.shape[1] // 2
    u = zuv_ref[:, :d_sgu].astype(F32)
    gv = zuv_ref[:, d_sgu:].astype(F32)
    xc = gv - jnp.mean(gv, axis=-1, keepdims=True)
    v = xc * lax.rsqrt(jnp.mean(xc * xc, axis=-1, keepdims=True) + LN_EPS)
    v = (v * lng_ref[...] + lnb_ref[...]).astype(BF16)
    lo = lax.broadcasted_iota(I32, (CHUNK, LANES), 1) < (LANES // 2)
    zero = jnp.zeros((CHUNK, LANES), BF16)
    rows = []
    for c in range(tm // CHUNK):
        cols = []
        for j in range(d_sgu // LANES):
            vp = v[c * CHUNK:(c + 1) * CHUNK, j * LANES:(j + 1) * LANES]
            rhs = jnp.concatenate([jnp.where(lo, vp, zero), jnp.where(lo, zero, vp)], axis=0)
            cols.append(jnp.dot(ws_ref[j], rhs, preferred_element_type=F32))
        rows.append(jnp.concatenate(cols, axis=1) + bs_ref[...])
    s = jnp.concatenate(rows, axis=0)
    rb = _rms(u * s, gsgu_ref[...]).astype(BF16)
    return jnp.concatenate([ra, rb], axis=1)


def _sorted_rows(tile):
    return TOP_K * tile + MXU_DIM


def _route_kernel(xp_ref, xs_ref, yp_ref, ys_ref, zp_ref, zs_ref, wglu_ref, bglu_ref, gssm_ref, lng_ref,
                  lnb_ref, ws_ref, bs_ref, gsgu_ref, wout_ref, gffn_ref, whi_ref, wcat_ref, br_ref, tri_ref,
                  ltri_ref, x1_ref, xsort_ref, meta_ref, c8_ref, ya_scr, *, n_prompt_tiles, n_tiles, n_exp):
    i = pl.program_id(0)
    tm = x1_ref.shape[0]
    p_rows = xsort_ref.shape[0]

    def body(x_ref, y_ref, zuv_ref):
        mixed = _mixer(y_ref, zuv_ref, wglu_ref, bglu_ref, gssm_ref, lng_ref, lnb_ref, ws_ref, bs_ref,
                       gsgu_ref, ya_scr)
        x1 = x_ref[...] + jnp.dot(mixed, wout_ref[...], preferred_element_type=F32)
        x1_ref[...] = x1
        h2 = _rms(x1, gffn_ref[...])
        hi = h2.astype(BF16)
        lo = (h2 - hi.astype(F32)).astype(BF16)
        hw = jnp.dot(hi, wcat_ref[...], preferred_element_type=F32)
        lt = (hw[:, :LANES] + (jnp.dot(lo, whi_ref[...], preferred_element_type=F32) + hw[:, LANES:])
              + br_ref[...])
        logits = lt.T[:n_exp]
        eio = lax.broadcasted_iota(I32, (n_exp, tm), 0)
        vals, idxs = [], []
        l = logits
        for _ in range(TOP_K):
            m = jnp.max(l, axis=0, keepdims=True)
            idx = jnp.min(jnp.where(l == m, eio, n_exp), axis=0, keepdims=True)
            vals.append(m)
            idxs.append(idx)
            l = jnp.where(eio == idx, -jnp.inf, l)
        ex = [jnp.exp(v - vals[0]) for v in vals]
        den = ex[0] + ex[1] + ex[2] + ex[3]
        ws = [e / den for e in ex]
        hot = [eio == idx for idx in idxs]
        cnt = sum(h.astype(F32) for h in hot)
        prefix = jnp.dot(cnt.astype(BF16), tri_ref[...], preferred_element_type=F32)
        c = jnp.sum(cnt, axis=1, keepdims=True)
        c8 = jnp.floor((c + (SUBLANES - 1)) * (1.0 / SUBLANES)) * SUBLANES
        c8b = jnp.broadcast_to(c8, (n_exp, LANES))
        c8_ref[...] = c8b
        run0 = jnp.dot(ltri_ref[...], c8b.astype(BF16), preferred_element_type=F32)[:, 0:1]
        base = run0 + prefix
        pos = [jnp.sum(jnp.where(h, base, 0.0), axis=0, keepdims=True) for h in hot]
        meta = jnp.concatenate(ws + pos + [jnp.zeros((LANES - 2 * TOP_K, tm), F32)], axis=0)
        meta_ref[...] = meta.T
        rio = lax.broadcasted_iota(I32, (SORT_CHUNK, tm), 0).astype(F32).astype(BF16)
        one = jnp.ones((SORT_CHUNK, tm), BF16)
        zero = jnp.zeros((SORT_CHUNK, tm), BF16)
        for r in range(p_rows // SORT_CHUNK):
            lo_r = float(r * SORT_CHUNK)
            rel = [jnp.where((p >= lo_r) & (p < lo_r + SORT_CHUNK), p - lo_r, -1.0).astype(BF16) for p in pos]
            sel = (rio == rel[0]) | (rio == rel[1]) | (rio == rel[2]) | (rio == rel[3])
            srt = jnp.dot(jnp.where(sel, one, zero), hi, preferred_element_type=F32)
            xsort_ref[r * SORT_CHUNK:(r + 1) * SORT_CHUNK, :] = _pack_bf16_pairs(srt)

    @pl.when(i < n_prompt_tiles)
    def _():
        body(xp_ref, yp_ref, zp_ref)

    @pl.when((i >= n_prompt_tiles) & (i < n_tiles))
    def _():
        body(xs_ref, ys_ref, zs_ref)

    @pl.when(i == n_tiles)
    def _():
        xsort_ref[...] = jnp.zeros_like(xsort_ref)


def _dual_specs(tile, width, n_prompt_tiles, n_tiles):
    last_p = n_prompt_tiles - 1
    last_s = n_tiles - n_prompt_tiles - 1
    sp = pl.BlockSpec((tile, width), lambda i: (jnp.minimum(i, last_p), 0))
    ss = pl.BlockSpec((tile, width), lambda i: (jnp.clip(i - n_prompt_tiles, 0, last_s), 0))
    return sp, ss


def _route(xp, xs, y_p, y_s, zuv_p, zuv_s, mix_params, w_out, g_ffn, wr_hi, wr_cat, br_pad, n_exp):
    tp, d = xp.shape
    t = tp + xs.shape[0]
    tm = TOKEN_TILE
    npt = tp // tm
    nt = t // tm
    p_rows = _sorted_rows(tm)
    d_ssm = mix_params[0].shape[0]
    iota = lambda n, ax: lax.broadcasted_iota(I32, (n, n), ax)
    tri = (iota(tm, 0) < iota(tm, 1)).astype(BF16)
    ltri = (iota(n_exp, 1) < iota(n_exp, 0)).astype(BF16)
    const = lambda shape: pl.BlockSpec(shape, lambda i: (0,) * len(shape))
    tile = lambda i: jnp.minimum(i, nt - 1)
    dual = lambda rows, width: _dual_specs(rows, width, npt, nt)
    return pl.pallas_call(
        functools.partial(_route_kernel, n_prompt_tiles=npt, n_tiles=nt, n_exp=n_exp),
        grid=(nt + 1,),
        in_specs=[*dual(tm, d), *dual(tm // SSM_CHUNK, y_p.shape[1]), *dual(tm, zuv_p.shape[1]),
                  *[const(a.shape) for a in mix_params],
                  const(w_out.shape), const((1, d)), const(wr_hi.shape), const(wr_cat.shape),
                  const((1, LANES)), const((tm, tm)), const((n_exp, n_exp))],
        out_specs=[pl.BlockSpec((tm, d), lambda i: (tile(i), 0)),
                   pl.BlockSpec((p_rows, d // 2), lambda i: (i, 0)),
                   pl.BlockSpec((tm, LANES), lambda i: (tile(i), 0)),
                   pl.BlockSpec((None, n_exp, LANES), lambda i: (tile(i), 0, 0))],
        out_shape=[jax.ShapeDtypeStruct((t, d), F32),
                   jax.ShapeDtypeStruct(((nt + 1) * p_rows, d // 2), U32),
                   jax.ShapeDtypeStruct((t, LANES), F32),
                   jax.ShapeDtypeStruct((nt, n_exp, LANES), F32)],
        scratch_shapes=[pltpu.VMEM((d_ssm // LANES, tm, LANES), F32)],
        compiler_params=_cparams(),
        name="route",
    )(xp, xs, y_p, y_s, zuv_p, zuv_s, *mix_params, w_out, g_ffn, wr_hi, wr_cat, br_pad, tri, ltri)


def _slot_tables(c8, p_rows, n_blocks):
    nt, n_exp = c8.shape
    run0 = jnp.cumsum(c8, axis=1) - c8
    seg_len = c8.T
    cum = jnp.cumsum(seg_len, axis=1)
    tot = cum[:, -1]
    padded = (tot + MOE_TILE - 1) // MOE_TILE * MOE_TILE
    ends = jnp.cumsum(padded)
    starts = ends - padded
    n_used = (ends[-1] // MOE_TILE).astype(I32)
    blk = jnp.arange(n_blocks, dtype=I32)
    be = jnp.minimum(jnp.sum(ends[None, :] <= (blk * MOE_TILE)[:, None], axis=1), n_exp - 1).astype(I32)
    block_e = jnp.where(blk < n_used, be, be[jnp.maximum(n_used - 1, 0)])
    onehot = (be[:, None] == jnp.arange(n_exp, dtype=I32)[None, :]).astype(F32)
    pick = lambda tbl: jnp.dot(onehot, tbl.astype(F32), precision=lax.Precision.HIGHEST)
    run_end = pick(cum)
    run_beg = run_end - pick(seg_len)
    shift = pick((jnp.arange(nt, dtype=I32) * p_rows)[None, :] + run0.T) - run_beg
    grow = (blk[:, None] * MOE_TILE + jnp.arange(MOE_GROUPS, dtype=I32)[None, :] * SUBLANES).astype(F32)
    rel = grow - pick(starts[:, None])
    inside = (run_beg[:, None, :] <= rel[:, :, None]) & (rel[:, :, None] < run_end[:, None, :])
    src_row = (rel + jnp.sum(jnp.where(inside, shift[:, None, :], 0.0), axis=2)).astype(I32)
    valid = jnp.any(inside, axis=2) & (blk < n_used)[:, None]
    scratch0 = nt * p_rows // SUBLANES
    zero_group = scratch0 + 2 * MOE_GROUPS
    g_in = jnp.where(valid, src_row // SUBLANES, zero_group).astype(I32)
    g_scr = scratch0 + (blk[:, None] % 2) * MOE_GROUPS + jnp.arange(MOE_GROUPS, dtype=I32)[None, :]
    g_out = jnp.where(valid, src_row // SUBLANES, g_scr).astype(I32)
    return block_e, n_used, g_in, g_out


def _moe_kernel(be_ref, nu_ref, gin_ref, gnext_ref, gout_ref, xs_hbm, wgu_ref, bgu_ref, wd_ref, bd_ref,
                eo_hbm, xbuf, obuf, wgu_s, wd_s, isem, osem):
    j = pl.program_id(0)
    nu = nu_ref[0]
    f = wd_ref.shape[0]

    @pl.when((j < nu) & ((j == 0) | (be_ref[j] != be_ref[jnp.maximum(j - 1, 0)])))
    def _():
        for c in range(0, wgu_ref.shape[0], CAST_ROWS):
            wgu_s[c:c + CAST_ROWS, :] = wgu_ref[c:c + CAST_ROWS, :].astype(BF16)
        for c in range(0, wd_ref.shape[0], CAST_ROWS):
            wd_s[c:c + CAST_ROWS, :] = wd_ref[c:c + CAST_ROWS, :].astype(BF16)

    def in_copy(tbl_ref, r, sl):
        return pltpu.make_async_copy(xs_hbm.at[tbl_ref[0, r]], xbuf.at[sl, r], isem.at[sl])

    def out_copy(r, sl):
        return pltpu.make_async_copy(obuf.at[sl, r], eo_hbm.at[gout_ref[0, r]], osem.at[sl])

    def wait_in(sl):
        pltpu.make_async_copy(xs_hbm.at[pl.ds(0, MOE_GROUPS)], xbuf.at[sl], isem.at[sl]).wait()

    def wait_out(sl):
        pltpu.make_async_copy(obuf.at[sl], eo_hbm.at[pl.ds(0, MOE_GROUPS)], osem.at[sl]).wait()

    @pl.when(j == 0)
    def _():
        for r in range(MOE_GROUPS):
            in_copy(gin_ref, r, 0).start()

    def step(slot):
        other = 1 - slot

        @pl.when(j + 1 < nu)
        def _():
            for r in range(MOE_GROUPS):
                in_copy(gnext_ref, r, other).start()

        wait_in(slot)

        @pl.when(j >= 2)
        def _():
            wait_out(slot)

        x = _unpack_bf16_pairs(xbuf[slot].reshape(MOE_TILE, xbuf.shape[-1]))
        gate = jnp.dot(x, wgu_s[:, :f], preferred_element_type=F32) + bgu_ref[:, :f]
        gate = jnp.minimum(gate, SWIGLU_LIMIT)
        sg = gate * jax.nn.sigmoid(SWIGLU_ALPHA * gate)
        up = jnp.dot(x, wgu_s[:, f:], preferred_element_type=F32) + bgu_ref[:, f:]
        act = sg * (jnp.clip(up, -SWIGLU_LIMIT, SWIGLU_LIMIT) + 1.0)
        out = jnp.dot(act.astype(BF16), wd_s[...], preferred_element_type=F32) + bd_ref[...]
        obuf[slot] = _pack_bf16_pairs(out.astype(BF16).astype(F32)).reshape(obuf.shape[1:])
        for r in range(MOE_GROUPS):
            out_copy(r, slot).start()

        @pl.when(j == nu - 1)
        def _():
            wait_out(slot)

            @pl.when(j >= 1)
            def _():
                wait_out(other)

    for parity in range(2):
        pl.when((j < nu) & (lax.rem(j, 2) == parity))(functools.partial(step, parity))


def _moe(block_e, n_used, g_in, g_out, xsorted, wgu, bgu, wd, bd):
    n_blocks = block_e.shape[0]
    dh = xsorted.shape[1]
    d = 2 * dh
    f2 = wgu.shape[2]
    f = wd.shape[1]
    g_in3 = g_in.reshape(n_blocks, 1, MOE_GROUPS)
    g_next3 = jnp.concatenate([g_in3[1:], g_in3[:1]], axis=0)
    g_out3 = g_out.reshape(n_blocks, 1, MOE_GROUPS)
    xs3 = xsorted.reshape(xsorted.shape[0] // SUBLANES, SUBLANES, dh)
    tbl = pl.BlockSpec((None, 1, MOE_GROUPS), lambda j, be, nu: (j, 0, 0), memory_space=pltpu.SMEM)
    return pl.pallas_call(
        _moe_kernel,
        grid_spec=pltpu.PrefetchScalarGridSpec(
            num_scalar_prefetch=2, grid=(n_blocks,),
            in_specs=[tbl, tbl, tbl,
                      pl.BlockSpec(memory_space=pl.ANY),
                      pl.BlockSpec((None, d, f2), lambda j, be, nu: (be[j], 0, 0)),
                      pl.BlockSpec((None, 1, f2), lambda j, be, nu: (be[j], 0, 0)),
                      pl.BlockSpec((None, f, d), lambda j, be, nu: (be[j], 0, 0)),
                      pl.BlockSpec((None, 1, d), lambda j, be, nu: (be[j], 0, 0))],
            out_specs=pl.BlockSpec(memory_space=pl.ANY),
            scratch_shapes=[pltpu.VMEM((2, MOE_GROUPS, SUBLANES, dh), U32),
                            pltpu.VMEM((2, MOE_GROUPS, SUBLANES, dh), U32),
                            pltpu.VMEM((d, f2), BF16), pltpu.VMEM((f, d), BF16),
                            pltpu.SemaphoreType.DMA((2,)), pltpu.SemaphoreType.DMA((2,))]),
        out_shape=jax.ShapeDtypeStruct(xs3.shape, U32),
        input_output_aliases={5: 0},
        compiler_params=_cparams(),
        name="moe",
    )(block_e, n_used, g_in3, g_next3, g_out3, xs3, wgu, bgu, wd, bd).reshape(xsorted.shape)


def _combine_kernel(meta_ref, x1_ref, g_ref, eo_ref, y_ref):
    tm = x1_ref.shape[0]
    p_rows = eo_ref.shape[0]
    lane = lax.broadcasted_iota(I32, (tm, SORT_CHUNK), 1).astype(F32).astype(BF16)
    ws = [meta_ref[:, k:k + 1].astype(BF16) for k in range(TOP_K)]
    pos = [meta_ref[:, TOP_K + k:TOP_K + k + 1] for k in range(TOP_K)]
    acc = x1_ref[...]
    for r in range(p_rows // SORT_CHUNK):
        buf = _unpack_bf16_pairs(eo_ref[r * SORT_CHUNK:(r + 1) * SORT_CHUNK, :])
        lo_r = float(r * SORT_CHUNK)
        wm = jnp.zeros((tm, SORT_CHUNK), BF16)
        for k in range(TOP_K):
            rel = jnp.where((pos[k] >= lo_r) & (pos[k] < lo_r + SORT_CHUNK), pos[k] - lo_r, -1.0).astype(BF16)
            wm = jnp.where(lane == rel, ws[k], wm)
        acc = acc + jnp.dot(wm, buf, preferred_element_type=F32)
    y_ref[...] = _rms(acc, g_ref[...])


def _combine(meta, x1, g_final, eo, tile_off, n_tokens):
    d = x1.shape[1]
    tm = TOKEN_TILE
    p_rows = _sorted_rows(tm)
    return pl.pallas_call(
        _combine_kernel,
        grid=(n_tokens // tm,),
        in_specs=[pl.BlockSpec((tm, LANES), lambda i: (i + tile_off, 0)),
                  pl.BlockSpec((tm, d), lambda i: (i + tile_off, 0)),
                  pl.BlockSpec((1, d), lambda i: (0, 0)),
                  pl.BlockSpec((p_rows, d // 2), lambda i: (i + tile_off, 0))],
        out_specs=pl.BlockSpec((tm, d), lambda i: (i, 0)),
        out_shape=jax.ShapeDtypeStruct((n_tokens, d), F32),
        compiler_params=_cparams(),
        name="combine",
    )(meta, x1, g_final, eo)


def kernel(x_prompt, x_sample, g_mix, w_in, ssm_lam_re, ssm_lam_im, ssm_log_dt, ssm_b_re, ssm_b_im,
           ssm_c_re, ssm_c_im, ssm_d, w_glu, b_glu, sgu_ln_g, sgu_ln_b, sgu_w_s, sgu_b_s,
           g_out_ssm, g_out_sgu, w_out, g_ffn, w_router, b_router, w_gate_up, b_gate_up,
           w_down, b_down, g_final):
    assert g_mix.shape[0] == 1, "single-layer trunk"
    bp, lp, d = x_prompt.shape
    bs, ls, _ = x_sample.shape
    tp, ts = bp * lp, bs * ls
    t = tp + ts
    d_ssm = ssm_d.shape[1]
    d_sgu = sgu_ln_g.shape[1]
    n_exp = w_router.shape[2]
    assert all(n % tile == 0 for n in (tp, ts) for tile in (SEQ_TILE, TOKEN_TILE))
    assert lp % CHUNK == 0 and ls % CHUNK == 0 and TOKEN_TILE % CHUNK == 0 and CHUNK % SSM_CHUNK == 0
    assert SEQ_TILE % SSM_CHUNK == 0
    assert SSM_CHUNK * SSM_GROUP == MXU_DIM and 2 * ssm_lam_re.shape[-1] == LANES
    assert d_sgu // SGU_HEADS == LANES // 2 and n_exp <= LANES
    assert n_exp * (SUBLANES - 1) <= MXU_DIM and _sorted_rows(TOKEN_TILE) % SORT_CHUNK == 0
    assert (2 * MOE_GROUPS + 1) * SUBLANES <= _sorted_rows(TOKEN_TILE)

    xp = x_prompt.reshape(tp, d)
    xs = x_sample.reshape(ts, d)
    row = lambda a: a.reshape(1, -1).astype(F32)

    w_in_b = w_in[0].astype(BF16)
    u_p, zuv_p = _inproj(xp, row(g_mix[0]), w_in_b, d_ssm)
    u_s, zuv_s = _inproj(xs, row(g_mix[0]), w_in_b, d_ssm)

    mats = _ssm_matrices(ssm_lam_re[0], ssm_lam_im[0], ssm_log_dt[0], ssm_b_re[0], ssm_b_im[0],
                         ssm_c_re[0], ssm_c_im[0], ssm_d[0])
    y_p = _ssm_trunk(u_p, bp, mats)
    y_s = _ssm_trunk(u_s, bs, mats)

    ws = sgu_w_s[0]
    ws_pairs = jnp.concatenate([ws[0::2], ws[1::2]], axis=2).astype(BF16)
    bias_s = jnp.repeat(sgu_b_s[0].T, d_sgu // SGU_HEADS, axis=1).astype(F32)
    mix_params = (w_glu[0].astype(BF16), row(b_glu[0]), row(g_out_ssm[0]), row(sgu_ln_g[0]), row(sgu_ln_b[0]),
                  ws_pairs, bias_s, row(g_out_sgu[0]))

    wr_pad = jnp.pad(w_router[0].astype(F32), ((0, 0), (0, LANES - n_exp)))
    wr_hi = wr_pad.astype(BF16)
    wr_cat = jnp.concatenate([wr_hi, (wr_pad - wr_hi.astype(F32)).astype(BF16)], axis=1)
    br_pad = jnp.pad(b_router[0].astype(F32), (0, LANES - n_exp)).reshape(1, LANES)
    x1, xsorted, meta, c8 = _route(xp, xs, y_p, y_s, zuv_p, zuv_s, mix_params, w_out[0].astype(BF16),
                                   row(g_ffn[0]), wr_hi, wr_cat, br_pad, n_exp)

    nt = t // TOKEN_TILE
    p_rows = _sorted_rows(TOKEN_TILE)
    max_rows = t * TOP_K + nt * n_exp * (SUBLANES - 1) + n_exp * (MOE_TILE - 1)
    n_blocks = -(-max_rows // MOE_TILE)
    block_e, n_used, g_in, g_out = _slot_tables(c8[:, :, 0].astype(I32), p_rows, n_blocks)
    eo = _moe(block_e, n_used.reshape(1), g_in, g_out, xsorted, w_gate_up[0],
              b_gate_up[0][:, None, :].astype(F32), w_down[0], b_down[0][:, None, :].astype(F32))

    gf = row(g_final)
    y_prompt = _combine(meta, x1, gf, eo, 0, tp)
    y_sample = _combine(meta, x1, gf, eo, tp // TOKEN_TILE, ts)
    return y_prompt.reshape(bp, lp, d), y_sample.reshape(bs, ls, d)
```

```python
import functools
import math

import jax
import jax.numpy as jnp
from jax import lax
from jax.experimental import pallas as pl
from jax.experimental.pallas import tpu as pltpu

F32 = jnp.float32
BF16 = jnp.bfloat16
I32 = jnp.int32
U32 = jnp.uint32

SSM_GROUP = 16
SGU_HEADS = 8
CHUNK = 128
TOP_K = 4
SWIGLU_LIMIT = 7.0
SWIGLU_ALPHA = 1.702
RMS_EPS = 1e-6
LN_EPS = 1e-5

LANES = 128
SUBLANES = 8
MXU_DIM = 256
SSM_CHUNK = MXU_DIM // SSM_GROUP
PAIR_W = 2 * SSM_CHUNK * SSM_GROUP
ATOM = 2 * SSM_GROUP
ATOMS = LANES // ATOM

TOKEN_TILE = 512
SEQ_TILE = 1024
MOE_TILE = 1024
MOE_GROUPS = MOE_TILE // SUBLANES
SORT_CHUNK = 256
CAST_ROWS = 128
SSM_ROW_TILE = 2048
SCAN_SEQS = 4
VMEM_LIMIT = 56 * 1024 * 1024


def _cparams(n_axes=1, vmem=None):
    return pltpu.CompilerParams(
        dimension_semantics=("arbitrary",) * n_axes,
        vmem_limit_bytes=vmem if vmem is not None else VMEM_LIMIT,
    )


def _rms(x, g):
    return x * lax.rsqrt(jnp.mean(x * x, axis=-1, keepdims=True) + RMS_EPS) * g


def _gelu(x):
    return x * (lax.erf(x * (1.0 / math.sqrt(2.0))) + 1.0) * 0.5


def _atom_masks(rows):
    lane = lax.broadcasted_iota(I32, (rows, LANES), 1)
    return [(lane >= a * ATOM) & (lane < (a + 1) * ATOM) for a in range(ATOMS)]


def _atom_transpose(src, masks):
    dst = []
    for b in range(ATOMS):
        acc = None
        for a in range(ATOMS):
            r = (a - b) % ATOMS
            piece = src[a] if r == 0 else pltpu.roll(src[a], ATOM * r, axis=1)
            acc = piece if acc is None else jnp.where(masks[a], piece, acc)
        dst.append(acc)
    return dst


def _pack_bf16_pairs(x):
    w = x.shape[1] // 2
    lo = lax.bitcast_convert_type(x[:, :w], U32) >> 16
    hi = lax.bitcast_convert_type(x[:, w:], U32) & jnp.uint32(0xFFFF0000)
    return hi | lo


def _unpack_bf16_pairs(u):
    lo = lax.bitcast_convert_type(u << 16, F32)
    hi = lax.bitcast_convert_type(u & jnp.uint32(0xFFFF0000), F32)
    return jnp.concatenate([lo.astype(BF16), hi.astype(BF16)], axis=1)


def _inproj_kernel(x_ref, g_ref, w_ref, u_ref, zuv_ref, za_scr, *, d_ssm):
    tm = x_ref.shape[0]
    nc = tm // SSM_CHUNK
    h = _rms(x_ref[...], g_ref[...])
    z = jnp.dot(h.astype(BF16), w_ref[...], preferred_element_type=F32)
    zuv_ref[...] = _gelu(z[:, d_ssm:]).astype(BF16)
    n_blk = d_ssm // LANES
    for b in range(n_blk):
        za_scr[b] = z[:, b * LANES:(b + 1) * LANES]
    masks = _atom_masks(nc)
    n_quads = SSM_CHUNK // ATOMS
    for b in range(n_blk):
        for v in range(n_quads):
            src = [za_scr[b, pl.ds(ATOMS * v + jj, nc, stride=SSM_CHUNK), :] for jj in range(ATOMS)]
            dst = _atom_transpose(src, masks)
            for pi in range(ATOMS):
                c0 = (ATOMS * b + pi) * PAIR_W + v * LANES
                u_ref[:, c0:c0 + LANES] = dst[pi].astype(BF16)


def _inproj(x, g_mix, w_in, d_ssm):
    t, d = x.shape
    tm = SEQ_TILE
    d_in = w_in.shape[1]
    nc = tm // SSM_CHUNK
    uw = d_ssm * SSM_CHUNK
    return pl.pallas_call(
        functools.partial(_inproj_kernel, d_ssm=d_ssm),
        grid=(t // tm,),
        in_specs=[pl.BlockSpec((tm, d), lambda i: (i, 0)),
                  pl.BlockSpec((1, d), lambda i: (0, 0)),
                  pl.BlockSpec((d, d_in), lambda i: (0, 0))],
        out_specs=[pl.BlockSpec((nc, uw), lambda i: (i, 0)),
                   pl.BlockSpec((tm, d_in - d_ssm), lambda i: (i, 0))],
        out_shape=[jax.ShapeDtypeStruct((t // SSM_CHUNK, uw), BF16),
                   jax.ShapeDtypeStruct((t, d_in - d_ssm), BF16)],
        scratch_shapes=[pltpu.VMEM((d_ssm // LANES, tm, LANES), F32)],
        compiler_params=_cparams(),
        name="inproj",
    )(x, g_mix, w_in)


def _ssm_matrices(lam_re, lam_im, log_dt, b_re, b_im, c_re, c_im, d_skip):
    hp = lax.Precision.HIGHEST
    _, g, n = lam_re.shape
    p = b_re.shape[-1]
    lc = SSM_CHUNK
    np_ = g // 2
    dt = jnp.exp(log_dt)[..., None]
    mag = jnp.exp(lam_re * dt)
    ar = mag * jnp.cos(lam_im * dt)
    ai = mag * jnp.sin(lam_im * dt)
    den = lam_re * lam_re + lam_im * lam_im
    nr = ar - 1.0
    fr = (nr * lam_re + ai * lam_im) / den
    fi = (ai * lam_re - nr * lam_im) / den
    bbr = fr[..., None] * b_re - fi[..., None] * b_im
    bbi = fr[..., None] * b_im + fi[..., None] * b_re

    prs, pis = [jnp.ones_like(ar)], [jnp.zeros_like(ai)]
    for _ in range(lc):
        pr, pi = prs[-1], pis[-1]
        prs.append(pr * ar - pi * ai)
        pis.append(pr * ai + pi * ar)
    pw_r, pw_i = jnp.stack(prs), jnp.stack(pis)

    lane_k = jnp.arange(lc * p)
    til_k = (lane_k[None, :] % p == jnp.arange(p)[:, None]).astype(F32)
    rep_k = (lane_k[None, :] // p == jnp.arange(lc)[:, None]).astype(F32)
    ex = lambda x, e: jnp.einsum('dgnk,kx->dgnx', x, e, precision=hp)
    ct_r = ex(jnp.transpose(c_re, (0, 1, 3, 2)), til_k)
    ct_i = ex(jnp.transpose(c_im, (0, 1, 3, 2)), til_k)
    at_r = ex(jnp.transpose(pw_r[:lc], (1, 2, 3, 0)), rep_k)
    at_i = ex(jnp.transpose(pw_i[:lc], (1, 2, 3, 0)), rep_k)
    kk = (jnp.einsum('dgnq,dgnx->dgqx', bbr, ct_r * at_r - ct_i * at_i, precision=hp)
          - jnp.einsum('dgnq,dgnx->dgqx', bbi, ct_r * at_i + ct_i * at_r, precision=hp))
    kk = kk.reshape(2, g, p, lc, p)
    kf, kb = kk[0], kk[1]
    eye = jnp.eye(2, dtype=F32)
    a2 = p * 2

    k_all = jnp.concatenate([kb[:, :, :0:-1], (kf[:, :, :1] + kb[:, :, :1]), kf[:, :, 1:]], axis=2)
    k_all = k_all.reshape(np_, 2, p, 2 * lc - 1, p)
    kflat = jnp.einsum('nsqup,st->nsqutp', k_all, eye).reshape(np_, a2, (2 * lc - 1) * a2)
    kflat = jnp.pad(kflat, ((0, 0), (0, 0), (0, a2)))

    def atoms_q(x):
        x = jnp.transpose(x, (0, 1, 3, 2)).reshape(2, np_, 2, p, n)
        return jnp.einsum('dnsqm,st->dnsqtm', x, eye).reshape(2, np_, a2, 2 * n)
    bq_r, bq_i = atoms_q(bbr), atoms_q(bbi)
    bq = jnp.stack([bq_r[0], bq_i[0], bq_r[1], bq_i[1]], axis=1)
    ef = lc - 1 - jnp.arange(lc)
    eb = jnp.arange(lc)
    rows_q = lambda w, e, d: jnp.transpose(w[e, d].reshape(lc, np_, 2 * n), (1, 0, 2))
    wq = jnp.stack([rows_q(pw_r, ef, 0), rows_q(pw_i, ef, 0),
                    rows_q(pw_r, eb, 1), rows_q(pw_i, eb, 1)], axis=1)

    def atoms_p(x):
        x = jnp.transpose(x, (0, 1, 3, 2)).reshape(2, np_, 2, n, p)
        return jnp.einsum('dnsmp,st->dnsmtp', x, eye).reshape(2, np_, 2 * n, a2)
    cp_r, cp_i = atoms_p(c_re), atoms_p(c_im)
    cp = jnp.stack([cp_r[0], cp_i[0], cp_r[1], cp_i[1]], axis=1)
    pf_e = jnp.arange(lc) + 1
    pb_e = lc - jnp.arange(lc)
    cols_p = lambda w, e, d: jnp.transpose(w[e, d].reshape(lc, np_, 2 * n), (1, 2, 0))
    wp = jnp.stack([cols_p(pw_r, pf_e, 0), cols_p(pw_i, pf_e, 0),
                    cols_p(pw_r, pb_e, 1), cols_p(pw_i, pb_e, 1)], axis=1)
    lane = jnp.arange(lc * a2)
    til = (lane[None, :] % a2 == jnp.arange(a2)[:, None]).astype(BF16)
    rep = (lane[None, :] // a2 == jnp.arange(lc)[:, None]).astype(BF16)

    al = jnp.stack([pw_r[lc, 0], pw_i[lc, 0], pw_r[lc, 1], pw_i[lc, 1]])
    alpha = jnp.transpose(al.reshape(4, np_, 2 * n), (1, 0, 2))
    skip = jnp.broadcast_to(d_skip.reshape(np_, 1, a2), (np_, lc, a2)).reshape(np_, 1, lc * a2)
    return kflat, bq, wq, cp, wp, til, rep, alpha, skip.astype(F32)


def _ssm_v_kernel(u_ref, bq_ref, wq_ref, v_ref, q_scr):
    a2 = bq_ref.shape[1]
    w = bq_ref.shape[2]

    @pl.when(pl.program_id(1) == 0)
    def _():
        for d in range(2):
            br, bi = bq_ref[2 * d], bq_ref[2 * d + 1]
            for j in range(SSM_CHUNK):
                wr = wq_ref[2 * d, j:j + 1, :]
                wi = wq_ref[2 * d + 1, j:j + 1, :]
                q_scr[j * a2:(j + 1) * a2, (2 * d) * w:(2 * d + 1) * w] = (wr * br - wi * bi).astype(BF16)
                q_scr[j * a2:(j + 1) * a2, (2 * d + 1) * w:(2 * d + 2) * w] = (wr * bi + wi * br).astype(BF16)

    v_ref[...] = jnp.dot(u_ref[...], q_scr[...], preferred_element_type=F32)


def _ssm_scan_kernel(v_ref, a_ref, s_ref, *, n_chunks, bsz):
    w = LANES
    sub = SUBLANES
    n_groups = n_chunks // sub
    row = lax.broadcasted_iota(I32, (sub, w), 0)

    def cmul(ar, ai, xr, xi):
        return ar * xr - ai * xi, ar * xi + ai * xr

    def powers(ar, ai):
        p2 = cmul(ar, ai, ar, ai)
        p4 = cmul(*p2, *p2)
        p8 = cmul(*p4, *p4)
        p3 = cmul(*p2, ar, ai)
        p5 = cmul(*p4, ar, ai)
        p6 = cmul(*p4, *p2)
        p7 = cmul(*p4, *p3)
        seq = [(jnp.ones_like(ar), jnp.zeros_like(ai)), (ar, ai), p2, p3, p4, p5, p6, p7]
        tr = jnp.concatenate([s[0] for s in seq], axis=0)
        ti = jnp.concatenate([s[1] for s in seq], axis=0)
        return (ar, ai), p2, p4, p8, (tr, ti)

    def tile_scan(vr, vi, pw, reverse):
        a1, a2, a4, _, _ = pw

        def shift(x, s):
            if reverse:
                return jnp.where(row < sub - s, pltpu.roll(x, sub - s, axis=0), 0.0)
            return jnp.where(row >= s, pltpu.roll(x, s, axis=0), 0.0)

        xr, xi = shift(vr, 1), shift(vi, 1)
        for s, (ar, ai) in ((1, a1), (2, a2), (4, a4)):
            mr, mi = cmul(ar, ai, shift(xr, s), shift(xi, s))
            xr, xi = xr + mr, xi + mi
        e = 0 if reverse else sub - 1
        lr, li = cmul(a1[0], a1[1], xr[e:e + 1], xi[e:e + 1])
        return xr, xi, lr + vr[e:e + 1], li + vi[e:e + 1]

    pw_f = powers(a_ref[0:1, :], a_ref[1:2, :])
    pw_b = powers(a_ref[2:3, :], a_ref[3:4, :])
    tbr = jnp.concatenate([pw_b[4][0][sub - 1 - k:sub - k] for k in range(sub)], axis=0)
    tbi = jnp.concatenate([pw_b[4][1][sub - 1 - k:sub - k] for k in range(sub)], axis=0)

    def one_tile(row0, col0, pw, table, cr, ci, reverse):
        rows = pl.ds(pl.multiple_of(row0, sub), sub)
        xr, xi, lr, li = tile_scan(v_ref[rows, col0:col0 + w], v_ref[rows, col0 + w:col0 + 2 * w], pw, reverse)
        mr, mi = cmul(table[0], table[1], cr, ci)
        nr, ni = cmul(pw[3][0], pw[3][1], cr, ci)
        return xr + mr, xi + mi, nr + lr, ni + li

    pair = 2 * sub
    n_pairs = n_groups // 2

    def step(g, carry, b0):
        out = []
        for b in range(b0, b0 + len(carry) // 4):
            cfr, cfi, cbr, cbi = carry[4 * (b - b0):4 * (b - b0) + 4]
            rf = pl.multiple_of(b * n_chunks + g * pair, pair)
            rb = pl.multiple_of(b * n_chunks + (n_pairs - 1 - g) * pair, pair)
            ar, ai, cfr, cfi = one_tile(rf, 0, pw_f, pw_f[4], cfr, cfi, False)
            br, bi, cfr, cfi = one_tile(rf + sub, 0, pw_f, pw_f[4], cfr, cfi, False)
            s_ref[pl.ds(rf, pair), 0:w] = jnp.concatenate([ar, br], axis=0).astype(BF16)
            s_ref[pl.ds(rf, pair), w:2 * w] = jnp.concatenate([ai, bi], axis=0).astype(BF16)
            out += [cfr, cfi]
            br, bi, cbr, cbi = one_tile(rb + sub, 2 * w, pw_b, (tbr, tbi), cbr, cbi, True)
            ar, ai, cbr, cbi = one_tile(rb, 2 * w, pw_b, (tbr, tbi), cbr, cbi, True)
            s_ref[pl.ds(rb, pair), 2 * w:3 * w] = jnp.concatenate([ar, br], axis=0).astype(BF16)
            s_ref[pl.ds(rb, pair), 3 * w:4 * w] = jnp.concatenate([ai, bi], axis=0).astype(BF16)
            out += [cbr, cbi]
        return tuple(out)

    zero = jnp.zeros((1, w), F32)
    for b0 in range(0, bsz, SCAN_SEQS):
        nb = min(SCAN_SEQS, bsz - b0)
        lax.fori_loop(0, n_pairs, functools.partial(step, b0=b0), (zero,) * (4 * nb))


def _ssm_y_kernel(u_ref, s_ref, kflat_ref, cp_ref, wp_ref, til_ref, rep_ref, d_ref, y_ref, m_scr, p_scr):
    a2 = kflat_ref.shape[0]
    w = cp_ref.shape[1]

    def expand(x, e_ref):
        hi = x.astype(BF16)
        lo = (x - hi.astype(F32)).astype(BF16)
        return (jnp.dot(hi, e_ref[...], preferred_element_type=F32)
                + jnp.dot(lo, e_ref[...], preferred_element_type=F32))

    @pl.when(pl.program_id(1) == 0)
    def _():
        kflat = kflat_ref[...]
        for j in range(SSM_CHUNK):
            off = (SSM_CHUNK - 1 - j) * a2
            m_scr[j * a2:(j + 1) * a2, :] = kflat[:, off:off + PAIR_W].astype(BF16)
        for d in range(2):
            cr, ci = expand(cp_ref[2 * d], til_ref), expand(cp_ref[2 * d + 1], til_ref)
            wr, wi = expand(wp_ref[2 * d], rep_ref), expand(wp_ref[2 * d + 1], rep_ref)
            p_scr[(2 * d) * w:(2 * d + 1) * w, :] = (cr * wr - ci * wi).astype(BF16)
            p_scr[(2 * d + 1) * w:(2 * d + 2) * w, :] = (-(cr * wi + ci * wr)).astype(BF16)

    u = u_ref[...]
    y = jnp.dot(u, m_scr[...], preferred_element_type=F32)
    y = y + jnp.dot(s_ref[...], p_scr[...], preferred_element_type=F32)
    y_ref[...] = _gelu(y + d_ref[...] * u.astype(F32)).astype(BF16)


def _ssm_trunk(u, bsz, mats):
    kflat, bq, wq, cp, wp, til, rep, alpha, skip = mats
    rows = u.shape[0]
    nc = rows // bsz
    np_ = kflat.shape[0]
    pw = PAIR_W
    sw = 4 * bq.shape[3]
    tr = min(SSM_ROW_TILE, rows)
    per_pair = lambda a: pl.BlockSpec((None,) + a.shape[1:], lambda p, r: (p,) + (0,) * (a.ndim - 1))
    whole = lambda a: pl.BlockSpec(a.shape, lambda p, r: (0,) * a.ndim)
    v = pl.pallas_call(
        _ssm_v_kernel,
        grid=(np_, rows // tr),
        in_specs=[pl.BlockSpec((tr, pw), lambda p, r: (r, p)), per_pair(bq), per_pair(wq)],
        out_specs=pl.BlockSpec((tr, sw), lambda p, r: (r, p)),
        out_shape=jax.ShapeDtypeStruct((rows, np_ * sw), F32),
        scratch_shapes=[pltpu.VMEM((pw, sw), BF16)],
        compiler_params=_cparams(2),
        name="ssm_v",
    )(u, bq, wq)
    s = pl.pallas_call(
        functools.partial(_ssm_scan_kernel, n_chunks=nc, bsz=bsz),
        grid=(np_,),
        in_specs=[pl.BlockSpec((rows, sw), lambda p: (0, p)),
                  pl.BlockSpec((None, 4, LANES), lambda p: (p, 0, 0))],
        out_specs=pl.BlockSpec((rows, sw), lambda p: (0, p)),
        out_shape=jax.ShapeDtypeStruct((rows, np_ * sw), BF16),
        compiler_params=_cparams(1),
        name="ssm_scan",
    )(v, alpha)
    y = pl.pallas_call(
        _ssm_y_kernel,
        grid=(np_, rows // tr),
        in_specs=[pl.BlockSpec((tr, pw), lambda p, r: (r, p)),
                  pl.BlockSpec((tr, sw), lambda p, r: (r, p)),
                  per_pair(kflat), per_pair(cp), per_pair(wp), whole(til), whole(rep), per_pair(skip)],
        out_specs=pl.BlockSpec((tr, pw), lambda p, r: (r, p)),
        out_shape=jax.ShapeDtypeStruct((rows, np_ * pw), BF16),
        scratch_shapes=[pltpu.VMEM((pw, pw), BF16), pltpu.VMEM((sw, pw), BF16)],
        compiler_params=_cparams(2),
        name="ssm_y",
    )(u, s, kflat, cp, wp, til, rep, skip)
    return y


def _mixer(y_ref, zuv_ref, wglu_ref, bglu_ref, gssm_ref, lng_ref, lnb_ref, ws_ref, bs_ref, gsgu_ref, ya_scr):
    tm = zuv_ref.shape[0]
    n_blk = ya_scr.shape[0]
    nc = tm // SSM_CHUNK
    masks = _atom_masks(nc)
    for b in range(n_blk):
        for v in range(SSM_CHUNK // ATOMS):
            src = [y_ref[:, (ATOMS * b + pi) * PAIR_W + v * LANES:
                         (ATOMS * b + pi) * PAIR_W + (v + 1) * LANES].astype(F32)
                   for pi in range(ATOMS)]
            dst = _atom_transpose(src, masks)
            for jj in range(ATOMS):
                ya_scr[b, pl.ds(ATOMS * v + jj, nc, stride=SSM_CHUNK), :] = dst[jj]
    gl = jnp.concatenate([ya_scr[b] for b in range(n_blk)], axis=1)
    gate = jnp.dot(gl.astype(BF16), wglu_ref[...], preferred_element_type=F32) + bglu_ref[...]
    ra = _rms(gl * jax.nn.sigmoid(gate), gssm_ref[...]).astype(BF16)
    d_sgu = zuv_ref.shape[1] // 2
    u = zuv_ref[:, :d_sgu].astype(F32)
    gv = zuv_ref[:, d_sgu:].astype(F32)
    xc = gv - jnp.mean(gv, axis=-1, keepdims=True)
    v = xc * lax.rsqrt(jnp.mean(xc * xc, axis=-1, keepdims=True) + LN_EPS)
    v = (v * lng_ref[...] + lnb_ref[...]).astype(BF16)
    lo = lax.broadcasted_iota(I32, (CHUNK, LANES), 1) < (LANES // 2)
    zero = jnp.zeros((CHUNK, LANES), BF16)
    rows = []
    for c in range(tm // CHUNK):
        cols = []
        for j in range(d_sgu // LANES):
            vp = v[c * CHUNK:(c + 1) * CHUNK, j * LANES:(j + 1) * LANES]
            rhs = jnp.concatenate([jnp.where(lo, vp, zero), jnp.where(lo, zero, vp)], axis=0)
            cols.append(jnp.dot(ws_ref[j], rhs, preferred_element_type=F32))
        rows.append(jnp.concatenate(cols, axis=1) + bs_ref[...])
    s = jnp.concatenate(rows, axis=0)
    rb = _rms(u * s, gsgu_ref[...]).astype(BF16)
    return jnp.concatenate([ra, rb], axis=1)


def _sorted_rows(tile):
    return TOP_K * tile + MXU_DIM


def _route_kernel(xp_ref, xs_ref, yp_ref, ys_ref, zp_ref, zs_ref, wglu_ref, bglu_ref, gssm_ref, lng_ref,
                  lnb_ref, ws_ref, bs_ref, gsgu_ref, wout_ref, gffn_ref, whi_ref, wcat_ref, br_ref, tri_ref,
                  ltri_ref, x1_ref, xsort_ref, meta_ref, c8_ref, ya_scr, *, n_prompt_tiles, n_tiles, n_exp):
    i = pl.program_id(0)
    tm = x1_ref.shape[0]
    p_rows = xsort_ref.shape[0]

    def body(x_ref, y_ref, zuv_ref):
        mixed = _mixer(y_ref, zuv_ref, wglu_ref, bglu_ref, gssm_ref, lng_ref, lnb_ref, ws_ref, bs_ref,
                       gsgu_ref, ya_scr)
        x1 = x_ref[...] + jnp.dot(mixed, wout_ref[...], preferred_element_type=F32)
        x1_ref[...] = x1
        h2 = _rms(x1, gffn_ref[...])
        hi = h2.astype(BF16)
        lo = (h2 - hi.astype(F32)).astype(BF16)
        hw = jnp.dot(hi, wcat_ref[...], preferred_element_type=F32)
        lt = (hw[:, :LANES] + (jnp.dot(lo, whi_ref[...], preferred_element_type=F32) + hw[:, LANES:])
              + br_ref[...])
        logits = lt.T[:n_exp]
        eio = lax.broadcasted_iota(I32, (n_exp, tm), 0)
        vals, idxs = [], []
        l = logits
        for _ in range(TOP_K):
            m = jnp.max(l, axis=0, keepdims=True)
            idx = jnp.min(jnp.where(l == m, eio, n_exp), axis=0, keepdims=True)
            vals.append(m)
            idxs.append(idx)
            l = jnp.where(eio == idx, -jnp.inf, l)
        ex = [jnp.exp(v - vals[0]) for v in vals]
        den = ex[0] + ex[1] + ex[2] + ex[3]
        ws = [e / den for e in ex]
        hot = [eio == idx for idx in idxs]
        cnt = sum(h.astype(F32) for h in hot)
        prefix = jnp.dot(cnt.astype(BF16), tri_ref[...], preferred_element_type=F32)
        c = jnp.sum(cnt, axis=1, keepdims=True)
        c8 = jnp.floor((c + (SUBLANES - 1)) * (1.0 / SUBLANES)) * SUBLANES
        c8b = jnp.broadcast_to(c8, (n_exp, LANES))
        c8_ref[...] = c8b
        run0 = jnp.dot(ltri_ref[...], c8b.astype(BF16), preferred_element_type=F32)[:, 0:1]
        base = run0 + prefix
        pos = [jnp.sum(jnp.where(h, base, 0.0), axis=0, keepdims=True) for h in hot]
        meta = jnp.concatenate(ws + pos + [jnp.zeros((LANES - 2 * TOP_K, tm), F32)], axis=0)
        meta_ref[...] = meta.T
        rio = lax.broadcasted_iota(I32, (SORT_CHUNK, tm), 0).astype(F32).astype(BF16)
        one = jnp.ones((SORT_CHUNK, tm), BF16)
        zero = jnp.zeros((SORT_CHUNK, tm), BF16)
        for r in range(p_rows // SORT_CHUNK):
            lo_r = float(r * SORT_CHUNK)
            rel = [jnp.where((p >= lo_r) & (p < lo_r + SORT_CHUNK), p - lo_r, -1.0).astype(BF16) for p in pos]
            sel = (rio == rel[0]) | (rio == rel[1]) | (rio == rel[2]) | (rio == rel[3])
            srt = jnp.dot(jnp.where(sel, one, zero), hi, preferred_element_type=F32)
            xsort_ref[r * SORT_CHUNK:(r + 1) * SORT_CHUNK, :] = _pack_bf16_pairs(srt)

    @pl.when(i < n_prompt_tiles)
    def _():
        body(xp_ref, yp_ref, zp_ref)

    @pl.when((i >= n_prompt_tiles) & (i < n_tiles))
    def _():
        body(xs_ref, ys_ref, zs_ref)

    @pl.when(i == n_tiles)
    def _():
        xsort_ref[...] = jnp.zeros_like(xsort_ref)


def _dual_specs(tile, width, n_prompt_tiles, n_tiles):
    last_p = n_prompt_tiles - 1
    last_s = n_tiles - n_prompt_tiles - 1
    sp = pl.BlockSpec((tile, width), lambda i: (jnp.minimum(i, last_p), 0))
    ss = pl.BlockSpec((tile, width), lambda i: (jnp.clip(i - n_prompt_tiles, 0, last_s), 0))
    return sp, ss


def _route(xp, xs, y_p, y_s, zuv_p, zuv_s, mix_params, w_out, g_ffn, wr_hi, wr_cat, br_pad, n_exp):
    tp, d = xp.shape
    t = tp + xs.shape[0]
    tm = TOKEN_TILE
    npt = tp // tm
    nt = t // tm
    p_rows = _sorted_rows(tm)
    d_ssm = mix_params[0].shape[0]
    iota = lambda n, ax: lax.broadcasted_iota(I32, (n, n), ax)
    tri = (iota(tm, 0) < iota(tm, 1)).astype(BF16)
    ltri = (iota(n_exp, 1) < iota(n_exp, 0)).astype(BF16)
    const = lambda shape: pl.BlockSpec(shape, lambda i: (0,) * len(shape))
    tile = lambda i: jnp.minimum(i, nt - 1)
    dual = lambda rows, width: _dual_specs(rows, width, npt, nt)
    return pl.pallas_call(
        functools.partial(_route_kernel, n_prompt_tiles=npt, n_tiles=nt, n_exp=n_exp),
        grid=(nt + 1,),
        in_specs=[*dual(tm, d), *dual(tm // SSM_CHUNK, y_p.shape[1]), *dual(tm, zuv_p.shape[1]),
                  *[const(a.shape) for a in mix_params],
                  const(w_out.shape), const((1, d)), const(wr_hi.shape), const(wr_cat.shape),
                  const((1, LANES)), const((tm, tm)), const((n_exp, n_exp))],
        out_specs=[pl.BlockSpec((tm, d), lambda i: (tile(i), 0)),
                   pl.BlockSpec((p_rows, d // 2), lambda i: (i, 0)),
                   pl.BlockSpec((tm, LANES), lambda i: (tile(i), 0)),
                   pl.BlockSpec((None, n_exp, LANES), lambda i: (tile(i), 0, 0))],
        out_shape=[jax.ShapeDtypeStruct((t, d), F32),
                   jax.ShapeDtypeStruct(((nt + 1) * p_rows, d // 2), U32),
                   jax.ShapeDtypeStruct((t, LANES), F32),
                   jax.ShapeDtypeStruct((nt, n_exp, LANES), F32)],
        scratch_shapes=[pltpu.VMEM((d_ssm // LANES, tm, LANES), F32)],
        compiler_params=_cparams(),
        name="route",
    )(xp, xs, y_p, y_s, zuv_p, zuv_s, *mix_params, w_out, g_ffn, wr_hi, wr_cat, br_pad, tri, ltri)


def _slot_tables(c8, p_rows, n_blocks):
    nt, n_exp = c8.shape
    run0 = jnp.cumsum(c8, axis=1) - c8
    seg_len = c8.T
    cum = jnp.cumsum(seg_len, axis=1)
    tot = cum[:, -1]
    padded = (tot + MOE_TILE - 1) // MOE_TILE * MOE_TILE
    ends = jnp.cumsum(padded)
    starts = ends - padded
    n_used = (ends[-1] // MOE_TILE).astype(I32)
    blk = jnp.arange(n_blocks, dtype=I32)
    be = jnp.minimum(jnp.sum(ends[None, :] <= (blk * MOE_TILE)[:, None], axis=1), n_exp - 1).astype(I32)
    block_e = jnp.where(blk < n_used, be, be[jnp.maximum(n_used - 1, 0)])
    onehot = (be[:, None] == jnp.arange(n_exp, dtype=I32)[None, :]).astype(F32)
    pick = lambda tbl: jnp.dot(onehot, tbl.astype(F32), precision=lax.Precision.HIGHEST)
    run_end = pick(cum)
    run_beg = run_end - pick(seg_len)
    shift = pick((jnp.arange(nt, dtype=I32) * p_rows)[None, :] + run0.T) - run_beg
    grow = (blk[:, None] * MOE_TILE + jnp.arange(MOE_GROUPS, dtype=I32)[None, :] * SUBLANES).astype(F32)
    rel = grow - pick(starts[:, None])
    inside = (run_beg[:, None, :] <= rel[:, :, None]) & (rel[:, :, None] < run_end[:, None, :])
    src_row = (rel + jnp.sum(jnp.where(inside, shift[:, None, :], 0.0), axis=2)).astype(I32)
    valid = jnp.any(inside, axis=2) & (blk < n_used)[:, None]
    scratch0 = nt * p_rows // SUBLANES
    zero_group = scratch0 + 2 * MOE_GROUPS
    g_in = jnp.where(valid, src_row // SUBLANES, zero_group).astype(I32)
    g_scr = scratch0 + (blk[:, None] % 2) * MOE_GROUPS + jnp.arange(MOE_GROUPS, dtype=I32)[None, :]
    g_out = jnp.where(valid, src_row // SUBLANES, g_scr).astype(I32)
    return block_e, n_used, g_in, g_out


def _moe_kernel(be_ref, nu_ref, gin_ref, gnext_ref, gout_ref, xs_hbm, wgu_ref, bgu_ref, wd_ref, bd_ref,
                eo_hbm, xbuf, obuf, wgu_s, wd_s, isem, osem):
    j = pl.program_id(0)
    nu = nu_ref[0]
    f = wd_ref.shape[0]

    @pl.when((j < nu) & ((j == 0) | (be_ref[j] != be_ref[jnp.maximum(j - 1, 0)])))
    def _():
        for c in range(0, wgu_ref.shape[0], CAST_ROWS):
            wgu_s[c:c + CAST_ROWS, :] = wgu_ref[c:c + CAST_ROWS, :].astype(BF16)
        for c in range(0, wd_ref.shape[0], CAST_ROWS):
            wd_s[c:c + CAST_ROWS, :] = wd_ref[c:c + CAST_ROWS, :].astype(BF16)

    def in_copy(tbl_ref, r, sl):
        return pltpu.make_async_copy(xs_hbm.at[tbl_ref[0, r]], xbuf.at[sl, r], isem.at[sl])

    def out_copy(r, sl):
        return pltpu.make_async_copy(obuf.at[sl, r], eo_hbm.at[gout_ref[0, r]], osem.at[sl])

    def wait_in(sl):
        pltpu.make_async_copy(xs_hbm.at[pl.ds(0, MOE_GROUPS)], xbuf.at[sl], isem.at[sl]).wait()

    def wait_out(sl):
        pltpu.make_async_copy(obuf.at[sl], eo_hbm.at[pl.ds(0, MOE_GROUPS)], osem.at[sl]).wait()

    @pl.when(j == 0)
    def _():
        for r in range(MOE_GROUPS):
            in_copy(gin_ref, r, 0).start()

    def step(slot):
        other = 1 - slot

        @pl.when(j + 1 < nu)
        def _():
            for r in range(MOE_GROUPS):
                in_copy(gnext_ref, r, other).start()

        wait_in(slot)

        @pl.when(j >= 2)
        def _():
            wait_out(slot)

        x = _unpack_bf16_pairs(xbuf[slot].reshape(MOE_TILE, xbuf.shape[-1]))
        gate = jnp.dot(x, wgu_s[:, :f], preferred_element_type=F32) + bgu_ref[:, :f]
        gate = jnp.minimum(gate, SWIGLU_LIMIT)
        sg = gate * jax.nn.sigmoid(SWIGLU_ALPHA * gate)
        up = jnp.dot(x, wgu_s[:, f:], preferred_element_type=F32) + bgu_ref[:, f:]
        act = sg * (jnp.clip(up, -SWIGLU_LIMIT, SWIGLU_LIMIT) + 1.0)
        out = jnp.dot(act.astype(BF16), wd_s[...], preferred_element_type=F32) + bd_ref[...]
        obuf[slot] = _pack_bf16_pairs(out.astype(BF16).astype(F32)).reshape(obuf.shape[1:])
        for r in range(MOE_GROUPS):
            out_copy(r, slot).start()

        @pl.when(j == nu - 1)
        def _():
            wait_out(slot)

            @pl.when(j >= 1)
            def _():
                wait_out(other)

    for parity in range(2):
        pl.when((j < nu) & (lax.rem(j, 2) == parity))(functools.partial(step, parity))


def _moe(block_e, n_used, g_in, g_out, xsorted, wgu, bgu, wd, bd):
    n_blocks = block_e.shape[0]
    dh = xsorted.shape[1]
    d = 2 * dh
    f2 = wgu.shape[2]
    f = wd.shape[1]
    g_in3 = g_in.reshape(n_blocks, 1, MOE_GROUPS)
    g_next3 = jnp.concatenate([g_in3[1:], g_in3[:1]], axis=0)
    g_out3 = g_out.reshape(n_blocks, 1, MOE_GROUPS)
    xs3 = xsorted.reshape(xsorted.shape[0] // SUBLANES, SUBLANES, dh)
    tbl = pl.BlockSpec((None, 1, MOE_GROUPS), lambda j, be, nu: (j, 0, 0), memory_space=pltpu.SMEM)
    return pl.pallas_call(
        _moe_kernel,
        grid_spec=pltpu.PrefetchScalarGridSpec(
            num_scalar_prefetch=2, grid=(n_blocks,),
            in_specs=[tbl, tbl, tbl,
                      pl.BlockSpec(memory_space=pl.ANY),
                      pl.BlockSpec((None, d, f2), lambda j, be, nu: (be[j], 0, 0)),
                      pl.BlockSpec((None, 1, f2), lambda j, be, nu: (be[j], 0, 0)),
                      pl.BlockSpec((None, f, d), lambda j, be, nu: (be[j], 0, 0)),
                      pl.BlockSpec((None, 1, d), lambda j, be, nu: (be[j], 0, 0))],
            out_specs=pl.BlockSpec(memory_space=pl.ANY),
            scratch_shapes=[pltpu.VMEM((2, MOE_GROUPS, SUBLANES, dh), U32),
                            pltpu.VMEM((2, MOE_GROUPS, SUBLANES, dh), U32),
                            pltpu.VMEM((d, f2), BF16), pltpu.VMEM((f, d), BF16),
                            pltpu.SemaphoreType.DMA((2,)), pltpu.SemaphoreType.DMA((2,))]),
        out_shape=jax.ShapeDtypeStruct(xs3.shape, U32),
        input_output_aliases={5: 0},
        compiler_params=_cparams(),
        name="moe",
    )(block_e, n_used, g_in3, g_next3, g_out3, xs3, wgu, bgu, wd, bd).reshape(xsorted.shape)


def _combine_kernel(meta_ref, x1_ref, g_ref, eo_ref, y_ref):
    tm = x1_ref.shape[0]
    p_rows = eo_ref.shape[0]
    lane = lax.broadcasted_iota(I32, (tm, SORT_CHUNK), 1).astype(F32).astype(BF16)
    ws = [meta_ref[:, k:k + 1].astype(BF16) for k in range(TOP_K)]
    pos = [meta_ref[:, TOP_K + k:TOP_K + k + 1] for k in range(TOP_K)]
    acc = x1_ref[...]
    for r in range(p_rows // SORT_CHUNK):
        buf = _unpack_bf16_pairs(eo_ref[r * SORT_CHUNK:(r + 1) * SORT_CHUNK, :])
        lo_r = float(r * SORT_CHUNK)
        wm = jnp.zeros((tm, SORT_CHUNK), BF16)
        for k in range(TOP_K):
            rel = jnp.where((pos[k] >= lo_r) & (pos[k] < lo_r + SORT_CHUNK), pos[k] - lo_r, -1.0).astype(BF16)
            wm = jnp.where(lane == rel, ws[k], wm)
        acc = acc + jnp.dot(wm, buf, preferred_element_type=F32)
    y_ref[...] = _rms(acc, g_ref[...])


def _combine(meta, x1, g_final, eo, tile_off, n_tokens):
    d = x1.shape[1]
    tm = TOKEN_TILE
    p_rows = _sorted_rows(tm)
    return pl.pallas_call(
        _combine_kernel,
        grid=(n_tokens // tm,),
        in_specs=[pl.BlockSpec((tm, LANES), lambda i: (i + tile_off, 0)),
                  pl.BlockSpec((tm, d), lambda i: (i + tile_off, 0)),
                  pl.BlockSpec((1, d), lambda i: (0, 0)),
                  pl.BlockSpec((p_rows, d // 2), lambda i: (i + tile_off, 0))],
        out_specs=pl.BlockSpec((tm, d), lambda i: (i, 0)),
        out_shape=jax.ShapeDtypeStruct((n_tokens, d), F32),
        compiler_params=_cparams(),
        name="combine",
    )(meta, x1, g_final, eo)


def kernel(x_prompt, x_sample, g_mix, w_in, ssm_lam_re, ssm_lam_im, ssm_log_dt, ssm_b_re, ssm_b_im,
           ssm_c_re, ssm_c_im, ssm_d, w_glu, b_glu, sgu_ln_g, sgu_ln_b, sgu_w_s, sgu_b_s,
           g_out_ssm, g_out_sgu, w_out, g_ffn, w_router, b_router, w_gate_up, b_gate_up,
           w_down, b_down, g_final):
    assert g_mix.shape[0] == 1, "single-layer trunk"
    bp, lp, d = x_prompt.shape
    bs, ls, _ = x_sample.shape
    tp, ts = bp * lp, bs * ls
    t = tp + ts
    d_ssm = ssm_d.shape[1]
    d_sgu = sgu_ln_g.shape[1]
    n_exp = w_router.shape[2]
    assert all(n % tile == 0 for n in (tp, ts) for tile in (SEQ_TILE, TOKEN_TILE))
    assert lp % CHUNK == 0 and ls % CHUNK == 0 and TOKEN_TILE % CHUNK == 0 and CHUNK % SSM_CHUNK == 0
    assert SEQ_TILE % SSM_CHUNK == 0
    assert SSM_CHUNK * SSM_GROUP == MXU_DIM and 2 * ssm_lam_re.shape[-1] == LANES
    assert d_sgu // SGU_HEADS == LANES // 2 and n_exp <= LANES
    assert n_exp * (SUBLANES - 1) <= MXU_DIM and _sorted_rows(TOKEN_TILE) % SORT_CHUNK == 0
    assert (2 * MOE_GROUPS + 1) * SUBLANES <= _sorted_rows(TOKEN_TILE)

    xp = x_prompt.reshape(tp, d)
    xs = x_sample.reshape(ts, d)
    row = lambda a: a.reshape(1, -1).astype(F32)

    w_in_b = w_in[0].astype(BF16)
    u_p, zuv_p = _inproj(xp, row(g_mix[0]), w_in_b, d_ssm)
    u_s, zuv_s = _inproj(xs, row(g_mix[0]), w_in_b, d_ssm)

    mats = _ssm_matrices(ssm_lam_re[0], ssm_lam_im[0], ssm_log_dt[0], ssm_b_re[0], ssm_b_im[0],
                         ssm_c_re[0], ssm_c_im[0], ssm_d[0])
    y_p = _ssm_trunk(u_p, bp, mats)
    y_s = _ssm_trunk(u_s, bs, mats)

    ws = sgu_w_s[0]
    ws_pairs = jnp.concatenate([ws[0::2], ws[1::2]], axis=2).astype(BF16)
    bias_s = jnp.repeat(sgu_b_s[0].T, d_sgu // SGU_HEADS, axis=1).astype(F32)
    mix_params = (w_glu[0].astype(BF16), row(b_glu[0]), row(g_out_ssm[0]), row(sgu_ln_g[0]), row(sgu_ln_b[0]),
                  ws_pairs, bias_s, row(g_out_sgu[0]))

    wr_pad = jnp.pad(w_router[0].astype(F32), ((0, 0), (0, LANES - n_exp)))
    wr_hi = wr_pad.astype(BF16)
    wr_cat = jnp.concatenate([wr_hi, (wr_pad - wr_hi.astype(F32)).astype(BF16)], axis=1)
    br_pad = jnp.pad(b_router[0].astype(F32), (0, LANES - n_exp)).reshape(1, LANES)
    x1, xsorted, meta, c8 = _route(xp, xs, y_p, y_s, zuv_p, zuv_s, mix_params, w_out[0].astype(BF16),
                                   row(g_ffn[0]), wr_hi, wr_cat, br_pad, n_exp)

    nt = t // TOKEN_TILE
    p_rows = _sorted_rows(TOKEN_TILE)
    max_rows = t * TOP_K + nt * n_exp * (SUBLANES - 1) + n_exp * (MOE_TILE - 1)
    n_blocks = -(-max_rows // MOE_TILE)
    block_e, n_used, g_in, g_out = _slot_tables(c8[:, :, 0].astype(I32), p_rows, n_blocks)
    eo = _moe(block_e, n_used.reshape(1), g_in, g_out, xsorted, w_gate_up[0],
              b_gate_up[0][:, None, :].astype(F32), w_down[0], b_down[0][:, None, :].astype(F32))

    gf = row(g_final)
    y_prompt = _combine(meta, x1, gf, eo, 0, tp)
    y_sample = _combine(meta, x1, gf, eo, tp // TOKEN_TILE, ts)
    return y_prompt.reshape(bp, lp, d), y_sample.reshape(bs, ls, d)
```

```python
import functools
import math

import jax
import jax.numpy as jnp
from jax import lax
from jax.experimental import pallas as pl
from jax.experimental.pallas import tpu as pltpu

F32 = jnp.float32
BF16 = jnp.bfloat16
I32 = jnp.int32
U32 = jnp.uint32

SSM_GROUP = 16
SGU_HEADS = 8
CHUNK = 128
TOP_K = 4
SWIGLU_LIMIT = 7.0
SWIGLU_ALPHA = 1.702
RMS_EPS = 1e-6
LN_EPS = 1e-5

LANES = 128
SUBLANES = 8
MXU_DIM = 256
SSM_CHUNK = MXU_DIM // SSM_GROUP
PAIR_W = 2 * SSM_CHUNK * SSM_GROUP
ATOM = 2 * SSM_GROUP
ATOMS = LANES // ATOM

TOKEN_TILE = 512
SEQ_TILE = 1024
MOE_TILE = 768
MOE_GROUPS = MOE_TILE // SUBLANES
SORT_CHUNK = 256
CAST_ROWS = 128
SSM_ROW_TILE = 2048
SCAN_SEQS = 4
VMEM_LIMIT = 56 * 1024 * 1024


def _cparams(n_axes=1, vmem=None):
    return pltpu.CompilerParams(
        dimension_semantics=("arbitrary",) * n_axes,
        vmem_limit_bytes=vmem if vmem is not None else VMEM_LIMIT,
    )


def _rms(x, g):
    return x * lax.rsqrt(jnp.mean(x * x, axis=-1, keepdims=True) + RMS_EPS) * g


def _gelu(x):
    return x * (lax.erf(x * (1.0 / math.sqrt(2.0))) + 1.0) * 0.5


def _atom_masks(rows):
    lane = lax.broadcasted_iota(I32, (rows, LANES), 1)
    return [(lane >= a * ATOM) & (lane < (a + 1) * ATOM) for a in range(ATOMS)]


def _atom_transpose(src, masks):
    dst = []
    for b in range(ATOMS):
        acc = None
        for a in range(ATOMS):
            r = (a - b) % ATOMS
            piece = src[a] if r == 0 else pltpu.roll(src[a], ATOM * r, axis=1)
            acc = piece if acc is None else jnp.where(masks[a], piece, acc)
        dst.append(acc)
    return dst


def _pack_bf16_pairs(x):
    w = x.shape[1] // 2
    lo = lax.bitcast_convert_type(x[:, :w], U32) >> 16
    hi = lax.bitcast_convert_type(x[:, w:], U32) & jnp.uint32(0xFFFF0000)
    return hi | lo


def _unpack_bf16_pairs(u):
    lo = lax.bitcast_convert_type(u << 16, F32)
    hi = lax.bitcast_convert_type(u & jnp.uint32(0xFFFF0000), F32)
    return jnp.concatenate([lo.astype(BF16), hi.astype(BF16)], axis=1)


def _inproj_kernel(x_ref, g_ref, w_ref, u_ref, zuv_ref, za_scr, *, d_ssm):
    tm = x_ref.shape[0]
    nc = tm // SSM_CHUNK
    h = _rms(x_ref[...], g_ref[...])
    z = jnp.dot(h.astype(BF16), w_ref[...], preferred_element_type=F32)
    zuv_ref[...] = _gelu(z[:, d_ssm:]).astype(BF16)
    n_blk = d_ssm // LANES
    for b in range(n_blk):
        za_scr[b] = z[:, b * LANES:(b + 1) * LANES]
    masks = _atom_masks(nc)
    n_quads = SSM_CHUNK // ATOMS
    for b in range(n_blk):
        for v in range(n_quads):
            src = [za_scr[b, pl.ds(ATOMS * v + jj, nc, stride=SSM_CHUNK), :] for jj in range(ATOMS)]
            dst = _atom_transpose(src, masks)
            for pi in range(ATOMS):
                c0 = (ATOMS * b + pi) * PAIR_W + v * LANES
                u_ref[:, c0:c0 + LANES] = dst[pi].astype(BF16)


def _inproj(x, g_mix, w_in, d_ssm):
    t, d = x.shape
    tm = SEQ_TILE
    d_in = w_in.shape[1]
    nc = tm // SSM_CHUNK
    uw = d_ssm * SSM_CHUNK
    return pl.pallas_call(
        functools.partial(_inproj_kernel, d_ssm=d_ssm),
        grid=(t // tm,),
        in_specs=[pl.BlockSpec((tm, d), lambda i: (i, 0)),
                  pl.BlockSpec((1, d), lambda i: (0, 0)),
                  pl.BlockSpec((d, d_in), lambda i: (0, 0))],
        out_specs=[pl.BlockSpec((nc, uw), lambda i: (i, 0)),
                   pl.BlockSpec((tm, d_in - d_ssm), lambda i: (i, 0))],
        out_shape=[jax.ShapeDtypeStruct((t // SSM_CHUNK, uw), BF16),
                   jax.ShapeDtypeStruct((t, d_in - d_ssm), BF16)],
        scratch_shapes=[pltpu.VMEM((d_ssm // LANES, tm, LANES), F32)],
        compiler_params=_cparams(),
        name="inproj",
    )(x, g_mix, w_in)


def _ssm_matrices(lam_re, lam_im, log_dt, b_re, b_im, c_re, c_im, d_skip):
    hp = lax.Precision.HIGHEST
    _, g, n = lam_re.shape
    p = b_re.shape[-1]
    lc = SSM_CHUNK
    np_ = g // 2
    dt = jnp.exp(log_dt)[..., None]
    mag = jnp.exp(lam_re * dt)
    ar = mag * jnp.cos(lam_im * dt)
    ai = mag * jnp.sin(lam_im * dt)
    den = lam_re * lam_re + lam_im * lam_im
    nr = ar - 1.0
    fr = (nr * lam_re + ai * lam_im) / den
    fi = (ai * lam_re - nr * lam_im) / den
    bbr = fr[..., None] * b_re - fi[..., None] * b_im
    bbi = fr[..., None] * b_im + fi[..., None] * b_re

    prs, pis = [jnp.ones_like(ar)], [jnp.zeros_like(ai)]
    for _ in range(lc):
        pr, pi = prs[-1], pis[-1]
        prs.append(pr * ar - pi * ai)
        pis.append(pr * ai + pi * ar)
    pw_r, pw_i = jnp.stack(prs), jnp.stack(pis)

    lane_k = jnp.arange(lc * p)
    til_k = (lane_k[None, :] % p == jnp.arange(p)[:, None]).astype(F32)
    rep_k = (lane_k[None, :] // p == jnp.arange(lc)[:, None]).astype(F32)
    ex = lambda x, e: jnp.einsum('dgnk,kx->dgnx', x, e, precision=hp)
    ct_r = ex(jnp.transpose(c_re, (0, 1, 3, 2)), til_k)
    ct_i = ex(jnp.transpose(c_im, (0, 1, 3, 2)), til_k)
    at_r = ex(jnp.transpose(pw_r[:lc], (1, 2, 3, 0)), rep_k)
    at_i = ex(jnp.transpose(pw_i[:lc], (1, 2, 3, 0)), rep_k)
    kk = (jnp.einsum('dgnq,dgnx->dgqx', bbr, ct_r * at_r - ct_i * at_i, precision=hp)
          - jnp.einsum('dgnq,dgnx->dgqx', bbi, ct_r * at_i + ct_i * at_r, precision=hp))
    kk = kk.reshape(2, g, p, lc, p)
    kf, kb = kk[0], kk[1]
    eye = jnp.eye(2, dtype=F32)
    a2 = p * 2

    k_all = jnp.concatenate([kb[:, :, :0:-1], (kf[:, :, :1] + kb[:, :, :1]), kf[:, :, 1:]], axis=2)
    k_all = k_all.reshape(np_, 2, p, 2 * lc - 1, p)
    kflat = jnp.einsum('nsqup,st->nsqutp', k_all, eye).reshape(np_, a2, (2 * lc - 1) * a2)
    kflat = jnp.pad(kflat, ((0, 0), (0, 0), (0, a2)))

    def atoms_q(x):
        x = jnp.transpose(x, (0, 1, 3, 2)).reshape(2, np_, 2, p, n)
        return jnp.einsum('dnsqm,st->dnsqtm', x, eye).reshape(2, np_, a2, 2 * n)
    bq_r, bq_i = atoms_q(bbr), atoms_q(bbi)
    bq = jnp.stack([bq_r[0], bq_i[0], bq_r[1], bq_i[1]], axis=1)
    ef = lc - 1 - jnp.arange(lc)
    eb = jnp.arange(lc)
    rows_q = lambda w, e, d: jnp.transpose(w[e, d].reshape(lc, np_, 2 * n), (1, 0, 2))
    wq = jnp.stack([rows_q(pw_r, ef, 0), rows_q(pw_i, ef, 0),
                    rows_q(pw_r, eb, 1), rows_q(pw_i, eb, 1)], axis=1)

    def atoms_p(x):
        x = jnp.transpose(x, (0, 1, 3, 2)).reshape(2, np_, 2, n, p)
        return jnp.einsum('dnsmp,st->dnsmtp', x, eye).reshape(2, np_, 2 * n, a2)
    cp_r, cp_i = atoms_p(c_re), atoms_p(c_im)
    cp = jnp.stack([cp_r[0], cp_i[0], cp_r[1], cp_i[1]], axis=1)
    pf_e = jnp.arange(lc) + 1
    pb_e = lc - jnp.arange(lc)
    cols_p = lambda w, e, d: jnp.transpose(w[e, d].reshape(lc, np_, 2 * n), (1, 2, 0))
    wp = jnp.stack([cols_p(pw_r, pf_e, 0), cols_p(pw_i, pf_e, 0),
                    cols_p(pw_r, pb_e, 1), cols_p(pw_i, pb_e, 1)], axis=1)
    lane = jnp.arange(lc * a2)
    til = (lane[None, :] % a2 == jnp.arange(a2)[:, None]).astype(BF16)
    rep = (lane[None, :] // a2 == jnp.arange(lc)[:, None]).astype(BF16)

    al = jnp.stack([pw_r[lc, 0], pw_i[lc, 0], pw_r[lc, 1], pw_i[lc, 1]])
    alpha = jnp.transpose(al.reshape(4, np_, 2 * n), (1, 0, 2))
    skip = jnp.broadcast_to(d_skip.reshape(np_, 1, a2), (np_, lc, a2)).reshape(np_, 1, lc * a2)
    return kflat, bq, wq, cp, wp, til, rep, alpha, skip.astype(F32)


def _ssm_v_kernel(u_ref, bq_ref, wq_ref, v_ref, q_scr):
    a2 = bq_ref.shape[1]
    w = bq_ref.shape[2]

    @pl.when(pl.program_id(1) == 0)
    def _():
        for d in range(2):
            br, bi = bq_ref[2 * d], bq_ref[2 * d + 1]
            for j in range(SSM_CHUNK):
                wr = wq_ref[2 * d, j:j + 1, :]
                wi = wq_ref[2 * d + 1, j:j + 1, :]
                q_scr[j * a2:(j + 1) * a2, (2 * d) * w:(2 * d + 1) * w] = (wr * br - wi * bi).astype(BF16)
                q_scr[j * a2:(j + 1) * a2, (2 * d + 1) * w:(2 * d + 2) * w] = (wr * bi + wi * br).astype(BF16)

    v_ref[...] = jnp.dot(u_ref[...], q_scr[...], preferred_element_type=F32)


def _ssm_scan_kernel(v_ref, a_ref, s_ref, *, n_chunks, bsz):
    w = LANES
    sub = SUBLANES
    n_groups = n_chunks // sub
    row = lax.broadcasted_iota(I32, (sub, w), 0)

    def cmul(ar, ai, xr, xi):
        return ar * xr - ai * xi, ar * xi + ai * xr

    def powers(ar, ai):
        p2 = cmul(ar, ai, ar, ai)
        p4 = cmul(*p2, *p2)
        p8 = cmul(*p4, *p4)
        p3 = cmul(*p2, ar, ai)
        p5 = cmul(*p4, ar, ai)
        p6 = cmul(*p4, *p2)
        p7 = cmul(*p4, *p3)
        seq = [(jnp.ones_like(ar), jnp.zeros_like(ai)), (ar, ai), p2, p3, p4, p5, p6, p7]
        tr = jnp.concatenate([s[0] for s in seq], axis=0)
        ti = jnp.concatenate([s[1] for s in seq], axis=0)
        return (ar, ai), p2, p4, p8, (tr, ti)

    def tile_scan(vr, vi, pw, reverse):
        a1, a2, a4, _, _ = pw

        def shift(x, s):
            if reverse:
                return jnp.where(row < sub - s, pltpu.roll(x, sub - s, axis=0), 0.0)
            return jnp.where(row >= s, pltpu.roll(x, s, axis=0), 0.0)

        xr, xi = shift(vr, 1), shift(vi, 1)
        for s, (ar, ai) in ((1, a1), (2, a2), (4, a4)):
            mr, mi = cmul(ar, ai, shift(xr, s), shift(xi, s))
            xr, xi = xr + mr, xi + mi
        e = 0 if reverse else sub - 1
        lr, li = cmul(a1[0], a1[1], xr[e:e + 1], xi[e:e + 1])
        return xr, xi, lr + vr[e:e + 1], li + vi[e:e + 1]

    pw_f = powers(a_ref[0:1, :], a_ref[1:2, :])
    pw_b = powers(a_ref[2:3, :], a_ref[3:4, :])
    tbr = jnp.concatenate([pw_b[4][0][sub - 1 - k:sub - k] for k in range(sub)], axis=0)
    tbi = jnp.concatenate([pw_b[4][1][sub - 1 - k:sub - k] for k in range(sub)], axis=0)

    def one_tile(row0, col0, pw, table, cr, ci, reverse):
        rows = pl.ds(pl.multiple_of(row0, sub), sub)
        xr, xi, lr, li = tile_scan(v_ref[rows, col0:col0 + w], v_ref[rows, col0 + w:col0 + 2 * w], pw, reverse)
        mr, mi = cmul(table[0], table[1], cr, ci)
        nr, ni = cmul(pw[3][0], pw[3][1], cr, ci)
        return xr + mr, xi + mi, nr + lr, ni + li

    pair = 2 * sub
    n_pairs = n_groups // 2

    def step(g, carry, b0):
        out = []
        for b in range(b0, b0 + len(carry) // 4):
            cfr, cfi, cbr, cbi = carry[4 * (b - b0):4 * (b - b0) + 4]
            rf = pl.multiple_of(b * n_chunks + g * pair, pair)
            rb = pl.multiple_of(b * n_chunks + (n_pairs - 1 - g) * pair, pair)
            ar, ai, cfr, cfi = one_tile(rf, 0, pw_f, pw_f[4], cfr, cfi, False)
            br, bi, cfr, cfi = one_tile(rf + sub, 0, pw_f, pw_f[4], cfr, cfi, False)
            s_ref[pl.ds(rf, pair), 0:w] = jnp.concatenate([ar, br], axis=0).astype(BF16)
            s_ref[pl.ds(rf, pair), w:2 * w] = jnp.concatenate([ai, bi], axis=0).astype(BF16)
            out += [cfr, cfi]
            br, bi, cbr, cbi = one_tile(rb + sub, 2 * w, pw_b, (tbr, tbi), cbr, cbi, True)
            ar, ai, cbr, cbi = one_tile(rb, 2 * w, pw_b, (tbr, tbi), cbr, cbi, True)
            s_ref[pl.ds(rb, pair), 2 * w:3 * w] = jnp.concatenate([ar, br], axis=0).astype(BF16)
            s_ref[pl.ds(rb, pair), 3 * w:4 * w] = jnp.concatenate([ai, bi], axis=0).astype(BF16)
            out += [cbr, cbi]
        return tuple(out)

    zero = jnp.zeros((1, w), F32)
    for b0 in range(0, bsz, SCAN_SEQS):
        nb = min(SCAN_SEQS, bsz - b0)
        lax.fori_loop(0, n_pairs, functools.partial(step, b0=b0), (zero,) * (4 * nb))


def _ssm_y_kernel(u_ref, s_ref, kflat_ref, cp_ref, wp_ref, til_ref, rep_ref, d_ref, y_ref, m_scr, p_scr):
    a2 = kflat_ref.shape[0]
    w = cp_ref.shape[1]

    def expand(x, e_ref):
        hi = x.astype(BF16)
        lo = (x - hi.astype(F32)).astype(BF16)
        return (jnp.dot(hi, e_ref[...], preferred_element_type=F32)
                + jnp.dot(lo, e_ref[...], preferred_element_type=F32))

    @pl.when(pl.program_id(1) == 0)
    def _():
        kflat = kflat_ref[...]
        for j in range(SSM_CHUNK):
            off = (SSM_CHUNK - 1 - j) * a2
            m_scr[j * a2:(j + 1) * a2, :] = kflat[:, off:off + PAIR_W].astype(BF16)
        for d in range(2):
            cr, ci = expand(cp_ref[2 * d], til_ref), expand(cp_ref[2 * d + 1], til_ref)
            wr, wi = expand(wp_ref[2 * d], rep_ref), expand(wp_ref[2 * d + 1], rep_ref)
            p_scr[(2 * d) * w:(2 * d + 1) * w, :] = (cr * wr - ci * wi).astype(BF16)
            p_scr[(2 * d + 1) * w:(2 * d + 2) * w, :] = (-(cr * wi + ci * wr)).astype(BF16)

    u = u_ref[...]
    y = jnp.dot(u, m_scr[...], preferred_element_type=F32)
    y = y + jnp.dot(s_ref[...], p_scr[...], preferred_element_type=F32)
    y_ref[...] = _gelu(y + d_ref[...] * u.astype(F32)).astype(BF16)


def _ssm_trunk(u, bsz, mats):
    kflat, bq, wq, cp, wp, til, rep, alpha, skip = mats
    rows = u.shape[0]
    nc = rows // bsz
    np_ = kflat.shape[0]
    pw = PAIR_W
    sw = 4 * bq.shape[3]
    tr = min(SSM_ROW_TILE, rows)
    per_pair = lambda a: pl.BlockSpec((None,) + a.shape[1:], lambda p, r: (p,) + (0,) * (a.ndim - 1))
    whole = lambda a: pl.BlockSpec(a.shape, lambda p, r: (0,) * a.ndim)
    v = pl.pallas_call(
        _ssm_v_kernel,
        grid=(np_, rows // tr),
        in_specs=[pl.BlockSpec((tr, pw), lambda p, r: (r, p)), per_pair(bq), per_pair(wq)],
        out_specs=pl.BlockSpec((tr, sw), lambda p, r: (r, p)),
        out_shape=jax.ShapeDtypeStruct((rows, np_ * sw), F32),
        scratch_shapes=[pltpu.VMEM((pw, sw), BF16)],
        compiler_params=_cparams(2),
        name="ssm_v",
    )(u, bq, wq)
    s = pl.pallas_call(
        functools.partial(_ssm_scan_kernel, n_chunks=nc, bsz=bsz),
        grid=(np_,),
        in_specs=[pl.BlockSpec((rows, sw), lambda p: (0, p)),
                  pl.BlockSpec((None, 4, LANES), lambda p: (p, 0, 0))],
        out_specs=pl.BlockSpec((rows, sw), lambda p: (0, p)),
        out_shape=jax.ShapeDtypeStruct((rows, np_ * sw), BF16),
        compiler_params=_cparams(1),
        name="ssm_scan",
    )(v, alpha)
    y = pl.pallas_call(
        _ssm_y_kernel,
        grid=(np_, rows // tr),
        in_specs=[pl.BlockSpec((tr, pw), lambda p, r: (r, p)),
                  pl.BlockSpec((tr, sw), lambda p, r: (r, p)),
                  per_pair(kflat), per_pair(cp), per_pair(wp), whole(til), whole(rep), per_pair(skip)],
        out_specs=pl.BlockSpec((tr, pw), lambda p, r: (r, p)),
        out_shape=jax.ShapeDtypeStruct((rows, np_ * pw), BF16),
        scratch_shapes=[pltpu.VMEM((pw, pw), BF16), pltpu.VMEM((sw, pw), BF16)],
        compiler_params=_cparams(2),
        name="ssm_y",
    )(u, s, kflat, cp, wp, til, rep, skip)
    return y


def _mixer(y_ref, zuv_ref, wglu_ref, bglu_ref, gssm_ref, lng_ref, lnb_ref, ws_ref, bs_ref, gsgu_ref, ya_scr):
    tm = zuv_ref.shape[0]
    n_blk = ya_scr.shape[0]
    nc = tm // SSM_CHUNK
    masks = _atom_masks(nc)
    for b in range(n_blk):
        for v in range(SSM_CHUNK // ATOMS):
            src = [y_ref[:, (ATOMS * b + pi) * PAIR_W + v * LANES:
                         (ATOMS * b + pi) * PAIR_W + (v + 1) * LANES].astype(F32)
                   for pi in range(ATOMS)]
            dst = _atom_transpose(src, masks)
            for jj in range(ATOMS):
                ya_scr[b, pl.ds(ATOMS * v + jj, nc, stride=SSM_CHUNK), :] = dst[jj]
    gl = jnp.concatenate([ya_scr[b] for b in range(n_blk)], axis=1)
    gate = jnp.dot(gl.astype(BF16), wglu_ref[...], preferred_element_type=F32) + bglu_ref[...]
    ra = _rms(gl * jax.nn.sigmoid(gate), gssm_ref[...]).astype(BF16)
    d_sgu = zuv_ref.shape[1] // 2
    u = zuv_ref[:, :d_sgu].astype(F32)
    gv = zuv_ref[:, d_sgu:].astype(F32)
    xc = gv - jnp.mean(gv, axis=-1, keepdims=True)
    v = xc * lax.rsqrt(jnp.mean(xc * xc, axis=-1, keepdims=True) + LN_EPS)
    v = (v * lng_ref[...] + lnb_ref[...]).astype(BF16)
    lo = lax.broadcasted_iota(I32, (CHUNK, LANES), 1) < (LANES // 2)
    zero = jnp.zeros((CHUNK, LANES), BF16)
    rows = []
    for c in range(tm // CHUNK):
        cols = []
        for j in range(d_sgu // LANES):
            vp = v[c * CHUNK:(c + 1) * CHUNK, j * LANES:(j + 1) * LANES]
            rhs = jnp.concatenate([jnp.where(lo, vp, zero), jnp.where(lo, zero, vp)], axis=0)
            cols.append(jnp.dot(ws_ref[j], rhs, preferred_element_type=F32))
        rows.append(jnp.concatenate(cols, axis=1) + bs_ref[...])
    s = jnp.concatenate(rows, axis=0)
    rb = _rms(u * s, gsgu_ref[...]).astype(BF16)
    return jnp.concatenate([ra, rb], axis=1)


def _sorted_rows(tile):
    return TOP_K * tile + MXU_DIM


def _route_kernel(xp_ref, xs_ref, yp_ref, ys_ref, zp_ref, zs_ref, wglu_ref, bglu_ref, gssm_ref, lng_ref,
                  lnb_ref, ws_ref, bs_ref, gsgu_ref, wout_ref, gffn_ref, whi_ref, wcat_ref, br_ref, tri_ref,
                  ltri_ref, x1_ref, xsort_ref, meta_ref, c8_ref, ya_scr, *, n_prompt_tiles, n_tiles, n_exp):
    i = pl.program_id(0)
    tm = x1_ref.shape[0]
    p_rows = xsort_ref.shape[0]

    def body(x_ref, y_ref, zuv_ref):
        mixed = _mixer(y_ref, zuv_ref, wglu_ref, bglu_ref, gssm_ref, lng_ref, lnb_ref, ws_ref, bs_ref,
                       gsgu_ref, ya_scr)
        x1 = x_ref[...] + jnp.dot(mixed, wout_ref[...], preferred_element_type=F32)
        x1_ref[...] = x1
        h2 = _rms(x1, gffn_ref[...])
        hi = h2.astype(BF16)
        lo = (h2 - hi.astype(F32)).astype(BF16)
        hw = jnp.dot(hi, wcat_ref[...], preferred_element_type=F32)
        lt = (hw[:, :LANES] + (jnp.dot(lo, whi_ref[...], preferred_element_type=F32) + hw[:, LANES:])
              + br_ref[...])
        logits = lt.T[:n_exp]
        eio = lax.broadcasted_iota(I32, (n_exp, tm), 0)
        vals, idxs = [], []
        l = logits
        for _ in range(TOP_K):
            m = jnp.max(l, axis=0, keepdims=True)
            idx = jnp.min(jnp.where(l == m, eio, n_exp), axis=0, keepdims=True)
            vals.append(m)
            idxs.append(idx)
            l = jnp.where(eio == idx, -jnp.inf, l)
        ex = [jnp.exp(v - vals[0]) for v in vals]
        den = ex[0] + ex[1] + ex[2] + ex[3]
        ws = [e / den for e in ex]
        hot = [eio == idx for idx in idxs]
        cnt = sum(h.astype(F32) for h in hot)
        prefix = jnp.dot(cnt.astype(BF16), tri_ref[...], preferred_element_type=F32)
        c = jnp.sum(cnt, axis=1, keepdims=True)
        c8 = jnp.floor((c + (SUBLANES - 1)) * (1.0 / SUBLANES)) * SUBLANES
        c8b = jnp.broadcast_to(c8, (n_exp, LANES))
        c8_ref[...] = c8b
        run0 = jnp.dot(ltri_ref[...], c8b.astype(BF16), preferred_element_type=F32)[:, 0:1]
        base = run0 + prefix
        pos = [jnp.sum(jnp.where(h, base, 0.0), axis=0, keepdims=True) for h in hot]
        meta = jnp.concatenate(ws + pos + [jnp.zeros((LANES - 2 * TOP_K, tm), F32)], axis=0)
        meta_ref[...] = meta.T
        rio = lax.broadcasted_iota(I32, (SORT_CHUNK, tm), 0).astype(F32).astype(BF16)
        one = jnp.ones((SORT_CHUNK, tm), BF16)
        zero = jnp.zeros((SORT_CHUNK, tm), BF16)
        for r in range(p_rows // SORT_CHUNK):
            lo_r = float(r * SORT_CHUNK)
            rel = [jnp.where((p >= lo_r) & (p < lo_r + SORT_CHUNK), p - lo_r, -1.0).astype(BF16) for p in pos]
            sel = (rio == rel[0]) | (rio == rel[1]) | (rio == rel[2]) | (rio == rel[3])
            srt = jnp.dot(jnp.where(sel, one, zero), hi, preferred_element_type=F32)
            xsort_ref[r * SORT_CHUNK:(r + 1) * SORT_CHUNK, :] = _pack_bf16_pairs(srt)

    @pl.when(i < n_prompt_tiles)
    def _():
        body(xp_ref, yp_ref, zp_ref)

    @pl.when((i >= n_prompt_tiles) & (i < n_tiles))
    def _():
        body(xs_ref, ys_ref, zs_ref)

    @pl.when(i == n_tiles)
    def _():
        xsort_ref[...] = jnp.zeros_like(xsort_ref)


def _dual_specs(tile, width, n_prompt_tiles, n_tiles):
    last_p = n_prompt_tiles - 1
    last_s = n_tiles - n_prompt_tiles - 1
    sp = pl.BlockSpec((tile, width), lambda i: (jnp.minimum(i, last_p), 0))
    ss = pl.BlockSpec((tile, width), lambda i: (jnp.clip(i - n_prompt_tiles, 0, last_s), 0))
    return sp, ss


def _route(xp, xs, y_p, y_s, zuv_p, zuv_s, mix_params, w_out, g_ffn, wr_hi, wr_cat, br_pad, n_exp):
    tp, d = xp.shape
    t = tp + xs.shape[0]
    tm = TOKEN_TILE
    npt = tp // tm
    nt = t // tm
    p_rows = _sorted_rows(tm)
    d_ssm = mix_params[0].shape[0]
    iota = lambda n, ax: lax.broadcasted_iota(I32, (n, n), ax)
    tri = (iota(tm, 0) < iota(tm, 1)).astype(BF16)
    ltri = (iota(n_exp, 1) < iota(n_exp, 0)).astype(BF16)
    const = lambda shape: pl.BlockSpec(shape, lambda i: (0,) * len(shape))
    tile = lambda i: jnp.minimum(i, nt - 1)
    dual = lambda rows, width: _dual_specs(rows, width, npt, nt)
    return pl.pallas_call(
        functools.partial(_route_kernel, n_prompt_tiles=npt, n_tiles=nt, n_exp=n_exp),
        grid=(nt + 1,),
        in_specs=[*dual(tm, d), *dual(tm // SSM_CHUNK, y_p.shape[1]), *dual(tm, zuv_p.shape[1]),
                  *[const(a.shape) for a in mix_params],
                  const(w_out.shape), const((1, d)), const(wr_hi.shape), const(wr_cat.shape),
                  const((1, LANES)), const((tm, tm)), const((n_exp, n_exp))],
        out_specs=[pl.BlockSpec((tm, d), lambda i: (tile(i), 0)),
                   pl.BlockSpec((p_rows, d // 2), lambda i: (i, 0)),
                   pl.BlockSpec((tm, LANES), lambda i: (tile(i), 0)),
                   pl.BlockSpec((None, n_exp, LANES), lambda i: (tile(i), 0, 0))],
        out_shape=[jax.ShapeDtypeStruct((t, d), F32),
                   jax.ShapeDtypeStruct(((nt + 1) * p_rows, d // 2), U32),
                   jax.ShapeDtypeStruct((t, LANES), F32),
                   jax.ShapeDtypeStruct((nt, n_exp, LANES), F32)],
        scratch_shapes=[pltpu.VMEM((d_ssm // LANES, tm, LANES), F32)],
        compiler_params=_cparams(),
        name="route",
    )(xp, xs, y_p, y_s, zuv_p, zuv_s, *mix_params, w_out, g_ffn, wr_hi, wr_cat, br_pad, tri, ltri)


def _slot_tables(c8, p_rows, n_blocks):
    nt, n_exp = c8.shape
    run0 = jnp.cumsum(c8, axis=1) - c8
    seg_len = c8.T
    cum = jnp.cumsum(seg_len, axis=1)
    tot = cum[:, -1]
    padded = (tot + MOE_TILE - 1) // MOE_TILE * MOE_TILE
    ends = jnp.cumsum(padded)
    starts = ends - padded
    n_used = (ends[-1] // MOE_TILE).astype(I32)
    blk = jnp.arange(n_blocks, dtype=I32)
    be = jnp.minimum(jnp.sum(ends[None, :] <= (blk * MOE_TILE)[:, None], axis=1), n_exp - 1).astype(I32)
    block_e = jnp.where(blk < n_used, be, be[jnp.maximum(n_used - 1, 0)])
    onehot = (be[:, None] == jnp.arange(n_exp, dtype=I32)[None, :]).astype(F32)
    pick = lambda tbl: jnp.dot(onehot, tbl.astype(F32), precision=lax.Precision.HIGHEST)
    run_end = pick(cum)
    run_beg = run_end - pick(seg_len)
    shift = pick((jnp.arange(nt, dtype=I32) * p_rows)[None, :] + run0.T) - run_beg
    grow = (blk[:, None] * MOE_TILE + jnp.arange(MOE_GROUPS, dtype=I32)[None, :] * SUBLANES).astype(F32)
    rel = grow - pick(starts[:, None])
    inside = (run_beg[:, None, :] <= rel[:, :, None]) & (rel[:, :, None] < run_end[:, None, :])
    src_row = (rel + jnp.sum(jnp.where(inside, shift[:, None, :], 0.0), axis=2)).astype(I32)
    valid = jnp.any(inside, axis=2) & (blk < n_used)[:, None]
    scratch0 = nt * p_rows // SUBLANES
    zero_group = scratch0 + 2 * MOE_GROUPS
    g_in = jnp.where(valid, src_row // SUBLANES, zero_group).astype(I32)
    g_scr = scratch0 + (blk[:, None] % 2) * MOE_GROUPS + jnp.arange(MOE_GROUPS, dtype=I32)[None, :]
    g_out = jnp.where(valid, src_row // SUBLANES, g_scr).astype(I32)
    return block_e, n_used, g_in, g_out


def _moe_kernel(be_ref, nu_ref, gin_ref, gnext_ref, gout_ref, xs_hbm, wgu_ref, bgu_ref, wd_ref, bd_ref,
                eo_hbm, xbuf, obuf, wgu_s, wd_s, isem, osem):
    j = pl.program_id(0)
    nu = nu_ref[0]
    f = wd_ref.shape[0]

    @pl.when((j < nu) & ((j == 0) | (be_ref[j] != be_ref[jnp.maximum(j - 1, 0)])))
    def _():
        for c in range(0, wgu_ref.shape[0], CAST_ROWS):
            wgu_s[c:c + CAST_ROWS, :] = wgu_ref[c:c + CAST_ROWS, :].astype(BF16)
        for c in range(0, wd_ref.shape[0], CAST_ROWS):
            wd_s[c:c + CAST_ROWS, :] = wd_ref[c:c + CAST_ROWS, :].astype(BF16)

    def in_copy(tbl_ref, r, sl):
        return pltpu.make_async_copy(xs_hbm.at[tbl_ref[0, r]], xbuf.at[sl, r], isem.at[sl])

    def out_copy(r, sl):
        return pltpu.make_async_copy(obuf.at[sl, r], eo_hbm.at[gout_ref[0, r]], osem.at[sl])

    def wait_in(sl):
        pltpu.make_async_copy(xs_hbm.at[pl.ds(0, MOE_GROUPS)], xbuf.at[sl], isem.at[sl]).wait()

    def wait_out(sl):
        pltpu.make_async_copy(obuf.at[sl], eo_hbm.at[pl.ds(0, MOE_GROUPS)], osem.at[sl]).wait()

    @pl.when(j == 0)
    def _():
        for r in range(MOE_GROUPS):
            in_copy(gin_ref, r, 0).start()

    def step(slot):
        other = 1 - slot

        @pl.when(j + 1 < nu)
        def _():
            for r in range(MOE_GROUPS):
                in_copy(gnext_ref, r, other).start()

        wait_in(slot)

        @pl.when(j >= 2)
        def _():
            wait_out(slot)

        x = _unpack_bf16_pairs(xbuf[slot].reshape(MOE_TILE, xbuf.shape[-1]))
        gate = jnp.dot(x, wgu_s[:, :f], preferred_element_type=F32) + bgu_ref[:, :f]
        gate = jnp.minimum(gate, SWIGLU_LIMIT)
        sg = gate * jax.nn.sigmoid(SWIGLU_ALPHA * gate)
        up = jnp.dot(x, wgu_s[:, f:], preferred_element_type=F32) + bgu_ref[:, f:]
        act = sg * (jnp.clip(up, -SWIGLU_LIMIT, SWIGLU_LIMIT) + 1.0)
        out = jnp.dot(act.astype(BF16), wd_s[...], preferred_element_type=F32) + bd_ref[...]
        obuf[slot] = _pack_bf16_pairs(out.astype(BF16).astype(F32)).reshape(obuf.shape[1:])
        for r in range(MOE_GROUPS):
            out_copy(r, slot).start()

        @pl.when(j == nu - 1)
        def _():
            wait_out(slot)

            @pl.when(j >= 1)
            def _():
                wait_out(other)

    for parity in range(2):
        pl.when((j < nu) & (lax.rem(j, 2) == parity))(functools.partial(step, parity))


def _moe(block_e, n_used, g_in, g_out, xsorted, wgu, bgu, wd, bd):
    n_blocks = block_e.shape[0]
    dh = xsorted.shape[1]
    d = 2 * dh
    f2 = wgu.shape[2]
    f = wd.shape[1]
    g_in3 = g_in.reshape(n_blocks, 1, MOE_GROUPS)
    g_next3 = jnp.concatenate([g_in3[1:], g_in3[:1]], axis=0)
    g_out3 = g_out.reshape(n_blocks, 1, MOE_GROUPS)
    xs3 = xsorted.reshape(xsorted.shape[0] // SUBLANES, SUBLANES, dh)
    tbl = pl.BlockSpec((None, 1, MOE_GROUPS), lambda j, be, nu: (j, 0, 0), memory_space=pltpu.SMEM)
    return pl.pallas_call(
        _moe_kernel,
        grid_spec=pltpu.PrefetchScalarGridSpec(
            num_scalar_prefetch=2, grid=(n_blocks,),
            in_specs=[tbl, tbl, tbl,
                      pl.BlockSpec(memory_space=pl.ANY),
                      pl.BlockSpec((None, d, f2), lambda j, be, nu: (be[j], 0, 0)),
                      pl.BlockSpec((None, 1, f2), lambda j, be, nu: (be[j], 0, 0)),
                      pl.BlockSpec((None, f, d), lambda j, be, nu: (be[j], 0, 0)),
                      pl.BlockSpec((None, 1, d), lambda j, be, nu: (be[j], 0, 0))],
            out_specs=pl.BlockSpec(memory_space=pl.ANY),
            scratch_shapes=[pltpu.VMEM((2, MOE_GROUPS, SUBLANES, dh), U32),
                            pltpu.VMEM((2, MOE_GROUPS, SUBLANES, dh), U32),
                            pltpu.VMEM((d, f2), BF16), pltpu.VMEM((f, d), BF16),
                            pltpu.SemaphoreType.DMA((2,)), pltpu.SemaphoreType.DMA((2,))]),
        out_shape=jax.ShapeDtypeStruct(xs3.shape, U32),
        input_output_aliases={5: 0},
        compiler_params=_cparams(),
        name="moe",
    )(block_e, n_used, g_in3, g_next3, g_out3, xs3, wgu, bgu, wd, bd).reshape(xsorted.shape)


def _combine_kernel(meta_ref, x1_ref, g_ref, eo_ref, y_ref):
    tm = x1_ref.shape[0]
    p_rows = eo_ref.shape[0]
    lane = lax.broadcasted_iota(I32, (tm, SORT_CHUNK), 1).astype(F32).astype(BF16)
    ws = [meta_ref[:, k:k + 1].astype(BF16) for k in range(TOP_K)]
    pos = [meta_ref[:, TOP_K + k:TOP_K + k + 1] for k in range(TOP_K)]
    acc = x1_ref[...]
    for r in range(p_rows // SORT_CHUNK):
        buf = _unpack_bf16_pairs(eo_ref[r * SORT_CHUNK:(r + 1) * SORT_CHUNK, :])
        lo_r = float(r * SORT_CHUNK)
        wm = jnp.zeros((tm, SORT_CHUNK), BF16)
        for k in range(TOP_K):
            rel = jnp.where((pos[k] >= lo_r) & (pos[k] < lo_r + SORT_CHUNK), pos[k] - lo_r, -1.0).astype(BF16)
            wm = jnp.where(lane == rel, ws[k], wm)
        acc = acc + jnp.dot(wm, buf, preferred_element_type=F32)
    y_ref[...] = _rms(acc, g_ref[...])


def _combine(meta, x1, g_final, eo, tile_off, n_tokens):
    d = x1.shape[1]
    tm = TOKEN_TILE
    p_rows = _sorted_rows(tm)
    return pl.pallas_call(
        _combine_kernel,
        grid=(n_tokens // tm,),
        in_specs=[pl.BlockSpec((tm, LANES), lambda i: (i + tile_off, 0)),
                  pl.BlockSpec((tm, d), lambda i: (i + tile_off, 0)),
                  pl.BlockSpec((1, d), lambda i: (0, 0)),
                  pl.BlockSpec((p_rows, d // 2), lambda i: (i + tile_off, 0))],
        out_specs=pl.BlockSpec((tm, d), lambda i: (i, 0)),
        out_shape=jax.ShapeDtypeStruct((n_tokens, d), F32),
        compiler_params=_cparams(),
        name="combine",
    )(meta, x1, g_final, eo)


def kernel(x_prompt, x_sample, g_mix, w_in, ssm_lam_re, ssm_lam_im, ssm_log_dt, ssm_b_re, ssm_b_im,
           ssm_c_re, ssm_c_im, ssm_d, w_glu, b_glu, sgu_ln_g, sgu_ln_b, sgu_w_s, sgu_b_s,
           g_out_ssm, g_out_sgu, w_out, g_ffn, w_router, b_router, w_gate_up, b_gate_up,
           w_down, b_down, g_final):
    assert g_mix.shape[0] == 1, "single-layer trunk"
    bp, lp, d = x_prompt.shape
    bs, ls, _ = x_sample.shape
    tp, ts = bp * lp, bs * ls
    t = tp + ts
    d_ssm = ssm_d.shape[1]
    d_sgu = sgu_ln_g.shape[1]
    n_exp = w_router.shape[2]
    assert all(n % tile == 0 for n in (tp, ts) for tile in (SEQ_TILE, TOKEN_TILE))
    assert lp % CHUNK == 0 and ls % CHUNK == 0 and TOKEN_TILE % CHUNK == 0 and CHUNK % SSM_CHUNK == 0
    assert SEQ_TILE % SSM_CHUNK == 0
    assert SSM_CHUNK * SSM_GROUP == MXU_DIM and 2 * ssm_lam_re.shape[-1] == LANES
    assert d_sgu // SGU_HEADS == LANES // 2 and n_exp <= LANES
    assert n_exp * (SUBLANES - 1) <= MXU_DIM and _sorted_rows(TOKEN_TILE) % SORT_CHUNK == 0
    assert (2 * MOE_GROUPS + 1) * SUBLANES <= _sorted_rows(TOKEN_TILE)

    xp = x_prompt.reshape(tp, d)
    xs = x_sample.reshape(ts, d)
    row = lambda a: a.reshape(1, -1).astype(F32)

    w_in_b = w_in[0].astype(BF16)
    u_p, zuv_p = _inproj(xp, row(g_mix[0]), w_in_b, d_ssm)
    u_s, zuv_s = _inproj(xs, row(g_mix[0]), w_in_b, d_ssm)

    mats = _ssm_matrices(ssm_lam_re[0], ssm_lam_im[0], ssm_log_dt[0], ssm_b_re[0], ssm_b_im[0],
                         ssm_c_re[0], ssm_c_im[0], ssm_d[0])
    y_p = _ssm_trunk(u_p, bp, mats)
    y_s = _ssm_trunk(u_s, bs, mats)

    ws = sgu_w_s[0]
    ws_pairs = jnp.concatenate([ws[0::2], ws[1::2]], axis=2).astype(BF16)
    bias_s = jnp.repeat(sgu_b_s[0].T, d_sgu // SGU_HEADS, axis=1).astype(F32)
    mix_params = (w_glu[0].astype(BF16), row(b_glu[0]), row(g_out_ssm[0]), row(sgu_ln_g[0]), row(sgu_ln_b[0]),
                  ws_pairs, bias_s, row(g_out_sgu[0]))

    wr_pad = jnp.pad(w_router[0].astype(F32), ((0, 0), (0, LANES - n_exp)))
    wr_hi = wr_pad.astype(BF16)
    wr_cat = jnp.concatenate([wr_hi, (wr_pad - wr_hi.astype(F32)).astype(BF16)], axis=1)
    br_pad = jnp.pad(b_router[0].astype(F32), (0, LANES - n_exp)).reshape(1, LANES)
    x1, xsorted, meta, c8 = _route(xp, xs, y_p, y_s, zuv_p, zuv_s, mix_params, w_out[0].astype(BF16),
                                   row(g_ffn[0]), wr_hi, wr_cat, br_pad, n_exp)

    nt = t // TOKEN_TILE
    p_rows = _sorted_rows(TOKEN_TILE)
    max_rows = t * TOP_K + nt * n_exp * (SUBLANES - 1) + n_exp * (MOE_TILE - 1)
    n_blocks = -(-max_rows // MOE_TILE)
    block_e, n_used, g_in, g_out = _slot_tables(c8[:, :, 0].astype(I32), p_rows, n_blocks)
    eo = _moe(block_e, n_used.reshape(1), g_in, g_out, xsorted, w_gate_up[0],
              b_gate_up[0][:, None, :].astype(F32), w_down[0], b_down[0][:, None, :].astype(F32))

    gf = row(g_final)
    y_prompt = _combine(meta, x1, gf, eo, 0, tp)
    y_sample = _combine(meta, x1, gf, eo, tp // TOKEN_TILE, ts)
    return y_prompt.reshape(bp, lp, d), y_sample.reshape(bs, ls, d)
```

```python
import functools
import math

import jax
import jax.numpy as jnp
from jax import lax
from jax.experimental import pallas as pl
from jax.experimental.pallas import tpu as pltpu

F32 = jnp.float32
BF16 = jnp.bfloat16
I32 = jnp.int32
U32 = jnp.uint32

SSM_GROUP = 16
SGU_HEADS = 8
CHUNK = 128
TOP_K = 4
SWIGLU_LIMIT = 7.0
SWIGLU_ALPHA = 1.702
RMS_EPS = 1e-6
LN_EPS = 1e-5

LANES = 128
SUBLANES = 8
MXU_DIM = 256
SSM_CHUNK = MXU_DIM // SSM_GROUP
PAIR_W = 2 * SSM_CHUNK * SSM_GROUP
ATOM = 2 * SSM_GROUP
ATOMS = LANES // ATOM

TOKEN_TILE = 512
SEQ_TILE = 1024
MOE_TILE = 1024
MOE_GROUPS = MOE_TILE // SUBLANES
SORT_CHUNK = 256
CAST_ROWS = 128
SSM_ROW_TILE = 2048
SCAN_SEQS = 4
VMEM_LIMIT = 56 * 1024 * 1024


def _cparams(n_axes=1, vmem=None):
    return pltpu.CompilerParams(
        dimension_semantics=("arbitrary",) * n_axes,
        vmem_limit_bytes=vmem if vmem is not None else VMEM_LIMIT,
    )


def _rms(x, g):
    return x * lax.rsqrt(jnp.mean(x * x, axis=-1, keepdims=True) + RMS_EPS) * g


def _gelu(x):
    return x * (lax.erf(x * (1.0 / math.sqrt(2.0))) + 1.0) * 0.5


def _atom_masks(rows):
    lane = lax.broadcasted_iota(I32, (rows, LANES), 1)
    return [(lane >= a * ATOM) & (lane < (a + 1) * ATOM) for a in range(ATOMS)]


def _atom_transpose(src, masks):
    dst = []
    for b in range(ATOMS):
        acc = None
        for a in range(ATOMS):
            r = (a - b) % ATOMS
            piece = src[a] if r == 0 else pltpu.roll(src[a], ATOM * r, axis=1)
            acc = piece if acc is None else jnp.where(masks[a], piece, acc)
        dst.append(acc)
    return dst


def _pack_bf16_pairs(x):
    w = x.shape[1] // 2
    lo = lax.bitcast_convert_type(x[:, :w], U32) >> 16
    hi = lax.bitcast_convert_type(x[:, w:], U32) & jnp.uint32(0xFFFF0000)
    return hi | lo


def _unpack_bf16_pairs(u):
    lo = lax.bitcast_convert_type(u << 16, F32)
    hi = lax.bitcast_convert_type(u & jnp.uint32(0xFFFF0000), F32)
    return jnp.concatenate([lo.astype(BF16), hi.astype(BF16)], axis=1)


def _inproj_kernel(x_ref, g_ref, w_ref, u_ref, zuv_ref, za_scr, *, d_ssm):
    tm = x_ref.shape[0]
    nc = tm // SSM_CHUNK
    h = _rms(x_ref[...], g_ref[...])
    z = jnp.dot(h.astype(BF16), w_ref[...], preferred_element_type=F32)
    zuv_ref[...] = _gelu(z[:, d_ssm:]).astype(BF16)
    n_blk = d_ssm // LANES
    for b in range(n_blk):
        za_scr[b] = z[:, b * LANES:(b + 1) * LANES]
    masks = _atom_masks(nc)
    n_quads = SSM_CHUNK // ATOMS
    for b in range(n_blk):
        for v in range(n_quads):
            src = [za_scr[b, pl.ds(ATOMS * v + jj, nc, stride=SSM_CHUNK), :] for jj in range(ATOMS)]
            dst = _atom_transpose(src, masks)
            for pi in range(ATOMS):
                c0 = (ATOMS * b + pi) * PAIR_W + v * LANES
                u_ref[:, c0:c0 + LANES] = dst[pi].astype(BF16)


def _inproj(x, g_mix, w_in, d_ssm):
    t, d = x.shape
    tm = SEQ_TILE
    d_in = w_in.shape[1]
    nc = tm // SSM_CHUNK
    uw = d_ssm * SSM_CHUNK
    return pl.pallas_call(
        functools.partial(_inproj_kernel, d_ssm=d_ssm),
        grid=(t // tm,),
        in_specs=[pl.BlockSpec((tm, d), lambda i: (i, 0)),
                  pl.BlockSpec((1, d), lambda i: (0, 0)),
                  pl.BlockSpec((d, d_in), lambda i: (0, 0))],
        out_specs=[pl.BlockSpec((nc, uw), lambda i: (i, 0)),
                   pl.BlockSpec((tm, d_in - d_ssm), lambda i: (i, 0))],
        out_shape=[jax.ShapeDtypeStruct((t // SSM_CHUNK, uw), BF16),
                   jax.ShapeDtypeStruct((t, d_in - d_ssm), BF16)],
        scratch_shapes=[pltpu.VMEM((d_ssm // LANES, tm, LANES), F32)],
        compiler_params=_cparams(),
        name="inproj",
    )(x, g_mix, w_in)


def _ssm_matrices(lam_re, lam_im, log_dt, b_re, b_im, c_re, c_im, d_skip):
    hp = lax.Precision.HIGHEST
    _, g, n = lam_re.shape
    p = b_re.shape[-1]
    lc = SSM_CHUNK
    np_ = g // 2
    dt = jnp.exp(log_dt)[..., None]
    mag = jnp.exp(lam_re * dt)
    ar = mag * jnp.cos(lam_im * dt)
    ai = mag * jnp.sin(lam_im * dt)
    den = lam_re * lam_re + lam_im * lam_im
    nr = ar - 1.0
    fr = (nr * lam_re + ai * lam_im) / den
    fi = (ai * lam_re - nr * lam_im) / den
    bbr = fr[..., None] * b_re - fi[..., None] * b_im
    bbi = fr[..., None] * b_im + fi[..., None] * b_re

    prs, pis = [jnp.ones_like(ar)], [jnp.zeros_like(ai)]
    for _ in range(lc):
        pr, pi = prs[-1], pis[-1]
        prs.append(pr * ar - pi * ai)
        pis.append(pr * ai + pi * ar)
    pw_r, pw_i = jnp.stack(prs), jnp.stack(pis)

    lane_k = jnp.arange(lc * p)
    til_k = (lane_k[None, :] % p == jnp.arange(p)[:, None]).astype(F32)
    rep_k = (lane_k[None, :] // p == jnp.arange(lc)[:, None]).astype(F32)
    ex = lambda x, e: jnp.einsum('dgnk,kx->dgnx', x, e, precision=hp)
    ct_r = ex(jnp.transpose(c_re, (0, 1, 3, 2)), til_k)
    ct_i = ex(jnp.transpose(c_im, (0, 1, 3, 2)), til_k)
    at_r = ex(jnp.transpose(pw_r[:lc], (1, 2, 3, 0)), rep_k)
    at_i = ex(jnp.transpose(pw_i[:lc], (1, 2, 3, 0)), rep_k)
    kk = (jnp.einsum('dgnq,dgnx->dgqx', bbr, ct_r * at_r - ct_i * at_i, precision=hp)
          - jnp.einsum('dgnq,dgnx->dgqx', bbi, ct_r * at_i + ct_i * at_r, precision=hp))
    kk = kk.reshape(2, g, p, lc, p)
    kf, kb = kk[0], kk[1]
    eye = jnp.eye(2, dtype=F32)
    a2 = p * 2

    k_all = jnp.concatenate([kb[:, :, :0:-1], (kf[:, :, :1] + kb[:, :, :1]), kf[:, :, 1:]], axis=2)
    k_all = k_all.reshape(np_, 2, p, 2 * lc - 1, p)
    kflat = jnp.einsum('nsqup,st->nsqutp', k_all, eye).reshape(np_, a2, (2 * lc - 1) * a2)
    kflat = jnp.pad(kflat, ((0, 0), (0, 0), (0, a2)))

    def atoms_q(x):
        x = jnp.transpose(x, (0, 1, 3, 2)).reshape(2, np_, 2, p, n)
        return jnp.einsum('dnsqm,st->dnsqtm', x, eye).reshape(2, np_, a2, 2 * n)
    bq_r, bq_i = atoms_q(bbr), atoms_q(bbi)
    bq = jnp.stack([bq_r[0], bq_i[0], bq_r[1], bq_i[1]], axis=1)
    ef = lc - 1 - jnp.arange(lc)
    eb = jnp.arange(lc)
    rows_q = lambda w, e, d: jnp.transpose(w[e, d].reshape(lc, np_, 2 * n), (1, 0, 2))
    wq = jnp.stack([rows_q(pw_r, ef, 0), rows_q(pw_i, ef, 0),
                    rows_q(pw_r, eb, 1), rows_q(pw_i, eb, 1)], axis=1)

    def atoms_p(x):
        x = jnp.transpose(x, (0, 1, 3, 2)).reshape(2, np_, 2, n, p)
        return jnp.einsum('dnsmp,st->dnsmtp', x, eye).reshape(2, np_, 2 * n, a2)
    cp_r, cp_i = atoms_p(c_re), atoms_p(c_im)
    cp = jnp.stack([cp_r[0], cp_i[0], cp_r[1], cp_i[1]], axis=1)
    pf_e = jnp.arange(lc) + 1
    pb_e = lc - jnp.arange(lc)
    cols_p = lambda w, e, d: jnp.transpose(w[e, d].reshape(lc, np_, 2 * n), (1, 2, 0))
    wp = jnp.stack([cols_p(pw_r, pf_e, 0), cols_p(pw_i, pf_e, 0),
                    cols_p(pw_r, pb_e, 1), cols_p(pw_i, pb_e, 1)], axis=1)
    lane = jnp.arange(lc * a2)
    til = (lane[None, :] % a2 == jnp.arange(a2)[:, None]).astype(BF16)
    rep = (lane[None, :] // a2 == jnp.arange(lc)[:, None]).astype(BF16)

    al = jnp.stack([pw_r[lc, 0], pw_i[lc, 0], pw_r[lc, 1], pw_i[lc, 1]])
    alpha = jnp.transpose(al.reshape(4, np_, 2 * n), (1, 0, 2))
    skip = jnp.broadcast_to(d_skip.reshape(np_, 1, a2), (np_, lc, a2)).reshape(np_, 1, lc * a2)
    return kflat, bq, wq, cp, wp, til, rep, alpha, skip.astype(F32)


def _ssm_v_kernel(u_ref, bq_ref, wq_ref, v_ref, q_scr):
    a2 = bq_ref.shape[1]
    w = bq_ref.shape[2]

    @pl.when(pl.program_id(1) == 0)
    def _():
        for d in range(2):
            br, bi = bq_ref[2 * d], bq_ref[2 * d + 1]
            for j in range(SSM_CHUNK):
                wr = wq_ref[2 * d, j:j + 1, :]
                wi = wq_ref[2 * d + 1, j:j + 1, :]
                q_scr[j * a2:(j + 1) * a2, (2 * d) * w:(2 * d + 1) * w] = (wr * br - wi * bi).astype(BF16)
                q_scr[j * a2:(j + 1) * a2, (2 * d + 1) * w:(2 * d + 2) * w] = (wr * bi + wi * br).astype(BF16)

    v_ref[...] = jnp.dot(u_ref[...], q_scr[...], preferred_element_type=F32)


def _ssm_scan_kernel(v_ref, a_ref, s_ref, *, n_chunks, bsz):
    w = LANES
    sub = SUBLANES
    n_groups = n_chunks // sub
    row = lax.broadcasted_iota(I32, (sub, w), 0)

    def cmul(ar, ai, xr, xi):
        return ar * xr - ai * xi, ar * xi + ai * xr

    def powers(ar, ai):
        p2 = cmul(ar, ai, ar, ai)
        p4 = cmul(*p2, *p2)
        p8 = cmul(*p4, *p4)
        p3 = cmul(*p2, ar, ai)
        p5 = cmul(*p4, ar, ai)
        p6 = cmul(*p4, *p2)
        p7 = cmul(*p4, *p3)
        seq = [(jnp.ones_like(ar), jnp.zeros_like(ai)), (ar, ai), p2, p3, p4, p5, p6, p7]
        tr = jnp.concatenate([s[0] for s in seq], axis=0)
        ti = jnp.concatenate([s[1] for s in seq], axis=0)
        return (ar, ai), p2, p4, p8, (tr, ti)

    def tile_scan(vr, vi, pw, reverse):
        a1, a2, a4, _, _ = pw

        def shift(x, s):
            if reverse:
                return jnp.where(row < sub - s, pltpu.roll(x, sub - s, axis=0), 0.0)
            return jnp.where(row >= s, pltpu.roll(x, s, axis=0), 0.0)

        xr, xi = shift(vr, 1), shift(vi, 1)
        for s, (ar, ai) in ((1, a1), (2, a2), (4, a4)):
            mr, mi = cmul(ar, ai, shift(xr, s), shift(xi, s))
            xr, xi = xr + mr, xi + mi
        e = 0 if reverse else sub - 1
        lr, li = cmul(a1[0], a1[1], xr[e:e + 1], xi[e:e + 1])
        return xr, xi, lr + vr[e:e + 1], li + vi[e:e + 1]

    pw_f = powers(a_ref[0:1, :], a_ref[1:2, :])
    pw_b = powers(a_ref[2:3, :], a_ref[3:4, :])
    tbr = jnp.concatenate([pw_b[4][0][sub - 1 - k:sub - k] for k in range(sub)], axis=0)
    tbi = jnp.concatenate([pw_b[4][1][sub - 1 - k:sub - k] for k in range(sub)], axis=0)

    def one_tile(row0, col0, pw, table, cr, ci, reverse):
        rows = pl.ds(pl.multiple_of(row0, sub), sub)
        xr, xi, lr, li = tile_scan(v_ref[rows, col0:col0 + w], v_ref[rows, col0 + w:col0 + 2 * w], pw, reverse)
        mr, mi = cmul(table[0], table[1], cr, ci)
        nr, ni = cmul(pw[3][0], pw[3][1], cr, ci)
        return xr + mr, xi + mi, nr + lr, ni + li

    pair = 2 * sub
    n_pairs = n_groups // 2

    def step(g, carry, b0):
        out = []
        for b in range(b0, b0 + len(carry) // 4):
            cfr, cfi, cbr, cbi = carry[4 * (b - b0):4 * (b - b0) + 4]
            rf = pl.multiple_of(b * n_chunks + g * pair, pair)
            rb = pl.multiple_of(b * n_chunks + (n_pairs - 1 - g) * pair, pair)
            ar, ai, cfr, cfi = one_tile(rf, 0, pw_f, pw_f[4], cfr, cfi, False)
            br, bi, cfr, cfi = one_tile(rf + sub, 0, pw_f, pw_f[4], cfr, cfi, False)
            s_ref[pl.ds(rf, pair), 0:w] = jnp.concatenate([ar, br], axis=0).astype(BF16)
            s_ref[pl.ds(rf, pair), w:2 * w] = jnp.concatenate([ai, bi], axis=0).astype(BF16)
            out += [cfr, cfi]
            br, bi, cbr, cbi = one_tile(rb + sub, 2 * w, pw_b, (tbr, tbi), cbr, cbi, True)
            ar, ai, cbr, cbi = one_tile(rb, 2 * w, pw_b, (tbr, tbi), cbr, cbi, True)
            s_ref[pl.ds(rb, pair), 2 * w:3 * w] = jnp.concatenate([ar, br], axis=0).astype(BF16)
            s_ref[pl.ds(rb, pair), 3 * w:4 * w] = jnp.concatenate([ai, bi], axis=0).astype(BF16)
            out += [cbr, cbi]
        return tuple(out)

    zero = jnp.zeros((1, w), F32)
    for b0 in range(0, bsz, SCAN_SEQS):
        nb = min(SCAN_SEQS, bsz - b0)
        lax.fori_loop(0, n_pairs, functools.partial(step, b0=b0), (zero,) * (4 * nb))


def _ssm_y_kernel(u_ref, s_ref, kflat_ref, cp_ref, wp_ref, til_ref, rep_ref, d_ref, y_ref, m_scr, p_scr):
    a2 = kflat_ref.shape[0]
    w = cp_ref.shape[1]

    def expand(x, e_ref):
        hi = x.astype(BF16)
        lo = (x - hi.astype(F32)).astype(BF16)
        return (jnp.dot(hi, e_ref[...], preferred_element_type=F32)
                + jnp.dot(lo, e_ref[...], preferred_element_type=F32))

    @pl.when(pl.program_id(1) == 0)
    def _():
        kflat = kflat_ref[...]
        for j in range(SSM_CHUNK):
            off = (SSM_CHUNK - 1 - j) * a2
            m_scr[j * a2:(j + 1) * a2, :] = kflat[:, off:off + PAIR_W].astype(BF16)
        for d in range(2):
            cr, ci = expand(cp_ref[2 * d], til_ref), expand(cp_ref[2 * d + 1], til_ref)
            wr, wi = expand(wp_ref[2 * d], rep_ref), expand(wp_ref[2 * d + 1], rep_ref)
            p_scr[(2 * d) * w:(2 * d + 1) * w, :] = (cr * wr - ci * wi).astype(BF16)
            p_scr[(2 * d + 1) * w:(2 * d + 2) * w, :] = (-(cr * wi + ci * wr)).astype(BF16)

    u = u_ref[...]
    y = jnp.dot(u, m_scr[...], preferred_element_type=F32)
    y = y + jnp.dot(s_ref[...], p_scr[...], preferred_element_type=F32)
    y_ref[...] = _gelu(y + d_ref[...] * u.astype(F32)).astype(BF16)


def _ssm_trunk(u, bsz, mats):
    kflat, bq, wq, cp, wp, til, rep, alpha, skip = mats
    rows = u.shape[0]
    nc = rows // bsz
    np_ = kflat.shape[0]
    pw = PAIR_W
    sw = 4 * bq.shape[3]
    tr = min(SSM_ROW_TILE, rows)
    per_pair = lambda a: pl.BlockSpec((None,) + a.shape[1:], lambda p, r: (p,) + (0,) * (a.ndim - 1))
    whole = lambda a: pl.BlockSpec(a.shape, lambda p, r: (0,) * a.ndim)
    v = pl.pallas_call(
        _ssm_v_kernel,
        grid=(np_, rows // tr),
        in_specs=[pl.BlockSpec((tr, pw), lambda p, r: (r, p)), per_pair(bq), per_pair(wq)],
        out_specs=pl.BlockSpec((tr, sw), lambda p, r: (r, p)),
        out_shape=jax.ShapeDtypeStruct((rows, np_ * sw), F32),
        scratch_shapes=[pltpu.VMEM((pw, sw), BF16)],
        compiler_params=_cparams(2),
        name="ssm_v",
    )(u, bq, wq)
    s = pl.pallas_call(
        functools.partial(_ssm_scan_kernel, n_chunks=nc, bsz=bsz),
        grid=(np_,),
        in_specs=[pl.BlockSpec((rows, sw), lambda p: (0, p)),
                  pl.BlockSpec((None, 4, LANES), lambda p: (p, 0, 0))],
        out_specs=pl.BlockSpec((rows, sw), lambda p: (0, p)),
        out_shape=jax.ShapeDtypeStruct((rows, np_ * sw), BF16),
        compiler_params=_cparams(1),
        name="ssm_scan",
    )(v, alpha)
    y = pl.pallas_call(
        _ssm_y_kernel,
        grid=(np_, rows // tr),
        in_specs=[pl.BlockSpec((tr, pw), lambda p, r: (r, p)),
                  pl.BlockSpec((tr, sw), lambda p, r: (r, p)),
                  per_pair(kflat), per_pair(cp), per_pair(wp), whole(til), whole(rep), per_pair(skip)],
        out_specs=pl.BlockSpec((tr, pw), lambda p, r: (r, p)),
        out_shape=jax.ShapeDtypeStruct((rows, np_ * pw), BF16),
        scratch_shapes=[pltpu.VMEM((pw, pw), BF16), pltpu.VMEM((sw, pw), BF16)],
        compiler_params=_cparams(2),
        name="ssm_y",
    )(u, s, kflat, cp, wp, til, rep, skip)
    return y


def _mixer(y_ref, zuv_ref, wglu_ref, bglu_ref, gssm_ref, lng_ref, lnb_ref, ws_ref, bs_ref, gsgu_ref, ya_scr):
    tm = zuv_ref.shape[0]
    n_blk = ya_scr.shape[0]
    nc = tm // SSM_CHUNK
    masks = _atom_masks(nc)
    for b in range(n_blk):
        for v in range(SSM_CHUNK // ATOMS):
            src = [y_ref[:, (ATOMS * b + pi) * PAIR_W + v * LANES:
                         (ATOMS * b + pi) * PAIR_W + (v + 1) * LANES].astype(F32)
                   for pi in range(ATOMS)]
            dst = _atom_transpose(src, masks)
            for jj in range(ATOMS):
                ya_scr[b, pl.ds(ATOMS * v + jj, nc, stride=SSM_CHUNK), :] = dst[jj]
    gl = jnp.concatenate([ya_scr[b] for b in range(n_blk)], axis=1)
    gate = jnp.dot(gl.astype(BF16), wglu_ref[...], preferred_element_type=F32) + bglu_ref[...]
    ra = _rms(gl * jax.nn.sigmoid(gate), gssm_ref[...]).astype(BF16)
    d_sgu = zuv_ref.shape[1] // 2
    u = zuv_ref[:, :d_sgu].astype(F32)
    gv = zuv_ref[:, d_sgu:].astype(F32)
    xc = gv - jnp.mean(gv, axis=-1, keepdims=True)
    v = xc * lax.rsqrt(jnp.mean(xc * xc, axis=-1, keepdims=True) + LN_EPS)
    v = (v * lng_ref[...] + lnb_ref[...]).astype(BF16)
    lo = lax.broadcasted_iota(I32, (CHUNK, LANES), 1) < (LANES // 2)
    zero = jnp.zeros((CHUNK, LANES), BF16)
    rows = []
    for c in range(tm // CHUNK):
        cols = []
        for j in range(d_sgu // LANES):
            vp = v[c * CHUNK:(c + 1) * CHUNK, j * LANES:(j + 1) * LANES]
            rhs = jnp.concatenate([jnp.where(lo, vp, zero), jnp.where(lo, zero, vp)], axis=0)
            cols.append(jnp.dot(ws_ref[j], rhs, preferred_element_type=F32))
        rows.append(jnp.concatenate(cols, axis=1) + bs_ref[...])
    s = jnp.concatenate(rows, axis=0)
    rb = _rms(u * s, gsgu_ref[...]).astype(BF16)
    return jnp.concatenate([ra, rb], axis=1)


def _sorted_rows(tile):
    return TOP_K * tile + MXU_DIM


def _route_kernel(xp_ref, xs_ref, yp_ref, ys_ref, zp_ref, zs_ref, wglu_ref, bglu_ref, gssm_ref, lng_ref,
                  lnb_ref, ws_ref, bs_ref, gsgu_ref, wout_ref, gffn_ref, whi_ref, wcat_ref, br_ref, tri_ref,
                  ltri_ref, x1_ref, xsort_ref, meta_ref, c8_ref, ya_scr, *, n_prompt_tiles, n_tiles, n_exp):
    i = pl.program_id(0)
    tm = x1_ref.shape[0]
    p_rows = xsort_ref.shape[0]

    def body(x_ref, y_ref, zuv_ref):
        mixed = _mixer(y_ref, zuv_ref, wglu_ref, bglu_ref, gssm_ref, lng_ref, lnb_ref, ws_ref, bs_ref,
                       gsgu_ref, ya_scr)
        x1 = x_ref[...] + jnp.dot(mixed, wout_ref[...], preferred_element_type=F32)
        x1_ref[...] = x1
        h2 = _rms(x1, gffn_ref[...])
        hi = h2.astype(BF16)
        lo = (h2 - hi.astype(F32)).astype(BF16)
        hw = jnp.dot(hi, wcat_ref[...], preferred_element_type=F32)
        lt = (hw[:, :LANES] + (jnp.dot(lo, whi_ref[...], preferred_element_type=F32) + hw[:, LANES:])
              + br_ref[...])
        logits = lt.T[:n_exp]
        eio = lax.broadcasted_iota(I32, (n_exp, tm), 0)
        vals, idxs = [], []
        l = logits
        for _ in range(TOP_K):
            m = jnp.max(l, axis=0, keepdims=True)
            idx = jnp.min(jnp.where(l == m, eio, n_exp), axis=0, keepdims=True)
            vals.append(m)
            idxs.append(idx)
            l = jnp.where(eio == idx, -jnp.inf, l)
        ex = [jnp.exp(v - vals[0]) for v in vals]
        den = ex[0] + ex[1] + ex[2] + ex[3]
        ws = [e / den for e in ex]
        hot = [eio == idx for idx in idxs]
        cnt = sum(h.astype(F32) for h in hot)
        prefix = jnp.dot(cnt.astype(BF16), tri_ref[...], preferred_element_type=F32)
        c = jnp.sum(cnt, axis=1, keepdims=True)
        c8 = jnp.floor((c + (SUBLANES - 1)) * (1.0 / SUBLANES)) * SUBLANES
        c8b = jnp.broadcast_to(c8, (n_exp, LANES))
        c8_ref[...] = c8b
        run0 = jnp.dot(ltri_ref[...], c8b.astype(BF16), preferred_element_type=F32)[:, 0:1]
        base = run0 + prefix
        pos = [jnp.sum(jnp.where(h, base, 0.0), axis=0, keepdims=True) for h in hot]
        meta = jnp.concatenate(ws + pos + [jnp.zeros((LANES - 2 * TOP_K, tm), F32)], axis=0)
        meta_ref[...] = meta.T
        rio = lax.broadcasted_iota(I32, (SORT_CHUNK, tm), 0).astype(F32).astype(BF16)
        one = jnp.ones((SORT_CHUNK, tm), BF16)
        zero = jnp.zeros((SORT_CHUNK, tm), BF16)
        for r in range(p_rows // SORT_CHUNK):
            lo_r = float(r * SORT_CHUNK)
            rel = [jnp.where((p >= lo_r) & (p < lo_r + SORT_CHUNK), p - lo_r, -1.0).astype(BF16) for p in pos]
            sel = (rio == rel[0]) | (rio == rel[1]) | (rio == rel[2]) | (rio == rel[3])
            srt = jnp.dot(jnp.where(sel, one, zero), hi, preferred_element_type=F32)
            xsort_ref[r * SORT_CHUNK:(r + 1) * SORT_CHUNK, :] = _pack_bf16_pairs(srt)

    @pl.when(i < n_prompt_tiles)
    def _():
        body(xp_ref, yp_ref, zp_ref)

    @pl.when((i >= n_prompt_tiles) & (i < n_tiles))
    def _():
        body(xs_ref, ys_ref, zs_ref)

    @pl.when(i == n_tiles)
    def _():
        xsort_ref[...] = jnp.zeros_like(xsort_ref)


def _dual_specs(tile, width, n_prompt_tiles, n_tiles):
    last_p = n_prompt_tiles - 1
    last_s = n_tiles - n_prompt_tiles - 1
    sp = pl.BlockSpec((tile, width), lambda i: (jnp.minimum(i, last_p), 0))
    ss = pl.BlockSpec((tile, width), lambda i: (jnp.clip(i - n_prompt_tiles, 0, last_s), 0))
    return sp, ss


def _route(xp, xs, y_p, y_s, zuv_p, zuv_s, mix_params, w_out, g_ffn, wr_hi, wr_cat, br_pad, n_exp):
    tp, d = xp.shape
    t = tp + xs.shape[0]
    tm = TOKEN_TILE
    npt = tp // tm
    nt = t // tm
    p_rows = _sorted_rows(tm)
    d_ssm = mix_params[0].shape[0]
    iota = lambda n, ax: lax.broadcasted_iota(I32, (n, n), ax)
    tri = (iota(tm, 0) < iota(tm, 1)).astype(BF16)
    ltri = (iota(n_exp, 1) < iota(n_exp, 0)).astype(BF16)
    const = lambda shape: pl.BlockSpec(shape, lambda i: (0,) * len(shape))
    tile = lambda i: jnp.minimum(i, nt - 1)
    dual = lambda rows, width: _dual_specs(rows, width, npt, nt)
    return pl.pallas_call(
        functools.partial(_route_kernel, n_prompt_tiles=npt, n_tiles=nt, n_exp=n_exp),
        grid=(nt + 1,),
        in_specs=[*dual(tm, d), *dual(tm // SSM_CHUNK, y_p.shape[1]), *dual(tm, zuv_p.shape[1]),
                  *[const(a.shape) for a in mix_params],
                  const(w_out.shape), const((1, d)), const(wr_hi.shape), const(wr_cat.shape),
                  const((1, LANES)), const((tm, tm)), const((n_exp, n_exp))],
        out_specs=[pl.BlockSpec((tm, d), lambda i: (tile(i), 0)),
                   pl.BlockSpec((p_rows, d // 2), lambda i: (i, 0)),
                   pl.BlockSpec((tm, LANES), lambda i: (tile(i), 0)),
                   pl.BlockSpec((None, n_exp, LANES), lambda i: (tile(i), 0, 0))],
        out_shape=[jax.ShapeDtypeStruct((t, d), F32),
                   jax.ShapeDtypeStruct(((nt + 1) * p_rows, d // 2), U32),
                   jax.ShapeDtypeStruct((t, LANES), F32),
                   jax.ShapeDtypeStruct((nt, n_exp, LANES), F32)],
        scratch_shapes=[pltpu.VMEM((d_ssm // LANES, tm, LANES), F32)],
        compiler_params=_cparams(),
        name="route",
    )(xp, xs, y_p, y_s, zuv_p, zuv_s, *mix_params, w_out, g_ffn, wr_hi, wr_cat, br_pad, tri, ltri)


def _slot_tables(c8, p_rows, n_blocks):
    nt, n_exp = c8.shape
    run0 = jnp.cumsum(c8, axis=1) - c8
    seg_len = c8.T
    cum = jnp.cumsum(seg_len, axis=1)
    tot = cum[:, -1]
    padded = (tot + MOE_TILE - 1) // MOE_TILE * MOE_TILE
    ends = jnp.cumsum(padded)
    starts = ends - padded
    n_used = (ends[-1] // MOE_TILE).astype(I32)
    blk = jnp.arange(n_blocks, dtype=I32)
    be = jnp.minimum(jnp.sum(ends[None, :] <= (blk * MOE_TILE)[:, None], axis=1), n_exp - 1).astype(I32)
    block_e = jnp.where(blk < n_used, be, be[jnp.maximum(n_used - 1, 0)])
    onehot = (be[:, None] == jnp.arange(n_exp, dtype=I32)[None, :]).astype(F32)
    pick = lambda tbl: jnp.dot(onehot, tbl.astype(F32), precision=lax.Precision.HIGHEST)
    run_end = pick(cum)
    run_beg = run_end - pick(seg_len)
    shift = pick((jnp.arange(nt, dtype=I32) * p_rows)[None, :] + run0.T) - run_beg
    grow = (blk[:, None] * MOE_TILE + jnp.arange(MOE_GROUPS, dtype=I32)[None, :] * SUBLANES).astype(F32)
    rel = grow - pick(starts[:, None])
    inside = (run_beg[:, None, :] <= rel[:, :, None]) & (rel[:, :, None] < run_end[:, None, :])
    src_row = (rel + jnp.sum(jnp.where(inside, shift[:, None, :], 0.0), axis=2)).astype(I32)
    valid = jnp.any(inside, axis=2) & (blk < n_used)[:, None]
    scratch0 = nt * p_rows // SUBLANES
    zero_group = scratch0 + 2 * MOE_GROUPS
    g_in = jnp.where(valid, src_row // SUBLANES, zero_group).astype(I32)
    g_scr = scratch0 + (blk[:, None] % 2) * MOE_GROUPS + jnp.arange(MOE_GROUPS, dtype=I32)[None, :]
    g_out = jnp.where(valid, src_row // SUBLANES, g_scr).astype(I32)
    used = jnp.concatenate([n_used.reshape(1), jnp.sum(valid, axis=1).astype(I32)])
    return block_e, used, g_in, g_out


def _moe_kernel(be_ref, nu_ref, gin_ref, gnext_ref, gout_ref, xs_hbm, wgu_ref, bgu_ref, wd_ref, bd_ref,
                eo_hbm, xbuf, obuf, wgu_s, wd_s, isem, osem):
    j = pl.program_id(0)
    nu = nu_ref[0]
    f = wd_ref.shape[0]

    @pl.when((j < nu) & ((j == 0) | (be_ref[j] != be_ref[jnp.maximum(j - 1, 0)])))
    def _():
        for c in range(0, wgu_ref.shape[0], CAST_ROWS):
            wgu_s[c:c + CAST_ROWS, :] = wgu_ref[c:c + CAST_ROWS, :].astype(BF16)
        for c in range(0, wd_ref.shape[0], CAST_ROWS):
            wd_s[c:c + CAST_ROWS, :] = wd_ref[c:c + CAST_ROWS, :].astype(BF16)

    def in_copy(tbl_ref, r, sl):
        return pltpu.make_async_copy(xs_hbm.at[tbl_ref[0, r]], xbuf.at[sl, r], isem.at[sl])

    def out_copy(r, sl):
        return pltpu.make_async_copy(obuf.at[sl, r], eo_hbm.at[gout_ref[0, r]], osem.at[sl])

    def wait_in(sl):
        pltpu.make_async_copy(xs_hbm.at[pl.ds(0, MOE_GROUPS)], xbuf.at[sl], isem.at[sl]).wait()

    def wait_out(sl):
        pltpu.make_async_copy(obuf.at[sl], eo_hbm.at[pl.ds(0, MOE_GROUPS)], osem.at[sl]).wait()

    @pl.when(j == 0)
    def _():
        for r in range(MOE_GROUPS):
            in_copy(gin_ref, r, 0).start()
        obuf[...] = jnp.zeros_like(obuf)

    def step(slot):
        other = 1 - slot

        @pl.when(j + 1 < nu)
        def _():
            for r in range(MOE_GROUPS):
                in_copy(gnext_ref, r, other).start()

        wait_in(slot)

        @pl.when(j >= 2)
        def _():
            wait_out(slot)

        def mlp(n_groups):
            x = _unpack_bf16_pairs(xbuf[slot, 0:n_groups].reshape(n_groups * SUBLANES, xbuf.shape[-1]))
            gate = jnp.dot(x, wgu_s[:, :f], preferred_element_type=F32) + bgu_ref[:, :f]
            gate = jnp.minimum(gate, SWIGLU_LIMIT)
            sg = gate * jax.nn.sigmoid(SWIGLU_ALPHA * gate)
            up = jnp.dot(x, wgu_s[:, f:], preferred_element_type=F32) + bgu_ref[:, f:]
            act = sg * (jnp.clip(up, -SWIGLU_LIMIT, SWIGLU_LIMIT) + 1.0)
            out = jnp.dot(act.astype(BF16), wd_s[...], preferred_element_type=F32) + bd_ref[...]
            obuf[slot, 0:n_groups] = _pack_bf16_pairs(out.astype(BF16).astype(F32)).reshape(
                (n_groups,) + obuf.shape[2:])

        used_groups = nu_ref[1 + j]
        pl.when(used_groups > MOE_GROUPS // 2)(functools.partial(mlp, MOE_GROUPS))
        pl.when(used_groups <= MOE_GROUPS // 2)(functools.partial(mlp, MOE_GROUPS // 2))
        for r in range(MOE_GROUPS):
            out_copy(r, slot).start()

        @pl.when(j == nu - 1)
        def _():
            wait_out(slot)

            @pl.when(j >= 1)
            def _():
                wait_out(other)

    for parity in range(2):
        pl.when((j < nu) & (lax.rem(j, 2) == parity))(functools.partial(step, parity))


def _moe(block_e, used, g_in, g_out, xsorted, wgu, bgu, wd, bd):
    n_blocks = block_e.shape[0]
    dh = xsorted.shape[1]
    d = 2 * dh
    f2 = wgu.shape[2]
    f = wd.shape[1]
    g_in3 = g_in.reshape(n_blocks, 1, MOE_GROUPS)
    g_next3 = jnp.concatenate([g_in3[1:], g_in3[:1]], axis=0)
    g_out3 = g_out.reshape(n_blocks, 1, MOE_GROUPS)
    xs3 = xsorted.reshape(xsorted.shape[0] // SUBLANES, SUBLANES, dh)
    tbl = pl.BlockSpec((None, 1, MOE_GROUPS), lambda j, be, nu: (j, 0, 0), memory_space=pltpu.SMEM)
    return pl.pallas_call(
        _moe_kernel,
        grid_spec=pltpu.PrefetchScalarGridSpec(
            num_scalar_prefetch=2, grid=(n_blocks,),
            in_specs=[tbl, tbl, tbl,
                      pl.BlockSpec(memory_space=pl.ANY),
                      pl.BlockSpec((None, d, f2), lambda j, be, nu: (be[j], 0, 0)),
                      pl.BlockSpec((None, 1, f2), lambda j, be, nu: (be[j], 0, 0)),
                      pl.BlockSpec((None, f, d), lambda j, be, nu: (be[j], 0, 0)),
                      pl.BlockSpec((None, 1, d), lambda j, be, nu: (be[j], 0, 0))],
            out_specs=pl.BlockSpec(memory_space=pl.ANY),
            scratch_shapes=[pltpu.VMEM((2, MOE_GROUPS, SUBLANES, dh), U32),
                            pltpu.VMEM((2, MOE_GROUPS, SUBLANES, dh), U32),
                            pltpu.VMEM((d, f2), BF16), pltpu.VMEM((f, d), BF16),
                            pltpu.SemaphoreType.DMA((2,)), pltpu.SemaphoreType.DMA((2,))]),
        out_shape=jax.ShapeDtypeStruct(xs3.shape, U32),
        input_output_aliases={5: 0},
        compiler_params=_cparams(),
        name="moe",
    )(block_e, used, g_in3, g_next3, g_out3, xs3, wgu, bgu, wd, bd).reshape(xsorted.shape)


def _combine_kernel(meta_ref, x1_ref, g_ref, eo_ref, y_ref):
    tm = x1_ref.shape[0]
    p_rows = eo_ref.shape[0]
    lane = lax.broadcasted_iota(I32, (tm, SORT_CHUNK), 1).astype(F32).astype(BF16)
    ws = [meta_ref[:, k:k + 1].astype(BF16) for k in range(TOP_K)]
    pos = [meta_ref[:, TOP_K + k:TOP_K + k + 1] for k in range(TOP_K)]
    acc = x1_ref[...]
    for r in range(p_rows // SORT_CHUNK):
        buf = _unpack_bf16_pairs(eo_ref[r * SORT_CHUNK:(r + 1) * SORT_CHUNK, :])
        lo_r = float(r * SORT_CHUNK)
        wm = jnp.zeros((tm, SORT_CHUNK), BF16)
        for k in range(TOP_K):
            rel = jnp.where((pos[k] >= lo_r) & (pos[k] < lo_r + SORT_CHUNK), pos[k] - lo_r, -1.0).astype(BF16)
            wm = jnp.where(lane == rel, ws[k], wm)
        acc = acc + jnp.dot(wm, buf, preferred_element_type=F32)
    y_ref[...] = _rms(acc, g_ref[...])


def _combine(meta, x1, g_final, eo, tile_off, n_tokens):
    d = x1.shape[1]
    tm = TOKEN_TILE
    p_rows = _sorted_rows(tm)
    return pl.pallas_call(
        _combine_kernel,
        grid=(n_tokens // tm,),
        in_specs=[pl.BlockSpec((tm, LANES), lambda i: (i + tile_off, 0)),
                  pl.BlockSpec((tm, d), lambda i: (i + tile_off, 0)),
                  pl.BlockSpec((1, d), lambda i: (0, 0)),
                  pl.BlockSpec((p_rows, d // 2), lambda i: (i + tile_off, 0))],
        out_specs=pl.BlockSpec((tm, d), lambda i: (i, 0)),
        out_shape=jax.ShapeDtypeStruct((n_tokens, d), F32),
        compiler_params=_cparams(),
        name="combine",
    )(meta, x1, g_final, eo)


def kernel(x_prompt, x_sample, g_mix, w_in, ssm_lam_re, ssm_lam_im, ssm_log_dt, ssm_b_re, ssm_b_im,
           ssm_c_re, ssm_c_im, ssm_d, w_glu, b_glu, sgu_ln_g, sgu_ln_b, sgu_w_s, sgu_b_s,
           g_out_ssm, g_out_sgu, w_out, g_ffn, w_router, b_router, w_gate_up, b_gate_up,
           w_down, b_down, g_final):
    assert g_mix.shape[0] == 1, "single-layer trunk"
    bp, lp, d = x_prompt.shape
    bs, ls, _ = x_sample.shape
    tp, ts = bp * lp, bs * ls
    t = tp + ts
    d_ssm = ssm_d.shape[1]
    d_sgu = sgu_ln_g.shape[1]
    n_exp = w_router.shape[2]
    assert all(n % tile == 0 for n in (tp, ts) for tile in (SEQ_TILE, TOKEN_TILE))
    assert lp % CHUNK == 0 and ls % CHUNK == 0 and TOKEN_TILE % CHUNK == 0 and CHUNK % SSM_CHUNK == 0
    assert SEQ_TILE % SSM_CHUNK == 0
    assert SSM_CHUNK * SSM_GROUP == MXU_DIM and 2 * ssm_lam_re.shape[-1] == LANES
    assert d_sgu // SGU_HEADS == LANES // 2 and n_exp <= LANES
    assert n_exp * (SUBLANES - 1) <= MXU_DIM and _sorted_rows(TOKEN_TILE) % SORT_CHUNK == 0
    assert (2 * MOE_GROUPS + 1) * SUBLANES <= _sorted_rows(TOKEN_TILE)

    xp = x_prompt.reshape(tp, d)
    xs = x_sample.reshape(ts, d)
    row = lambda a: a.reshape(1, -1).astype(F32)

    w_in_b = w_in[0].astype(BF16)
    u_p, zuv_p = _inproj(xp, row(g_mix[0]), w_in_b, d_ssm)
    u_s, zuv_s = _inproj(xs, row(g_mix[0]), w_in_b, d_ssm)

    mats = _ssm_matrices(ssm_lam_re[0], ssm_lam_im[0], ssm_log_dt[0], ssm_b_re[0], ssm_b_im[0],
                         ssm_c_re[0], ssm_c_im[0], ssm_d[0])
    y_p = _ssm_trunk(u_p, bp, mats)
    y_s = _ssm_trunk(u_s, bs, mats)

    ws = sgu_w_s[0]
    ws_pairs = jnp.concatenate([ws[0::2], ws[1::2]], axis=2).astype(BF16)
    bias_s = jnp.repeat(sgu_b_s[0].T, d_sgu // SGU_HEADS, axis=1).astype(F32)
    mix_params = (w_glu[0].astype(BF16), row(b_glu[0]), row(g_out_ssm[0]), row(sgu_ln_g[0]), row(sgu_ln_b[0]),
                  ws_pairs, bias_s, row(g_out_sgu[0]))

    wr_pad = jnp.pad(w_router[0].astype(F32), ((0, 0), (0, LANES - n_exp)))
    wr_hi = wr_pad.astype(BF16)
    wr_cat = jnp.concatenate([wr_hi, (wr_pad - wr_hi.astype(F32)).astype(BF16)], axis=1)
    br_pad = jnp.pad(b_router[0].astype(F32), (0, LANES - n_exp)).reshape(1, LANES)
    x1, xsorted, meta, c8 = _route(xp, xs, y_p, y_s, zuv_p, zuv_s, mix_params, w_out[0].astype(BF16),
                                   row(g_ffn[0]), wr_hi, wr_cat, br_pad, n_exp)

    nt = t // TOKEN_TILE
    p_rows = _sorted_rows(TOKEN_TILE)
    max_rows = t * TOP_K + nt * n_exp * (SUBLANES - 1) + n_exp * (MOE_TILE - 1)
    n_blocks = -(-max_rows // MOE_TILE)
    block_e, used, g_in, g_out = _slot_tables(c8[:, :, 0].astype(I32), p_rows, n_blocks)
    eo = _moe(block_e, used, g_in, g_out, xsorted, w_gate_up[0],
              b_gate_up[0][:, None, :].astype(F32), w_down[0], b_down[0][:, None, :].astype(F32))

    gf = row(g_final)
    y_prompt = _combine(meta, x1, gf, eo, 0, tp)
    y_sample = _combine(meta, x1, gf, eo, tp // TOKEN_TILE, ts)
    return y_prompt.reshape(bp, lp, d), y_sample.reshape(bs, ls, d)
```

```python
import functools
import math

import jax
import jax.numpy as jnp
from jax import lax
from jax.experimental import pallas as pl
from jax.experimental.pallas import tpu as pltpu

F32 = jnp.float32
BF16 = jnp.bfloat16
I32 = jnp.int32
U32 = jnp.uint32

SSM_GROUP = 16
SGU_HEADS = 8
CHUNK = 128
TOP_K = 4
SWIGLU_LIMIT = 7.0
SWIGLU_ALPHA = 1.702
RMS_EPS = 1e-6
LN_EPS = 1e-5

LANES = 128
SUBLANES = 8
MXU_DIM = 256
DMA_PRIORITIES = 2
SSM_CHUNK = MXU_DIM // SSM_GROUP
PAIR_W = 2 * SSM_CHUNK * SSM_GROUP
ATOM = 2 * SSM_GROUP
ATOMS = LANES // ATOM

TOKEN_TILE = 512
SEQ_TILE = 1024
MOE_TILE = 1024
MOE_GROUPS = MOE_TILE // SUBLANES
SORT_CHUNK = 256
CAST_ROWS = 128
SSM_ROW_TILE = 2048
SCAN_SEQS = 4
VMEM_LIMIT = 56 * 1024 * 1024


def _cparams(n_axes=1, vmem=None):
    return pltpu.CompilerParams(
        dimension_semantics=("arbitrary",) * n_axes,
        vmem_limit_bytes=vmem if vmem is not None else VMEM_LIMIT,
    )


def _rms(x, g):
    return x * lax.rsqrt(jnp.mean(x * x, axis=-1, keepdims=True) + RMS_EPS) * g


def _gelu(x):
    return x * (lax.erf(x * (1.0 / math.sqrt(2.0))) + 1.0) * 0.5


def _atom_masks(rows):
    lane = lax.broadcasted_iota(I32, (rows, LANES), 1)
    return [(lane >= a * ATOM) & (lane < (a + 1) * ATOM) for a in range(ATOMS)]


def _atom_transpose(src, masks):
    dst = []
    for b in range(ATOMS):
        acc = None
        for a in range(ATOMS):
            r = (a - b) % ATOMS
            piece = src[a] if r == 0 else pltpu.roll(src[a], ATOM * r, axis=1)
            acc = piece if acc is None else jnp.where(masks[a], piece, acc)
        dst.append(acc)
    return dst


def _pack_bf16_pairs(x):
    w = x.shape[1] // 2
    lo = lax.bitcast_convert_type(x[:, :w], U32) >> 16
    hi = lax.bitcast_convert_type(x[:, w:], U32) & jnp.uint32(0xFFFF0000)
    return hi | lo


def _unpack_bf16_pairs(u):
    lo = lax.bitcast_convert_type(u << 16, F32)
    hi = lax.bitcast_convert_type(u & jnp.uint32(0xFFFF0000), F32)
    return jnp.concatenate([lo.astype(BF16), hi.astype(BF16)], axis=1)


def _inproj_kernel(x_ref, g_ref, w_ref, u_ref, zuv_ref, za_scr, *, d_ssm):
    tm = x_ref.shape[0]
    nc = tm // SSM_CHUNK
    h = _rms(x_ref[...], g_ref[...])
    z = jnp.dot(h.astype(BF16), w_ref[...], preferred_element_type=F32)
    zuv_ref[...] = _gelu(z[:, d_ssm:]).astype(BF16)
    n_blk = d_ssm // LANES
    for b in range(n_blk):
        za_scr[b] = z[:, b * LANES:(b + 1) * LANES]
    masks = _atom_masks(nc)
    n_quads = SSM_CHUNK // ATOMS
    for b in range(n_blk):
        for v in range(n_quads):
            src = [za_scr[b, pl.ds(ATOMS * v + jj, nc, stride=SSM_CHUNK), :] for jj in range(ATOMS)]
            dst = _atom_transpose(src, masks)
            for pi in range(ATOMS):
                c0 = (ATOMS * b + pi) * PAIR_W + v * LANES
                u_ref[:, c0:c0 + LANES] = dst[pi].astype(BF16)


def _inproj(x, g_mix, w_in, d_ssm):
    t, d = x.shape
    tm = SEQ_TILE
    d_in = w_in.shape[1]
    nc = tm // SSM_CHUNK
    uw = d_ssm * SSM_CHUNK
    return pl.pallas_call(
        functools.partial(_inproj_kernel, d_ssm=d_ssm),
        grid=(t // tm,),
        in_specs=[pl.BlockSpec((tm, d), lambda i: (i, 0)),
                  pl.BlockSpec((1, d), lambda i: (0, 0)),
                  pl.BlockSpec((d, d_in), lambda i: (0, 0))],
        out_specs=[pl.BlockSpec((nc, uw), lambda i: (i, 0)),
                   pl.BlockSpec((tm, d_in - d_ssm), lambda i: (i, 0))],
        out_shape=[jax.ShapeDtypeStruct((t // SSM_CHUNK, uw), BF16),
                   jax.ShapeDtypeStruct((t, d_in - d_ssm), BF16)],
        scratch_shapes=[pltpu.VMEM((d_ssm // LANES, tm, LANES), F32)],
        compiler_params=_cparams(),
        name="inproj",
    )(x, g_mix, w_in)


def _ssm_matrices(lam_re, lam_im, log_dt, b_re, b_im, c_re, c_im, d_skip):
    hp = lax.Precision.HIGHEST
    _, g, n = lam_re.shape
    p = b_re.shape[-1]
    lc = SSM_CHUNK
    np_ = g // 2
    dt = jnp.exp(log_dt)[..., None]
    mag = jnp.exp(lam_re * dt)
    ar = mag * jnp.cos(lam_im * dt)
    ai = mag * jnp.sin(lam_im * dt)
    den = lam_re * lam_re + lam_im * lam_im
    nr = ar - 1.0
    fr = (nr * lam_re + ai * lam_im) / den
    fi = (ai * lam_re - nr * lam_im) / den
    bbr = fr[..., None] * b_re - fi[..., None] * b_im
    bbi = fr[..., None] * b_im + fi[..., None] * b_re

    prs, pis = [jnp.ones_like(ar)], [jnp.zeros_like(ai)]
    for _ in range(lc):
        pr, pi = prs[-1], pis[-1]
        prs.append(pr * ar - pi * ai)
        pis.append(pr * ai + pi * ar)
    pw_r, pw_i = jnp.stack(prs), jnp.stack(pis)

    lane_k = jnp.arange(lc * p)
    til_k = (lane_k[None, :] % p == jnp.arange(p)[:, None]).astype(F32)
    rep_k = (lane_k[None, :] // p == jnp.arange(lc)[:, None]).astype(F32)
    ex = lambda x, e: jnp.einsum('dgnk,kx->dgnx', x, e, precision=hp)
    ct_r = ex(jnp.transpose(c_re, (0, 1, 3, 2)), til_k)
    ct_i = ex(jnp.transpose(c_im, (0, 1, 3, 2)), til_k)
    at_r = ex(jnp.transpose(pw_r[:lc], (1, 2, 3, 0)), rep_k)
    at_i = ex(jnp.transpose(pw_i[:lc], (1, 2, 3, 0)), rep_k)
    kk = (jnp.einsum('dgnq,dgnx->dgqx', bbr, ct_r * at_r - ct_i * at_i, precision=hp)
          - jnp.einsum('dgnq,dgnx->dgqx', bbi, ct_r * at_i + ct_i * at_r, precision=hp))
    kk = kk.reshape(2, g, p, lc, p)
    kf, kb = kk[0], kk[1]
    eye = jnp.eye(2, dtype=F32)
    a2 = p * 2

    k_all = jnp.concatenate([kb[:, :, :0:-1], (kf[:, :, :1] + kb[:, :, :1]), kf[:, :, 1:]], axis=2)
    k_all = k_all.reshape(np_, 2, p, 2 * lc - 1, p)
    kflat = jnp.einsum('nsqup,st->nsqutp', k_all, eye).reshape(np_, a2, (2 * lc - 1) * a2)
    kflat = jnp.pad(kflat, ((0, 0), (0, 0), (0, a2)))

    def atoms_q(x):
        x = jnp.transpose(x, (0, 1, 3, 2)).reshape(2, np_, 2, p, n)
        return jnp.einsum('dnsqm,st->dnsqtm', x, eye).reshape(2, np_, a2, 2 * n)
    bq_r, bq_i = atoms_q(bbr), atoms_q(bbi)
    bq = jnp.stack([bq_r[0], bq_i[0], bq_r[1], bq_i[1]], axis=1)
    ef = lc - 1 - jnp.arange(lc)
    eb = jnp.arange(lc)
    rows_q = lambda w, e, d: jnp.transpose(w[e, d].reshape(lc, np_, 2 * n), (1, 0, 2))
    wq = jnp.stack([rows_q(pw_r, ef, 0), rows_q(pw_i, ef, 0),
                    rows_q(pw_r, eb, 1), rows_q(pw_i, eb, 1)], axis=1)

    def atoms_p(x):
        x = jnp.transpose(x, (0, 1, 3, 2)).reshape(2, np_, 2, n, p)
        return jnp.einsum('dnsmp,st->dnsmtp', x, eye).reshape(2, np_, 2 * n, a2)
    cp_r, cp_i = atoms_p(c_re), atoms_p(c_im)
    cp = jnp.stack([cp_r[0], cp_i[0], cp_r[1], cp_i[1]], axis=1)
    pf_e = jnp.arange(lc) + 1
    pb_e = lc - jnp.arange(lc)
    cols_p = lambda w, e, d: jnp.transpose(w[e, d].reshape(lc, np_, 2 * n), (1, 2, 0))
    wp = jnp.stack([cols_p(pw_r, pf_e, 0), cols_p(pw_i, pf_e, 0),
                    cols_p(pw_r, pb_e, 1), cols_p(pw_i, pb_e, 1)], axis=1)
    lane = jnp.arange(lc * a2)
    til = (lane[None, :] % a2 == jnp.arange(a2)[:, None]).astype(BF16)
    rep = (lane[None, :] // a2 == jnp.arange(lc)[:, None]).astype(BF16)

    al = jnp.stack([pw_r[lc, 0], pw_i[lc, 0], pw_r[lc, 1], pw_i[lc, 1]])
    alpha = jnp.transpose(al.reshape(4, np_, 2 * n), (1, 0, 2))
    skip = jnp.broadcast_to(d_skip.reshape(np_, 1, a2), (np_, lc, a2)).reshape(np_, 1, lc * a2)
    return kflat, bq, wq, cp, wp, til, rep, alpha, skip.astype(F32)


def _ssm_v_kernel(u_ref, bq_ref, wq_ref, v_ref, q_scr):
    a2 = bq_ref.shape[1]
    w = bq_ref.shape[2]

    @pl.when(pl.program_id(1) == 0)
    def _():
        for d in range(2):
            br, bi = bq_ref[2 * d], bq_ref[2 * d + 1]
            for j in range(SSM_CHUNK):
                wr = wq_ref[2 * d, j:j + 1, :]
                wi = wq_ref[2 * d + 1, j:j + 1, :]
                q_scr[j * a2:(j + 1) * a2, (2 * d) * w:(2 * d + 1) * w] = (wr * br - wi * bi).astype(BF16)
                q_scr[j * a2:(j + 1) * a2, (2 * d + 1) * w:(2 * d + 2) * w] = (wr * bi + wi * br).astype(BF16)

    v_ref[...] = jnp.dot(u_ref[...], q_scr[...], preferred_element_type=F32)


def _ssm_scan_kernel(v_ref, a_ref, s_ref, *, n_chunks, bsz):
    w = LANES
    sub = SUBLANES
    n_groups = n_chunks // sub
    row = lax.broadcasted_iota(I32, (sub, w), 0)

    def cmul(ar, ai, xr, xi):
        return ar * xr - ai * xi, ar * xi + ai * xr

    def powers(ar, ai):
        p2 = cmul(ar, ai, ar, ai)
        p4 = cmul(*p2, *p2)
        p8 = cmul(*p4, *p4)
        p3 = cmul(*p2, ar, ai)
        p5 = cmul(*p4, ar, ai)
        p6 = cmul(*p4, *p2)
        p7 = cmul(*p4, *p3)
        seq = [(jnp.ones_like(ar), jnp.zeros_like(ai)), (ar, ai), p2, p3, p4, p5, p6, p7]
        tr = jnp.concatenate([s[0] for s in seq], axis=0)
        ti = jnp.concatenate([s[1] for s in seq], axis=0)
        return (ar, ai), p2, p4, p8, (tr, ti)

    def tile_scan(vr, vi, pw, reverse):
        a1, a2, a4, _, _ = pw

        def shift(x, s):
            if reverse:
                return jnp.where(row < sub - s, pltpu.roll(x, sub - s, axis=0), 0.0)
            return jnp.where(row >= s, pltpu.roll(x, s, axis=0), 0.0)

        xr, xi = shift(vr, 1), shift(vi, 1)
        for s, (ar, ai) in ((1, a1), (2, a2), (4, a4)):
            mr, mi = cmul(ar, ai, shift(xr, s), shift(xi, s))
            xr, xi = xr + mr, xi + mi
        e = 0 if reverse else sub - 1
        lr, li = cmul(a1[0], a1[1], xr[e:e + 1], xi[e:e + 1])
        return xr, xi, lr + vr[e:e + 1], li + vi[e:e + 1]

    pw_f = powers(a_ref[0:1, :], a_ref[1:2, :])
    pw_b = powers(a_ref[2:3, :], a_ref[3:4, :])
    tbr = jnp.concatenate([pw_b[4][0][sub - 1 - k:sub - k] for k in range(sub)], axis=0)
    tbi = jnp.concatenate([pw_b[4][1][sub - 1 - k:sub - k] for k in range(sub)], axis=0)

    def one_tile(row0, col0, pw, table, cr, ci, reverse):
        rows = pl.ds(pl.multiple_of(row0, sub), sub)
        xr, xi, lr, li = tile_scan(v_ref[rows, col0:col0 + w], v_ref[rows, col0 + w:col0 + 2 * w], pw, reverse)
        mr, mi = cmul(table[0], table[1], cr, ci)
        nr, ni = cmul(pw[3][0], pw[3][1], cr, ci)
        return xr + mr, xi + mi, nr + lr, ni + li

    pair = 2 * sub
    n_pairs = n_groups // 2

    def step(g, carry, b0):
        out = []
        for b in range(b0, b0 + len(carry) // 4):
            cfr, cfi, cbr, cbi = carry[4 * (b - b0):4 * (b - b0) + 4]
            rf = pl.multiple_of(b * n_chunks + g * pair, pair)
            rb = pl.multiple_of(b * n_chunks + (n_pairs - 1 - g) * pair, pair)
            ar, ai, cfr, cfi = one_tile(rf, 0, pw_f, pw_f[4], cfr, cfi, False)
            br, bi, cfr, cfi = one_tile(rf + sub, 0, pw_f, pw_f[4], cfr, cfi, False)
            s_ref[pl.ds(rf, pair), 0:w] = jnp.concatenate([ar, br], axis=0).astype(BF16)
            s_ref[pl.ds(rf, pair), w:2 * w] = jnp.concatenate([ai, bi], axis=0).astype(BF16)
            out += [cfr, cfi]
            br, bi, cbr, cbi = one_tile(rb + sub, 2 * w, pw_b, (tbr, tbi), cbr, cbi, True)
            ar, ai, cbr, cbi = one_tile(rb, 2 * w, pw_b, (tbr, tbi), cbr, cbi, True)
            s_ref[pl.ds(rb, pair), 2 * w:3 * w] = jnp.concatenate([ar, br], axis=0).astype(BF16)
            s_ref[pl.ds(rb, pair), 3 * w:4 * w] = jnp.concatenate([ai, bi], axis=0).astype(BF16)
            out += [cbr, cbi]
        return tuple(out)

    zero = jnp.zeros((1, w), F32)
    for b0 in range(0, bsz, SCAN_SEQS):
        nb = min(SCAN_SEQS, bsz - b0)
        lax.fori_loop(0, n_pairs, functools.partial(step, b0=b0), (zero,) * (4 * nb))


def _ssm_y_kernel(u_ref, s_ref, kflat_ref, cp_ref, wp_ref, til_ref, rep_ref, d_ref, y_ref, m_scr, p_scr):
    a2 = kflat_ref.shape[0]
    w = cp_ref.shape[1]

    def expand(x, e_ref):
        hi = x.astype(BF16)
        lo = (x - hi.astype(F32)).astype(BF16)
        return (jnp.dot(hi, e_ref[...], preferred_element_type=F32)
                + jnp.dot(lo, e_ref[...], preferred_element_type=F32))

    @pl.when(pl.program_id(1) == 0)
    def _():
        kflat = kflat_ref[...]
        for j in range(SSM_CHUNK):
            off = (SSM_CHUNK - 1 - j) * a2
            m_scr[j * a2:(j + 1) * a2, :] = kflat[:, off:off + PAIR_W].astype(BF16)
        for d in range(2):
            cr, ci = expand(cp_ref[2 * d], til_ref), expand(cp_ref[2 * d + 1], til_ref)
            wr, wi = expand(wp_ref[2 * d], rep_ref), expand(wp_ref[2 * d + 1], rep_ref)
            p_scr[(2 * d) * w:(2 * d + 1) * w, :] = (cr * wr - ci * wi).astype(BF16)
            p_scr[(2 * d + 1) * w:(2 * d + 2) * w, :] = (-(cr * wi + ci * wr)).astype(BF16)

    u = u_ref[...]
    y = jnp.dot(u, m_scr[...], preferred_element_type=F32)
    y = y + jnp.dot(s_ref[...], p_scr[...], preferred_element_type=F32)
    y_ref[...] = _gelu(y + d_ref[...] * u.astype(F32)).astype(BF16)


def _ssm_trunk(u, bsz, mats):
    kflat, bq, wq, cp, wp, til, rep, alpha, skip = mats
    rows = u.shape[0]
    nc = rows // bsz
    np_ = kflat.shape[0]
    pw = PAIR_W
    sw = 4 * bq.shape[3]
    tr = min(SSM_ROW_TILE, rows)
    per_pair = lambda a: pl.BlockSpec((None,) + a.shape[1:], lambda p, r: (p,) + (0,) * (a.ndim - 1))
    whole = lambda a: pl.BlockSpec(a.shape, lambda p, r: (0,) * a.ndim)
    v = pl.pallas_call(
        _ssm_v_kernel,
        grid=(np_, rows // tr),
        in_specs=[pl.BlockSpec((tr, pw), lambda p, r: (r, p)), per_pair(bq), per_pair(wq)],
        out_specs=pl.BlockSpec((tr, sw), lambda p, r: (r, p)),
        out_shape=jax.ShapeDtypeStruct((rows, np_ * sw), F32),
        scratch_shapes=[pltpu.VMEM((pw, sw), BF16)],
        compiler_params=_cparams(2),
        name="ssm_v",
    )(u, bq, wq)
    s = pl.pallas_call(
        functools.partial(_ssm_scan_kernel, n_chunks=nc, bsz=bsz),
        grid=(np_,),
        in_specs=[pl.BlockSpec((rows, sw), lambda p: (0, p)),
                  pl.BlockSpec((None, 4, LANES), lambda p: (p, 0, 0))],
        out_specs=pl.BlockSpec((rows, sw), lambda p: (0, p)),
        out_shape=jax.ShapeDtypeStruct((rows, np_ * sw), BF16),
        compiler_params=_cparams(1),
        name="ssm_scan",
    )(v, alpha)
    y = pl.pallas_call(
        _ssm_y_kernel,
        grid=(np_, rows // tr),
        in_specs=[pl.BlockSpec((tr, pw), lambda p, r: (r, p)),
                  pl.BlockSpec((tr, sw), lambda p, r: (r, p)),
                  per_pair(kflat), per_pair(cp), per_pair(wp), whole(til), whole(rep), per_pair(skip)],
        out_specs=pl.BlockSpec((tr, pw), lambda p, r: (r, p)),
        out_shape=jax.ShapeDtypeStruct((rows, np_ * pw), BF16),
        scratch_shapes=[pltpu.VMEM((pw, pw), BF16), pltpu.VMEM((sw, pw), BF16)],
        compiler_params=_cparams(2),
        name="ssm_y",
    )(u, s, kflat, cp, wp, til, rep, skip)
    return y


def _mixer(y_ref, zuv_ref, wglu_ref, bglu_ref, gssm_ref, lng_ref, lnb_ref, ws_ref, bs_ref, gsgu_ref, ya_scr):
    tm = zuv_ref.shape[0]
    n_blk = ya_scr.shape[0]
    nc = tm // SSM_CHUNK
    masks = _atom_masks(nc)
    for b in range(n_blk):
        for v in range(SSM_CHUNK // ATOMS):
            src = [y_ref[:, (ATOMS * b + pi) * PAIR_W + v * LANES:
                         (ATOMS * b + pi) * PAIR_W + (v + 1) * LANES].astype(F32)
                   for pi in range(ATOMS)]
            dst = _atom_transpose(src, masks)
            for jj in range(ATOMS):
                ya_scr[b, pl.ds(ATOMS * v + jj, nc, stride=SSM_CHUNK), :] = dst[jj]
    gl = jnp.concatenate([ya_scr[b] for b in range(n_blk)], axis=1)
    gate = jnp.dot(gl.astype(BF16), wglu_ref[...], preferred_element_type=F32) + bglu_ref[...]
    ra = _rms(gl * jax.nn.sigmoid(gate), gssm_ref[...]).astype(BF16)
    d_sgu = zuv_ref.shape[1] // 2
    u = zuv_ref[:, :d_sgu].astype(F32)
    gv = zuv_ref[:, d_sgu:].astype(F32)
    xc = gv - jnp.mean(gv, axis=-1, keepdims=True)
    v = xc * lax.rsqrt(jnp.mean(xc * xc, axis=-1, keepdims=True) + LN_EPS)
    v = (v * lng_ref[...] + lnb_ref[...]).astype(BF16)
    lo = lax.broadcasted_iota(I32, (CHUNK, LANES), 1) < (LANES // 2)
    zero = jnp.zeros((CHUNK, LANES), BF16)
    rows = []
    for c in range(tm // CHUNK):
        cols = []
        for j in range(d_sgu // LANES):
            vp = v[c * CHUNK:(c + 1) * CHUNK, j * LANES:(j + 1) * LANES]
            rhs = jnp.concatenate([jnp.where(lo, vp, zero), jnp.where(lo, zero, vp)], axis=0)
            cols.append(jnp.dot(ws_ref[j], rhs, preferred_element_type=F32))
        rows.append(jnp.concatenate(cols, axis=1) + bs_ref[...])
    s = jnp.concatenate(rows, axis=0)
    rb = _rms(u * s, gsgu_ref[...]).astype(BF16)
    return jnp.concatenate([ra, rb], axis=1)


def _sorted_rows(tile):
    return TOP_K * tile + MXU_DIM


def _route_kernel(xp_ref, xs_ref, yp_ref, ys_ref, zp_ref, zs_ref, wglu_ref, bglu_ref, gssm_ref, lng_ref,
                  lnb_ref, ws_ref, bs_ref, gsgu_ref, wout_ref, gffn_ref, whi_ref, wcat_ref, br_ref, tri_ref,
                  ltri_ref, x1_ref, xsort_ref, meta_ref, c8_ref, ya_scr, *, n_prompt_tiles, n_tiles, n_exp):
    i = pl.program_id(0)
    tm = x1_ref.shape[0]
    p_rows = xsort_ref.shape[0]

    def body(x_ref, y_ref, zuv_ref):
        mixed = _mixer(y_ref, zuv_ref, wglu_ref, bglu_ref, gssm_ref, lng_ref, lnb_ref, ws_ref, bs_ref,
                       gsgu_ref, ya_scr)
        x1 = x_ref[...] + jnp.dot(mixed, wout_ref[...], preferred_element_type=F32)
        x1_ref[...] = x1
        h2 = _rms(x1, gffn_ref[...])
        hi = h2.astype(BF16)
        lo = (h2 - hi.astype(F32)).astype(BF16)
        hw = jnp.dot(hi, wcat_ref[...], preferred_element_type=F32)
        lt = (hw[:, :LANES] + (jnp.dot(lo, whi_ref[...], preferred_element_type=F32) + hw[:, LANES:])
              + br_ref[...])
        logits = lt.T[:n_exp]
        eio = lax.broadcasted_iota(I32, (n_exp, tm), 0)
        vals, idxs = [], []
        l = logits
        for _ in range(TOP_K):
            m = jnp.max(l, axis=0, keepdims=True)
            idx = jnp.min(jnp.where(l == m, eio, n_exp), axis=0, keepdims=True)
            vals.append(m)
            idxs.append(idx)
            l = jnp.where(eio == idx, -jnp.inf, l)
        ex = [jnp.exp(v - vals[0]) for v in vals]
        den = ex[0] + ex[1] + ex[2] + ex[3]
        ws = [e / den for e in ex]
        hot = [eio == idx for idx in idxs]
        cnt = sum(h.astype(F32) for h in hot)
        prefix = jnp.dot(cnt.astype(BF16), tri_ref[...], preferred_element_type=F32)
        c = jnp.sum(cnt, axis=1, keepdims=True)
        c8 = jnp.floor((c + (SUBLANES - 1)) * (1.0 / SUBLANES)) * SUBLANES
        c8b = jnp.broadcast_to(c8, (n_exp, LANES))
        c8_ref[...] = c8b
        run0 = jnp.dot(ltri_ref[...], c8b.astype(BF16), preferred_element_type=F32)[:, 0:1]
        base = run0 + prefix
        pos = [jnp.sum(jnp.where(h, base, 0.0), axis=0, keepdims=True) for h in hot]
        meta = jnp.concatenate(ws + pos + [jnp.zeros((LANES - 2 * TOP_K, tm), F32)], axis=0)
        meta_ref[...] = meta.T
        rio = lax.broadcasted_iota(I32, (SORT_CHUNK, tm), 0).astype(F32).astype(BF16)
        one = jnp.ones((SORT_CHUNK, tm), BF16)
        zero = jnp.zeros((SORT_CHUNK, tm), BF16)
        for r in range(p_rows // SORT_CHUNK):
            lo_r = float(r * SORT_CHUNK)
            rel = [jnp.where((p >= lo_r) & (p < lo_r + SORT_CHUNK), p - lo_r, -1.0).astype(BF16) for p in pos]
            sel = (rio == rel[0]) | (rio == rel[1]) | (rio == rel[2]) | (rio == rel[3])
            srt = jnp.dot(jnp.where(sel, one, zero), hi, preferred_element_type=F32)
            xsort_ref[r * SORT_CHUNK:(r + 1) * SORT_CHUNK, :] = _pack_bf16_pairs(srt)

    @pl.when(i < n_prompt_tiles)
    def _():
        body(xp_ref, yp_ref, zp_ref)

    @pl.when((i >= n_prompt_tiles) & (i < n_tiles))
    def _():
        body(xs_ref, ys_ref, zs_ref)

    @pl.when(i == n_tiles)
    def _():
        xsort_ref[...] = jnp.zeros_like(xsort_ref)


def _dual_specs(tile, width, n_prompt_tiles, n_tiles):
    last_p = n_prompt_tiles - 1
    last_s = n_tiles - n_prompt_tiles - 1
    sp = pl.BlockSpec((tile, width), lambda i: (jnp.minimum(i, last_p), 0))
    ss = pl.BlockSpec((tile, width), lambda i: (jnp.clip(i - n_prompt_tiles, 0, last_s), 0))
    return sp, ss


def _route(xp, xs, y_p, y_s, zuv_p, zuv_s, mix_params, w_out, g_ffn, wr_hi, wr_cat, br_pad, n_exp):
    tp, d = xp.shape
    t = tp + xs.shape[0]
    tm = TOKEN_TILE
    npt = tp // tm
    nt = t // tm
    p_rows = _sorted_rows(tm)
    d_ssm = mix_params[0].shape[0]
    iota = lambda n, ax: lax.broadcasted_iota(I32, (n, n), ax)
    tri = (iota(tm, 0) < iota(tm, 1)).astype(BF16)
    ltri = (iota(n_exp, 1) < iota(n_exp, 0)).astype(BF16)
    const = lambda shape: pl.BlockSpec(shape, lambda i: (0,) * len(shape))
    tile = lambda i: jnp.minimum(i, nt - 1)
    dual = lambda rows, width: _dual_specs(rows, width, npt, nt)
    return pl.pallas_call(
        functools.partial(_route_kernel, n_prompt_tiles=npt, n_tiles=nt, n_exp=n_exp),
        grid=(nt + 1,),
        in_specs=[*dual(tm, d), *dual(tm // SSM_CHUNK, y_p.shape[1]), *dual(tm, zuv_p.shape[1]),
                  *[const(a.shape) for a in mix_params],
                  const(w_out.shape), const((1, d)), const(wr_hi.shape), const(wr_cat.shape),
                  const((1, LANES)), const((tm, tm)), const((n_exp, n_exp))],
        out_specs=[pl.BlockSpec((tm, d), lambda i: (tile(i), 0)),
                   pl.BlockSpec((p_rows, d // 2), lambda i: (i, 0)),
                   pl.BlockSpec((tm, LANES), lambda i: (tile(i), 0)),
                   pl.BlockSpec((None, n_exp, LANES), lambda i: (tile(i), 0, 0))],
        out_shape=[jax.ShapeDtypeStruct((t, d), F32),
                   jax.ShapeDtypeStruct(((nt + 1) * p_rows, d // 2), U32),
                   jax.ShapeDtypeStruct((t, LANES), F32),
                   jax.ShapeDtypeStruct((nt, n_exp, LANES), F32)],
        scratch_shapes=[pltpu.VMEM((d_ssm // LANES, tm, LANES), F32)],
        compiler_params=_cparams(),
        name="route",
    )(xp, xs, y_p, y_s, zuv_p, zuv_s, *mix_params, w_out, g_ffn, wr_hi, wr_cat, br_pad, tri, ltri)


def _slot_tables(c8, p_rows, n_blocks):
    nt, n_exp = c8.shape
    run0 = jnp.cumsum(c8, axis=1) - c8
    seg_len = c8.T
    cum = jnp.cumsum(seg_len, axis=1)
    tot = cum[:, -1]
    padded = (tot + MOE_TILE - 1) // MOE_TILE * MOE_TILE
    ends = jnp.cumsum(padded)
    starts = ends - padded
    n_used = (ends[-1] // MOE_TILE).astype(I32)
    blk = jnp.arange(n_blocks, dtype=I32)
    be = jnp.minimum(jnp.sum(ends[None, :] <= (blk * MOE_TILE)[:, None], axis=1), n_exp - 1).astype(I32)
    block_e = jnp.where(blk < n_used, be, be[jnp.maximum(n_used - 1, 0)])
    onehot = (be[:, None] == jnp.arange(n_exp, dtype=I32)[None, :]).astype(F32)
    pick = lambda tbl: jnp.dot(onehot, tbl.astype(F32), precision=lax.Precision.HIGHEST)
    run_end = pick(cum)
    run_beg = run_end - pick(seg_len)
    shift = pick((jnp.arange(nt, dtype=I32) * p_rows)[None, :] + run0.T) - run_beg
    grow = (blk[:, None] * MOE_TILE + jnp.arange(MOE_GROUPS, dtype=I32)[None, :] * SUBLANES).astype(F32)
    rel = grow - pick(starts[:, None])
    inside = (run_beg[:, None, :] <= rel[:, :, None]) & (rel[:, :, None] < run_end[:, None, :])
    src_row = (rel + jnp.sum(jnp.where(inside, shift[:, None, :], 0.0), axis=2)).astype(I32)
    valid = jnp.any(inside, axis=2) & (blk < n_used)[:, None]
    scratch0 = nt * p_rows // SUBLANES
    zero_group = scratch0 + 2 * MOE_GROUPS
    g_in = jnp.where(valid, src_row // SUBLANES, zero_group).astype(I32)
    g_scr = scratch0 + (blk[:, None] % 2) * MOE_GROUPS + jnp.arange(MOE_GROUPS, dtype=I32)[None, :]
    g_out = jnp.where(valid, src_row // SUBLANES, g_scr).astype(I32)
    return block_e, n_used, g_in, g_out


def _moe_kernel(be_ref, nu_ref, gin_ref, gnext_ref, gout_ref, xs_hbm, wgu_ref, bgu_ref, wd_ref, bd_ref,
                eo_hbm, xbuf, obuf, wgu_s, wd_s, isem, osem):
    j = pl.program_id(0)
    nu = nu_ref[0]
    f = wd_ref.shape[0]

    @pl.when((j < nu) & ((j == 0) | (be_ref[j] != be_ref[jnp.maximum(j - 1, 0)])))
    def _():
        for c in range(0, wgu_ref.shape[0], CAST_ROWS):
            wgu_s[c:c + CAST_ROWS, :] = wgu_ref[c:c + CAST_ROWS, :].astype(BF16)
        for c in range(0, wd_ref.shape[0], CAST_ROWS):
            wd_s[c:c + CAST_ROWS, :] = wd_ref[c:c + CAST_ROWS, :].astype(BF16)

    def in_copy(tbl_ref, r, sl):
        return pltpu.make_async_copy(xs_hbm.at[tbl_ref[0, r]], xbuf.at[sl, r], isem.at[sl])

    def out_copy(r, sl):
        return pltpu.make_async_copy(obuf.at[sl, r], eo_hbm.at[gout_ref[0, r]], osem.at[sl])

    def wait_in(sl):
        pltpu.make_async_copy(xs_hbm.at[pl.ds(0, MOE_GROUPS)], xbuf.at[sl], isem.at[sl]).wait()

    def wait_out(sl):
        pltpu.make_async_copy(obuf.at[sl], eo_hbm.at[pl.ds(0, MOE_GROUPS)], osem.at[sl]).wait()

    @pl.when(j == 0)
    def _():
        for r in range(MOE_GROUPS):
            in_copy(gin_ref, r, 0).start(priority=r % DMA_PRIORITIES)

    def step(slot):
        other = 1 - slot

        @pl.when(j + 1 < nu)
        def _():
            for r in range(MOE_GROUPS):
                in_copy(gnext_ref, r, other).start(priority=r % DMA_PRIORITIES)

        wait_in(slot)

        @pl.when(j >= 2)
        def _():
            wait_out(slot)

        x = _unpack_bf16_pairs(xbuf[slot].reshape(MOE_TILE, xbuf.shape[-1]))
        gate = jnp.dot(x, wgu_s[:, :f], preferred_element_type=F32) + bgu_ref[:, :f]
        gate = jnp.minimum(gate, SWIGLU_LIMIT)
        sg = gate * jax.nn.sigmoid(SWIGLU_ALPHA * gate)
        up = jnp.dot(x, wgu_s[:, f:], preferred_element_type=F32) + bgu_ref[:, f:]
        act = sg * (jnp.clip(up, -SWIGLU_LIMIT, SWIGLU_LIMIT) + 1.0)
        out = jnp.dot(act.astype(BF16), wd_s[...], preferred_element_type=F32) + bd_ref[...]
        obuf[slot] = _pack_bf16_pairs(out.astype(BF16).astype(F32)).reshape(obuf.shape[1:])
        for r in range(MOE_GROUPS):
            out_copy(r, slot).start(priority=r % DMA_PRIORITIES)

        @pl.when(j == nu - 1)
        def _():
            wait_out(slot)

            @pl.when(j >= 1)
            def _():
                wait_out(other)

    for parity in range(2):
        pl.when((j < nu) & (lax.rem(j, 2) == parity))(functools.partial(step, parity))


def _moe(block_e, n_used, g_in, g_out, xsorted, wgu, bgu, wd, bd):
    n_blocks = block_e.shape[0]
    dh = xsorted.shape[1]
    d = 2 * dh
    f2 = wgu.shape[2]
    f = wd.shape[1]
    g_in3 = g_in.reshape(n_blocks, 1, MOE_GROUPS)
    g_next3 = jnp.concatenate([g_in3[1:], g_in3[:1]], axis=0)
    g_out3 = g_out.reshape(n_blocks, 1, MOE_GROUPS)
    xs3 = xsorted.reshape(xsorted.shape[0] // SUBLANES, SUBLANES, dh)
    tbl = pl.BlockSpec((None, 1, MOE_GROUPS), lambda j, be, nu: (j, 0, 0), memory_space=pltpu.SMEM)
    return pl.pallas_call(
        _moe_kernel,
        grid_spec=pltpu.PrefetchScalarGridSpec(
            num_scalar_prefetch=2, grid=(n_blocks,),
            in_specs=[tbl, tbl, tbl,
                      pl.BlockSpec(memory_space=pl.ANY),
                      pl.BlockSpec((None, d, f2), lambda j, be, nu: (be[j], 0, 0)),
                      pl.BlockSpec((None, 1, f2), lambda j, be, nu: (be[j], 0, 0)),
                      pl.BlockSpec((None, f, d), lambda j, be, nu: (be[j], 0, 0)),
                      pl.BlockSpec((None, 1, d), lambda j, be, nu: (be[j], 0, 0))],
            out_specs=pl.BlockSpec(memory_space=pl.ANY),
            scratch_shapes=[pltpu.VMEM((2, MOE_GROUPS, SUBLANES, dh), U32),
                            pltpu.VMEM((2, MOE_GROUPS, SUBLANES, dh), U32),
                            pltpu.VMEM((d, f2), BF16), pltpu.VMEM((f, d), BF16),
                            pltpu.SemaphoreType.DMA((2,)), pltpu.SemaphoreType.DMA((2,))]),
        out_shape=jax.ShapeDtypeStruct(xs3.shape, U32),
        input_output_aliases={5: 0},
        compiler_params=_cparams(),
        name="moe",
    )(block_e, n_used, g_in3, g_next3, g_out3, xs3, wgu, bgu, wd, bd).reshape(xsorted.shape)


def _combine_kernel(meta_ref, x1_ref, g_ref, eo_ref, y_ref):
    tm = x1_ref.shape[0]
    p_rows = eo_ref.shape[0]
    lane = lax.broadcasted_iota(I32, (tm, SORT_CHUNK), 1).astype(F32).astype(BF16)
    ws = [meta_ref[:, k:k + 1].astype(BF16) for k in range(TOP_K)]
    pos = [meta_ref[:, TOP_K + k:TOP_K + k + 1] for k in range(TOP_K)]
    acc = x1_ref[...]
    for r in range(p_rows // SORT_CHUNK):
        buf = _unpack_bf16_pairs(eo_ref[r * SORT_CHUNK:(r + 1) * SORT_CHUNK, :])
        lo_r = float(r * SORT_CHUNK)
        wm = jnp.zeros((tm, SORT_CHUNK), BF16)
        for k in range(TOP_K):
            rel = jnp.where((pos[k] >= lo_r) & (pos[k] < lo_r + SORT_CHUNK), pos[k] - lo_r, -1.0).astype(BF16)
            wm = jnp.where(lane == rel, ws[k], wm)
        acc = acc + jnp.dot(wm, buf, preferred_element_type=F32)
    y_ref[...] = _rms(acc, g_ref[...])


def _combine(meta, x1, g_final, eo, tile_off, n_tokens):
    d = x1.shape[1]
    tm = TOKEN_TILE
    p_rows = _sorted_rows(tm)
    return pl.pallas_call(
        _combine_kernel,
        grid=(n_tokens // tm,),
        in_specs=[pl.BlockSpec((tm, LANES), lambda i: (i + tile_off, 0)),
                  pl.BlockSpec((tm, d), lambda i: (i + tile_off, 0)),
                  pl.BlockSpec((1, d), lambda i: (0, 0)),
                  pl.BlockSpec((p_rows, d // 2), lambda i: (i + tile_off, 0))],
        out_specs=pl.BlockSpec((tm, d), lambda i: (i, 0)),
        out_shape=jax.ShapeDtypeStruct((n_tokens, d), F32),
        compiler_params=_cparams(),
        name="combine",
    )(meta, x1, g_final, eo)


def kernel(x_prompt, x_sample, g_mix, w_in, ssm_lam_re, ssm_lam_im, ssm_log_dt, ssm_b_re, ssm_b_im,
           ssm_c_re, ssm_c_im, ssm_d, w_glu, b_glu, sgu_ln_g, sgu_ln_b, sgu_w_s, sgu_b_s,
           g_out_ssm, g_out_sgu, w_out, g_ffn, w_router, b_router, w_gate_up, b_gate_up,
           w_down, b_down, g_final):
    assert g_mix.shape[0] == 1, "single-layer trunk"
    bp, lp, d = x_prompt.shape
    bs, ls, _ = x_sample.shape
    tp, ts = bp * lp, bs * ls
    t = tp + ts
    d_ssm = ssm_d.shape[1]
    d_sgu = sgu_ln_g.shape[1]
    n_exp = w_router.shape[2]
    assert all(n % tile == 0 for n in (tp, ts) for tile in (SEQ_TILE, TOKEN_TILE))
    assert lp % CHUNK == 0 and ls % CHUNK == 0 and TOKEN_TILE % CHUNK == 0 and CHUNK % SSM_CHUNK == 0
    assert SEQ_TILE % SSM_CHUNK == 0
    assert SSM_CHUNK * SSM_GROUP == MXU_DIM and 2 * ssm_lam_re.shape[-1] == LANES
    assert d_sgu // SGU_HEADS == LANES // 2 and n_exp <= LANES
    assert n_exp * (SUBLANES - 1) <= MXU_DIM and _sorted_rows(TOKEN_TILE) % SORT_CHUNK == 0
    assert (2 * MOE_GROUPS + 1) * SUBLANES <= _sorted_rows(TOKEN_TILE)

    xp = x_prompt.reshape(tp, d)
    xs = x_sample.reshape(ts, d)
    row = lambda a: a.reshape(1, -1).astype(F32)

    w_in_b = w_in[0].astype(BF16)
    u_p, zuv_p = _inproj(xp, row(g_mix[0]), w_in_b, d_ssm)
    u_s, zuv_s = _inproj(xs, row(g_mix[0]), w_in_b, d_ssm)

    mats = _ssm_matrices(ssm_lam_re[0], ssm_lam_im[0], ssm_log_dt[0], ssm_b_re[0], ssm_b_im[0],
                         ssm_c_re[0], ssm_c_im[0], ssm_d[0])
    y_p = _ssm_trunk(u_p, bp, mats)
    y_s = _ssm_trunk(u_s, bs, mats)

    ws = sgu_w_s[0]
    ws_pairs = jnp.concatenate([ws[0::2], ws[1::2]], axis=2).astype(BF16)
    bias_s = jnp.repeat(sgu_b_s[0].T, d_sgu // SGU_HEADS, axis=1).astype(F32)
    mix_params = (w_glu[0].astype(BF16), row(b_glu[0]), row(g_out_ssm[0]), row(sgu_ln_g[0]), row(sgu_ln_b[0]),
                  ws_pairs, bias_s, row(g_out_sgu[0]))

    wr_pad = jnp.pad(w_router[0].astype(F32), ((0, 0), (0, LANES - n_exp)))
    wr_hi = wr_pad.astype(BF16)
    wr_cat = jnp.concatenate([wr_hi, (wr_pad - wr_hi.astype(F32)).astype(BF16)], axis=1)
    br_pad = jnp.pad(b_router[0].astype(F32), (0, LANES - n_exp)).reshape(1, LANES)
    x1, xsorted, meta, c8 = _route(xp, xs, y_p, y_s, zuv_p, zuv_s, mix_params, w_out[0].astype(BF16),
                                   row(g_ffn[0]), wr_hi, wr_cat, br_pad, n_exp)

    nt = t // TOKEN_TILE
    p_rows = _sorted_rows(TOKEN_TILE)
    max_rows = t * TOP_K + nt * n_exp * (SUBLANES - 1) + n_exp * (MOE_TILE - 1)
    n_blocks = -(-max_rows // MOE_TILE)
    block_e, n_used, g_in, g_out = _slot_tables(c8[:, :, 0].astype(I32), p_rows, n_blocks)
    eo = _moe(block_e, n_used.reshape(1), g_in, g_out, xsorted, w_gate_up[0],
              b_gate_up[0][:, None, :].astype(F32), w_down[0], b_down[0][:, None, :].astype(F32))

    gf = row(g_final)
    y_prompt = _combine(meta, x1, gf, eo, 0, tp)
    y_sample = _combine(meta, x1, gf, eo, tp // TOKEN_TILE, ts)
    return y_prompt.reshape(bp, lp, d), y_sample.reshape(bs, ls, d)
```

```python
import functools
import math

import jax
import jax.numpy as jnp
from jax import lax
from jax.experimental import pallas as pl
from jax.experimental.pallas import tpu as pltpu

F32 = jnp.float32
BF16 = jnp.bfloat16
I32 = jnp.int32
U32 = jnp.uint32

SSM_GROUP = 16
SGU_HEADS = 8
CHUNK = 128
TOP_K = 4
SWIGLU_LIMIT = 7.0
SWIGLU_ALPHA = 1.702
RMS_EPS = 1e-6
LN_EPS = 1e-5

LANES = 128
SUBLANES = 8
MXU_DIM = 256
SSM_CHUNK = MXU_DIM // SSM_GROUP
PAIR_W = 2 * SSM_CHUNK * SSM_GROUP
ATOM = 2 * SSM_GROUP
ATOMS = LANES // ATOM

TOKEN_TILE = 512
SEQ_TILE = 1024
MOE_TILE = 1024
MOE_GROUPS = MOE_TILE // SUBLANES
SORT_CHUNK = 256
CAST_ROWS = 128
SSM_ROW_TILE = 2048
SCAN_SEQS = 4
VMEM_LIMIT = 56 * 1024 * 1024


def _cparams(n_axes=1, vmem=None):
    return pltpu.CompilerParams(
        dimension_semantics=("arbitrary",) * n_axes,
        vmem_limit_bytes=vmem if vmem is not None else VMEM_LIMIT,
    )


def _rms(x, g):
    return x * lax.rsqrt(jnp.mean(x * x, axis=-1, keepdims=True) + RMS_EPS) * g


def _gelu(x):
    return x * (lax.erf(x * (1.0 / math.sqrt(2.0))) + 1.0) * 0.5


def _atom_masks(rows):
    lane = lax.broadcasted_iota(I32, (rows, LANES), 1)
    return [(lane >= a * ATOM) & (lane < (a + 1) * ATOM) for a in range(ATOMS)]


def _atom_transpose(src, masks):
    dst = []
    for b in range(ATOMS):
        acc = None
        for a in range(ATOMS):
            r = (a - b) % ATOMS
            piece = src[a] if r == 0 else pltpu.roll(src[a], ATOM * r, axis=1)
            acc = piece if acc is None else jnp.where(masks[a], piece, acc)
        dst.append(acc)
    return dst


def _pack_bf16_pairs(x):
    w = x.shape[1] // 2
    lo = lax.bitcast_convert_type(x[:, :w], U32) >> 16
    hi = lax.bitcast_convert_type(x[:, w:], U32) & jnp.uint32(0xFFFF0000)
    return hi | lo


def _unpack_bf16_pairs(u):
    lo = lax.bitcast_convert_type(u << 16, F32)
    hi = lax.bitcast_convert_type(u & jnp.uint32(0xFFFF0000), F32)
    return jnp.concatenate([lo.astype(BF16), hi.astype(BF16)], axis=1)


def _inproj_kernel(x_ref, g_ref, w_ref, u_ref, zuv_ref, za_scr, *, d_ssm):
    tm = x_ref.shape[0]
    nc = tm // SSM_CHUNK
    h = _rms(x_ref[...], g_ref[...])
    z = jnp.dot(h.astype(BF16), w_ref[...], preferred_element_type=F32)
    zuv_ref[...] = _gelu(z[:, d_ssm:]).astype(BF16)
    n_blk = d_ssm // LANES
    for b in range(n_blk):
        za_scr[b] = z[:, b * LANES:(b + 1) * LANES]
    masks = _atom_masks(nc)
    n_quads = SSM_CHUNK // ATOMS
    for b in range(n_blk):
        for v in range(n_quads):
            src = [za_scr[b, pl.ds(ATOMS * v + jj, nc, stride=SSM_CHUNK), :] for jj in range(ATOMS)]
            dst = _atom_transpose(src, masks)
            for pi in range(ATOMS):
                c0 = (ATOMS * b + pi) * PAIR_W + v * LANES
                u_ref[:, c0:c0 + LANES] = dst[pi].astype(BF16)


def _inproj(x, g_mix, w_in, d_ssm):
    t, d = x.shape
    tm = SEQ_TILE
    d_in = w_in.shape[1]
    nc = tm // SSM_CHUNK
    uw = d_ssm * SSM_CHUNK
    return pl.pallas_call(
        functools.partial(_inproj_kernel, d_ssm=d_ssm),
        grid=(t // tm,),
        in_specs=[pl.BlockSpec((tm, d), lambda i: (i, 0)),
                  pl.BlockSpec((1, d), lambda i: (0, 0)),
                  pl.BlockSpec((d, d_in), lambda i: (0, 0))],
        out_specs=[pl.BlockSpec((nc, uw), lambda i: (i, 0)),
                   pl.BlockSpec((tm, d_in - d_ssm), lambda i: (i, 0))],
        out_shape=[jax.ShapeDtypeStruct((t // SSM_CHUNK, uw), BF16),
                   jax.ShapeDtypeStruct((t, d_in - d_ssm), BF16)],
        scratch_shapes=[pltpu.VMEM((d_ssm // LANES, tm, LANES), F32)],
        compiler_params=_cparams(),
        name="inproj",
    )(x, g_mix, w_in)


def _ssm_matrices(lam_re, lam_im, log_dt, b_re, b_im, c_re, c_im, d_skip):
    hp = lax.Precision.HIGHEST
    _, g, n = lam_re.shape
    p = b_re.shape[-1]
    lc = SSM_CHUNK
    np_ = g // 2
    dt = jnp.exp(log_dt)[..., None]
    mag = jnp.exp(lam_re * dt)
    ar = mag * jnp.cos(lam_im * dt)
    ai = mag * jnp.sin(lam_im * dt)
    den = lam_re * lam_re + lam_im * lam_im
    nr = ar - 1.0
    fr = (nr * lam_re + ai * lam_im) / den
    fi = (ai * lam_re - nr * lam_im) / den
    bbr = fr[..., None] * b_re - fi[..., None] * b_im
    bbi = fr[..., None] * b_im + fi[..., None] * b_re

    prs, pis = [jnp.ones_like(ar)], [jnp.zeros_like(ai)]
    for _ in range(lc):
        pr, pi = prs[-1], pis[-1]
        prs.append(pr * ar - pi * ai)
        pis.append(pr * ai + pi * ar)
    pw_r, pw_i = jnp.stack(prs), jnp.stack(pis)

    lane_k = jnp.arange(lc * p)
    til_k = (lane_k[None, :] % p == jnp.arange(p)[:, None]).astype(F32)
    rep_k = (lane_k[None, :] // p == jnp.arange(lc)[:, None]).astype(F32)
    ex = lambda x, e: jnp.einsum('dgnk,kx->dgnx', x, e, precision=hp)
    ct_r = ex(jnp.transpose(c_re, (0, 1, 3, 2)), til_k)
    ct_i = ex(jnp.transpose(c_im, (0, 1, 3, 2)), til_k)
    at_r = ex(jnp.transpose(pw_r[:lc], (1, 2, 3, 0)), rep_k)
    at_i = ex(jnp.transpose(pw_i[:lc], (1, 2, 3, 0)), rep_k)
    kk = (jnp.einsum('dgnq,dgnx->dgqx', bbr, ct_r * at_r - ct_i * at_i, precision=hp)
          - jnp.einsum('dgnq,dgnx->dgqx', bbi, ct_r * at_i + ct_i * at_r, precision=hp))
    kk = kk.reshape(2, g, p, lc, p)
    kf, kb = kk[0], kk[1]
    eye = jnp.eye(2, dtype=F32)
    a2 = p * 2

    k_all = jnp.concatenate([kb[:, :, :0:-1], (kf[:, :, :1] + kb[:, :, :1]), kf[:, :, 1:]], axis=2)
    k_all = k_all.reshape(np_, 2, p, 2 * lc - 1, p)
    kflat = jnp.einsum('nsqup,st->nsqutp', k_all, eye).reshape(np_, a2, (2 * lc - 1) * a2)
    kflat = jnp.pad(kflat, ((0, 0), (0, 0), (0, a2)))

    def atoms_q(x):
        x = jnp.transpose(x, (0, 1, 3, 2)).reshape(2, np_, 2, p, n)
        return jnp.einsum('dnsqm,st->dnsqtm', x, eye).reshape(2, np_, a2, 2 * n)
    bq_r, bq_i = atoms_q(bbr), atoms_q(bbi)
    bq = jnp.stack([bq_r[0], bq_i[0], bq_r[1], bq_i[1]], axis=1)
    ef = lc - 1 - jnp.arange(lc)
    eb = jnp.arange(lc)
    rows_q = lambda w, e, d: jnp.transpose(w[e, d].reshape(lc, np_, 2 * n), (1, 0, 2))
    wq = jnp.stack([rows_q(pw_r, ef, 0), rows_q(pw_i, ef, 0),
                    rows_q(pw_r, eb, 1), rows_q(pw_i, eb, 1)], axis=1)

    def atoms_p(x):
        x = jnp.transpose(x, (0, 1, 3, 2)).reshape(2, np_, 2, n, p)
        return jnp.einsum('dnsmp,st->dnsmtp', x, eye).reshape(2, np_, 2 * n, a2)
    cp_r, cp_i = atoms_p(c_re), atoms_p(c_im)
    cp = jnp.stack([cp_r[0], cp_i[0], cp_r[1], cp_i[1]], axis=1)
    pf_e = jnp.arange(lc) + 1
    pb_e = lc - jnp.arange(lc)
    cols_p = lambda w, e, d: jnp.transpose(w[e, d].reshape(lc, np_, 2 * n), (1, 2, 0))
    wp = jnp.stack([cols_p(pw_r, pf_e, 0), cols_p(pw_i, pf_e, 0),
                    cols_p(pw_r, pb_e, 1), cols_p(pw_i, pb_e, 1)], axis=1)
    lane = jnp.arange(lc * a2)
    til = (lane[None, :] % a2 == jnp.arange(a2)[:, None]).astype(BF16)
    rep = (lane[None, :] // a2 == jnp.arange(lc)[:, None]).astype(BF16)

    al = jnp.stack([pw_r[lc, 0], pw_i[lc, 0], pw_r[lc, 1], pw_i[lc, 1]])
    alpha = jnp.transpose(al.reshape(4, np_, 2 * n), (1, 0, 2))
    skip = jnp.broadcast_to(d_skip.reshape(np_, 1, a2), (np_, lc, a2)).reshape(np_, 1, lc * a2)
    return kflat, bq, wq, cp, wp, til, rep, alpha, skip.astype(F32)


def _ssm_v_kernel(u_ref, bq_ref, wq_ref, v_ref, q_scr):
    a2 = bq_ref.shape[1]
    w = bq_ref.shape[2]

    @pl.when(pl.program_id(1) == 0)
    def _():
        for d in range(2):
            br, bi = bq_ref[2 * d], bq_ref[2 * d + 1]
            for j in range(SSM_CHUNK):
                wr = wq_ref[2 * d, j:j + 1, :]
                wi = wq_ref[2 * d + 1, j:j + 1, :]
                q_scr[j * a2:(j + 1) * a2, (2 * d) * w:(2 * d + 1) * w] = (wr * br - wi * bi).astype(BF16)
                q_scr[j * a2:(j + 1) * a2, (2 * d + 1) * w:(2 * d + 2) * w] = (wr * bi + wi * br).astype(BF16)

    v_ref[...] = jnp.dot(u_ref[...], q_scr[...], preferred_element_type=F32)


def _ssm_scan_kernel(v_ref, a_ref, s_ref, *, n_chunks, bsz):
    w = LANES
    sub = SUBLANES
    n_groups = n_chunks // sub
    row = lax.broadcasted_iota(I32, (sub, w), 0)

    def cmul(ar, ai, xr, xi):
        return ar * xr - ai * xi, ar * xi + ai * xr

    def powers(ar, ai):
        p2 = cmul(ar, ai, ar, ai)
        p4 = cmul(*p2, *p2)
        p8 = cmul(*p4, *p4)
        p3 = cmul(*p2, ar, ai)
        p5 = cmul(*p4, ar, ai)
        p6 = cmul(*p4, *p2)
        p7 = cmul(*p4, *p3)
        seq = [(jnp.ones_like(ar), jnp.zeros_like(ai)), (ar, ai), p2, p3, p4, p5, p6, p7]
        tr = jnp.concatenate([s[0] for s in seq], axis=0)
        ti = jnp.concatenate([s[1] for s in seq], axis=0)
        return (ar, ai), p2, p4, p8, (tr, ti)

    def tile_scan(vr, vi, pw, reverse):
        a1, a2, a4, _, _ = pw

        def shift(x, s):
            if reverse:
                return jnp.where(row < sub - s, pltpu.roll(x, sub - s, axis=0), 0.0)
            return jnp.where(row >= s, pltpu.roll(x, s, axis=0), 0.0)

        xr, xi = shift(vr, 1), shift(vi, 1)
        for s, (ar, ai) in ((1, a1), (2, a2), (4, a4)):
            mr, mi = cmul(ar, ai, shift(xr, s), shift(xi, s))
            xr, xi = xr + mr, xi + mi
        e = 0 if reverse else sub - 1
        lr, li = cmul(a1[0], a1[1], xr[e:e + 1], xi[e:e + 1])
        return xr, xi, lr + vr[e:e + 1], li + vi[e:e + 1]

    pw_f = powers(a_ref[0:1, :], a_ref[1:2, :])
    pw_b = powers(a_ref[2:3, :], a_ref[3:4, :])
    tbr = jnp.concatenate([pw_b[4][0][sub - 1 - k:sub - k] for k in range(sub)], axis=0)
    tbi = jnp.concatenate([pw_b[4][1][sub - 1 - k:sub - k] for k in range(sub)], axis=0)

    def one_tile(row0, col0, pw, table, cr, ci, reverse):
        rows = pl.ds(pl.multiple_of(row0, sub), sub)
        xr, xi, lr, li = tile_scan(v_ref[rows, col0:col0 + w], v_ref[rows, col0 + w:col0 + 2 * w], pw, reverse)
        mr, mi = cmul(table[0], table[1], cr, ci)
        nr, ni = cmul(pw[3][0], pw[3][1], cr, ci)
        return xr + mr, xi + mi, nr + lr, ni + li

    pair = 2 * sub
    n_pairs = n_groups // 2

    def step(g, carry, b0):
        out = []
        for b in range(b0, b0 + len(carry) // 4):
            cfr, cfi, cbr, cbi = carry[4 * (b - b0):4 * (b - b0) + 4]
            rf = pl.multiple_of(b * n_chunks + g * pair, pair)
            rb = pl.multiple_of(b * n_chunks + (n_pairs - 1 - g) * pair, pair)
            ar, ai, cfr, cfi = one_tile(rf, 0, pw_f, pw_f[4], cfr, cfi, False)
            br, bi, cfr, cfi = one_tile(rf + sub, 0, pw_f, pw_f[4], cfr, cfi, False)
            s_ref[pl.ds(rf, pair), 0:w] = jnp.concatenate([ar, br], axis=0).astype(BF16)
            s_ref[pl.ds(rf, pair), w:2 * w] = jnp.concatenate([ai, bi], axis=0).astype(BF16)
            out += [cfr, cfi]
            br, bi, cbr, cbi = one_tile(rb + sub, 2 * w, pw_b, (tbr, tbi), cbr, cbi, True)
            ar, ai, cbr, cbi = one_tile(rb, 2 * w, pw_b, (tbr, tbi), cbr, cbi, True)
            s_ref[pl.ds(rb, pair), 2 * w:3 * w] = jnp.concatenate([ar, br], axis=0).astype(BF16)
            s_ref[pl.ds(rb, pair), 3 * w:4 * w] = jnp.concatenate([ai, bi], axis=0).astype(BF16)
            out += [cbr, cbi]
        return tuple(out)

    zero = jnp.zeros((1, w), F32)
    for b0 in range(0, bsz, SCAN_SEQS):
        nb = min(SCAN_SEQS, bsz - b0)
        lax.fori_loop(0, n_pairs, functools.partial(step, b0=b0), (zero,) * (4 * nb))


def _ssm_y_kernel(u_ref, s_ref, kflat_ref, cp_ref, wp_ref, til_ref, rep_ref, d_ref, y_ref, m_scr, p_scr):
    a2 = kflat_ref.shape[0]
    w = cp_ref.shape[1]

    def expand(x, e_ref):
        hi = x.astype(BF16)
        lo = (x - hi.astype(F32)).astype(BF16)
        return (jnp.dot(hi, e_ref[...], preferred_element_type=F32)
                + jnp.dot(lo, e_ref[...], preferred_element_type=F32))

    @pl.when(pl.program_id(1) == 0)
    def _():
        kflat = kflat_ref[...]
        for j in range(SSM_CHUNK):
            off = (SSM_CHUNK - 1 - j) * a2
            m_scr[j * a2:(j + 1) * a2, :] = kflat[:, off:off + PAIR_W].astype(BF16)
        for d in range(2):
            cr, ci = expand(cp_ref[2 * d], til_ref), expand(cp_ref[2 * d + 1], til_ref)
            wr, wi = expand(wp_ref[2 * d], rep_ref), expand(wp_ref[2 * d + 1], rep_ref)
            p_scr[(2 * d) * w:(2 * d + 1) * w, :] = (cr * wr - ci * wi).astype(BF16)
            p_scr[(2 * d + 1) * w:(2 * d + 2) * w, :] = (-(cr * wi + ci * wr)).astype(BF16)

    u = u_ref[...]
    y = jnp.dot(u, m_scr[...], preferred_element_type=F32)
    y = y + jnp.dot(s_ref[...], p_scr[...], preferred_element_type=F32)
    y_ref[...] = _gelu(y + d_ref[...] * u.astype(F32)).astype(BF16)


def _ssm_trunk(u, bsz, mats):
    kflat, bq, wq, cp, wp, til, rep, alpha, skip = mats
    rows = u.shape[0]
    nc = rows // bsz
    np_ = kflat.shape[0]
    pw = PAIR_W
    sw = 4 * bq.shape[3]
    tr = min(SSM_ROW_TILE, rows)
    per_pair = lambda a: pl.BlockSpec((None,) + a.shape[1:], lambda p, r: (p,) + (0,) * (a.ndim - 1))
    whole = lambda a: pl.BlockSpec(a.shape, lambda p, r: (0,) * a.ndim)
    v = pl.pallas_call(
        _ssm_v_kernel,
        grid=(np_, rows // tr),
        in_specs=[pl.BlockSpec((tr, pw), lambda p, r: (r, p)), per_pair(bq), per_pair(wq)],
        out_specs=pl.BlockSpec((tr, sw), lambda p, r: (r, p)),
        out_shape=jax.ShapeDtypeStruct((rows, np_ * sw), F32),
        scratch_shapes=[pltpu.VMEM((pw, sw), BF16)],
        compiler_params=_cparams(2),
        name="ssm_v",
    )(u, bq, wq)
    s = pl.pallas_call(
        functools.partial(_ssm_scan_kernel, n_chunks=nc, bsz=bsz),
        grid=(np_,),
        in_specs=[pl.BlockSpec((rows, sw), lambda p: (0, p)),
                  pl.BlockSpec((None, 4, LANES), lambda p: (p, 0, 0))],
        out_specs=pl.BlockSpec((rows, sw), lambda p: (0, p)),
        out_shape=jax.ShapeDtypeStruct((rows, np_ * sw), BF16),
        compiler_params=_cparams(1),
        name="ssm_scan",
    )(v, alpha)
    y = pl.pallas_call(
        _ssm_y_kernel,
        grid=(np_, rows // tr),
        in_specs=[pl.BlockSpec((tr, pw), lambda p, r: (r, p)),
                  pl.BlockSpec((tr, sw), lambda p, r: (r, p)),
                  per_pair(kflat), per_pair(cp), per_pair(wp), whole(til), whole(rep), per_pair(skip)],
        out_specs=pl.BlockSpec((tr, pw), lambda p, r: (r, p)),
        out_shape=jax.ShapeDtypeStruct((rows, np_ * pw), BF16),
        scratch_shapes=[pltpu.VMEM((pw, pw), BF16), pltpu.VMEM((sw, pw), BF16)],
        compiler_params=_cparams(2),
        name="ssm_y",
    )(u, s, kflat, cp, wp, til, rep, skip)
    return y


def _mixer(y_ref, zuv_ref, wglu_ref, bglu_ref, gssm_ref, lng_ref, lnb_ref, ws_ref, bs_ref, gsgu_ref, ya_scr):
    tm = zuv_ref.shape[0]
    n_blk = ya_scr.shape[0]
    nc = tm // SSM_CHUNK
    masks = _atom_masks(nc)
    for b in range(n_blk):
        for v in range(SSM_CHUNK // ATOMS):
            src = [y_ref[:, (ATOMS * b + pi) * PAIR_W + v * LANES:
                         (ATOMS * b + pi) * PAIR_W + (v + 1) * LANES].astype(F32)
                   for pi in range(ATOMS)]
            dst = _atom_transpose(src, masks)
            for jj in range(ATOMS):
                ya_scr[b, pl.ds(ATOMS * v + jj, nc, stride=SSM_CHUNK), :] = dst[jj]
    gl = jnp.concatenate([ya_scr[b] for b in range(n_blk)], axis=1)
    gate = jnp.dot(gl.astype(BF16), wglu_ref[...], preferred_element_type=F32) + bglu_ref[...]
    ra = _rms(gl * jax.nn.sigmoid(gate), gssm_ref[...]).astype(BF16)
    d_sgu = zuv_ref.shape[1] // 2
    u = zuv_ref[:, :d_sgu].astype(F32)
    gv = zuv_ref[:, d_sgu:].astype(F32)
    xc = gv - jnp.mean(gv, axis=-1, keepdims=True)
    v = xc * lax.rsqrt(jnp.mean(xc * xc, axis=-1, keepdims=True) + LN_EPS)
    v = (v * lng_ref[...] + lnb_ref[...]).astype(BF16)
    lo = lax.broadcasted_iota(I32, (CHUNK, LANES), 1) < (LANES // 2)
    zero = jnp.zeros((CHUNK, LANES), BF16)
    rows = []
    for c in range(tm // CHUNK):
        cols = []
        for j in range(d_sgu // LANES):
            vp = v[c * CHUNK:(c + 1) * CHUNK, j * LANES:(j + 1) * LANES]
            rhs = jnp.concatenate([jnp.where(lo, vp, zero), jnp.where(lo, zero, vp)], axis=0)
            cols.append(jnp.dot(ws_ref[j], rhs, preferred_element_type=F32))
        rows.append(jnp.concatenate(cols, axis=1) + bs_ref[...])
    s = jnp.concatenate(rows, axis=0)
    rb = _rms(u * s, gsgu_ref[...]).astype(BF16)
    return jnp.concatenate([ra, rb], axis=1)


def _sorted_rows(tile):
    return TOP_K * tile + MXU_DIM


def _route_kernel(xp_ref, xs_ref, yp_ref, ys_ref, zp_ref, zs_ref, wglu_ref, bglu_ref, gssm_ref, lng_ref,
                  lnb_ref, ws_ref, bs_ref, gsgu_ref, wout_ref, gffn_ref, whi_ref, wcat_ref, br_ref, tri_ref,
                  ltri_ref, x1_ref, xsort_ref, meta_ref, c8_ref, ya_scr, *, n_prompt_tiles, n_tiles, n_exp):
    i = pl.program_id(0)
    tm = x1_ref.shape[0]
    p_rows = xsort_ref.shape[0]

    def body(x_ref, y_ref, zuv_ref):
        mixed = _mixer(y_ref, zuv_ref, wglu_ref, bglu_ref, gssm_ref, lng_ref, lnb_ref, ws_ref, bs_ref,
                       gsgu_ref, ya_scr)
        x1 = x_ref[...] + jnp.dot(mixed, wout_ref[...], preferred_element_type=F32)
        x1_ref[...] = x1
        h2 = _rms(x1, gffn_ref[...])
        hi = h2.astype(BF16)
        lo = (h2 - hi.astype(F32)).astype(BF16)
        hw = jnp.dot(hi, wcat_ref[...], preferred_element_type=F32)
        lt = (hw[:, :LANES] + (jnp.dot(lo, whi_ref[...], preferred_element_type=F32) + hw[:, LANES:])
              + br_ref[...])
        logits = lt.T[:n_exp]
        eio = lax.broadcasted_iota(I32, (n_exp, tm), 0)
        vals, idxs = [], []
        l = logits
        for _ in range(TOP_K):
            m = jnp.max(l, axis=0, keepdims=True)
            idx = jnp.min(jnp.where(l == m, eio, n_exp), axis=0, keepdims=True)
            vals.append(m)
            idxs.append(idx)
            l = jnp.where(eio == idx, -jnp.inf, l)
        ex = [jnp.exp(v - vals[0]) for v in vals]
        den = ex[0] + ex[1] + ex[2] + ex[3]
        ws = [e / den for e in ex]
        hot = [eio == idx for idx in idxs]
        cnt = sum(h.astype(F32) for h in hot)
        prefix = jnp.dot(cnt.astype(BF16), tri_ref[...], preferred_element_type=F32)
        c = jnp.sum(cnt, axis=1, keepdims=True)
        c8 = jnp.floor((c + (SUBLANES - 1)) * (1.0 / SUBLANES)) * SUBLANES
        c8b = jnp.broadcast_to(c8, (n_exp, LANES))
        c8_ref[...] = c8b
        run0 = jnp.dot(ltri_ref[...], c8b.astype(BF16), preferred_element_type=F32)[:, 0:1]
        base = run0 + prefix
        pos = [jnp.sum(jnp.where(h, base, 0.0), axis=0, keepdims=True) for h in hot]
        meta = jnp.concatenate(ws + pos + [jnp.zeros((LANES - 2 * TOP_K, tm), F32)], axis=0)
        meta_ref[...] = meta.T
        rio = lax.broadcasted_iota(I32, (SORT_CHUNK, tm), 0).astype(F32).astype(BF16)
        one = jnp.ones((SORT_CHUNK, tm), BF16)
        zero = jnp.zeros((SORT_CHUNK, tm), BF16)
        for r in range(p_rows // SORT_CHUNK):
            lo_r = float(r * SORT_CHUNK)
            rel = [jnp.where((p >= lo_r) & (p < lo_r + SORT_CHUNK), p - lo_r, -1.0).astype(BF16) for p in pos]
            sel = (rio == rel[0]) | (rio == rel[1]) | (rio == rel[2]) | (rio == rel[3])
            srt = jnp.dot(jnp.where(sel, one, zero), hi, preferred_element_type=F32)
            xsort_ref[r * SORT_CHUNK:(r + 1) * SORT_CHUNK, :] = _pack_bf16_pairs(srt)

    @pl.when(i < n_prompt_tiles)
    def _():
        body(xp_ref, yp_ref, zp_ref)

    @pl.when((i >= n_prompt_tiles) & (i < n_tiles))
    def _():
        body(xs_ref, ys_ref, zs_ref)

    @pl.when(i == n_tiles)
    def _():
        xsort_ref[...] = jnp.zeros_like(xsort_ref)


def _dual_specs(tile, width, n_prompt_tiles, n_tiles):
    last_p = n_prompt_tiles - 1
    last_s = n_tiles - n_prompt_tiles - 1
    sp = pl.BlockSpec((tile, width), lambda i: (jnp.minimum(i, last_p), 0))
    ss = pl.BlockSpec((tile, width), lambda i: (jnp.clip(i - n_prompt_tiles, 0, last_s), 0))
    return sp, ss


def _route(xp, xs, y_p, y_s, zuv_p, zuv_s, mix_params, w_out, g_ffn, wr_hi, wr_cat, br_pad, n_exp):
    tp, d = xp.shape
    t = tp + xs.shape[0]
    tm = TOKEN_TILE
    npt = tp // tm
    nt = t // tm
    p_rows = _sorted_rows(tm)
    d_ssm = mix_params[0].shape[0]
    iota = lambda n, ax: lax.broadcasted_iota(I32, (n, n), ax)
    tri = (iota(tm, 0) < iota(tm, 1)).astype(BF16)
    ltri = (iota(n_exp, 1) < iota(n_exp, 0)).astype(BF16)
    const = lambda shape: pl.BlockSpec(shape, lambda i: (0,) * len(shape))
    tile = lambda i: jnp.minimum(i, nt - 1)
    dual = lambda rows, width: _dual_specs(rows, width, npt, nt)
    return pl.pallas_call(
        functools.partial(_route_kernel, n_prompt_tiles=npt, n_tiles=nt, n_exp=n_exp),
        grid=(nt + 1,),
        in_specs=[*dual(tm, d), *dual(tm // SSM_CHUNK, y_p.shape[1]), *dual(tm, zuv_p.shape[1]),
                  *[const(a.shape) for a in mix_params],
                  const(w_out.shape), const((1, d)), const(wr_hi.shape), const(wr_cat.shape),
                  const((1, LANES)), const((tm, tm)), const((n_exp, n_exp))],
        out_specs=[pl.BlockSpec((tm, d), lambda i: (tile(i), 0)),
                   pl.BlockSpec((p_rows, d // 2), lambda i: (i, 0)),
                   pl.BlockSpec((tm, LANES), lambda i: (tile(i), 0)),
                   pl.BlockSpec((None, n_exp, LANES), lambda i: (tile(i), 0, 0))],
        out_shape=[jax.ShapeDtypeStruct((t, d), F32),
                   jax.ShapeDtypeStruct(((nt + 1) * p_rows, d // 2), U32),
                   jax.ShapeDtypeStruct((t, LANES), F32),
                   jax.ShapeDtypeStruct((nt, n_exp, LANES), F32)],
        scratch_shapes=[pltpu.VMEM((d_ssm // LANES, tm, LANES), F32)],
        compiler_params=_cparams(),
        name="route",
    )(xp, xs, y_p, y_s, zuv_p, zuv_s, *mix_params, w_out, g_ffn, wr_hi, wr_cat, br_pad, tri, ltri)


def _slot_tables(c8, p_rows, n_blocks):
    nt, n_exp = c8.shape
    run0 = jnp.cumsum(c8, axis=1) - c8
    seg_len = c8.T
    cum = jnp.cumsum(seg_len, axis=1)
    tot = cum[:, -1]
    padded = (tot + MOE_TILE - 1) // MOE_TILE * MOE_TILE
    ends = jnp.cumsum(padded)
    starts = ends - padded
    n_used = (ends[-1] // MOE_TILE).astype(I32)
    blk = jnp.arange(n_blocks, dtype=I32)
    be = jnp.minimum(jnp.sum(ends[None, :] <= (blk * MOE_TILE)[:, None], axis=1), n_exp - 1).astype(I32)
    block_e = jnp.where(blk < n_used, be, be[jnp.maximum(n_used - 1, 0)])
    onehot = (be[:, None] == jnp.arange(n_exp, dtype=I32)[None, :]).astype(F32)
    pick = lambda tbl: jnp.dot(onehot, tbl.astype(F32), precision=lax.Precision.HIGHEST)
    run_end = pick(cum)
    run_beg = run_end - pick(seg_len)
    shift = pick((jnp.arange(nt, dtype=I32) * p_rows)[None, :] + run0.T) - run_beg
    grow = (blk[:, None] * MOE_TILE + jnp.arange(MOE_GROUPS, dtype=I32)[None, :] * SUBLANES).astype(F32)
    rel = grow - pick(starts[:, None])
    inside = (run_beg[:, None, :] <= rel[:, :, None]) & (rel[:, :, None] < run_end[:, None, :])
    src_row = (rel + jnp.sum(jnp.where(inside, shift[:, None, :], 0.0), axis=2)).astype(I32)
    valid = jnp.any(inside, axis=2) & (blk < n_used)[:, None]
    scratch0 = nt * p_rows // SUBLANES
    zero_group = scratch0 + 2 * MOE_GROUPS
    g_in = jnp.where(valid, src_row // SUBLANES, zero_group).astype(I32)
    g_scr = scratch0 + (blk[:, None] % 2) * MOE_GROUPS + jnp.arange(MOE_GROUPS, dtype=I32)[None, :]
    g_out = jnp.where(valid, src_row // SUBLANES, g_scr).astype(I32)
    return block_e, n_used, g_in, g_out


def _moe_kernel(be_ref, nu_ref, gin_ref, gnext_ref, gout_ref, xs_hbm, wgu_ref, bgu_ref, wd_ref, bd_ref,
                eo_hbm, xbuf, obuf, wgu_s, wd_s, isem, osem):
    j = pl.program_id(0)
    nu = nu_ref[0]
    f = wd_ref.shape[0]

    @pl.when((j < nu) & ((j == 0) | (be_ref[j] != be_ref[jnp.maximum(j - 1, 0)])))
    def _():
        for c in range(0, wgu_ref.shape[0], CAST_ROWS):
            wgu_s[c:c + CAST_ROWS, :] = wgu_ref[c:c + CAST_ROWS, :].astype(BF16)
        for c in range(0, wd_ref.shape[0], CAST_ROWS):
            wd_s[c:c + CAST_ROWS, :] = wd_ref[c:c + CAST_ROWS, :].astype(BF16)

    def in_copy(tbl_ref, r, sl):
        return pltpu.make_async_copy(xs_hbm.at[tbl_ref[0, r]], xbuf.at[sl, r], isem.at[sl])

    def out_copy(r, sl):
        return pltpu.make_async_copy(obuf.at[sl, r], eo_hbm.at[gout_ref[0, r]], osem.at[sl])

    def wait_in(sl):
        pltpu.make_async_copy(xs_hbm.at[pl.ds(0, MOE_GROUPS)], xbuf.at[sl], isem.at[sl]).wait()

    def wait_out(sl):
        pltpu.make_async_copy(obuf.at[sl], eo_hbm.at[pl.ds(0, MOE_GROUPS)], osem.at[sl]).wait()

    @pl.when(j == 0)
    def _():
        for r in range(MOE_GROUPS):
            in_copy(gin_ref, r, 0).start()

    def step(slot):
        other = 1 - slot
        wait_in(slot)

        @pl.when(j >= 2)
        def _():
            wait_out(slot)

        x = _unpack_bf16_pairs(xbuf[slot].reshape(MOE_TILE, xbuf.shape[-1]))
        gate = jnp.dot(x, wgu_s[:, :f], preferred_element_type=F32) + bgu_ref[:, :f]
        gate = jnp.minimum(gate, SWIGLU_LIMIT)
        sg = gate * jax.nn.sigmoid(SWIGLU_ALPHA * gate)
        up = jnp.dot(x, wgu_s[:, f:], preferred_element_type=F32) + bgu_ref[:, f:]
        act = sg * (jnp.clip(up, -SWIGLU_LIMIT, SWIGLU_LIMIT) + 1.0)
        for r in range(MOE_GROUPS):
            in_copy(gnext_ref, r, other).start()
        out = jnp.dot(act.astype(BF16), wd_s[...], preferred_element_type=F32) + bd_ref[...]
        obuf[slot] = _pack_bf16_pairs(out.astype(BF16).astype(F32)).reshape(obuf.shape[1:])
        for r in range(MOE_GROUPS):
            out_copy(r, slot).start()

        @pl.when(j == nu - 1)
        def _():
            wait_out(slot)
            wait_in(other)

            @pl.when(j >= 1)
            def _():
                wait_out(other)

    for parity in range(2):
        pl.when((j < nu) & (lax.rem(j, 2) == parity))(functools.partial(step, parity))


def _moe(block_e, n_used, g_in, g_out, xsorted, wgu, bgu, wd, bd):
    n_blocks = block_e.shape[0]
    dh = xsorted.shape[1]
    d = 2 * dh
    f2 = wgu.shape[2]
    f = wd.shape[1]
    g_in3 = g_in.reshape(n_blocks, 1, MOE_GROUPS)
    g_next3 = jnp.concatenate([g_in3[1:], g_in3[:1]], axis=0)
    g_out3 = g_out.reshape(n_blocks, 1, MOE_GROUPS)
    xs3 = xsorted.reshape(xsorted.shape[0] // SUBLANES, SUBLANES, dh)
    tbl = pl.BlockSpec((None, 1, MOE_GROUPS), lambda j, be, nu: (j, 0, 0), memory_space=pltpu.SMEM)
    return pl.pallas_call(
        _moe_kernel,
        grid_spec=pltpu.PrefetchScalarGridSpec(
            num_scalar_prefetch=2, grid=(n_blocks,),
            in_specs=[tbl, tbl, tbl,
                      pl.BlockSpec(memory_space=pl.ANY),
                      pl.BlockSpec((None, d, f2), lambda j, be, nu: (be[j], 0, 0)),
                      pl.BlockSpec((None, 1, f2), lambda j, be, nu: (be[j], 0, 0)),
                      pl.BlockSpec((None, f, d), lambda j, be, nu: (be[j], 0, 0)),
                      pl.BlockSpec((None, 1, d), lambda j, be, nu: (be[j], 0, 0))],
            out_specs=pl.BlockSpec(memory_space=pl.ANY),
            scratch_shapes=[pltpu.VMEM((2, MOE_GROUPS, SUBLANES, dh), U32),
                            pltpu.VMEM((2, MOE_GROUPS, SUBLANES, dh), U32),
                            pltpu.VMEM((d, f2), BF16), pltpu.VMEM((f, d), BF16),
                            pltpu.SemaphoreType.DMA((2,)), pltpu.SemaphoreType.DMA((2,))]),
        out_shape=jax.ShapeDtypeStruct(xs3.shape, U32),
        input_output_aliases={5: 0},
        compiler_params=_cparams(),
        name="moe",
    )(block_e, n_used, g_in3, g_next3, g_out3, xs3, wgu, bgu, wd, bd).reshape(xsorted.shape)


def _combine_kernel(meta_ref, x1_ref, g_ref, eo_ref, y_ref):
    tm = x1_ref.shape[0]
    p_rows = eo_ref.shape[0]
    lane = lax.broadcasted_iota(I32, (tm, SORT_CHUNK), 1).astype(F32).astype(BF16)
    ws = [meta_ref[:, k:k + 1].astype(BF16) for k in range(TOP_K)]
    pos = [meta_ref[:, TOP_K + k:TOP_K + k + 1] for k in range(TOP_K)]
    acc = x1_ref[...]
    for r in range(p_rows // SORT_CHUNK):
        buf = _unpack_bf16_pairs(eo_ref[r * SORT_CHUNK:(r + 1) * SORT_CHUNK, :])
        lo_r = float(r * SORT_CHUNK)
        wm = jnp.zeros((tm, SORT_CHUNK), BF16)
        for k in range(TOP_K):
            rel = jnp.where((pos[k] >= lo_r) & (pos[k] < lo_r + SORT_CHUNK), pos[k] - lo_r, -1.0).astype(BF16)
            wm = jnp.where(lane == rel, ws[k], wm)
        acc = acc + jnp.dot(wm, buf, preferred_element_type=F32)
    y_ref[...] = _rms(acc, g_ref[...])


def _combine(meta, x1, g_final, eo, tile_off, n_tokens):
    d = x1.shape[1]
    tm = TOKEN_TILE
    p_rows = _sorted_rows(tm)
    return pl.pallas_call(
        _combine_kernel,
        grid=(n_tokens // tm,),
        in_specs=[pl.BlockSpec((tm, LANES), lambda i: (i + tile_off, 0)),
                  pl.BlockSpec((tm, d), lambda i: (i + tile_off, 0)),
                  pl.BlockSpec((1, d), lambda i: (0, 0)),
                  pl.BlockSpec((p_rows, d // 2), lambda i: (i + tile_off, 0))],
        out_specs=pl.BlockSpec((tm, d), lambda i: (i, 0)),
        out_shape=jax.ShapeDtypeStruct((n_tokens, d), F32),
        compiler_params=_cparams(),
        name="combine",
    )(meta, x1, g_final, eo)


def kernel(x_prompt, x_sample, g_mix, w_in, ssm_lam_re, ssm_lam_im, ssm_log_dt, ssm_b_re, ssm_b_im,
           ssm_c_re, ssm_c_im, ssm_d, w_glu, b_glu, sgu_ln_g, sgu_ln_b, sgu_w_s, sgu_b_s,
           g_out_ssm, g_out_sgu, w_out, g_ffn, w_router, b_router, w_gate_up, b_gate_up,
           w_down, b_down, g_final):
    assert g_mix.shape[0] == 1, "single-layer trunk"
    bp, lp, d = x_prompt.shape
    bs, ls, _ = x_sample.shape
    tp, ts = bp * lp, bs * ls
    t = tp + ts
    d_ssm = ssm_d.shape[1]
    d_sgu = sgu_ln_g.shape[1]
    n_exp = w_router.shape[2]
    assert all(n % tile == 0 for n in (tp, ts) for tile in (SEQ_TILE, TOKEN_TILE))
    assert lp % CHUNK == 0 and ls % CHUNK == 0 and TOKEN_TILE % CHUNK == 0 and CHUNK % SSM_CHUNK == 0
    assert SEQ_TILE % SSM_CHUNK == 0
    assert SSM_CHUNK * SSM_GROUP == MXU_DIM and 2 * ssm_lam_re.shape[-1] == LANES
    assert d_sgu // SGU_HEADS == LANES // 2 and n_exp <= LANES
    assert n_exp * (SUBLANES - 1) <= MXU_DIM and _sorted_rows(TOKEN_TILE) % SORT_CHUNK == 0
    assert (2 * MOE_GROUPS + 1) * SUBLANES <= _sorted_rows(TOKEN_TILE)

    xp = x_prompt.reshape(tp, d)
    xs = x_sample.reshape(ts, d)
    row = lambda a: a.reshape(1, -1).astype(F32)

    w_in_b = w_in[0].astype(BF16)
    u_p, zuv_p = _inproj(xp, row(g_mix[0]), w_in_b, d_ssm)
    u_s, zuv_s = _inproj(xs, row(g_mix[0]), w_in_b, d_ssm)

    mats = _ssm_matrices(ssm_lam_re[0], ssm_lam_im[0], ssm_log_dt[0], ssm_b_re[0], ssm_b_im[0],
                         ssm_c_re[0], ssm_c_im[0], ssm_d[0])
    y_p = _ssm_trunk(u_p, bp, mats)
    y_s = _ssm_trunk(u_s, bs, mats)

    ws = sgu_w_s[0]
    ws_pairs = jnp.concatenate([ws[0::2], ws[1::2]], axis=2).astype(BF16)
    bias_s = jnp.repeat(sgu_b_s[0].T, d_sgu // SGU_HEADS, axis=1).astype(F32)
    mix_params = (w_glu[0].astype(BF16), row(b_glu[0]), row(g_out_ssm[0]), row(sgu_ln_g[0]), row(sgu_ln_b[0]),
                  ws_pairs, bias_s, row(g_out_sgu[0]))

    wr_pad = jnp.pad(w_router[0].astype(F32), ((0, 0), (0, LANES - n_exp)))
    wr_hi = wr_pad.astype(BF16)
    wr_cat = jnp.concatenate([wr_hi, (wr_pad - wr_hi.astype(F32)).astype(BF16)], axis=1)
    br_pad = jnp.pad(b_router[0].astype(F32), (0, LANES - n_exp)).reshape(1, LANES)
    x1, xsorted, meta, c8 = _route(xp, xs, y_p, y_s, zuv_p, zuv_s, mix_params, w_out[0].astype(BF16),
                                   row(g_ffn[0]), wr_hi, wr_cat, br_pad, n_exp)

    nt = t // TOKEN_TILE
    p_rows = _sorted_rows(TOKEN_TILE)
    max_rows = t * TOP_K + nt * n_exp * (SUBLANES - 1) + n_exp * (MOE_TILE - 1)
    n_blocks = -(-max_rows // MOE_TILE)
    block_e, n_used, g_in, g_out = _slot_tables(c8[:, :, 0].astype(I32), p_rows, n_blocks)
    eo = _moe(block_e, n_used.reshape(1), g_in, g_out, xsorted, w_gate_up[0],
              b_gate_up[0][:, None, :].astype(F32), w_down[0], b_down[0][:, None, :].astype(F32))

    gf = row(g_final)
    y_prompt = _combine(meta, x1, gf, eo, 0, tp)
    y_sample = _combine(meta, x1, gf, eo, tp // TOKEN_TILE, ts)
    return y_prompt.reshape(bp, lp, d), y_sample.reshape(bs, ls, d)
```
